```python
import math
import jax, jax.numpy as jnp
from jax import lax
import numpy as np

D_MODEL = 1024
BATCH = 8
SEQ = 4096
DEPTH = 1

D_MIX = D_MODEL
D_POOL = D_MIX // 2
D_CONV = D_MIX - D_POOL
POOL_WINDOWS = (2, 4, 8, 16)
N_POOL_GROUPS = len(POOL_WINDOWS)
POOL_GROUP_DIM = D_POOL // N_POOL_GROUPS
N_CONV_HEADS = 8
CONV_WIDTH = 3
D_IN_PROJ = D_POOL + 3 * D_CONV

N_EXPERTS = 32
TOP_K = 4
D_FF = D_MODEL
SWIGLU_LIMIT = 7.0
SWIGLU_ALPHA = 1.702
EXPERT_BLOCK = 128

PLE_DIM = 256

DEEPNORM_ALPHA = (2.0 * DEPTH) ** 0.25
DEEPNORM_BETA = (8.0 * DEPTH) ** -0.25
LN_EPS = 1e-5

kernel_name = "hybrid_pool_shortconv_moe_deepnorm"


def layer_norm(x, g, b):
    xf = x.astype(jnp.float32)
    mu = jnp.mean(xf, axis=-1, keepdims=True)
    var = jnp.mean(jnp.square(xf - mu), axis=-1, keepdims=True)
    y = (xf - mu) * lax.rsqrt(var + LN_EPS)
    return (y * g.astype(jnp.float32) + b.astype(jnp.float32)).astype(x.dtype)


def pool_mixer(v, pool_mix, pool_scale):
    B, S, _ = v.shape
    vf = v.astype(jnp.float32)
    t = jnp.arange(S)
    outs = []
    for g, w in enumerate(POOL_WINDOWS):
        vg = vf[..., g * POOL_GROUP_DIM:(g + 1) * POOL_GROUP_DIM]
        cs = jnp.cumsum(vg, axis=1)
        lagged = jnp.pad(cs, ((0, 0), (w, 0), (0, 0)))[:, :S]
        cnt = jnp.minimum(t + 1, w).astype(jnp.float32)[None, :, None]
        outs.append((cs - lagged) / cnt - vg)
    d = jnp.stack(outs, axis=2).astype(v.dtype)
    y = jnp.einsum('bsgc,gcd->bsgd', d, pool_mix).reshape(B, S, D_POOL)
    return y * pool_scale


def conv_mixer(b_gate, c_gate, v, conv_w):
    S = v.shape[1]
    u = c_gate * v
    up = jnp.pad(u, ((0, 0), (CONV_WIDTH - 1, 0), (0, 0)))
    y = conv_w[0] * up[:, 0:S]
    for k in range(1, CONV_WIDTH):
        y = y + conv_w[k] * up[:, k:k + S]
    return b_gate * y


def clamped_swiglu_expert(xb, w_gu, b_gu, w_dn, b_dn):
    gu = xb @ w_gu + b_gu
    gate = jnp.minimum(gu[:, :D_FF], SWIGLU_LIMIT)
    up = jnp.clip(gu[:, D_FF:], -SWIGLU_LIMIT, SWIGLU_LIMIT)
    glu = gate * jax.nn.sigmoid(SWIGLU_ALPHA * gate)
    return ((up + 1.0) * glu) @ w_dn + b_dn


def moe(h, router_w, router_b, w_gate_up, b_gate_up, w_down, b_down):
    B, S, D = h.shape
    N = B * S
    hf = h.reshape(N, D)
    logits = (hf @ router_w + router_b).astype(jnp.float32)
    top_v, top_i = lax.top_k(logits, TOP_K)
    gates = jax.nn.softmax(top_v, axis=-1)
    A = N * TOP_K
    flat_e = top_i.reshape(A).astype(jnp.int32)
    flat_tok = jnp.repeat(jnp.arange(N, dtype=jnp.int32), TOP_K)
    flat_g = gates.reshape(A)
    order = jnp.argsort(flat_e)
    e_sorted = flat_e[order]
    counts = jax.ops.segment_sum(jnp.ones((A,), jnp.int32), flat_e, num_segments=N_EXPERTS)
    offsets = jnp.cumsum(counts) - counts
    padded = ((counts + EXPERT_BLOCK - 1) // EXPERT_BLOCK) * EXPERT_BLOCK
    pad_end = jnp.cumsum(padded)
    pad_start = pad_end - padded
    dest = pad_start[e_sorted] + (jnp.arange(A, dtype=jnp.int32) - offsets[e_sorted])
    P = A + N_EXPERTS * EXPERT_BLOCK
    NB = P // EXPERT_BLOCK
    row_tok = jnp.zeros((P,), jnp.int32).at[dest].set(flat_tok[order])
    row_gate = jnp.zeros((P,), jnp.float32).at[dest].set(flat_g[order])
    blk_start = jnp.arange(NB, dtype=jnp.int32) * EXPERT_BLOCK
    blk_e = jnp.minimum(jnp.sum(blk_start[:, None] >= pad_end[None, :], axis=1),
                        N_EXPERTS - 1).astype(jnp.int32)
    xb = hf[row_tok].reshape(NB, EXPERT_BLOCK, D)

    def run_block(args):
        xblk, e = args
        return clamped_swiglu_expert(xblk, w_gate_up[e], b_gate_up[e], w_down[e], b_down[e])

    yb = lax.map(run_block, (xb, blk_e)).reshape(P, D)
    y = jax.ops.segment_sum(yb * row_gate[:, None].astype(yb.dtype), row_tok, num_segments=N)
    return y.reshape(B, S, D)


def setup_inputs(seed: int = 0) -> dict:
    key = jax.random.key(seed)
    ks = jax.random.split(key, 24)
    L, D, E, F = DEPTH, D_MODEL, N_EXPERTS, D_FF
    nrm = lambda k, shape, s: jax.random.normal(k, shape, jnp.float32) * s
    return {
        "x": nrm(ks[0], (BATCH, SEQ, D), 1.0),
        "p": nrm(ks[1], (L, BATCH, SEQ, PLE_DIM), 1.0),
        "w_in": nrm(ks[2], (L, D, D_IN_PROJ), D ** -0.5),
        "pool_mix": nrm(ks[3], (L, N_POOL_GROUPS, POOL_GROUP_DIM, POOL_GROUP_DIM), POOL_GROUP_DIM ** -0.5),
        "pool_scale": 1.0 + nrm(ks[4], (L, D_POOL), 0.1),
        "conv_w": nrm(ks[5], (L, CONV_WIDTH, D_CONV), CONV_WIDTH ** -0.5),
        "w_out": nrm(ks[6], (L, D_MIX, D), D_MIX ** -0.5 * DEEPNORM_BETA),
        "ln1_g": 1.0 + nrm(ks[7], (L, D), 0.02),
        "ln1_b": nrm(ks[8], (L, D), 0.02),
        "router_w": nrm(ks[9], (L, D, E), D ** -0.5),
        "router_b": nrm(ks[10], (L, E), 0.01),
        "w_gate_up": nrm(ks[11], (L, E, D, 2 * F), D ** -0.5),
        "b_gate_up": nrm(ks[12], (L, E, 2 * F), 0.02),
        "w_down": nrm(ks[13], (L, E, F, D), F ** -0.5 * DEEPNORM_BETA),
        "b_down": nrm(ks[14], (L, E, D), 0.02),
        "ln2_g": 1.0 + nrm(ks[15], (L, D), 0.02),
        "ln2_b": nrm(ks[16], (L, D), 0.02),
        "ple_proj": nrm(ks[17], (L, PLE_DIM, D), PLE_DIM ** -0.5 * DEEPNORM_BETA),
        "ple_gate_w": nrm(ks[18], (L, D, D), D ** -0.5),
        "ple_gate_b": nrm(ks[19], (L, D), 0.02),
        "ln3_g": 1.0 + nrm(ks[20], (L, D), 0.02),
        "ln3_b": nrm(ks[21], (L, D), 0.02),
    }


def reference(x, p, w_in, pool_mix, pool_scale, conv_w, w_out, ln1_g, ln1_b,
              router_w, router_b, w_gate_up, b_gate_up, w_down, b_down,
              ln2_g, ln2_b, ple_proj, ple_gate_w, ple_gate_b, ln3_g, ln3_b):
    for i in range(DEPTH):
        proj = x @ w_in[i]
        v_pool = proj[..., :D_POOL]
        b_gate = proj[..., D_POOL:D_POOL + D_CONV]
        c_gate = proj[..., D_POOL + D_CONV:D_POOL + 2 * D_CONV]
        v_conv = proj[..., D_POOL + 2 * D_CONV:]
        y_pool = pool_mixer(v_pool, pool_mix[i], pool_scale[i])
        y_conv = conv_mixer(b_gate, c_gate, v_conv, conv_w[i])
        mix = jnp.concatenate([y_pool, y_conv], axis=-1) @ w_out[i]
        x = layer_norm(DEEPNORM_ALPHA * x + mix, ln1_g[i], ln1_b[i])
        ffn = moe(x, router_w[i], router_b[i], w_gate_up[i], b_gate_up[i], w_down[i], b_down[i])
        x = layer_norm(DEEPNORM_ALPHA * x + ffn, ln2_g[i], ln2_b[i])
        gate = jax.nn.sigmoid((x @ ple_gate_w[i] + ple_gate_b[i]).astype(jnp.float32)).astype(x.dtype)
        ple = p[i] @ ple_proj[i]
        x = layer_norm(DEEPNORM_ALPHA * x + gate * ple, ln3_g[i], ln3_b[i])
    return x
```

```python
import functools

import jax
import jax.numpy as jnp
from jax import lax
from jax.experimental import pallas as pl
from jax.experimental.pallas import tpu as pltpu

D_MODEL = 1024
D_POOL = 512
D_CONV = 512
POOL_WINDOWS = (2, 4, 8, 16)
POOL_GROUP_DIM = 128
CONV_WIDTH = 3
D_IN_PROJ = D_POOL + 3 * D_CONV
N_EXPERTS = 32
TOP_K = 4
D_FF = 1024
SWIGLU_LIMIT = 7.0
SWIGLU_ALPHA = 1.702
PLE_DIM = 256
DEPTH = 1
DEEPNORM_ALPHA = (2.0 * DEPTH) ** 0.25
LN_EPS = 1e-5

LANES = 128
SUBLANES = 8
ROW_CHUNKS = D_MODEL // LANES
POOL_HALO = 16
CONV_HALO = 8

TS_MIX = 512
TS_DISPATCH = 512
TM_EXPERT = 512
TS_COMBINE = 256

_BF16 = jnp.bfloat16
_F32 = jnp.float32


def _layer_norm(h, g, b):
    mu = jnp.mean(h, axis=-1, keepdims=True)
    c = h - mu
    var = jnp.mean(c * c, axis=-1, keepdims=True)
    return c * lax.rsqrt(var + LN_EPS) * g + b


def _to_row_tiles(ref, val):
    rows = val.shape[0]
    for c in range(ROW_CHUNKS):
        ref[pl.ds(c, rows, stride=ROW_CHUNKS), :] = val[:, c * LANES:(c + 1) * LANES]


def _from_row_tiles(ref, rows):
    return jnp.concatenate(
        [ref[pl.ds(c, rows, stride=ROW_CHUNKS), :] for c in range(ROW_CHUNKS)], axis=1)


def _mix_route_kernel(x_ref, w_in_ref, pmix_ref, pscale_ref, convw_ref, w_out_ref, g1_ref, b1_ref,
                      rwt_ref, rb_ref, tri_ref,
                      x1rt_ref, ri_ref, gate_ref, cnt_ref,
                      carry_v, carry_u, cnt_sc):
    b = pl.program_id(0)
    s = pl.program_id(1)
    ts = TS_MIX

    @pl.when(jnp.logical_and(b == 0, s == 0))
    def _():
        cnt_sc[...] = jnp.zeros_like(cnt_sc)

    @pl.when(s == 0)
    def _():
        carry_v[...] = jnp.zeros_like(carry_v)
        carry_u[...] = jnp.zeros_like(carry_u)

    xb = x_ref[0]
    proj = jnp.dot(xb.astype(_BF16), w_in_ref[...], preferred_element_type=_F32)
    vp = proj[:, :D_POOL]
    bg = proj[:, D_POOL:D_POOL + D_CONV]
    cg = proj[:, D_POOL + D_CONV:D_POOL + 2 * D_CONV]
    vc = proj[:, D_POOL + 2 * D_CONV:]

    ext = jnp.concatenate([carry_v[...], vp], axis=0)
    pos = lax.broadcasted_iota(jnp.int32, (ts, 1), 0) + s * ts
    mixed = []
    for g, w in enumerate(POOL_WINDOWS):
        lo, hi = g * POOL_GROUP_DIM, (g + 1) * POOL_GROUP_DIM
        acc = ext[:, lo:hi]
        sh = 1
        while sh < w:
            acc = acc + pltpu.roll(acc, sh, 0)
            sh *= 2
        cnt = jnp.minimum(pos + 1, w).astype(_F32)
        d = acc[POOL_HALO:] / cnt - vp[:, lo:hi]
        yg = jnp.dot(d.astype(_BF16), pmix_ref[g], preferred_element_type=_F32)
        mixed.append(yg * pscale_ref[:, lo:hi])
    carry_v[...] = vp[ts - POOL_HALO:]

    u = cg * vc
    extu = jnp.concatenate([carry_u[...], u], axis=0)
    u1 = pltpu.roll(extu, 1, 0)[CONV_HALO:]
    u2 = pltpu.roll(extu, 2, 0)[CONV_HALO:]
    yc = bg * (convw_ref[0:1, :] * u2 + convw_ref[1:2, :] * u1 + convw_ref[2:3, :] * u)
    carry_u[...] = u[ts - CONV_HALO:]
    mixed.append(yc)

    mix_in = jnp.concatenate(mixed, axis=1).astype(_BF16)
    mix = jnp.dot(mix_in, w_out_ref[...], preferred_element_type=_F32)
    x1 = _layer_norm(DEEPNORM_ALPHA * xb + mix, g1_ref[...], b1_ref[...])
    _to_row_tiles(x1rt_ref, x1)

    logits = lax.dot_general(rwt_ref[...], x1.astype(_BF16), (((1,), (1,)), ((), ())),
                             preferred_element_type=_F32) + rb_ref[:, 0:1]
    eidx = lax.broadcasted_iota(jnp.int32, (N_EXPERTS, ts), 0).astype(_F32)
    vals, sels = [], []
    work = logits
    for k in range(TOP_K):
        m = jnp.max(work, axis=0, keepdims=True)
        first = jnp.min(jnp.where(work == m, eidx, float(N_EXPERTS)), axis=0, keepdims=True)
        sel = eidx == first
        work = jnp.where(sel, -jnp.inf, work)
        vals.append(m)
        sels.append(sel)
        ri_ref[k:k + 1, :] = first.astype(jnp.int32)
    exps = [jnp.exp(v - vals[0]) for v in vals]
    denom = exps[0] + exps[1] + exps[2] + exps[3]
    for k in range(TOP_K):
        gate_ref[k:k + 1, :] = exps[k] / denom
    gate_ref[TOP_K:, :] = jnp.zeros((SUBLANES - TOP_K, ts), _F32)

    chosen = jnp.zeros((N_EXPERTS, ts), _F32)
    for sel in sels:
        chosen = chosen + sel.astype(_F32)
    incl = jnp.dot(chosen.astype(_BF16), tri_ref[...], preferred_element_type=_F32)
    base = cnt_sc[:, 0:1] + (incl - chosen)
    for k in range(TOP_K):
        rank = jnp.sum(jnp.where(sels[k], base, 0.0), axis=0, keepdims=True)
        ri_ref[TOP_K + k:TOP_K + k + 1, :] = rank.astype(jnp.int32)
    total = cnt_sc[...] + jnp.sum(chosen, axis=1, keepdims=True)
    cnt_sc[...] = total
    cnt_ref[...] = total.astype(jnp.int32)


def _mix_route(x, w_in, pmix, pscale, convw, w_out, g1, b1, rwt, rb, tri):
    bsz, seq, d = x.shape
    n = bsz * seq
    ns = seq // TS_MIX
    full = lambda shape: pl.BlockSpec(shape, lambda b, s: (0,) * len(shape))
    return pl.pallas_call(
        _mix_route_kernel,
        grid=(bsz, ns),
        in_specs=[
            pl.BlockSpec((1, TS_MIX, d), lambda b, s: (b, s, 0)),
            full(w_in.shape), full(pmix.shape), full(pscale.shape), full(convw.shape),
            full(w_out.shape), full(g1.shape), full(b1.shape), full(rwt.shape), full(rb.shape),
            full(tri.shape),
        ],
        out_specs=[
            pl.BlockSpec((TS_MIX * ROW_CHUNKS, LANES), lambda b, s: (b * ns + s, 0)),
            pl.BlockSpec((2 * TOP_K, TS_MIX), lambda b, s: (0, b * ns + s)),
            pl.BlockSpec((SUBLANES, TS_MIX), lambda b, s: (0, b * ns + s)),
            pl.BlockSpec((N_EXPERTS, LANES), lambda b, s: (0, 0)),
        ],
        out_shape=[
            jax.ShapeDtypeStruct((n * ROW_CHUNKS, LANES), _F32),
            jax.ShapeDtypeStruct((2 * TOP_K, n), jnp.int32),
            jax.ShapeDtypeStruct((SUBLANES, n), _F32),
            jax.ShapeDtypeStruct((N_EXPERTS, LANES), jnp.int32),
        ],
        scratch_shapes=[
            pltpu.VMEM((POOL_HALO, D_POOL), _F32),
            pltpu.VMEM((CONV_HALO, D_CONV), _F32),
            pltpu.VMEM((N_EXPERTS, LANES), _F32),
        ],
        compiler_params=pltpu.CompilerParams(
            dimension_semantics=("arbitrary", "arbitrary"),
            vmem_limit_bytes=48 * 1024 * 1024),
        name="mix_route",
    )(x, w_in, pmix, pscale, convw, w_out, g1, b1, rwt, rb, tri)


def _make_dest_kernel(start_ref, ri_ref, dest_ref):
    ei = ri_ref[0:TOP_K, :]
    acc = ri_ref[TOP_K:, :]
    for e in range(N_EXPERTS):
        acc = acc + jnp.where(ei == e, start_ref[e], 0)
    dest_ref[...] = acc


def _make_dest(pad_start, ri):
    n = ri.shape[1]
    return pl.pallas_call(
        _make_dest_kernel,
        in_specs=[pl.BlockSpec(memory_space=pltpu.SMEM),
                  pl.BlockSpec((2 * TOP_K, n), lambda: (0, 0))],
        out_specs=pl.BlockSpec((TOP_K, n), lambda: (0, 0)),
        out_shape=jax.ShapeDtypeStruct((TOP_K, n), jnp.int32),
        name="make_dest",
    )(pad_start, ri)


def _dispatch_kernel(dest_ref, src_ref, init_ref, out_ref, sem):
    del init_ref
    i = pl.program_id(0)
    ts = TS_DISPATCH

    def row_copy(src_row, dst_row):
        return pltpu.make_async_copy(
            src_ref.at[pl.ds(pl.multiple_of(src_row * ROW_CHUNKS, ROW_CHUNKS), ROW_CHUNKS)],
            out_ref.at[pl.ds(pl.multiple_of(dst_row * ROW_CHUNKS, ROW_CHUNKS), ROW_CHUNKS)],
            sem)

    def body(t, carry):
        for k in range(TOP_K):
            row_copy(i * ts + t, dest_ref[k, t]).start()
        return carry

    lax.fori_loop(0, ts, body, 0, unroll=8)
    n_rows = TOP_K * ts * ROW_CHUNKS
    pltpu.make_async_copy(src_ref.at[pl.ds(0, n_rows)], out_ref.at[pl.ds(0, n_rows)], sem).wait()


def _dispatch(dest, x1rt, n_slots):
    n = dest.shape[1]
    init = jnp.zeros((n_slots * ROW_CHUNKS, LANES), _F32)
    return pl.pallas_call(
        _dispatch_kernel,
        grid=(n // TS_DISPATCH,),
        in_specs=[
            pl.BlockSpec((TOP_K, TS_DISPATCH), lambda i: (0, i), memory_space=pltpu.SMEM),
            pl.BlockSpec(memory_space=pl.ANY),
            pl.BlockSpec(memory_space=pl.ANY),
        ],
        out_specs=pl.BlockSpec(memory_space=pl.ANY),
        out_shape=jax.ShapeDtypeStruct(init.shape, _F32),
        scratch_shapes=[pltpu.SemaphoreType.DMA(())],
        input_output_aliases={2: 0},
        compiler_params=pltpu.CompilerParams(dimension_semantics=("arbitrary",)),
        name="dispatch",
    )(dest, x1rt, init)


def _experts_kernel(be_ref, nv_ref, xs_ref, wgu_ref, bgu_ref, wdn_ref, bdn_ref, out_ref):
    del be_ref
    j = pl.program_id(0)
    tm = TM_EXPERT

    @pl.when(j < nv_ref[0])
    def _():
        x = _from_row_tiles(xs_ref, tm).astype(_BF16)
        gu = jnp.dot(x, wgu_ref[0], preferred_element_type=_F32) + bgu_ref[0]
        gate = jnp.minimum(gu[:, :D_FF], SWIGLU_LIMIT)
        up = jnp.clip(gu[:, D_FF:], -SWIGLU_LIMIT, SWIGLU_LIMIT)
        glu = gate * jax.nn.sigmoid(SWIGLU_ALPHA * gate)
        act = ((up + 1.0) * glu).astype(_BF16)
        y = jnp.dot(act, wdn_ref[0], preferred_element_type=_F32) + bdn_ref[0]
        _to_row_tiles(out_ref, y)

    @pl.when(j >= nv_ref[0])
    def _():
        out_ref[...] = jnp.zeros_like(out_ref)


def _experts(blk_e, n_valid, xs, wgu, bgu, wdn, bdn):
    n_slots = xs.shape[0] // ROW_CHUNKS
    nb = n_slots // TM_EXPERT
    rows = TM_EXPERT * ROW_CHUNKS
    grid_spec = pltpu.PrefetchScalarGridSpec(
        num_scalar_prefetch=2,
        grid=(nb,),
        in_specs=[
            pl.BlockSpec((rows, LANES), lambda j, be, nv: (jnp.minimum(j, nv[0] - 1), 0)),
            pl.BlockSpec((1, D_MODEL, 2 * D_FF), lambda j, be, nv: (be[j], 0, 0)),
            pl.BlockSpec((1, 1, 2 * D_FF), lambda j, be, nv: (be[j], 0, 0)),
            pl.BlockSpec((1, D_FF, D_MODEL), lambda j, be, nv: (be[j], 0, 0)),
            pl.BlockSpec((1, 1, D_MODEL), lambda j, be, nv: (be[j], 0, 0)),
        ],
        out_specs=pl.BlockSpec((rows, LANES), lambda j, be, nv: (j, 0)),
    )
    return pl.pallas_call(
        _experts_kernel,
        grid_spec=grid_spec,
        out_shape=jax.ShapeDtypeStruct(xs.shape, _F32),
        compiler_params=pltpu.CompilerParams(
            dimension_semantics=("arbitrary",),
            vmem_limit_bytes=56 * 1024 * 1024),
        name="experts",
    )(blk_e, n_valid, xs, wgu, bgu, wdn, bdn)


def _combine_kernel(dest_ref, yb_ref, x1rt_ref, gt_ref, p_ref, g2_ref, b2_ref, pw_ref, gw_ref, gb_ref,
                    g3_ref, b3_ref, out_ref, gbuf, sem):
    ts = TS_COMBINE

    def row_copy(k, t, src_row):
        return pltpu.make_async_copy(
            yb_ref.at[pl.ds(pl.multiple_of(src_row * ROW_CHUNKS, ROW_CHUNKS), ROW_CHUNKS)],
            gbuf.at[k, pl.ds(pl.multiple_of(t * ROW_CHUNKS, ROW_CHUNKS), ROW_CHUNKS)],
            sem)

    def body(t, carry):
        for k in range(TOP_K):
            row_copy(k, t, dest_ref[k, t]).start()
        return carry

    lax.fori_loop(0, ts, body, 0, unroll=8)
    for k in range(TOP_K):
        pltpu.make_async_copy(yb_ref.at[pl.ds(0, ts * ROW_CHUNKS)], gbuf.at[k], sem).wait()

    x1 = _from_row_tiles(x1rt_ref, ts)
    ffn = jnp.zeros((ts, D_MODEL), _F32)
    for k in range(TOP_K):
        ffn = ffn + gt_ref[:, k:k + 1] * _from_row_tiles(gbuf.at[k], ts)
    x2 = _layer_norm(DEEPNORM_ALPHA * x1 + ffn, g2_ref[...], b2_ref[...])
    z = jnp.dot(x2.astype(_BF16), gw_ref[...], preferred_element_type=_F32) + gb_ref[...]
    ple = jnp.dot(p_ref[...].astype(_BF16), pw_ref[...], preferred_element_type=_F32)
    x3 = _layer_norm(DEEPNORM_ALPHA * x2 + jax.nn.sigmoid(z) * ple, g3_ref[...], b3_ref[...])
    out_ref[...] = x3


def _combine(dest, yb, x1rt, gates_t, p2d, g2, b2, pw, gw, gb, g3, b3):
    n = dest.shape[1]
    ts = TS_COMBINE
    full = lambda shape: pl.BlockSpec(shape, lambda i: (0,) * len(shape))
    return pl.pallas_call(
        _combine_kernel,
        grid=(n // ts,),
        in_specs=[
            pl.BlockSpec((TOP_K, ts), lambda i: (0, i), memory_space=pltpu.SMEM),
            pl.BlockSpec(memory_space=pl.ANY),
            pl.BlockSpec((ts * ROW_CHUNKS, LANES), lambda i: (i, 0)),
            pl.BlockSpec((ts, TOP_K), lambda i: (i, 0)),
            pl.BlockSpec((ts, PLE_DIM), lambda i: (i, 0)),
            full(g2.shape), full(b2.shape), full(pw.shape), full(gw.shape), full(gb.shape),
            full(g3.shape), full(b3.shape),
        ],
        out_specs=pl.BlockSpec((ts, D_MODEL), lambda i: (i, 0)),
        out_shape=jax.ShapeDtypeStruct((n, D_MODEL), _F32),
        scratch_shapes=[
            pltpu.VMEM((TOP_K, ts * ROW_CHUNKS, LANES), _F32),
            pltpu.SemaphoreType.DMA(()),
        ],
        compiler_params=pltpu.CompilerParams(
            dimension_semantics=("arbitrary",),
            vmem_limit_bytes=48 * 1024 * 1024),
        name="combine",
    )(dest, yb, x1rt, gates_t, p2d, g2, b2, pw, gw, gb, g3, b3)


def kernel(x, p, w_in, pool_mix, pool_scale, conv_w, w_out, ln1_g, ln1_b, router_w, router_b,
           w_gate_up, b_gate_up, w_down, b_down, ln2_g, ln2_b, ple_proj, ple_gate_w, ple_gate_b,
           ln3_g, ln3_b):
    assert DEPTH == 1 and x.shape[-1] == D_MODEL
    bsz, seq, d = x.shape
    n = bsz * seq
    assert seq % TS_MIX == 0 and n % TS_DISPATCH == 0 and n % TS_COMBINE == 0
    row = lambda v: v.reshape(1, -1)

    tri = jnp.triu(jnp.ones((TS_MIX, TS_MIX), _BF16))
    x1rt, ri, gates, counts = _mix_route(
        x, w_in[0].astype(_BF16), pool_mix[0].astype(_BF16), row(pool_scale[0]), conv_w[0],
        w_out[0].astype(_BF16), row(ln1_g[0]), row(ln1_b[0]),
        router_w[0].T.astype(_BF16), jnp.broadcast_to(router_b[0][:, None], (N_EXPERTS, LANES)), tri)

    tm = TM_EXPERT
    n_slots = n * TOP_K + N_EXPERTS * tm
    nb = n_slots // tm
    cnt = counts[:, 0]
    padded = ((cnt + tm - 1) // tm) * tm
    pad_end = jnp.cumsum(padded)
    pad_start = (pad_end - padded).astype(jnp.int32)
    n_valid = (pad_end[-1] // tm).astype(jnp.int32)
    tile_start = jnp.arange(nb, dtype=jnp.int32) * tm
    blk_e = jnp.minimum(jnp.sum(tile_start[:, None] >= pad_end[None, :], axis=1), N_EXPERTS - 1)
    blk_e = jnp.where(jnp.arange(nb) < n_valid, blk_e, blk_e[n_valid - 1]).astype(jnp.int32)

    dest = _make_dest(pad_start, ri)
    xs = _dispatch(dest, x1rt, n_slots)
    yb = _experts(blk_e, n_valid.reshape(1), xs,
                  w_gate_up[0].astype(_BF16), b_gate_up[0][:, None, :],
                  w_down[0].astype(_BF16), b_down[0][:, None, :])
    out = _combine(dest, yb, x1rt, gates[:TOP_K].T, p[0].reshape(n, PLE_DIM),
                   row(ln2_g[0]), row(ln2_b[0]), ple_proj[0].astype(_BF16),
                   ple_gate_w[0].astype(_BF16), row(ple_gate_b[0]), row(ln3_g[0]), row(ln3_b[0]))
    return out.reshape(bsz, seq, d)
```

```python
import functools

import jax
import jax.numpy as jnp
from jax import lax
from jax.experimental import pallas as pl
from jax.experimental.pallas import tpu as pltpu

D_MODEL = 1024
D_POOL = 512
D_CONV = 512
POOL_WINDOWS = (2, 4, 8, 16)
POOL_GROUP_DIM = 128
CONV_WIDTH = 3
D_IN_PROJ = D_POOL + 3 * D_CONV
N_EXPERTS = 32
TOP_K = 4
D_FF = 1024
SWIGLU_LIMIT = 7.0
SWIGLU_ALPHA = 1.702
PLE_DIM = 256
DEPTH = 1
DEEPNORM_ALPHA = (2.0 * DEPTH) ** 0.25
LN_EPS = 1e-5

LANES = 128
SUBLANES = 8
ROW_CHUNKS = D_MODEL // LANES
POOL_HALO = 16
CONV_HALO = 8

TS_MIX = 512
TS_DISPATCH = 512
TM_EXPERT = 512
TS_COMBINE = 256

_BF16 = jnp.bfloat16
_F32 = jnp.float32


def _layer_norm(h, g, b):
    mu = jnp.mean(h, axis=-1, keepdims=True)
    c = h - mu
    var = jnp.mean(c * c, axis=-1, keepdims=True)
    return c * lax.rsqrt(var + LN_EPS) * g + b


def _to_row_tiles(ref, val):
    rows = val.shape[0]
    for c in range(ROW_CHUNKS):
        ref[pl.ds(c, rows, stride=ROW_CHUNKS), :] = val[:, c * LANES:(c + 1) * LANES]


def _from_row_tiles(ref, rows):
    return jnp.concatenate(
        [ref[pl.ds(c, rows, stride=ROW_CHUNKS), :] for c in range(ROW_CHUNKS)], axis=1)


def _mix_route_kernel(x_ref, w_in_ref, pmix_ref, pscale_ref, convw_ref, w_out_ref, g1_ref, b1_ref,
                      rwt_ref, rb_ref, tri_ref,
                      x1rt_ref, ri_ref, gate_ref, cnt_ref,
                      carry_v, carry_u, cnt_sc):
    b = pl.program_id(0)
    s = pl.program_id(1)
    ts = TS_MIX

    @pl.when(jnp.logical_and(b == 0, s == 0))
    def _():
        cnt_sc[...] = jnp.zeros_like(cnt_sc)

    @pl.when(s == 0)
    def _():
        carry_v[...] = jnp.zeros_like(carry_v)
        carry_u[...] = jnp.zeros_like(carry_u)

    xb = x_ref[0]
    proj = jnp.dot(xb.astype(_BF16), w_in_ref[...], preferred_element_type=_F32)
    vp = proj[:, :D_POOL]
    bg = proj[:, D_POOL:D_POOL + D_CONV]
    cg = proj[:, D_POOL + D_CONV:D_POOL + 2 * D_CONV]
    vc = proj[:, D_POOL + 2 * D_CONV:]

    ext = jnp.concatenate([carry_v[...], vp], axis=0)
    pos = lax.broadcasted_iota(jnp.int32, (ts, 1), 0) + s * ts
    mixed = []
    for g, w in enumerate(POOL_WINDOWS):
        lo, hi = g * POOL_GROUP_DIM, (g + 1) * POOL_GROUP_DIM
        acc = ext[:, lo:hi]
        sh = 1
        while sh < w:
            acc = acc + pltpu.roll(acc, sh, 0)
            sh *= 2
        cnt = jnp.minimum(pos + 1, w).astype(_F32)
        d = acc[POOL_HALO:] / cnt - vp[:, lo:hi]
        yg = jnp.dot(d.astype(_BF16), pmix_ref[g], preferred_element_type=_F32)
        mixed.append(yg * pscale_ref[:, lo:hi])
    carry_v[...] = vp[ts - POOL_HALO:]

    u = cg * vc
    extu = jnp.concatenate([carry_u[...], u], axis=0)
    u1 = pltpu.roll(extu, 1, 0)[CONV_HALO:]
    u2 = pltpu.roll(extu, 2, 0)[CONV_HALO:]
    yc = bg * (convw_ref[0:1, :] * u2 + convw_ref[1:2, :] * u1 + convw_ref[2:3, :] * u)
    carry_u[...] = u[ts - CONV_HALO:]
    mixed.append(yc)

    mix_in = jnp.concatenate(mixed, axis=1).astype(_BF16)
    mix = jnp.dot(mix_in, w_out_ref[...], preferred_element_type=_F32)
    x1 = _layer_norm(DEEPNORM_ALPHA * xb + mix, g1_ref[...], b1_ref[...])
    _to_row_tiles(x1rt_ref, x1)

    logits = lax.dot_general(rwt_ref[...], x1.astype(_BF16), (((1,), (1,)), ((), ())),
                             preferred_element_type=_F32) + rb_ref[:, 0:1]
    eidx = lax.broadcasted_iota(jnp.int32, (N_EXPERTS, ts), 0).astype(_F32)
    vals, sels = [], []
    work = logits
    for k in range(TOP_K):
        m = jnp.max(work, axis=0, keepdims=True)
        first = jnp.min(jnp.where(work == m, eidx, float(N_EXPERTS)), axis=0, keepdims=True)
        sel = eidx == first
        work = jnp.where(sel, -jnp.inf, work)
        vals.append(m)
        sels.append(sel)
        ri_ref[k:k + 1, :] = first.astype(jnp.int32)
    exps = [jnp.exp(v - vals[0]) for v in vals]
    denom = exps[0] + exps[1] + exps[2] + exps[3]
    for k in range(TOP_K):
        gate_ref[k:k + 1, :] = exps[k] / denom
    gate_ref[TOP_K:, :] = jnp.zeros((SUBLANES - TOP_K, ts), _F32)

    chosen = jnp.zeros((N_EXPERTS, ts), _F32)
    for sel in sels:
        chosen = chosen + sel.astype(_F32)
    incl = jnp.dot(chosen.astype(_BF16), tri_ref[...], preferred_element_type=_F32)
    base = cnt_sc[:, 0:1] + (incl - chosen)
    for k in range(TOP_K):
        rank = jnp.sum(jnp.where(sels[k], base, 0.0), axis=0, keepdims=True)
        ri_ref[TOP_K + k:TOP_K + k + 1, :] = rank.astype(jnp.int32)
    total = cnt_sc[...] + jnp.sum(chosen, axis=1, keepdims=True)
    cnt_sc[...] = total
    cnt_ref[...] = total.astype(jnp.int32)


def _mix_route(x, w_in, pmix, pscale, convw, w_out, g1, b1, rwt, rb, tri):
    bsz, seq, d = x.shape
    n = bsz * seq
    ns = seq // TS_MIX
    full = lambda shape: pl.BlockSpec(shape, lambda b, s: (0,) * len(shape))
    return pl.pallas_call(
        _mix_route_kernel,
        grid=(bsz, ns),
        in_specs=[
            pl.BlockSpec((1, TS_MIX, d), lambda b, s: (b, s, 0)),
            full(w_in.shape), full(pmix.shape), full(pscale.shape), full(convw.shape),
            full(w_out.shape), full(g1.shape), full(b1.shape), full(rwt.shape), full(rb.shape),
            full(tri.shape),
        ],
        out_specs=[
            pl.BlockSpec((TS_MIX * ROW_CHUNKS, LANES), lambda b, s: (b * ns + s, 0)),
            pl.BlockSpec((2 * TOP_K, TS_MIX), lambda b, s: (0, b * ns + s)),
            pl.BlockSpec((SUBLANES, TS_MIX), lambda b, s: (0, b * ns + s)),
            pl.BlockSpec((N_EXPERTS, LANES), lambda b, s: (0, 0)),
        ],
        out_shape=[
            jax.ShapeDtypeStruct((n * ROW_CHUNKS, LANES), _F32),
            jax.ShapeDtypeStruct((2 * TOP_K, n), jnp.int32),
            jax.ShapeDtypeStruct((SUBLANES, n), _F32),
            jax.ShapeDtypeStruct((N_EXPERTS, LANES), jnp.int32),
        ],
        scratch_shapes=[
            pltpu.VMEM((POOL_HALO, D_POOL), _F32),
            pltpu.VMEM((CONV_HALO, D_CONV), _F32),
            pltpu.VMEM((N_EXPERTS, LANES), _F32),
        ],
        compiler_params=pltpu.CompilerParams(
            dimension_semantics=("arbitrary", "arbitrary"),
            vmem_limit_bytes=48 * 1024 * 1024),
        name="mix_route",
    )(x, w_in, pmix, pscale, convw, w_out, g1, b1, rwt, rb, tri)


def _make_dest_kernel(start_ref, ri_ref, dest_ref):
    ei = ri_ref[0:TOP_K, :]
    acc = ri_ref[TOP_K:, :]
    for e in range(N_EXPERTS):
        acc = acc + jnp.where(ei == e, start_ref[e], 0)
    dest_ref[...] = acc


def _make_dest(pad_start, ri):
    n = ri.shape[1]
    return pl.pallas_call(
        _make_dest_kernel,
        in_specs=[pl.BlockSpec(memory_space=pltpu.SMEM),
                  pl.BlockSpec((2 * TOP_K, n), lambda: (0, 0))],
        out_specs=pl.BlockSpec((TOP_K, n), lambda: (0, 0)),
        out_shape=jax.ShapeDtypeStruct((TOP_K, n), jnp.int32),
        name="make_dest",
    )(pad_start, ri)


def _dispatch_kernel(dest_ref, src_ref, init_ref, out_ref, sem):
    del init_ref
    i = pl.program_id(0)
    ts = TS_DISPATCH

    del i

    def row_copy(src_row, dst_row):
        return pltpu.make_async_copy(
            src_ref.at[pl.ds(pl.multiple_of(src_row * ROW_CHUNKS, ROW_CHUNKS), ROW_CHUNKS)],
            out_ref.at[pl.ds(pl.multiple_of(dst_row * ROW_CHUNKS, ROW_CHUNKS), ROW_CHUNKS)],
            sem)

    def body(t, carry):
        for k in range(TOP_K):
            row_copy(t, dest_ref[k, t]).start()
        return carry

    lax.fori_loop(0, ts, body, 0, unroll=8)
    for k in range(TOP_K):
        pltpu.make_async_copy(src_ref, out_ref.at[pl.ds(0, ts * ROW_CHUNKS)], sem).wait()


def _dispatch(dest, x1rt, n_slots):
    n = dest.shape[1]
    init = jnp.zeros((n_slots * ROW_CHUNKS, LANES), _F32)
    return pl.pallas_call(
        _dispatch_kernel,
        grid=(n // TS_DISPATCH,),
        in_specs=[
            pl.BlockSpec((TOP_K, TS_DISPATCH), lambda i: (0, i), memory_space=pltpu.SMEM),
            pl.BlockSpec((TS_DISPATCH * ROW_CHUNKS, LANES), lambda i: (i, 0)),
            pl.BlockSpec(memory_space=pl.ANY),
        ],
        out_specs=pl.BlockSpec(memory_space=pl.ANY),
        out_shape=jax.ShapeDtypeStruct(init.shape, _F32),
        scratch_shapes=[pltpu.SemaphoreType.DMA(())],
        input_output_aliases={2: 0},
        compiler_params=pltpu.CompilerParams(dimension_semantics=("arbitrary",)),
        name="dispatch",
    )(dest, x1rt, init)


def _experts_kernel(be_ref, nv_ref, xs_ref, wgu_ref, bgu_ref, wdn_ref, bdn_ref, out_ref):
    del be_ref
    j = pl.program_id(0)
    tm = TM_EXPERT

    @pl.when(j < nv_ref[0])
    def _():
        x = _from_row_tiles(xs_ref, tm).astype(_BF16)
        gu = jnp.dot(x, wgu_ref[0], preferred_element_type=_F32) + bgu_ref[0]
        gate = jnp.minimum(gu[:, :D_FF], SWIGLU_LIMIT)
        up = jnp.clip(gu[:, D_FF:], -SWIGLU_LIMIT, SWIGLU_LIMIT)
        glu = gate * jax.nn.sigmoid(SWIGLU_ALPHA * gate)
        act = ((up + 1.0) * glu).astype(_BF16)
        y = jnp.dot(act, wdn_ref[0], preferred_element_type=_F32) + bdn_ref[0]
        _to_row_tiles(out_ref, y)

    @pl.when(j >= nv_ref[0])
    def _():
        out_ref[...] = jnp.zeros_like(out_ref)


def _experts(blk_e, n_valid, xs, wgu, bgu, wdn, bdn):
    n_slots = xs.shape[0] // ROW_CHUNKS
    nb = n_slots // TM_EXPERT
    rows = TM_EXPERT * ROW_CHUNKS
    grid_spec = pltpu.PrefetchScalarGridSpec(
        num_scalar_prefetch=2,
        grid=(nb,),
        in_specs=[
            pl.BlockSpec((rows, LANES), lambda j, be, nv: (jnp.minimum(j, nv[0] - 1), 0)),
            pl.BlockSpec((1, D_MODEL, 2 * D_FF), lambda j, be, nv: (be[j], 0, 0)),
            pl.BlockSpec((1, 1, 2 * D_FF), lambda j, be, nv: (be[j], 0, 0)),
            pl.BlockSpec((1, D_FF, D_MODEL), lambda j, be, nv: (be[j], 0, 0)),
            pl.BlockSpec((1, 1, D_MODEL), lambda j, be, nv: (be[j], 0, 0)),
        ],
        out_specs=pl.BlockSpec((rows, LANES), lambda j, be, nv: (j, 0)),
    )
    return pl.pallas_call(
        _experts_kernel,
        grid_spec=grid_spec,
        out_shape=jax.ShapeDtypeStruct(xs.shape, _F32),
        compiler_params=pltpu.CompilerParams(
            dimension_semantics=("arbitrary",),
            vmem_limit_bytes=56 * 1024 * 1024),
        name="experts",
    )(blk_e, n_valid, xs, wgu, bgu, wdn, bdn)


def _combine_kernel(dest_ref, yb_ref, x1rt_ref, gt_ref, p_ref, g2_ref, b2_ref, pw_ref, gw_ref, gb_ref,
                    g3_ref, b3_ref, out_ref, gbuf, sem):
    ts = TS_COMBINE

    def row_copy(k, t, src_row):
        return pltpu.make_async_copy(
            yb_ref.at[pl.ds(pl.multiple_of(src_row * ROW_CHUNKS, ROW_CHUNKS), ROW_CHUNKS)],
            gbuf.at[k, pl.ds(pl.multiple_of(t * ROW_CHUNKS, ROW_CHUNKS), ROW_CHUNKS)],
            sem)

    def body(t, carry):
        for k in range(TOP_K):
            row_copy(k, t, dest_ref[k, t]).start()
        return carry

    lax.fori_loop(0, ts, body, 0, unroll=8)
    for k in range(TOP_K):
        pltpu.make_async_copy(yb_ref.at[pl.ds(0, ts * ROW_CHUNKS)], gbuf.at[k], sem).wait()

    x1 = _from_row_tiles(x1rt_ref, ts)
    ffn = jnp.zeros((ts, D_MODEL), _F32)
    for k in range(TOP_K):
        ffn = ffn + gt_ref[:, k:k + 1] * _from_row_tiles(gbuf.at[k], ts)
    x2 = _layer_norm(DEEPNORM_ALPHA * x1 + ffn, g2_ref[...], b2_ref[...])
    z = jnp.dot(x2.astype(_BF16), gw_ref[...], preferred_element_type=_F32) + gb_ref[...]
    ple = jnp.dot(p_ref[...].astype(_BF16), pw_ref[...], preferred_element_type=_F32)
    x3 = _layer_norm(DEEPNORM_ALPHA * x2 + jax.nn.sigmoid(z) * ple, g3_ref[...], b3_ref[...])
    out_ref[...] = x3


def _combine(dest, yb, x1rt, gates_t, p2d, g2, b2, pw, gw, gb, g3, b3):
    n = dest.shape[1]
    ts = TS_COMBINE
    full = lambda shape: pl.BlockSpec(shape, lambda i: (0,) * len(shape))
    return pl.pallas_call(
        _combine_kernel,
        grid=(n // ts,),
        in_specs=[
            pl.BlockSpec((TOP_K, ts), lambda i: (0, i), memory_space=pltpu.SMEM),
            pl.BlockSpec(memory_space=pl.ANY),
            pl.BlockSpec((ts * ROW_CHUNKS, LANES), lambda i: (i, 0)),
            pl.BlockSpec((ts, TOP_K), lambda i: (i, 0)),
            pl.BlockSpec((ts, PLE_DIM), lambda i: (i, 0)),
            full(g2.shape), full(b2.shape), full(pw.shape), full(gw.shape), full(gb.shape),
            full(g3.shape), full(b3.shape),
        ],
        out_specs=pl.BlockSpec((ts, D_MODEL), lambda i: (i, 0)),
        out_shape=jax.ShapeDtypeStruct((n, D_MODEL), _F32),
        scratch_shapes=[
            pltpu.VMEM((TOP_K, ts * ROW_CHUNKS, LANES), _F32),
            pltpu.SemaphoreType.DMA(()),
        ],
        compiler_params=pltpu.CompilerParams(
            dimension_semantics=("arbitrary",),
            vmem_limit_bytes=48 * 1024 * 1024),
        name="combine",
    )(dest, yb, x1rt, gates_t, p2d, g2, b2, pw, gw, gb, g3, b3)


def kernel(x, p, w_in, pool_mix, pool_scale, conv_w, w_out, ln1_g, ln1_b, router_w, router_b,
           w_gate_up, b_gate_up, w_down, b_down, ln2_g, ln2_b, ple_proj, ple_gate_w, ple_gate_b,
           ln3_g, ln3_b):
    assert DEPTH == 1 and x.shape[-1] == D_MODEL
    bsz, seq, d = x.shape
    n = bsz * seq
    assert seq % TS_MIX == 0 and n % TS_DISPATCH == 0 and n % TS_COMBINE == 0
    row = lambda v: v.reshape(1, -1)

    tri = jnp.triu(jnp.ones((TS_MIX, TS_MIX), _BF16))
    x1rt, ri, gates, counts = _mix_route(
        x, w_in[0].astype(_BF16), pool_mix[0].astype(_BF16), row(pool_scale[0]), conv_w[0],
        w_out[0].astype(_BF16), row(ln1_g[0]), row(ln1_b[0]),
        router_w[0].T.astype(_BF16), jnp.broadcast_to(router_b[0][:, None], (N_EXPERTS, LANES)), tri)

    tm = TM_EXPERT
    n_slots = n * TOP_K + N_EXPERTS * tm
    nb = n_slots // tm
    cnt = counts[:, 0]
    padded = ((cnt + tm - 1) // tm) * tm
    pad_end = jnp.cumsum(padded)
    pad_start = (pad_end - padded).astype(jnp.int32)
    n_valid = (pad_end[-1] // tm).astype(jnp.int32)
    tile_start = jnp.arange(nb, dtype=jnp.int32) * tm
    blk_e = jnp.minimum(jnp.sum(tile_start[:, None] >= pad_end[None, :], axis=1), N_EXPERTS - 1)
    blk_e = jnp.where(jnp.arange(nb) < n_valid, blk_e, blk_e[n_valid - 1]).astype(jnp.int32)

    dest = _make_dest(pad_start, ri)
    xs = _dispatch(dest, x1rt, n_slots)
    yb = _experts(blk_e, n_valid.reshape(1), xs,
                  w_gate_up[0].astype(_BF16), b_gate_up[0][:, None, :],
                  w_down[0].astype(_BF16), b_down[0][:, None, :])
    out = _combine(dest, yb, x1rt, gates[:TOP_K].T, p[0].reshape(n, PLE_DIM),
                   row(ln2_g[0]), row(ln2_b[0]), ple_proj[0].astype(_BF16),
                   ple_gate_w[0].astype(_BF16), row(ple_gate_b[0]), row(ln3_g[0]), row(ln3_b[0]))
    return out.reshape(bsz, seq, d)
```

```python
import functools

import jax
import jax.numpy as jnp
from jax import lax
from jax.experimental import pallas as pl
from jax.experimental.pallas import tpu as pltpu

D_MODEL = 1024
D_POOL = 512
D_CONV = 512
POOL_WINDOWS = (2, 4, 8, 16)
POOL_GROUP_DIM = 128
CONV_WIDTH = 3
D_IN_PROJ = D_POOL + 3 * D_CONV
N_EXPERTS = 32
TOP_K = 4
D_FF = 1024
SWIGLU_LIMIT = 7.0
SWIGLU_ALPHA = 1.702
PLE_DIM = 256
DEPTH = 1
DEEPNORM_ALPHA = (2.0 * DEPTH) ** 0.25
LN_EPS = 1e-5

LANES = 128
SUBLANES = 8
ROW_CHUNKS = D_MODEL // LANES
POOL_HALO = 16
CONV_HALO = 8

TS_MIX = 512
TS_DISPATCH = 512
TM_EXPERT = 512
TS_COMBINE = 256

_BF16 = jnp.bfloat16
_F32 = jnp.float32


def _layer_norm(h, g, b):
    mu = jnp.mean(h, axis=-1, keepdims=True)
    c = h - mu
    var = jnp.mean(c * c, axis=-1, keepdims=True)
    return c * lax.rsqrt(var + LN_EPS) * g + b


def _to_row_tiles(ref, val):
    rows = val.shape[0]
    for c in range(ROW_CHUNKS):
        ref[pl.ds(c, rows, stride=ROW_CHUNKS), :] = val[:, c * LANES:(c + 1) * LANES]


def _from_row_tiles(ref, rows):
    return jnp.concatenate(
        [ref[pl.ds(c, rows, stride=ROW_CHUNKS), :] for c in range(ROW_CHUNKS)], axis=1)


def _mix_route_kernel(x_ref, w_in_ref, pmix_ref, pscale_ref, convw_ref, w_out_ref, g1_ref, b1_ref,
                      rwt_ref, rb_ref, tri_ref,
                      x1rt_ref, ri_ref, gate_ref, cnt_ref,
                      carry_v, carry_u, cnt_sc):
    b = pl.program_id(0)
    s = pl.program_id(1)
    ts = TS_MIX

    @pl.when(jnp.logical_and(b == 0, s == 0))
    def _():
        cnt_sc[...] = jnp.zeros_like(cnt_sc)

    @pl.when(s == 0)
    def _():
        carry_v[...] = jnp.zeros_like(carry_v)
        carry_u[...] = jnp.zeros_like(carry_u)

    xb = x_ref[0]
    proj = jnp.dot(xb.astype(_BF16), w_in_ref[...], preferred_element_type=_F32)
    vp = proj[:, :D_POOL]
    bg = proj[:, D_POOL:D_POOL + D_CONV]
    cg = proj[:, D_POOL + D_CONV:D_POOL + 2 * D_CONV]
    vc = proj[:, D_POOL + 2 * D_CONV:]

    ext = jnp.concatenate([carry_v[...], vp], axis=0)
    pos = lax.broadcasted_iota(jnp.int32, (ts, 1), 0) + s * ts
    mixed = []
    for g, w in enumerate(POOL_WINDOWS):
        lo, hi = g * POOL_GROUP_DIM, (g + 1) * POOL_GROUP_DIM
        acc = ext[:, lo:hi]
        sh = 1
        while sh < w:
            acc = acc + pltpu.roll(acc, sh, 0)
            sh *= 2
        cnt = jnp.minimum(pos + 1, w).astype(_F32)
        d = acc[POOL_HALO:] / cnt - vp[:, lo:hi]
        yg = jnp.dot(d.astype(_BF16), pmix_ref[g], preferred_element_type=_F32)
        mixed.append(yg * pscale_ref[:, lo:hi])
    carry_v[...] = vp[ts - POOL_HALO:]

    u = cg * vc
    extu = jnp.concatenate([carry_u[...], u], axis=0)
    u1 = pltpu.roll(extu, 1, 0)[CONV_HALO:]
    u2 = pltpu.roll(extu, 2, 0)[CONV_HALO:]
    yc = bg * (convw_ref[0:1, :] * u2 + convw_ref[1:2, :] * u1 + convw_ref[2:3, :] * u)
    carry_u[...] = u[ts - CONV_HALO:]
    mixed.append(yc)

    mix_in = jnp.concatenate(mixed, axis=1).astype(_BF16)
    mix = jnp.dot(mix_in, w_out_ref[...], preferred_element_type=_F32)
    x1 = _layer_norm(DEEPNORM_ALPHA * xb + mix, g1_ref[...], b1_ref[...])
    _to_row_tiles(x1rt_ref, x1)

    logits = lax.dot_general(rwt_ref[...], x1.astype(_BF16), (((1,), (1,)), ((), ())),
                             preferred_element_type=_F32) + rb_ref[:, 0:1]
    eidx = lax.broadcasted_iota(jnp.int32, (N_EXPERTS, ts), 0).astype(_F32)
    vals, sels = [], []
    work = logits
    for k in range(TOP_K):
        m = jnp.max(work, axis=0, keepdims=True)
        first = jnp.min(jnp.where(work == m, eidx, float(N_EXPERTS)), axis=0, keepdims=True)
        sel = eidx == first
        work = jnp.where(sel, -jnp.inf, work)
        vals.append(m)
        sels.append(sel)
        ri_ref[k:k + 1, :] = first.astype(jnp.int32)
    exps = [jnp.exp(v - vals[0]) for v in vals]
    denom = exps[0] + exps[1] + exps[2] + exps[3]
    for k in range(TOP_K):
        gate_ref[k:k + 1, :] = exps[k] / denom
    gate_ref[TOP_K:, :] = jnp.zeros((SUBLANES - TOP_K, ts), _F32)

    chosen = jnp.zeros((N_EXPERTS, ts), _F32)
    for sel in sels:
        chosen = chosen + sel.astype(_F32)
    incl = jnp.dot(chosen.astype(_BF16), tri_ref[...], preferred_element_type=_F32)
    base = cnt_sc[:, 0:1] + (incl - chosen)
    for k in range(TOP_K):
        rank = jnp.sum(jnp.where(sels[k], base, 0.0), axis=0, keepdims=True)
        ri_ref[TOP_K + k:TOP_K + k + 1, :] = rank.astype(jnp.int32)
    total = cnt_sc[...] + jnp.sum(chosen, axis=1, keepdims=True)
    cnt_sc[...] = total
    cnt_ref[...] = total.astype(jnp.int32)


def _mix_route(x, w_in, pmix, pscale, convw, w_out, g1, b1, rwt, rb, tri):
    bsz, seq, d = x.shape
    n = bsz * seq
    ns = seq // TS_MIX
    full = lambda shape: pl.BlockSpec(shape, lambda b, s: (0,) * len(shape))
    return pl.pallas_call(
        _mix_route_kernel,
        grid=(bsz, ns),
        in_specs=[
            pl.BlockSpec((1, TS_MIX, d), lambda b, s: (b, s, 0)),
            full(w_in.shape), full(pmix.shape), full(pscale.shape), full(convw.shape),
            full(w_out.shape), full(g1.shape), full(b1.shape), full(rwt.shape), full(rb.shape),
            full(tri.shape),
        ],
        out_specs=[
            pl.BlockSpec((TS_MIX * ROW_CHUNKS, LANES), lambda b, s: (b * ns + s, 0)),
            pl.BlockSpec((2 * TOP_K, TS_MIX), lambda b, s: (0, b * ns + s)),
            pl.BlockSpec((SUBLANES, TS_MIX), lambda b, s: (0, b * ns + s)),
            pl.BlockSpec((N_EXPERTS, LANES), lambda b, s: (0, 0)),
        ],
        out_shape=[
            jax.ShapeDtypeStruct((n * ROW_CHUNKS, LANES), _F32),
            jax.ShapeDtypeStruct((2 * TOP_K, n), jnp.int32),
            jax.ShapeDtypeStruct((SUBLANES, n), _F32),
            jax.ShapeDtypeStruct((N_EXPERTS, LANES), jnp.int32),
        ],
        scratch_shapes=[
            pltpu.VMEM((POOL_HALO, D_POOL), _F32),
            pltpu.VMEM((CONV_HALO, D_CONV), _F32),
            pltpu.VMEM((N_EXPERTS, LANES), _F32),
        ],
        compiler_params=pltpu.CompilerParams(
            dimension_semantics=("arbitrary", "arbitrary"),
            vmem_limit_bytes=48 * 1024 * 1024),
        name="mix_route",
    )(x, w_in, pmix, pscale, convw, w_out, g1, b1, rwt, rb, tri)


def _make_dest_kernel(start_ref, ri_ref, dest_ref):
    ei = ri_ref[0:TOP_K, :]
    acc = ri_ref[TOP_K:, :]
    for e in range(N_EXPERTS):
        acc = acc + jnp.where(ei == e, start_ref[e], 0)
    dest_ref[...] = acc


def _make_dest(pad_start, ri):
    n = ri.shape[1]
    return pl.pallas_call(
        _make_dest_kernel,
        in_specs=[pl.BlockSpec(memory_space=pltpu.SMEM),
                  pl.BlockSpec((2 * TOP_K, n), lambda: (0, 0))],
        out_specs=pl.BlockSpec((TOP_K, n), lambda: (0, 0)),
        out_shape=jax.ShapeDtypeStruct((TOP_K, n), jnp.int32),
        name="make_dest",
    )(pad_start, ri)


def _dispatch_kernel(dest_ref, src_ref, init_ref, out_ref, sem):
    del init_ref
    ts = TS_DISPATCH

    def row_copy(src_row, dst_row):
        return pltpu.make_async_copy(
            src_ref.at[pl.ds(pl.multiple_of(src_row * ROW_CHUNKS, ROW_CHUNKS), ROW_CHUNKS)],
            out_ref.at[pl.ds(pl.multiple_of(dst_row * ROW_CHUNKS, ROW_CHUNKS), ROW_CHUNKS)],
            sem)

    def body(t, carry):
        for k in range(TOP_K):
            row_copy(t, dest_ref[t * TOP_K + k]).start(priority=k % 2)
        return carry

    lax.fori_loop(0, ts, body, 0, unroll=8)
    for k in range(TOP_K):
        pltpu.make_async_copy(src_ref, out_ref.at[pl.ds(0, ts * ROW_CHUNKS)], sem).wait()


def _dispatch(dest_flat, x1rt, n_slots):
    n = dest_flat.shape[0] // TOP_K
    init = jnp.zeros((n_slots * ROW_CHUNKS, LANES), _F32)
    return pl.pallas_call(
        _dispatch_kernel,
        grid=(n // TS_DISPATCH,),
        in_specs=[
            pl.BlockSpec((TOP_K * TS_DISPATCH,), lambda i: (i,), memory_space=pltpu.SMEM),
            pl.BlockSpec((TS_DISPATCH * ROW_CHUNKS, LANES), lambda i: (i, 0)),
            pl.BlockSpec(memory_space=pl.ANY),
        ],
        out_specs=pl.BlockSpec(memory_space=pl.ANY),
        out_shape=jax.ShapeDtypeStruct(init.shape, _F32),
        scratch_shapes=[pltpu.SemaphoreType.DMA(())],
        input_output_aliases={2: 0},
        compiler_params=pltpu.CompilerParams(dimension_semantics=("arbitrary",)),
        name="dispatch",
    )(dest_flat, x1rt, init)


def _experts_kernel(be_ref, nv_ref, xs_ref, wgu_ref, bgu_ref, wdn_ref, bdn_ref, out_ref):
    del be_ref
    j = pl.program_id(0)
    tm = TM_EXPERT

    @pl.when(j < nv_ref[0])
    def _():
        x = _from_row_tiles(xs_ref, tm).astype(_BF16)
        gu = jnp.dot(x, wgu_ref[0], preferred_element_type=_F32) + bgu_ref[0]
        gate = jnp.minimum(gu[:, :D_FF], SWIGLU_LIMIT)
        up = jnp.clip(gu[:, D_FF:], -SWIGLU_LIMIT, SWIGLU_LIMIT)
        glu = gate * jax.nn.sigmoid(SWIGLU_ALPHA * gate)
        act = ((up + 1.0) * glu).astype(_BF16)
        y = jnp.dot(act, wdn_ref[0], preferred_element_type=_F32) + bdn_ref[0]
        _to_row_tiles(out_ref, y)

    @pl.when(j >= nv_ref[0])
    def _():
        out_ref[...] = jnp.zeros_like(out_ref)


def _experts(blk_e, n_valid, xs, wgu, bgu, wdn, bdn):
    n_slots = xs.shape[0] // ROW_CHUNKS
    nb = n_slots // TM_EXPERT
    rows = TM_EXPERT * ROW_CHUNKS
    grid_spec = pltpu.PrefetchScalarGridSpec(
        num_scalar_prefetch=2,
        grid=(nb,),
        in_specs=[
            pl.BlockSpec((rows, LANES), lambda j, be, nv: (jnp.minimum(j, nv[0] - 1), 0)),
            pl.BlockSpec((1, D_MODEL, 2 * D_FF), lambda j, be, nv: (be[j], 0, 0)),
            pl.BlockSpec((1, 1, 2 * D_FF), lambda j, be, nv: (be[j], 0, 0)),
            pl.BlockSpec((1, D_FF, D_MODEL), lambda j, be, nv: (be[j], 0, 0)),
            pl.BlockSpec((1, 1, D_MODEL), lambda j, be, nv: (be[j], 0, 0)),
        ],
        out_specs=pl.BlockSpec((rows, LANES), lambda j, be, nv: (j, 0)),
    )
    return pl.pallas_call(
        _experts_kernel,
        grid_spec=grid_spec,
        out_shape=jax.ShapeDtypeStruct(xs.shape, _F32),
        compiler_params=pltpu.CompilerParams(
            dimension_semantics=("arbitrary",),
            vmem_limit_bytes=56 * 1024 * 1024),
        name="experts",
    )(blk_e, n_valid, xs, wgu, bgu, wdn, bdn)


def _combine_kernel(dcur_ref, dnext_ref, yb_ref, x1rt_ref, gt_ref, p_ref, g2_ref, b2_ref, pw_ref, gw_ref,
                    gb_ref, g3_ref, b3_ref, out_ref, gbuf, sems):
    ts = TS_COMBINE
    i = pl.program_id(0)
    slot = i % 2

    def start_gather(d_ref, s):
        def body(t, carry):
            for k in range(TOP_K):
                src_row = d_ref[t * TOP_K + k]
                pltpu.make_async_copy(
                    yb_ref.at[pl.ds(pl.multiple_of(src_row * ROW_CHUNKS, ROW_CHUNKS), ROW_CHUNKS)],
                    gbuf.at[s, k, pl.ds(pl.multiple_of(t * ROW_CHUNKS, ROW_CHUNKS), ROW_CHUNKS)],
                    sems.at[s]).start(priority=k % 2)
            return carry
        lax.fori_loop(0, ts, body, 0, unroll=8)

    @pl.when(i == 0)
    def _():
        start_gather(dcur_ref, 0)

    @pl.when(i + 1 < pl.num_programs(0))
    def _():
        start_gather(dnext_ref, 1 - slot)

    for k in range(TOP_K):
        pltpu.make_async_copy(yb_ref.at[pl.ds(0, ts * ROW_CHUNKS)], gbuf.at[slot, k], sems.at[slot]).wait()

    x1 = _from_row_tiles(x1rt_ref, ts)
    ffn = jnp.zeros((ts, D_MODEL), _F32)
    for k in range(TOP_K):
        ffn = ffn + gt_ref[:, k:k + 1] * _from_row_tiles(gbuf.at[slot, k], ts)
    x2 = _layer_norm(DEEPNORM_ALPHA * x1 + ffn, g2_ref[...], b2_ref[...])
    z = jnp.dot(x2.astype(_BF16), gw_ref[...], preferred_element_type=_F32) + gb_ref[...]
    ple = jnp.dot(p_ref[...].astype(_BF16), pw_ref[...], preferred_element_type=_F32)
    x3 = _layer_norm(DEEPNORM_ALPHA * x2 + jax.nn.sigmoid(z) * ple, g3_ref[...], b3_ref[...])
    out_ref[...] = x3


def _combine(dest_flat, yb, x1rt, gates_t, p2d, g2, b2, pw, gw, gb, g3, b3):
    n = dest_flat.shape[0] // TOP_K
    ts = TS_COMBINE
    last = n // ts - 1
    full = lambda shape: pl.BlockSpec(shape, lambda i: (0,) * len(shape))
    return pl.pallas_call(
        _combine_kernel,
        grid=(n // ts,),
        in_specs=[
            pl.BlockSpec((TOP_K * ts,), lambda i: (i,), memory_space=pltpu.SMEM),
            pl.BlockSpec((TOP_K * ts,), lambda i: (jnp.minimum(i + 1, last),), memory_space=pltpu.SMEM),
            pl.BlockSpec(memory_space=pl.ANY),
            pl.BlockSpec((ts * ROW_CHUNKS, LANES), lambda i: (i, 0)),
            pl.BlockSpec((ts, TOP_K), lambda i: (i, 0)),
            pl.BlockSpec((ts, PLE_DIM), lambda i: (i, 0)),
            full(g2.shape), full(b2.shape), full(pw.shape), full(gw.shape), full(gb.shape),
            full(g3.shape), full(b3.shape),
        ],
        out_specs=pl.BlockSpec((ts, D_MODEL), lambda i: (i, 0)),
        out_shape=jax.ShapeDtypeStruct((n, D_MODEL), _F32),
        scratch_shapes=[
            pltpu.VMEM((2, TOP_K, ts * ROW_CHUNKS, LANES), _F32),
            pltpu.SemaphoreType.DMA((2,)),
        ],
        compiler_params=pltpu.CompilerParams(
            dimension_semantics=("arbitrary",),
            vmem_limit_bytes=48 * 1024 * 1024),
        name="combine",
    )(dest_flat, dest_flat, yb, x1rt, gates_t, p2d, g2, b2, pw, gw, gb, g3, b3)


def kernel(x, p, w_in, pool_mix, pool_scale, conv_w, w_out, ln1_g, ln1_b, router_w, router_b,
           w_gate_up, b_gate_up, w_down, b_down, ln2_g, ln2_b, ple_proj, ple_gate_w, ple_gate_b,
           ln3_g, ln3_b):
    assert DEPTH == 1 and x.shape[-1] == D_MODEL
    bsz, seq, d = x.shape
    n = bsz * seq
    assert seq % TS_MIX == 0 and n % TS_DISPATCH == 0 and n % TS_COMBINE == 0
    row = lambda v: v.reshape(1, -1)

    tri = jnp.triu(jnp.ones((TS_MIX, TS_MIX), _BF16))
    x1rt, ri, gates, counts = _mix_route(
        x, w_in[0].astype(_BF16), pool_mix[0].astype(_BF16), row(pool_scale[0]), conv_w[0],
        w_out[0].astype(_BF16), row(ln1_g[0]), row(ln1_b[0]),
        router_w[0].T.astype(_BF16), jnp.broadcast_to(router_b[0][:, None], (N_EXPERTS, LANES)), tri)

    tm = TM_EXPERT
    n_slots = n * TOP_K + N_EXPERTS * tm
    nb = n_slots // tm
    cnt = counts[:, 0]
    padded = ((cnt + tm - 1) // tm) * tm
    pad_end = jnp.cumsum(padded)
    pad_start = (pad_end - padded).astype(jnp.int32)
    n_valid = (pad_end[-1] // tm).astype(jnp.int32)
    tile_start = jnp.arange(nb, dtype=jnp.int32) * tm
    blk_e = jnp.minimum(jnp.sum(tile_start[:, None] >= pad_end[None, :], axis=1), N_EXPERTS - 1)
    blk_e = jnp.where(jnp.arange(nb) < n_valid, blk_e, blk_e[n_valid - 1]).astype(jnp.int32)

    dest = _make_dest(pad_start, ri).T.reshape(-1)
    xs = _dispatch(dest, x1rt, n_slots)
    yb = _experts(blk_e, n_valid.reshape(1), xs,
                  w_gate_up[0].astype(_BF16), b_gate_up[0][:, None, :],
                  w_down[0].astype(_BF16), b_down[0][:, None, :])
    out = _combine(dest, yb, x1rt, gates[:TOP_K].T, p[0].reshape(n, PLE_DIM),
                   row(ln2_g[0]), row(ln2_b[0]), ple_proj[0].astype(_BF16),
                   ple_gate_w[0].astype(_BF16), row(ple_gate_b[0]), row(ln3_g[0]), row(ln3_b[0]))
    return out.reshape(bsz, seq, d)
```

```python
import functools

import jax
import jax.numpy as jnp
from jax import lax
from jax.experimental import pallas as pl
from jax.experimental.pallas import tpu as pltpu

D_MODEL = 1024
D_POOL = 512
D_CONV = 512
POOL_WINDOWS = (2, 4, 8, 16)
POOL_GROUP_DIM = 128
CONV_WIDTH = 3
D_IN_PROJ = D_POOL + 3 * D_CONV
N_EXPERTS = 32
TOP_K = 4
D_FF = 1024
SWIGLU_LIMIT = 7.0
SWIGLU_ALPHA = 1.702
PLE_DIM = 256
DEPTH = 1
DEEPNORM_ALPHA = (2.0 * DEPTH) ** 0.25
LN_EPS = 1e-5

LANES = 128
SUBLANES = 8
ROW_CHUNKS = D_MODEL // LANES
POOL_HALO = 16
CONV_HALO = 8

TS_MIX = 512
TS_DISPATCH = 512
TM_EXPERT = 512
TS_COMBINE = 256

_BF16 = jnp.bfloat16
_F32 = jnp.float32


def _layer_norm(h, g, b):
    mu = jnp.mean(h, axis=-1, keepdims=True)
    c = h - mu
    var = jnp.mean(c * c, axis=-1, keepdims=True)
    return c * lax.rsqrt(var + LN_EPS) * g + b


def _to_row_tiles(ref, val):
    rows = val.shape[0]
    for c in range(ROW_CHUNKS):
        ref[pl.ds(c, rows, stride=ROW_CHUNKS), :] = val[:, c * LANES:(c + 1) * LANES]


def _from_row_tiles(ref, rows):
    return jnp.concatenate(
        [ref[pl.ds(c, rows, stride=ROW_CHUNKS), :] for c in range(ROW_CHUNKS)], axis=1)


def _mix_route_kernel(x_ref, w_in_ref, pmix_ref, pscale_ref, convw_ref, w_out_ref, g1_ref, b1_ref,
                      rwt_ref, rb_ref, tri_ref,
                      x1rt_ref, ri_ref, gate_ref, cnt_ref,
                      carry_v, carry_u, cnt_sc):
    b = pl.program_id(0)
    s = pl.program_id(1)
    ts = TS_MIX

    @pl.when(jnp.logical_and(b == 0, s == 0))
    def _():
        cnt_sc[...] = jnp.zeros_like(cnt_sc)

    @pl.when(s == 0)
    def _():
        carry_v[...] = jnp.zeros_like(carry_v)
        carry_u[...] = jnp.zeros_like(carry_u)

    xb = x_ref[0]
    proj = jnp.dot(xb.astype(_BF16), w_in_ref[...], preferred_element_type=_F32)
    vp = proj[:, :D_POOL]
    bg = proj[:, D_POOL:D_POOL + D_CONV]
    cg = proj[:, D_POOL + D_CONV:D_POOL + 2 * D_CONV]
    vc = proj[:, D_POOL + 2 * D_CONV:]

    ext = jnp.concatenate([carry_v[...], vp], axis=0)
    pos = lax.broadcasted_iota(jnp.int32, (ts, 1), 0) + s * ts
    mixed = []
    for g, w in enumerate(POOL_WINDOWS):
        lo, hi = g * POOL_GROUP_DIM, (g + 1) * POOL_GROUP_DIM
        acc = ext[:, lo:hi]
        sh = 1
        while sh < w:
            acc = acc + pltpu.roll(acc, sh, 0)
            sh *= 2
        cnt = jnp.minimum(pos + 1, w).astype(_F32)
        d = acc[POOL_HALO:] / cnt - vp[:, lo:hi]
        yg = jnp.dot(d.astype(_BF16), pmix_ref[g], preferred_element_type=_F32)
        mixed.append(yg * pscale_ref[:, lo:hi])
    carry_v[...] = vp[ts - POOL_HALO:]

    u = cg * vc
    extu = jnp.concatenate([carry_u[...], u], axis=0)
    u1 = pltpu.roll(extu, 1, 0)[CONV_HALO:]
    u2 = pltpu.roll(extu, 2, 0)[CONV_HALO:]
    yc = bg * (convw_ref[0:1, :] * u2 + convw_ref[1:2, :] * u1 + convw_ref[2:3, :] * u)
    carry_u[...] = u[ts - CONV_HALO:]
    mixed.append(yc)

    mix_in = jnp.concatenate(mixed, axis=1).astype(_BF16)
    mix = jnp.dot(mix_in, w_out_ref[...], preferred_element_type=_F32)
    x1 = _layer_norm(DEEPNORM_ALPHA * xb + mix, g1_ref[...], b1_ref[...])
    _to_row_tiles(x1rt_ref, x1)

    logits = lax.dot_general(rwt_ref[...], x1.astype(_BF16), (((1,), (1,)), ((), ())),
                             preferred_element_type=_F32) + rb_ref[:, 0:1]
    eidx = lax.broadcasted_iota(jnp.int32, (N_EXPERTS, ts), 0).astype(_F32)
    vals, sels = [], []
    work = logits
    for k in range(TOP_K):
        m = jnp.max(work, axis=0, keepdims=True)
        first = jnp.min(jnp.where(work == m, eidx, float(N_EXPERTS)), axis=0, keepdims=True)
        sel = eidx == first
        work = jnp.where(sel, -jnp.inf, work)
        vals.append(m)
        sels.append(sel)
        ri_ref[k:k + 1, :] = first.astype(jnp.int32)
    exps = [jnp.exp(v - vals[0]) for v in vals]
    denom = exps[0] + exps[1] + exps[2] + exps[3]
    for k in range(TOP_K):
        gate_ref[k:k + 1, :] = exps[k] / denom
    gate_ref[TOP_K:, :] = jnp.zeros((SUBLANES - TOP_K, ts), _F32)

    chosen = jnp.zeros((N_EXPERTS, ts), _F32)
    for sel in sels:
        chosen = chosen + sel.astype(_F32)
    incl = jnp.dot(chosen.astype(_BF16), tri_ref[...], preferred_element_type=_F32)
    base = cnt_sc[:, 0:1] + (incl - chosen)
    for k in range(TOP_K):
        rank = jnp.sum(jnp.where(sels[k], base, 0.0), axis=0, keepdims=True)
        ri_ref[TOP_K + k:TOP_K + k + 1, :] = rank.astype(jnp.int32)
    total = cnt_sc[...] + jnp.sum(chosen, axis=1, keepdims=True)
    cnt_sc[...] = total
    cnt_ref[...] = total.astype(jnp.int32)


def _mix_route(x, w_in, pmix, pscale, convw, w_out, g1, b1, rwt, rb, tri):
    bsz, seq, d = x.shape
    n = bsz * seq
    ns = seq // TS_MIX
    full = lambda shape: pl.BlockSpec(shape, lambda b, s: (0,) * len(shape))
    return pl.pallas_call(
        _mix_route_kernel,
        grid=(bsz, ns),
        in_specs=[
            pl.BlockSpec((1, TS_MIX, d), lambda b, s: (b, s, 0)),
            full(w_in.shape), full(pmix.shape), full(pscale.shape), full(convw.shape),
            full(w_out.shape), full(g1.shape), full(b1.shape), full(rwt.shape), full(rb.shape),
            full(tri.shape),
        ],
        out_specs=[
            pl.BlockSpec((TS_MIX * ROW_CHUNKS, LANES), lambda b, s: (b * ns + s, 0)),
            pl.BlockSpec((2 * TOP_K, TS_MIX), lambda b, s: (0, b * ns + s)),
            pl.BlockSpec((SUBLANES, TS_MIX), lambda b, s: (0, b * ns + s)),
            pl.BlockSpec((N_EXPERTS, LANES), lambda b, s: (0, 0)),
        ],
        out_shape=[
            jax.ShapeDtypeStruct((n * ROW_CHUNKS, LANES), _F32),
            jax.ShapeDtypeStruct((2 * TOP_K, n), jnp.int32),
            jax.ShapeDtypeStruct((SUBLANES, n), _F32),
            jax.ShapeDtypeStruct((N_EXPERTS, LANES), jnp.int32),
        ],
        scratch_shapes=[
            pltpu.VMEM((POOL_HALO, D_POOL), _F32),
            pltpu.VMEM((CONV_HALO, D_CONV), _F32),
            pltpu.VMEM((N_EXPERTS, LANES), _F32),
        ],
        compiler_params=pltpu.CompilerParams(
            dimension_semantics=("arbitrary", "arbitrary"),
            vmem_limit_bytes=48 * 1024 * 1024),
        name="mix_route",
    )(x, w_in, pmix, pscale, convw, w_out, g1, b1, rwt, rb, tri)


def _make_dest_kernel(start_ref, ri_ref, dest_ref):
    ei = ri_ref[0:TOP_K, :]
    acc = ri_ref[TOP_K:, :]
    for e in range(N_EXPERTS):
        acc = acc + jnp.where(ei == e, start_ref[e], 0)
    dest_ref[...] = acc


def _make_dest(pad_start, ri):
    n = ri.shape[1]
    return pl.pallas_call(
        _make_dest_kernel,
        in_specs=[pl.BlockSpec(memory_space=pltpu.SMEM),
                  pl.BlockSpec((2 * TOP_K, n), lambda: (0, 0))],
        out_specs=pl.BlockSpec((TOP_K, n), lambda: (0, 0)),
        out_shape=jax.ShapeDtypeStruct((TOP_K, n), jnp.int32),
        name="make_dest",
    )(pad_start, ri)


def _dispatch_kernel(last_ref, nv_ref, dest_ref, src_ref, out_ref, zbuf, sem, zsem):
    ts = TS_DISPATCH
    tile_rows = TM_EXPERT * ROW_CHUNKS
    n_tiles = out_ref.shape[0] // tile_rows

    @pl.when(pl.program_id(0) == 0)
    def _():
        zbuf[...] = jnp.zeros_like(zbuf)

        def zero_tile(tile):
            return pltpu.make_async_copy(
                zbuf, out_ref.at[pl.ds(pl.multiple_of(tile * tile_rows, tile_rows), tile_rows)], zsem)

        def each_tile(fn):
            for e in range(N_EXPERTS):
                @pl.when(last_ref[e] >= 0)
                def _():
                    fn(zero_tile(last_ref[e]))

            def tail(j, carry):
                fn(zero_tile(j))
                return carry
            lax.fori_loop(nv_ref[0], n_tiles, tail, 0)

        each_tile(lambda c: c.start())
        each_tile(lambda c: c.wait())

    def row_copy(src_row, dst_row):
        return pltpu.make_async_copy(
            src_ref.at[pl.ds(pl.multiple_of(src_row * ROW_CHUNKS, ROW_CHUNKS), ROW_CHUNKS)],
            out_ref.at[pl.ds(pl.multiple_of(dst_row * ROW_CHUNKS, ROW_CHUNKS), ROW_CHUNKS)],
            sem)

    def body(t, carry):
        for k in range(TOP_K):
            row_copy(t, dest_ref[t * TOP_K + k]).start(priority=k % 2)
        return carry

    lax.fori_loop(0, ts, body, 0, unroll=8)
    for k in range(TOP_K):
        pltpu.make_async_copy(src_ref, out_ref.at[pl.ds(0, ts * ROW_CHUNKS)], sem).wait()


def _dispatch(last_tile, n_valid, dest_flat, x1rt, n_slots):
    n = dest_flat.shape[0] // TOP_K
    grid_spec = pltpu.PrefetchScalarGridSpec(
        num_scalar_prefetch=2,
        grid=(n // TS_DISPATCH,),
        in_specs=[
            pl.BlockSpec((TOP_K * TS_DISPATCH,), lambda i, lt, nv: (i,), memory_space=pltpu.SMEM),
            pl.BlockSpec((TS_DISPATCH * ROW_CHUNKS, LANES), lambda i, lt, nv: (i, 0)),
        ],
        out_specs=pl.BlockSpec(memory_space=pl.ANY),
        scratch_shapes=[
            pltpu.VMEM((TM_EXPERT * ROW_CHUNKS, LANES), _F32),
            pltpu.SemaphoreType.DMA(()),
            pltpu.SemaphoreType.DMA(()),
        ],
    )
    return pl.pallas_call(
        _dispatch_kernel,
        grid_spec=grid_spec,
        out_shape=jax.ShapeDtypeStruct((n_slots * ROW_CHUNKS, LANES), _F32),
        compiler_params=pltpu.CompilerParams(dimension_semantics=("arbitrary",)),
        name="dispatch",
    )(last_tile, n_valid, dest_flat, x1rt)


def _experts_kernel(be_ref, nv_ref, xs_ref, wgu_ref, bgu_ref, wdn_ref, bdn_ref, out_ref):
    del be_ref
    j = pl.program_id(0)
    tm = TM_EXPERT

    @pl.when(j < nv_ref[0])
    def _():
        x = _from_row_tiles(xs_ref, tm).astype(_BF16)
        gu = jnp.dot(x, wgu_ref[0], preferred_element_type=_F32) + bgu_ref[0]
        gate = jnp.minimum(gu[:, :D_FF], SWIGLU_LIMIT)
        up = jnp.clip(gu[:, D_FF:], -SWIGLU_LIMIT, SWIGLU_LIMIT)
        glu = gate * jax.nn.sigmoid(SWIGLU_ALPHA * gate)
        act = ((up + 1.0) * glu).astype(_BF16)
        y = jnp.dot(act, wdn_ref[0], preferred_element_type=_F32) + bdn_ref[0]
        _to_row_tiles(out_ref, y)

    @pl.when(j >= nv_ref[0])
    def _():
        out_ref[...] = jnp.zeros_like(out_ref)


def _experts(blk_e, n_valid, xs, wgu, bgu, wdn, bdn):
    n_slots = xs.shape[0] // ROW_CHUNKS
    nb = n_slots // TM_EXPERT
    rows = TM_EXPERT * ROW_CHUNKS
    grid_spec = pltpu.PrefetchScalarGridSpec(
        num_scalar_prefetch=2,
        grid=(nb,),
        in_specs=[
            pl.BlockSpec((rows, LANES), lambda j, be, nv: (jnp.minimum(j, nv[0] - 1), 0)),
            pl.BlockSpec((1, D_MODEL, 2 * D_FF), lambda j, be, nv: (be[j], 0, 0)),
            pl.BlockSpec((1, 1, 2 * D_FF), lambda j, be, nv: (be[j], 0, 0)),
            pl.BlockSpec((1, D_FF, D_MODEL), lambda j, be, nv: (be[j], 0, 0)),
            pl.BlockSpec((1, 1, D_MODEL), lambda j, be, nv: (be[j], 0, 0)),
        ],
        out_specs=pl.BlockSpec((rows, LANES), lambda j, be, nv: (j, 0)),
    )
    return pl.pallas_call(
        _experts_kernel,
        grid_spec=grid_spec,
        out_shape=jax.ShapeDtypeStruct(xs.shape, _F32),
        compiler_params=pltpu.CompilerParams(
            dimension_semantics=("arbitrary",),
            vmem_limit_bytes=56 * 1024 * 1024),
        name="experts",
    )(blk_e, n_valid, xs, wgu, bgu, wdn, bdn)


def _combine_kernel(dcur_ref, dnext_ref, yb_ref, x1rt_ref, gt_ref, p_ref, g2_ref, b2_ref, pw_ref, gw_ref,
                    gb_ref, g3_ref, b3_ref, out_ref, gbuf, sems):
    ts = TS_COMBINE
    i = pl.program_id(0)
    slot = i % 2

    def start_gather(d_ref, s):
        def body(t, carry):
            for k in range(TOP_K):
                src_row = d_ref[t * TOP_K + k]
                pltpu.make_async_copy(
                    yb_ref.at[pl.ds(pl.multiple_of(src_row * ROW_CHUNKS, ROW_CHUNKS), ROW_CHUNKS)],
                    gbuf.at[s, k, pl.ds(pl.multiple_of(t * ROW_CHUNKS, ROW_CHUNKS), ROW_CHUNKS)],
                    sems.at[s]).start(priority=k % 2)
            return carry
        lax.fori_loop(0, ts, body, 0, unroll=8)

    @pl.when(i == 0)
    def _():
        start_gather(dcur_ref, 0)

    @pl.when(i + 1 < pl.num_programs(0))
    def _():
        start_gather(dnext_ref, 1 - slot)

    for k in range(TOP_K):
        pltpu.make_async_copy(yb_ref.at[pl.ds(0, ts * ROW_CHUNKS)], gbuf.at[slot, k], sems.at[slot]).wait()

    x1 = _from_row_tiles(x1rt_ref, ts)
    ffn = jnp.zeros((ts, D_MODEL), _F32)
    for k in range(TOP_K):
        ffn = ffn + gt_ref[:, k:k + 1] * _from_row_tiles(gbuf.at[slot, k], ts)
    x2 = _layer_norm(DEEPNORM_ALPHA * x1 + ffn, g2_ref[...], b2_ref[...])
    z = jnp.dot(x2.astype(_BF16), gw_ref[...], preferred_element_type=_F32) + gb_ref[...]
    ple = jnp.dot(p_ref[...].astype(_BF16), pw_ref[...], preferred_element_type=_F32)
    x3 = _layer_norm(DEEPNORM_ALPHA * x2 + jax.nn.sigmoid(z) * ple, g3_ref[...], b3_ref[...])
    out_ref[...] = x3


def _combine(dest_flat, yb, x1rt, gates_t, p2d, g2, b2, pw, gw, gb, g3, b3):
    n = dest_flat.shape[0] // TOP_K
    ts = TS_COMBINE
    last = n // ts - 1
    full = lambda shape: pl.BlockSpec(shape, lambda i: (0,) * len(shape))
    return pl.pallas_call(
        _combine_kernel,
        grid=(n // ts,),
        in_specs=[
            pl.BlockSpec((TOP_K * ts,), lambda i: (i,), memory_space=pltpu.SMEM),
            pl.BlockSpec((TOP_K * ts,), lambda i: (jnp.minimum(i + 1, last),), memory_space=pltpu.SMEM),
            pl.BlockSpec(memory_space=pl.ANY),
            pl.BlockSpec((ts * ROW_CHUNKS, LANES), lambda i: (i, 0)),
            pl.BlockSpec((ts, TOP_K), lambda i: (i, 0)),
            pl.BlockSpec((ts, PLE_DIM), lambda i: (i, 0)),
            full(g2.shape), full(b2.shape), full(pw.shape), full(gw.shape), full(gb.shape),
            full(g3.shape), full(b3.shape),
        ],
        out_specs=pl.BlockSpec((ts, D_MODEL), lambda i: (i, 0)),
        out_shape=jax.ShapeDtypeStruct((n, D_MODEL), _F32),
        scratch_shapes=[
            pltpu.VMEM((2, TOP_K, ts * ROW_CHUNKS, LANES), _F32),
            pltpu.SemaphoreType.DMA((2,)),
        ],
        compiler_params=pltpu.CompilerParams(
            dimension_semantics=("arbitrary",),
            vmem_limit_bytes=48 * 1024 * 1024),
        name="combine",
    )(dest_flat, dest_flat, yb, x1rt, gates_t, p2d, g2, b2, pw, gw, gb, g3, b3)


def kernel(x, p, w_in, pool_mix, pool_scale, conv_w, w_out, ln1_g, ln1_b, router_w, router_b,
           w_gate_up, b_gate_up, w_down, b_down, ln2_g, ln2_b, ple_proj, ple_gate_w, ple_gate_b,
           ln3_g, ln3_b):
    assert DEPTH == 1 and x.shape[-1] == D_MODEL
    bsz, seq, d = x.shape
    n = bsz * seq
    assert seq % TS_MIX == 0 and n % TS_DISPATCH == 0 and n % TS_COMBINE == 0
    row = lambda v: v.reshape(1, -1)

    tri = jnp.triu(jnp.ones((TS_MIX, TS_MIX), _BF16))
    x1rt, ri, gates, counts = _mix_route(
        x, w_in[0].astype(_BF16), pool_mix[0].astype(_BF16), row(pool_scale[0]), conv_w[0],
        w_out[0].astype(_BF16), row(ln1_g[0]), row(ln1_b[0]),
        router_w[0].T.astype(_BF16), jnp.broadcast_to(router_b[0][:, None], (N_EXPERTS, LANES)), tri)

    tm = TM_EXPERT
    n_slots = n * TOP_K + N_EXPERTS * tm
    nb = n_slots // tm
    cnt = counts[:, 0]
    padded = ((cnt + tm - 1) // tm) * tm
    pad_end = jnp.cumsum(padded)
    pad_start = (pad_end - padded).astype(jnp.int32)
    n_valid = (pad_end[-1] // tm).astype(jnp.int32)
    tile_start = jnp.arange(nb, dtype=jnp.int32) * tm
    blk_e = jnp.minimum(jnp.sum(tile_start[:, None] >= pad_end[None, :], axis=1), N_EXPERTS - 1)
    blk_e = jnp.where(jnp.arange(nb) < n_valid, blk_e, blk_e[n_valid - 1]).astype(jnp.int32)

    last_tile = jnp.where(padded > 0, pad_end // tm - 1, -1).astype(jnp.int32)
    n_valid = n_valid.reshape(1)

    dest = _make_dest(pad_start, ri).T.reshape(-1)
    xs = _dispatch(last_tile, n_valid, dest, x1rt, n_slots)
    yb = _experts(blk_e, n_valid, xs,
                  w_gate_up[0].astype(_BF16), b_gate_up[0][:, None, :],
                  w_down[0].astype(_BF16), b_down[0][:, None, :])
    out = _combine(dest, yb, x1rt, gates[:TOP_K].T, p[0].reshape(n, PLE_DIM),
                   row(ln2_g[0]), row(ln2_b[0]), ple_proj[0].astype(_BF16),
                   ple_gate_w[0].astype(_BF16), row(ple_gate_b[0]), row(ln3_g[0]), row(ln3_b[0]))
    return out.reshape(bsz, seq, d)
```

```python
import functools

import jax
import jax.numpy as jnp
from jax import lax
from jax.experimental import pallas as pl
from jax.experimental.pallas import tpu as pltpu

D_MODEL = 1024
D_POOL = 512
D_CONV = 512
POOL_WINDOWS = (2, 4, 8, 16)
POOL_GROUP_DIM = 128
CONV_WIDTH = 3
D_IN_PROJ = D_POOL + 3 * D_CONV
N_EXPERTS = 32
TOP_K = 4
D_FF = 1024
SWIGLU_LIMIT = 7.0
SWIGLU_ALPHA = 1.702
PLE_DIM = 256
DEPTH = 1
DEEPNORM_ALPHA = (2.0 * DEPTH) ** 0.25
LN_EPS = 1e-5

LANES = 128
SUBLANES = 8
ROW_CHUNKS = D_MODEL // LANES
POOL_HALO = 16
CONV_HALO = 8

TS_MIX = 512
TS_DISPATCH = 512
TM_EXPERT = 512
TS_COMBINE = 256

_BF16 = jnp.bfloat16
_F32 = jnp.float32


def _layer_norm(h, g, b):
    mu = jnp.mean(h, axis=-1, keepdims=True)
    c = h - mu
    var = jnp.mean(c * c, axis=-1, keepdims=True)
    return c * lax.rsqrt(var + LN_EPS) * g + b


def _to_row_tiles(ref, val):
    rows = val.shape[0]
    for c in range(ROW_CHUNKS):
        ref[pl.ds(c, rows, stride=ROW_CHUNKS), :] = val[:, c * LANES:(c + 1) * LANES]


def _from_row_tiles(ref, rows):
    return jnp.concatenate(
        [ref[pl.ds(c, rows, stride=ROW_CHUNKS), :] for c in range(ROW_CHUNKS)], axis=1)


def _mix_route_kernel(x_ref, w_in_ref, pmix_ref, pscale_ref, convw_ref, w_out_ref, g1_ref, b1_ref,
                      rwt_ref, rb_ref, tri_ref,
                      x1rt_ref, ri_ref, gate_ref, cnt_ref,
                      carry_v, carry_u, cnt_sc):
    b = pl.program_id(0)
    s = pl.program_id(1)
    ts = TS_MIX

    @pl.when(jnp.logical_and(b == 0, s == 0))
    def _():
        cnt_sc[...] = jnp.zeros_like(cnt_sc)

    @pl.when(s == 0)
    def _():
        carry_v[...] = jnp.zeros_like(carry_v)
        carry_u[...] = jnp.zeros_like(carry_u)

    xb = x_ref[0]
    proj = jnp.dot(xb.astype(_BF16), w_in_ref[...], preferred_element_type=_F32)
    vp = proj[:, :D_POOL]
    bg = proj[:, D_POOL:D_POOL + D_CONV]
    cg = proj[:, D_POOL + D_CONV:D_POOL + 2 * D_CONV]
    vc = proj[:, D_POOL + 2 * D_CONV:]

    ext = jnp.concatenate([carry_v[...], vp], axis=0)
    pos = lax.broadcasted_iota(jnp.int32, (ts, 1), 0) + s * ts
    mixed = []
    for g, w in enumerate(POOL_WINDOWS):
        lo, hi = g * POOL_GROUP_DIM, (g + 1) * POOL_GROUP_DIM
        acc = ext[:, lo:hi]
        sh = 1
        while sh < w:
            acc = acc + pltpu.roll(acc, sh, 0)
            sh *= 2
        cnt = jnp.minimum(pos + 1, w).astype(_F32)
        d = acc[POOL_HALO:] / cnt - vp[:, lo:hi]
        yg = jnp.dot(d.astype(_BF16), pmix_ref[g], preferred_element_type=_F32)
        mixed.append(yg * pscale_ref[:, lo:hi])
    carry_v[...] = vp[ts - POOL_HALO:]

    u = cg * vc
    extu = jnp.concatenate([carry_u[...], u], axis=0)
    u1 = pltpu.roll(extu, 1, 0)[CONV_HALO:]
    u2 = pltpu.roll(extu, 2, 0)[CONV_HALO:]
    yc = bg * (convw_ref[0:1, :] * u2 + convw_ref[1:2, :] * u1 + convw_ref[2:3, :] * u)
    carry_u[...] = u[ts - CONV_HALO:]
    mixed.append(yc)

    mix_in = jnp.concatenate(mixed, axis=1).astype(_BF16)
    mix = jnp.dot(mix_in, w_out_ref[...], preferred_element_type=_F32)
    x1 = _layer_norm(DEEPNORM_ALPHA * xb + mix, g1_ref[...], b1_ref[...])
    _to_row_tiles(x1rt_ref, x1)

    logits = lax.dot_general(rwt_ref[...], x1.astype(_BF16), (((1,), (1,)), ((), ())),
                             preferred_element_type=_F32) + rb_ref[:, 0:1]
    eidx = lax.broadcasted_iota(jnp.int32, (N_EXPERTS, ts), 0).astype(_F32)
    vals, sels = [], []
    work = logits
    for k in range(TOP_K):
        m = jnp.max(work, axis=0, keepdims=True)
        first = jnp.min(jnp.where(work == m, eidx, float(N_EXPERTS)), axis=0, keepdims=True)
        sel = eidx == first
        work = jnp.where(sel, -jnp.inf, work)
        vals.append(m)
        sels.append(sel)
        ri_ref[k:k + 1, :] = first.astype(jnp.int32)
    exps = [jnp.exp(v - vals[0]) for v in vals]
    denom = exps[0] + exps[1] + exps[2] + exps[3]
    for k in range(TOP_K):
        gate_ref[k:k + 1, :] = exps[k] / denom
    gate_ref[TOP_K:, :] = jnp.zeros((SUBLANES - TOP_K, ts), _F32)

    chosen = jnp.zeros((N_EXPERTS, ts), _F32)
    for sel in sels:
        chosen = chosen + sel.astype(_F32)
    incl = jnp.dot(chosen.astype(_BF16), tri_ref[...], preferred_element_type=_F32)
    base = cnt_sc[:, 0:1] + (incl - chosen)
    for k in range(TOP_K):
        rank = jnp.sum(jnp.where(sels[k], base, 0.0), axis=0, keepdims=True)
        ri_ref[TOP_K + k:TOP_K + k + 1, :] = rank.astype(jnp.int32)
    total = cnt_sc[...] + jnp.sum(chosen, axis=1, keepdims=True)
    cnt_sc[...] = total
    cnt_ref[...] = total.astype(jnp.int32)


def _mix_route(x, w_in, pmix, pscale, convw, w_out, g1, b1, rwt, rb, tri):
    bsz, seq, d = x.shape
    n = bsz * seq
    ns = seq // TS_MIX
    full = lambda shape: pl.BlockSpec(shape, lambda b, s: (0,) * len(shape))
    return pl.pallas_call(
        _mix_route_kernel,
        grid=(bsz, ns),
        in_specs=[
            pl.BlockSpec((1, TS_MIX, d), lambda b, s: (b, s, 0)),
            full(w_in.shape), full(pmix.shape), full(pscale.shape), full(convw.shape),
            full(w_out.shape), full(g1.shape), full(b1.shape), full(rwt.shape), full(rb.shape),
            full(tri.shape),
        ],
        out_specs=[
            pl.BlockSpec((TS_MIX * ROW_CHUNKS, LANES), lambda b, s: (b * ns + s, 0)),
            pl.BlockSpec((2 * TOP_K, TS_MIX), lambda b, s: (0, b * ns + s)),
            pl.BlockSpec((SUBLANES, TS_MIX), lambda b, s: (0, b * ns + s)),
            pl.BlockSpec((N_EXPERTS, LANES), lambda b, s: (0, 0)),
        ],
        out_shape=[
            jax.ShapeDtypeStruct((n * ROW_CHUNKS, LANES), _F32),
            jax.ShapeDtypeStruct((2 * TOP_K, n), jnp.int32),
            jax.ShapeDtypeStruct((SUBLANES, n), _F32),
            jax.ShapeDtypeStruct((N_EXPERTS, LANES), jnp.int32),
        ],
        scratch_shapes=[
            pltpu.VMEM((POOL_HALO, D_POOL), _F32),
            pltpu.VMEM((CONV_HALO, D_CONV), _F32),
            pltpu.VMEM((N_EXPERTS, LANES), _F32),
        ],
        compiler_params=pltpu.CompilerParams(
            dimension_semantics=("arbitrary", "arbitrary"),
            vmem_limit_bytes=48 * 1024 * 1024),
        name="mix_route",
    )(x, w_in, pmix, pscale, convw, w_out, g1, b1, rwt, rb, tri)


def _make_dest_kernel(start_ref, ri_ref, dest_ref):
    ei = ri_ref[0:TOP_K, :]
    acc = ri_ref[TOP_K:, :]
    for e in range(N_EXPERTS):
        acc = acc + jnp.where(ei == e, start_ref[e], 0)
    dest_ref[...] = acc


def _make_dest(pad_start, ri):
    n = ri.shape[1]
    return pl.pallas_call(
        _make_dest_kernel,
        in_specs=[pl.BlockSpec(memory_space=pltpu.SMEM),
                  pl.BlockSpec((2 * TOP_K, n), lambda: (0, 0))],
        out_specs=pl.BlockSpec((TOP_K, n), lambda: (0, 0)),
        out_shape=jax.ShapeDtypeStruct((TOP_K, n), jnp.int32),
        name="make_dest",
    )(pad_start, ri)


def _dispatch_kernel(last_ref, nv_ref, dest_ref, src_ref, out_ref, zbuf, sem, zsem):
    ts = TS_DISPATCH
    tile_rows = TM_EXPERT * ROW_CHUNKS
    n_tiles = out_ref.shape[0] // tile_rows

    @pl.when(pl.program_id(0) == 0)
    def _():
        zbuf[...] = jnp.zeros_like(zbuf)

        def zero_tile(tile):
            return pltpu.make_async_copy(
                zbuf, out_ref.at[pl.ds(pl.multiple_of(tile * tile_rows, tile_rows), tile_rows)], zsem)

        def each_tile(fn):
            for e in range(N_EXPERTS):
                @pl.when(last_ref[e] >= 0)
                def _():
                    fn(zero_tile(last_ref[e]))

            def tail(j, carry):
                fn(zero_tile(j))
                return carry
            lax.fori_loop(nv_ref[0], n_tiles, tail, 0)

        each_tile(lambda c: c.start())
        each_tile(lambda c: c.wait())

    def row_copy(src_row, dst_row):
        return pltpu.make_async_copy(
            src_ref.at[pl.ds(pl.multiple_of(src_row * ROW_CHUNKS, ROW_CHUNKS), ROW_CHUNKS)],
            out_ref.at[pl.ds(pl.multiple_of(dst_row * ROW_CHUNKS, ROW_CHUNKS), ROW_CHUNKS)],
            sem)

    def body(t, carry):
        for k in range(TOP_K):
            row_copy(t, dest_ref[t * TOP_K + k]).start(priority=k % 2)
        return carry

    lax.fori_loop(0, ts, body, 0, unroll=8)
    for k in range(TOP_K):
        pltpu.make_async_copy(src_ref, out_ref.at[pl.ds(0, ts * ROW_CHUNKS)], sem).wait()


def _dispatch(last_tile, n_valid, dest_flat, x1rt, n_slots):
    n = dest_flat.shape[0] // TOP_K
    grid_spec = pltpu.PrefetchScalarGridSpec(
        num_scalar_prefetch=2,
        grid=(n // TS_DISPATCH,),
        in_specs=[
            pl.BlockSpec((TOP_K * TS_DISPATCH,), lambda i, lt, nv: (i,), memory_space=pltpu.SMEM),
            pl.BlockSpec((TS_DISPATCH * ROW_CHUNKS, LANES), lambda i, lt, nv: (i, 0)),
        ],
        out_specs=pl.BlockSpec(memory_space=pl.ANY),
        scratch_shapes=[
            pltpu.VMEM((TM_EXPERT * ROW_CHUNKS, LANES), _F32),
            pltpu.SemaphoreType.DMA(()),
            pltpu.SemaphoreType.DMA(()),
        ],
    )
    return pl.pallas_call(
        _dispatch_kernel,
        grid_spec=grid_spec,
        out_shape=jax.ShapeDtypeStruct((n_slots * ROW_CHUNKS, LANES), _F32),
        compiler_params=pltpu.CompilerParams(dimension_semantics=("arbitrary",)),
        name="dispatch",
    )(last_tile, n_valid, dest_flat, x1rt)


def _experts_kernel(be_ref, nv_ref, xs_ref, wgu_ref, bgu_ref, wdn_ref, bdn_ref, out_ref, wgu_bf, wdn_bf):
    j = pl.program_id(0)
    tm = TM_EXPERT
    valid = j < nv_ref[0]
    new_expert = jnp.logical_or(j == 0, be_ref[j] != be_ref[jnp.maximum(j - 1, 0)])

    @pl.when(jnp.logical_and(valid, new_expert))
    def _():
        wgu_bf[...] = wgu_ref[0].astype(_BF16)
        wdn_bf[...] = wdn_ref[0].astype(_BF16)

    @pl.when(valid)
    def _():
        x = _from_row_tiles(xs_ref, tm).astype(_BF16)
        gu = jnp.dot(x, wgu_bf[...], preferred_element_type=_F32) + bgu_ref[0]
        gate = jnp.minimum(gu[:, :D_FF], SWIGLU_LIMIT)
        up = jnp.clip(gu[:, D_FF:], -SWIGLU_LIMIT, SWIGLU_LIMIT)
        glu = gate * jax.nn.sigmoid(SWIGLU_ALPHA * gate)
        act = ((up + 1.0) * glu).astype(_BF16)
        y = jnp.dot(act, wdn_bf[...], preferred_element_type=_F32) + bdn_ref[0]
        _to_row_tiles(out_ref, y)

    @pl.when(j >= nv_ref[0])
    def _():
        out_ref[...] = jnp.zeros_like(out_ref)


def _experts(blk_e, n_valid, xs, wgu, bgu, wdn, bdn):
    n_slots = xs.shape[0] // ROW_CHUNKS
    nb = n_slots // TM_EXPERT
    rows = TM_EXPERT * ROW_CHUNKS
    grid_spec = pltpu.PrefetchScalarGridSpec(
        num_scalar_prefetch=2,
        grid=(nb,),
        in_specs=[
            pl.BlockSpec((rows, LANES), lambda j, be, nv: (jnp.minimum(j, nv[0] - 1), 0)),
            pl.BlockSpec((1, D_MODEL, 2 * D_FF), lambda j, be, nv: (be[j], 0, 0)),
            pl.BlockSpec((1, 1, 2 * D_FF), lambda j, be, nv: (be[j], 0, 0)),
            pl.BlockSpec((1, D_FF, D_MODEL), lambda j, be, nv: (be[j], 0, 0)),
            pl.BlockSpec((1, 1, D_MODEL), lambda j, be, nv: (be[j], 0, 0)),
        ],
        out_specs=pl.BlockSpec((rows, LANES), lambda j, be, nv: (j, 0)),
        scratch_shapes=[
            pltpu.VMEM((D_MODEL, 2 * D_FF), _BF16),
            pltpu.VMEM((D_FF, D_MODEL), _BF16),
        ],
    )
    return pl.pallas_call(
        _experts_kernel,
        grid_spec=grid_spec,
        out_shape=jax.ShapeDtypeStruct(xs.shape, _F32),
        compiler_params=pltpu.CompilerParams(
            dimension_semantics=("arbitrary",),
            vmem_limit_bytes=60 * 1024 * 1024),
        name="experts",
    )(blk_e, n_valid, xs, wgu, bgu, wdn, bdn)


def _combine_kernel(dcur_ref, dnext_ref, yb_ref, x1rt_ref, gt_ref, p_ref, g2_ref, b2_ref, pw_ref, gw_ref,
                    gb_ref, g3_ref, b3_ref, out_ref, gbuf, sems):
    ts = TS_COMBINE
    i = pl.program_id(0)
    slot = i % 2

    def start_gather(d_ref, s):
        def body(t, carry):
            for k in range(TOP_K):
                src_row = d_ref[t * TOP_K + k]
                pltpu.make_async_copy(
                    yb_ref.at[pl.ds(pl.multiple_of(src_row * ROW_CHUNKS, ROW_CHUNKS), ROW_CHUNKS)],
                    gbuf.at[s, k, pl.ds(pl.multiple_of(t * ROW_CHUNKS, ROW_CHUNKS), ROW_CHUNKS)],
                    sems.at[s]).start(priority=k % 2)
            return carry
        lax.fori_loop(0, ts, body, 0, unroll=8)

    @pl.when(i == 0)
    def _():
        start_gather(dcur_ref, 0)

    @pl.when(i + 1 < pl.num_programs(0))
    def _():
        start_gather(dnext_ref, 1 - slot)

    for k in range(TOP_K):
        pltpu.make_async_copy(yb_ref.at[pl.ds(0, ts * ROW_CHUNKS)], gbuf.at[slot, k], sems.at[slot]).wait()

    x1 = _from_row_tiles(x1rt_ref, ts)
    ffn = jnp.zeros((ts, D_MODEL), _F32)
    for k in range(TOP_K):
        ffn = ffn + gt_ref[:, k:k + 1] * _from_row_tiles(gbuf.at[slot, k], ts)
    x2 = _layer_norm(DEEPNORM_ALPHA * x1 + ffn, g2_ref[...], b2_ref[...])
    z = jnp.dot(x2.astype(_BF16), gw_ref[...], preferred_element_type=_F32) + gb_ref[...]
    ple = jnp.dot(p_ref[...].astype(_BF16), pw_ref[...], preferred_element_type=_F32)
    x3 = _layer_norm(DEEPNORM_ALPHA * x2 + jax.nn.sigmoid(z) * ple, g3_ref[...], b3_ref[...])
    out_ref[...] = x3


def _combine(dest_flat, yb, x1rt, gates_t, p2d, g2, b2, pw, gw, gb, g3, b3):
    n = dest_flat.shape[0] // TOP_K
    ts = TS_COMBINE
    last = n // ts - 1
    full = lambda shape: pl.BlockSpec(shape, lambda i: (0,) * len(shape))
    return pl.pallas_call(
        _combine_kernel,
        grid=(n // ts,),
        in_specs=[
            pl.BlockSpec((TOP_K * ts,), lambda i: (i,), memory_space=pltpu.SMEM),
            pl.BlockSpec((TOP_K * ts,), lambda i: (jnp.minimum(i + 1, last),), memory_space=pltpu.SMEM),
            pl.BlockSpec(memory_space=pl.ANY),
            pl.BlockSpec((ts * ROW_CHUNKS, LANES), lambda i: (i, 0)),
            pl.BlockSpec((ts, TOP_K), lambda i: (i, 0)),
            pl.BlockSpec((ts, PLE_DIM), lambda i: (i, 0)),
            full(g2.shape), full(b2.shape), full(pw.shape), full(gw.shape), full(gb.shape),
            full(g3.shape), full(b3.shape),
        ],
        out_specs=pl.BlockSpec((ts, D_MODEL), lambda i: (i, 0)),
        out_shape=jax.ShapeDtypeStruct((n, D_MODEL), _F32),
        scratch_shapes=[
            pltpu.VMEM((2, TOP_K, ts * ROW_CHUNKS, LANES), _F32),
            pltpu.SemaphoreType.DMA((2,)),
        ],
        compiler_params=pltpu.CompilerParams(
            dimension_semantics=("arbitrary",),
            vmem_limit_bytes=48 * 1024 * 1024),
        name="combine",
    )(dest_flat, dest_flat, yb, x1rt, gates_t, p2d, g2, b2, pw, gw, gb, g3, b3)


def kernel(x, p, w_in, pool_mix, pool_scale, conv_w, w_out, ln1_g, ln1_b, router_w, router_b,
           w_gate_up, b_gate_up, w_down, b_down, ln2_g, ln2_b, ple_proj, ple_gate_w, ple_gate_b,
           ln3_g, ln3_b):
    assert DEPTH == 1 and x.shape[-1] == D_MODEL
    bsz, seq, d = x.shape
    n = bsz * seq
    assert seq % TS_MIX == 0 and n % TS_DISPATCH == 0 and n % TS_COMBINE == 0
    row = lambda v: v.reshape(1, -1)

    tri = jnp.triu(jnp.ones((TS_MIX, TS_MIX), _BF16))
    x1rt, ri, gates, counts = _mix_route(
        x, w_in[0].astype(_BF16), pool_mix[0].astype(_BF16), row(pool_scale[0]), conv_w[0],
        w_out[0].astype(_BF16), row(ln1_g[0]), row(ln1_b[0]),
        router_w[0].T.astype(_BF16), jnp.broadcast_to(router_b[0][:, None], (N_EXPERTS, LANES)), tri)

    tm = TM_EXPERT
    n_slots = n * TOP_K + N_EXPERTS * tm
    nb = n_slots // tm
    cnt = counts[:, 0]
    padded = ((cnt + tm - 1) // tm) * tm
    pad_end = jnp.cumsum(padded)
    pad_start = (pad_end - padded).astype(jnp.int32)
    n_valid = (pad_end[-1] // tm).astype(jnp.int32)
    tile_start = jnp.arange(nb, dtype=jnp.int32) * tm
    blk_e = jnp.minimum(jnp.sum(tile_start[:, None] >= pad_end[None, :], axis=1), N_EXPERTS - 1)
    blk_e = jnp.where(jnp.arange(nb) < n_valid, blk_e, blk_e[n_valid - 1]).astype(jnp.int32)

    last_tile = jnp.where(padded > 0, pad_end // tm - 1, -1).astype(jnp.int32)
    n_valid = n_valid.reshape(1)

    dest = _make_dest(pad_start, ri).T.reshape(-1)
    xs = _dispatch(last_tile, n_valid, dest, x1rt, n_slots)
    yb = _experts(blk_e, n_valid, xs,
                  w_gate_up[0], b_gate_up[0][:, None, :], w_down[0], b_down[0][:, None, :])
    out = _combine(dest, yb, x1rt, gates[:TOP_K].T, p[0].reshape(n, PLE_DIM),
                   row(ln2_g[0]), row(ln2_b[0]), ple_proj[0].astype(_BF16),
                   ple_gate_w[0].astype(_BF16), row(ple_gate_b[0]), row(ln3_g[0]), row(ln3_b[0]))
    return out.reshape(bsz, seq, d)
```

```python
import functools

import jax
import jax.numpy as jnp
from jax import lax
from jax.experimental import pallas as pl
from jax.experimental.pallas import tpu as pltpu

D_MODEL = 1024
D_POOL = 512
D_CONV = 512
POOL_WINDOWS = (2, 4, 8, 16)
POOL_GROUP_DIM = 128
CONV_WIDTH = 3
D_IN_PROJ = D_POOL + 3 * D_CONV
N_EXPERTS = 32
TOP_K = 4
D_FF = 1024
SWIGLU_LIMIT = 7.0
SWIGLU_ALPHA = 1.702
PLE_DIM = 256
DEPTH = 1
DEEPNORM_ALPHA = (2.0 * DEPTH) ** 0.25
LN_EPS = 1e-5

LANES = 128
SUBLANES = 8
ROW_CHUNKS = D_MODEL // LANES
POOL_HALO = 16
CONV_HALO = 8

TS_MIX = 512
TS_DISPATCH = 1024
TM_EXPERT = 512
TS_COMBINE = 512

_BF16 = jnp.bfloat16
_F32 = jnp.float32


def _layer_norm(h, g, b):
    mu = jnp.mean(h, axis=-1, keepdims=True)
    c = h - mu
    var = jnp.mean(c * c, axis=-1, keepdims=True)
    return c * lax.rsqrt(var + LN_EPS) * g + b


def _to_row_tiles(ref, val):
    rows = val.shape[0]
    for c in range(ROW_CHUNKS):
        ref[pl.ds(c, rows, stride=ROW_CHUNKS), :] = val[:, c * LANES:(c + 1) * LANES]


def _from_row_tiles(ref, rows):
    return jnp.concatenate(
        [ref[pl.ds(c, rows, stride=ROW_CHUNKS), :] for c in range(ROW_CHUNKS)], axis=1)


def _mix_route_kernel(x_ref, w_in_ref, pmix_ref, pscale_ref, convw_ref, w_out_ref, g1_ref, b1_ref,
                      rwt_ref, rb_ref, tri_ref,
                      x1rt_ref, ri_ref, gate_ref, cnt_ref,
                      carry_v, carry_u, cnt_sc):
    b = pl.program_id(0)
    s = pl.program_id(1)
    ts = TS_MIX

    @pl.when(jnp.logical_and(b == 0, s == 0))
    def _():
        cnt_sc[...] = jnp.zeros_like(cnt_sc)

    @pl.when(s == 0)
    def _():
        carry_v[...] = jnp.zeros_like(carry_v)
        carry_u[...] = jnp.zeros_like(carry_u)

    xb = x_ref[0]
    proj = jnp.dot(xb.astype(_BF16), w_in_ref[...], preferred_element_type=_F32)
    vp = proj[:, :D_POOL]
    bg = proj[:, D_POOL:D_POOL + D_CONV]
    cg = proj[:, D_POOL + D_CONV:D_POOL + 2 * D_CONV]
    vc = proj[:, D_POOL + 2 * D_CONV:]

    ext = jnp.concatenate([carry_v[...], vp], axis=0)
    pos = lax.broadcasted_iota(jnp.int32, (ts, 1), 0) + s * ts
    mixed = []
    for g, w in enumerate(POOL_WINDOWS):
        lo, hi = g * POOL_GROUP_DIM, (g + 1) * POOL_GROUP_DIM
        acc = ext[:, lo:hi]
        sh = 1
        while sh < w:
            acc = acc + pltpu.roll(acc, sh, 0)
            sh *= 2
        cnt = jnp.minimum(pos + 1, w).astype(_F32)
        d = acc[POOL_HALO:] / cnt - vp[:, lo:hi]
        yg = jnp.dot(d.astype(_BF16), pmix_ref[g], preferred_element_type=_F32)
        mixed.append(yg * pscale_ref[:, lo:hi])
    carry_v[...] = vp[ts - POOL_HALO:]

    u = cg * vc
    extu = jnp.concatenate([carry_u[...], u], axis=0)
    u1 = pltpu.roll(extu, 1, 0)[CONV_HALO:]
    u2 = pltpu.roll(extu, 2, 0)[CONV_HALO:]
    yc = bg * (convw_ref[0:1, :] * u2 + convw_ref[1:2, :] * u1 + convw_ref[2:3, :] * u)
    carry_u[...] = u[ts - CONV_HALO:]
    mixed.append(yc)

    mix_in = jnp.concatenate(mixed, axis=1).astype(_BF16)
    mix = jnp.dot(mix_in, w_out_ref[...], preferred_element_type=_F32)
    x1 = _layer_norm(DEEPNORM_ALPHA * xb + mix, g1_ref[...], b1_ref[...])
    _to_row_tiles(x1rt_ref, x1)

    logits = lax.dot_general(rwt_ref[...], x1.astype(_BF16), (((1,), (1,)), ((), ())),
                             preferred_element_type=_F32) + rb_ref[:, 0:1]
    eidx = lax.broadcasted_iota(jnp.int32, (N_EXPERTS, ts), 0).astype(_F32)
    vals, sels = [], []
    work = logits
    for k in range(TOP_K):
        m = jnp.max(work, axis=0, keepdims=True)
        first = jnp.min(jnp.where(work == m, eidx, float(N_EXPERTS)), axis=0, keepdims=True)
        sel = eidx == first
        work = jnp.where(sel, -jnp.inf, work)
        vals.append(m)
        sels.append(sel)
        ri_ref[k:k + 1, :] = first.astype(jnp.int32)
    exps = [jnp.exp(v - vals[0]) for v in vals]
    denom = exps[0] + exps[1] + exps[2] + exps[3]
    for k in range(TOP_K):
        gate_ref[k:k + 1, :] = exps[k] / denom
    gate_ref[TOP_K:, :] = jnp.zeros((SUBLANES - TOP_K, ts), _F32)

    chosen = jnp.zeros((N_EXPERTS, ts), _F32)
    for sel in sels:
        chosen = chosen + sel.astype(_F32)
    incl = jnp.dot(chosen.astype(_BF16), tri_ref[...], preferred_element_type=_F32)
    base = cnt_sc[:, 0:1] + (incl - chosen)
    for k in range(TOP_K):
        rank = jnp.sum(jnp.where(sels[k], base, 0.0), axis=0, keepdims=True)
        ri_ref[TOP_K + k:TOP_K + k + 1, :] = rank.astype(jnp.int32)
    total = cnt_sc[...] + jnp.sum(chosen, axis=1, keepdims=True)
    cnt_sc[...] = total
    cnt_ref[...] = total.astype(jnp.int32)


def _mix_route(x, w_in, pmix, pscale, convw, w_out, g1, b1, rwt, rb, tri):
    bsz, seq, d = x.shape
    n = bsz * seq
    ns = seq // TS_MIX
    full = lambda shape: pl.BlockSpec(shape, lambda b, s: (0,) * len(shape))
    return pl.pallas_call(
        _mix_route_kernel,
        grid=(bsz, ns),
        in_specs=[
            pl.BlockSpec((1, TS_MIX, d), lambda b, s: (b, s, 0)),
            full(w_in.shape), full(pmix.shape), full(pscale.shape), full(convw.shape),
            full(w_out.shape), full(g1.shape), full(b1.shape), full(rwt.shape), full(rb.shape),
            full(tri.shape),
        ],
        out_specs=[
            pl.BlockSpec((TS_MIX * ROW_CHUNKS, LANES), lambda b, s: (b * ns + s, 0)),
            pl.BlockSpec((2 * TOP_K, TS_MIX), lambda b, s: (0, b * ns + s)),
            pl.BlockSpec((SUBLANES, TS_MIX), lambda b, s: (0, b * ns + s)),
            pl.BlockSpec((N_EXPERTS, LANES), lambda b, s: (0, 0)),
        ],
        out_shape=[
            jax.ShapeDtypeStruct((n * ROW_CHUNKS, LANES), _F32),
            jax.ShapeDtypeStruct((2 * TOP_K, n), jnp.int32),
            jax.ShapeDtypeStruct((SUBLANES, n), _F32),
            jax.ShapeDtypeStruct((N_EXPERTS, LANES), jnp.int32),
        ],
        scratch_shapes=[
            pltpu.VMEM((POOL_HALO, D_POOL), _F32),
            pltpu.VMEM((CONV_HALO, D_CONV), _F32),
            pltpu.VMEM((N_EXPERTS, LANES), _F32),
        ],
        compiler_params=pltpu.CompilerParams(
            dimension_semantics=("arbitrary", "arbitrary"),
            vmem_limit_bytes=48 * 1024 * 1024),
        name="mix_route",
    )(x, w_in, pmix, pscale, convw, w_out, g1, b1, rwt, rb, tri)


def _make_dest_kernel(start_ref, ri_ref, dest_ref):
    ei = ri_ref[0:TOP_K, :]
    acc = ri_ref[TOP_K:, :]
    for e in range(N_EXPERTS):
        acc = acc + jnp.where(ei == e, start_ref[e], 0)
    dest_ref[...] = acc


def _make_dest(pad_start, ri):
    n = ri.shape[1]
    return pl.pallas_call(
        _make_dest_kernel,
        in_specs=[pl.BlockSpec(memory_space=pltpu.SMEM),
                  pl.BlockSpec((2 * TOP_K, n), lambda: (0, 0))],
        out_specs=pl.BlockSpec((TOP_K, n), lambda: (0, 0)),
        out_shape=jax.ShapeDtypeStruct((TOP_K, n), jnp.int32),
        name="make_dest",
    )(pad_start, ri)


def _dispatch_kernel(last_ref, nv_ref, dest_ref, src_ref, out_ref, zbuf, sem, zsem):
    ts = TS_DISPATCH
    tile_rows = TM_EXPERT * ROW_CHUNKS
    n_tiles = out_ref.shape[0] // tile_rows

    @pl.when(pl.program_id(0) == 0)
    def _():
        zbuf[...] = jnp.zeros_like(zbuf)

        def zero_tile(tile):
            return pltpu.make_async_copy(
                zbuf, out_ref.at[pl.ds(pl.multiple_of(tile * tile_rows, tile_rows), tile_rows)], zsem)

        def each_tile(fn):
            for e in range(N_EXPERTS):
                @pl.when(last_ref[e] >= 0)
                def _():
                    fn(zero_tile(last_ref[e]))

            def tail(j, carry):
                fn(zero_tile(j))
                return carry
            lax.fori_loop(nv_ref[0], n_tiles, tail, 0)

        each_tile(lambda c: c.start())
        each_tile(lambda c: c.wait())

    def row_copy(src_row, dst_row):
        return pltpu.make_async_copy(
            src_ref.at[pl.ds(pl.multiple_of(src_row * ROW_CHUNKS, ROW_CHUNKS), ROW_CHUNKS)],
            out_ref.at[pl.ds(pl.multiple_of(dst_row * ROW_CHUNKS, ROW_CHUNKS), ROW_CHUNKS)],
            sem)

    def body(t, carry):
        for k in range(TOP_K):
            row_copy(t, dest_ref[t * TOP_K + k]).start(priority=k % 2)
        return carry

    lax.fori_loop(0, ts, body, 0, unroll=8)
    for k in range(TOP_K):
        pltpu.make_async_copy(src_ref, out_ref.at[pl.ds(0, ts * ROW_CHUNKS)], sem).wait()


def _dispatch(last_tile, n_valid, dest_flat, x1rt, n_slots):
    n = dest_flat.shape[0] // TOP_K
    grid_spec = pltpu.PrefetchScalarGridSpec(
        num_scalar_prefetch=2,
        grid=(n // TS_DISPATCH,),
        in_specs=[
            pl.BlockSpec((TOP_K * TS_DISPATCH,), lambda i, lt, nv: (i,), memory_space=pltpu.SMEM),
            pl.BlockSpec((TS_DISPATCH * ROW_CHUNKS, LANES), lambda i, lt, nv: (i, 0)),
        ],
        out_specs=pl.BlockSpec(memory_space=pl.ANY),
        scratch_shapes=[
            pltpu.VMEM((TM_EXPERT * ROW_CHUNKS, LANES), _F32),
            pltpu.SemaphoreType.DMA(()),
            pltpu.SemaphoreType.DMA(()),
        ],
    )
    return pl.pallas_call(
        _dispatch_kernel,
        grid_spec=grid_spec,
        out_shape=jax.ShapeDtypeStruct((n_slots * ROW_CHUNKS, LANES), _F32),
        compiler_params=pltpu.CompilerParams(dimension_semantics=("arbitrary",)),
        name="dispatch",
    )(last_tile, n_valid, dest_flat, x1rt)


def _experts_kernel(be_ref, nv_ref, xs_ref, wgu_ref, bgu_ref, wdn_ref, bdn_ref, out_ref, wgu_bf, wdn_bf):
    j = pl.program_id(0)
    tm = TM_EXPERT
    valid = j < nv_ref[0]
    new_expert = jnp.logical_or(j == 0, be_ref[j] != be_ref[jnp.maximum(j - 1, 0)])

    @pl.when(jnp.logical_and(valid, new_expert))
    def _():
        wgu_bf[...] = wgu_ref[0].astype(_BF16)
        wdn_bf[...] = wdn_ref[0].astype(_BF16)

    @pl.when(valid)
    def _():
        x = _from_row_tiles(xs_ref, tm).astype(_BF16)
        gu = jnp.dot(x, wgu_bf[...], preferred_element_type=_F32) + bgu_ref[0]
        gate = jnp.minimum(gu[:, :D_FF], SWIGLU_LIMIT)
        up = jnp.clip(gu[:, D_FF:], -SWIGLU_LIMIT, SWIGLU_LIMIT)
        glu = gate * jax.nn.sigmoid(SWIGLU_ALPHA * gate)
        act = ((up + 1.0) * glu).astype(_BF16)
        y = jnp.dot(act, wdn_bf[...], preferred_element_type=_F32) + bdn_ref[0]
        _to_row_tiles(out_ref, y)

    @pl.when(j >= nv_ref[0])
    def _():
        out_ref[...] = jnp.zeros_like(out_ref)


def _experts(blk_e, n_valid, xs, wgu, bgu, wdn, bdn):
    n_slots = xs.shape[0] // ROW_CHUNKS
    nb = n_slots // TM_EXPERT
    rows = TM_EXPERT * ROW_CHUNKS
    grid_spec = pltpu.PrefetchScalarGridSpec(
        num_scalar_prefetch=2,
        grid=(nb,),
        in_specs=[
            pl.BlockSpec((rows, LANES), lambda j, be, nv: (jnp.minimum(j, nv[0] - 1), 0)),
            pl.BlockSpec((1, D_MODEL, 2 * D_FF), lambda j, be, nv: (be[j], 0, 0)),
            pl.BlockSpec((1, 1, 2 * D_FF), lambda j, be, nv: (be[j], 0, 0)),
            pl.BlockSpec((1, D_FF, D_MODEL), lambda j, be, nv: (be[j], 0, 0)),
            pl.BlockSpec((1, 1, D_MODEL), lambda j, be, nv: (be[j], 0, 0)),
        ],
        out_specs=pl.BlockSpec((rows, LANES), lambda j, be, nv: (j, 0)),
        scratch_shapes=[
            pltpu.VMEM((D_MODEL, 2 * D_FF), _BF16),
            pltpu.VMEM((D_FF, D_MODEL), _BF16),
        ],
    )
    return pl.pallas_call(
        _experts_kernel,
        grid_spec=grid_spec,
        out_shape=jax.ShapeDtypeStruct(xs.shape, _F32),
        compiler_params=pltpu.CompilerParams(
            dimension_semantics=("arbitrary",),
            vmem_limit_bytes=60 * 1024 * 1024),
        name="experts",
    )(blk_e, n_valid, xs, wgu, bgu, wdn, bdn)


def _combine_kernel(dcur_ref, dnext_ref, yb_ref, x1rt_ref, gt_ref, p_ref, g2_ref, b2_ref, pw_ref, gw_ref,
                    gb_ref, g3_ref, b3_ref, out_ref, gbuf, sems):
    ts = TS_COMBINE
    i = pl.program_id(0)
    slot = i % 2

    def start_gather(d_ref, s):
        def body(t, carry):
            for k in range(TOP_K):
                src_row = d_ref[t * TOP_K + k]
                pltpu.make_async_copy(
                    yb_ref.at[pl.ds(pl.multiple_of(src_row * ROW_CHUNKS, ROW_CHUNKS), ROW_CHUNKS)],
                    gbuf.at[s, k, pl.ds(pl.multiple_of(t * ROW_CHUNKS, ROW_CHUNKS), ROW_CHUNKS)],
                    sems.at[s]).start(priority=k % 2)
            return carry
        lax.fori_loop(0, ts, body, 0, unroll=8)

    @pl.when(i == 0)
    def _():
        start_gather(dcur_ref, 0)

    @pl.when(i + 1 < pl.num_programs(0))
    def _():
        start_gather(dnext_ref, 1 - slot)

    for k in range(TOP_K):
        pltpu.make_async_copy(yb_ref.at[pl.ds(0, ts * ROW_CHUNKS)], gbuf.at[slot, k], sems.at[slot]).wait()

    x1 = _from_row_tiles(x1rt_ref, ts)
    ffn = jnp.zeros((ts, D_MODEL), _F32)
    for k in range(TOP_K):
        ffn = ffn + gt_ref[:, k:k + 1] * _from_row_tiles(gbuf.at[slot, k], ts)
    x2 = _layer_norm(DEEPNORM_ALPHA * x1 + ffn, g2_ref[...], b2_ref[...])
    z = jnp.dot(x2.astype(_BF16), gw_ref[...], preferred_element_type=_F32) + gb_ref[...]
    ple = jnp.dot(p_ref[...].astype(_BF16), pw_ref[...], preferred_element_type=_F32)
    x3 = _layer_norm(DEEPNORM_ALPHA * x2 + jax.nn.sigmoid(z) * ple, g3_ref[...], b3_ref[...])
    out_ref[...] = x3


def _combine(dest_flat, yb, x1rt, gates_t, p2d, g2, b2, pw, gw, gb, g3, b3):
    n = dest_flat.shape[0] // TOP_K
    ts = TS_COMBINE
    last = n // ts - 1
    full = lambda shape: pl.BlockSpec(shape, lambda i: (0,) * len(shape))
    return pl.pallas_call(
        _combine_kernel,
        grid=(n // ts,),
        in_specs=[
            pl.BlockSpec((TOP_K * ts,), lambda i: (i,), memory_space=pltpu.SMEM),
            pl.BlockSpec((TOP_K * ts,), lambda i: (jnp.minimum(i + 1, last),), memory_space=pltpu.SMEM),
            pl.BlockSpec(memory_space=pl.ANY),
            pl.BlockSpec((ts * ROW_CHUNKS, LANES), lambda i: (i, 0)),
            pl.BlockSpec((ts, TOP_K), lambda i: (i, 0)),
            pl.BlockSpec((ts, PLE_DIM), lambda i: (i, 0)),
            full(g2.shape), full(b2.shape), full(pw.shape), full(gw.shape), full(gb.shape),
            full(g3.shape), full(b3.shape),
        ],
        out_specs=pl.BlockSpec((ts, D_MODEL), lambda i: (i, 0)),
        out_shape=jax.ShapeDtypeStruct((n, D_MODEL), _F32),
        scratch_shapes=[
            pltpu.VMEM((2, TOP_K, ts * ROW_CHUNKS, LANES), _F32),
            pltpu.SemaphoreType.DMA((2,)),
        ],
        compiler_params=pltpu.CompilerParams(
            dimension_semantics=("arbitrary",),
            vmem_limit_bytes=48 * 1024 * 1024),
        name="combine",
    )(dest_flat, dest_flat, yb, x1rt, gates_t, p2d, g2, b2, pw, gw, gb, g3, b3)


def kernel(x, p, w_in, pool_mix, pool_scale, conv_w, w_out, ln1_g, ln1_b, router_w, router_b,
           w_gate_up, b_gate_up, w_down, b_down, ln2_g, ln2_b, ple_proj, ple_gate_w, ple_gate_b,
           ln3_g, ln3_b):
    assert DEPTH == 1 and x.shape[-1] == D_MODEL
    bsz, seq, d = x.shape
    n = bsz * seq
    assert seq % TS_MIX == 0 and n % TS_DISPATCH == 0 and n % TS_COMBINE == 0
    row = lambda v: v.reshape(1, -1)

    tri = jnp.triu(jnp.ones((TS_MIX, TS_MIX), _BF16))
    x1rt, ri, gates, counts = _mix_route(
        x, w_in[0].astype(_BF16), pool_mix[0].astype(_BF16), row(pool_scale[0]), conv_w[0],
        w_out[0].astype(_BF16), row(ln1_g[0]), row(ln1_b[0]),
        router_w[0].T.astype(_BF16), jnp.broadcast_to(router_b[0][:, None], (N_EXPERTS, LANES)), tri)

    tm = TM_EXPERT
    n_slots = n * TOP_K + N_EXPERTS * tm
    nb = n_slots // tm
    cnt = counts[:, 0]
    padded = ((cnt + tm - 1) // tm) * tm
    pad_end = jnp.cumsum(padded)
    pad_start = (pad_end - padded).astype(jnp.int32)
    n_valid = (pad_end[-1] // tm).astype(jnp.int32)
    tile_start = jnp.arange(nb, dtype=jnp.int32) * tm
    blk_e = jnp.minimum(jnp.sum(tile_start[:, None] >= pad_end[None, :], axis=1), N_EXPERTS - 1)
    blk_e = jnp.where(jnp.arange(nb) < n_valid, blk_e, blk_e[n_valid - 1]).astype(jnp.int32)

    last_tile = jnp.where(padded > 0, pad_end // tm - 1, -1).astype(jnp.int32)
    n_valid = n_valid.reshape(1)

    dest = _make_dest(pad_start, ri).T.reshape(-1)
    xs = _dispatch(last_tile, n_valid, dest, x1rt, n_slots)
    yb = _experts(blk_e, n_valid, xs,
                  w_gate_up[0], b_gate_up[0][:, None, :], w_down[0], b_down[0][:, None, :])
    out = _combine(dest, yb, x1rt, gates[:TOP_K].T, p[0].reshape(n, PLE_DIM),
                   row(ln2_g[0]), row(ln2_b[0]), ple_proj[0].astype(_BF16),
                   ple_gate_w[0].astype(_BF16), row(ple_gate_b[0]), row(ln3_g[0]), row(ln3_b[0]))
    return out.reshape(bsz, seq, d)
```

```python
import functools

import jax
import jax.numpy as jnp
from jax import lax
from jax.experimental import pallas as pl
from jax.experimental.pallas import tpu as pltpu

D_MODEL = 1024
D_POOL = 512
D_CONV = 512
POOL_WINDOWS = (2, 4, 8, 16)
POOL_GROUP_DIM = 128
CONV_WIDTH = 3
D_IN_PROJ = D_POOL + 3 * D_CONV
N_EXPERTS = 32
TOP_K = 4
D_FF = 1024
SWIGLU_LIMIT = 7.0
SWIGLU_ALPHA = 1.702
PLE_DIM = 256
DEPTH = 1
DEEPNORM_ALPHA = (2.0 * DEPTH) ** 0.25
LN_EPS = 1e-5

LANES = 128
SUBLANES = 8
ROW_CHUNKS = D_MODEL // LANES
POOL_HALO = 16
CONV_HALO = 8

TS_MIX = 512
TS_DISPATCH = 1024
TM_EXPERT = 512
TS_COMBINE = 256

_BF16 = jnp.bfloat16
_F32 = jnp.float32


def _layer_norm(h, g, b):
    mu = jnp.mean(h, axis=-1, keepdims=True)
    c = h - mu
    var = jnp.mean(c * c, axis=-1, keepdims=True)
    return c * lax.rsqrt(var + LN_EPS) * g + b


def _to_row_tiles(ref, val):
    rows = val.shape[0]
    for c in range(ROW_CHUNKS):
        ref[pl.ds(c, rows, stride=ROW_CHUNKS), :] = val[:, c * LANES:(c + 1) * LANES]


def _from_row_tiles(ref, rows):
    return jnp.concatenate(
        [ref[pl.ds(c, rows, stride=ROW_CHUNKS), :] for c in range(ROW_CHUNKS)], axis=1)


def _mix_route_kernel(x_ref, w_in_ref, pmix_ref, pscale_ref, convw_ref, w_out_ref, g1_ref, b1_ref,
                      rwt_ref, rb_ref, tri_ref,
                      x1rt_ref, ri_ref, gate_ref, cnt_ref,
                      carry_v, carry_u, cnt_sc):
    b = pl.program_id(0)
    s = pl.program_id(1)
    ts = TS_MIX

    @pl.when(jnp.logical_and(b == 0, s == 0))
    def _():
        cnt_sc[...] = jnp.zeros_like(cnt_sc)

    @pl.when(s == 0)
    def _():
        carry_v[...] = jnp.zeros_like(carry_v)
        carry_u[...] = jnp.zeros_like(carry_u)

    xb = x_ref[0]
    proj = jnp.dot(xb.astype(_BF16), w_in_ref[...], preferred_element_type=_F32)
    vp = proj[:, :D_POOL]
    bg = proj[:, D_POOL:D_POOL + D_CONV]
    cg = proj[:, D_POOL + D_CONV:D_POOL + 2 * D_CONV]
    vc = proj[:, D_POOL + 2 * D_CONV:]

    ext = jnp.concatenate([carry_v[...], vp], axis=0)
    pos = lax.broadcasted_iota(jnp.int32, (ts, 1), 0) + s * ts
    mixed = []
    for g, w in enumerate(POOL_WINDOWS):
        lo, hi = g * POOL_GROUP_DIM, (g + 1) * POOL_GROUP_DIM
        acc = ext[:, lo:hi]
        sh = 1
        while sh < w:
            acc = acc + pltpu.roll(acc, sh, 0)
            sh *= 2
        cnt = jnp.minimum(pos + 1, w).astype(_F32)
        d = acc[POOL_HALO:] / cnt - vp[:, lo:hi]
        yg = jnp.dot(d.astype(_BF16), pmix_ref[g], preferred_element_type=_F32)
        mixed.append(yg * pscale_ref[:, lo:hi])
    carry_v[...] = vp[ts - POOL_HALO:]

    u = cg * vc
    extu = jnp.concatenate([carry_u[...], u], axis=0)
    u1 = pltpu.roll(extu, 1, 0)[CONV_HALO:]
    u2 = pltpu.roll(extu, 2, 0)[CONV_HALO:]
    yc = bg * (convw_ref[0:1, :] * u2 + convw_ref[1:2, :] * u1 + convw_ref[2:3, :] * u)
    carry_u[...] = u[ts - CONV_HALO:]
    mixed.append(yc)

    mix_in = jnp.concatenate(mixed, axis=1).astype(_BF16)
    mix = jnp.dot(mix_in, w_out_ref[...], preferred_element_type=_F32)
    x1 = _layer_norm(DEEPNORM_ALPHA * xb + mix, g1_ref[...], b1_ref[...])
    _to_row_tiles(x1rt_ref, x1)

    logits = lax.dot_general(rwt_ref[...], x1.astype(_BF16), (((1,), (1,)), ((), ())),
                             preferred_element_type=_F32) + rb_ref[:, 0:1]
    eidx = lax.broadcasted_iota(jnp.int32, (N_EXPERTS, ts), 0).astype(_F32)
    vals, sels = [], []
    work = logits
    for k in range(TOP_K):
        m = jnp.max(work, axis=0, keepdims=True)
        first = jnp.min(jnp.where(work == m, eidx, float(N_EXPERTS)), axis=0, keepdims=True)
        sel = eidx == first
        work = jnp.where(sel, -jnp.inf, work)
        vals.append(m)
        sels.append(sel)
        ri_ref[k:k + 1, :] = first.astype(jnp.int32)
    exps = [jnp.exp(v - vals[0]) for v in vals]
    denom = exps[0] + exps[1] + exps[2] + exps[3]
    for k in range(TOP_K):
        gate_ref[k:k + 1, :] = exps[k] / denom
    gate_ref[TOP_K:, :] = jnp.zeros((SUBLANES - TOP_K, ts), _F32)

    chosen = jnp.zeros((N_EXPERTS, ts), _F32)
    for sel in sels:
        chosen = chosen + sel.astype(_F32)
    incl = jnp.dot(chosen.astype(_BF16), tri_ref[...], preferred_element_type=_F32)
    base = cnt_sc[:, 0:1] + (incl - chosen)
    for k in range(TOP_K):
        rank = jnp.sum(jnp.where(sels[k], base, 0.0), axis=0, keepdims=True)
        ri_ref[TOP_K + k:TOP_K + k + 1, :] = rank.astype(jnp.int32)
    total = cnt_sc[...] + jnp.sum(chosen, axis=1, keepdims=True)
    cnt_sc[...] = total
    cnt_ref[...] = total.astype(jnp.int32)


def _mix_route(x, w_in, pmix, pscale, convw, w_out, g1, b1, rwt, rb, tri):
    bsz, seq, d = x.shape
    n = bsz * seq
    ns = seq // TS_MIX
    full = lambda shape: pl.BlockSpec(shape, lambda b, s: (0,) * len(shape))
    return pl.pallas_call(
        _mix_route_kernel,
        grid=(bsz, ns),
        in_specs=[
            pl.BlockSpec((1, TS_MIX, d), lambda b, s: (b, s, 0)),
            full(w_in.shape), full(pmix.shape), full(pscale.shape), full(convw.shape),
            full(w_out.shape), full(g1.shape), full(b1.shape), full(rwt.shape), full(rb.shape),
            full(tri.shape),
        ],
        out_specs=[
            pl.BlockSpec((TS_MIX * ROW_CHUNKS, LANES), lambda b, s: (b * ns + s, 0)),
            pl.BlockSpec((2 * TOP_K, TS_MIX), lambda b, s: (0, b * ns + s)),
            pl.BlockSpec((SUBLANES, TS_MIX), lambda b, s: (0, b * ns + s)),
            pl.BlockSpec((N_EXPERTS, LANES), lambda b, s: (0, 0)),
        ],
        out_shape=[
            jax.ShapeDtypeStruct((n * ROW_CHUNKS, LANES), _F32),
            jax.ShapeDtypeStruct((2 * TOP_K, n), jnp.int32),
            jax.ShapeDtypeStruct((SUBLANES, n), _F32),
            jax.ShapeDtypeStruct((N_EXPERTS, LANES), jnp.int32),
        ],
        scratch_shapes=[
            pltpu.VMEM((POOL_HALO, D_POOL), _F32),
            pltpu.VMEM((CONV_HALO, D_CONV), _F32),
            pltpu.VMEM((N_EXPERTS, LANES), _F32),
        ],
        compiler_params=pltpu.CompilerParams(
            dimension_semantics=("arbitrary", "arbitrary"),
            vmem_limit_bytes=48 * 1024 * 1024),
        name="mix_route",
    )(x, w_in, pmix, pscale, convw, w_out, g1, b1, rwt, rb, tri)


def _make_dest_kernel(start_ref, ri_ref, dest_ref):
    ei = ri_ref[0:TOP_K, :]
    acc = ri_ref[TOP_K:, :]
    for e in range(N_EXPERTS):
        acc = acc + jnp.where(ei == e, start_ref[e], 0)
    dest_ref[...] = acc


def _make_dest(pad_start, ri):
    n = ri.shape[1]
    return pl.pallas_call(
        _make_dest_kernel,
        in_specs=[pl.BlockSpec(memory_space=pltpu.SMEM),
                  pl.BlockSpec((2 * TOP_K, n), lambda: (0, 0))],
        out_specs=pl.BlockSpec((TOP_K, n), lambda: (0, 0)),
        out_shape=jax.ShapeDtypeStruct((TOP_K, n), jnp.int32),
        name="make_dest",
    )(pad_start, ri)


def _dispatch_kernel(last_ref, nv_ref, dest_ref, src_ref, out_ref, zbuf, sem, zsem):
    ts = TS_DISPATCH
    tile_rows = TM_EXPERT * ROW_CHUNKS
    n_tiles = out_ref.shape[0] // tile_rows

    @pl.when(pl.program_id(0) == 0)
    def _():
        zbuf[...] = jnp.zeros_like(zbuf)

        def zero_tile(tile):
            return pltpu.make_async_copy(
                zbuf, out_ref.at[pl.ds(pl.multiple_of(tile * tile_rows, tile_rows), tile_rows)], zsem)

        def each_tile(fn):
            for e in range(N_EXPERTS):
                @pl.when(last_ref[e] >= 0)
                def _():
                    fn(zero_tile(last_ref[e]))

            def tail(j, carry):
                fn(zero_tile(j))
                return carry
            lax.fori_loop(nv_ref[0], n_tiles, tail, 0)

        each_tile(lambda c: c.start())
        each_tile(lambda c: c.wait())

    def row_copy(src_row, dst_row):
        return pltpu.make_async_copy(
            src_ref.at[pl.ds(pl.multiple_of(src_row * ROW_CHUNKS, ROW_CHUNKS), ROW_CHUNKS)],
            out_ref.at[pl.ds(pl.multiple_of(dst_row * ROW_CHUNKS, ROW_CHUNKS), ROW_CHUNKS)],
            sem)

    def body(t, carry):
        for k in range(TOP_K):
            row_copy(t, dest_ref[t * TOP_K + k]).start(priority=k % 2)
        return carry

    lax.fori_loop(0, ts, body, 0, unroll=8)
    for k in range(TOP_K):
        pltpu.make_async_copy(src_ref, out_ref.at[pl.ds(0, ts * ROW_CHUNKS)], sem).wait()


def _dispatch(last_tile, n_valid, dest_flat, x1rt, n_slots):
    n = dest_flat.shape[0] // TOP_K
    grid_spec = pltpu.PrefetchScalarGridSpec(
        num_scalar_prefetch=2,
        grid=(n // TS_DISPATCH,),
        in_specs=[
            pl.BlockSpec((TOP_K * TS_DISPATCH,), lambda i, lt, nv: (i,), memory_space=pltpu.SMEM),
            pl.BlockSpec((TS_DISPATCH * ROW_CHUNKS, LANES), lambda i, lt, nv: (i, 0)),
        ],
        out_specs=pl.BlockSpec(memory_space=pl.ANY),
        scratch_shapes=[
            pltpu.VMEM((TM_EXPERT * ROW_CHUNKS, LANES), _F32),
            pltpu.SemaphoreType.DMA(()),
            pltpu.SemaphoreType.DMA(()),
        ],
    )
    return pl.pallas_call(
        _dispatch_kernel,
        grid_spec=grid_spec,
        out_shape=jax.ShapeDtypeStruct((n_slots * ROW_CHUNKS, LANES), _F32),
        compiler_params=pltpu.CompilerParams(dimension_semantics=("arbitrary",)),
        name="dispatch",
    )(last_tile, n_valid, dest_flat, x1rt)


def _experts_kernel(be_ref, nv_ref, par_ref, nxt_ref, xs_ref, wgu_hbm, bgu_ref, wdn_hbm, bdn_ref, out_ref,
                    wgu_f32, wdn_f32, wgu_bf, wdn_bf, sems):
    j = pl.program_id(0)
    tm = TM_EXPERT
    valid = j < nv_ref[0]
    new_expert = jnp.logical_or(j == 0, be_ref[j] != be_ref[jnp.maximum(j - 1, 0)])

    def weight_copies(e, s):
        return (pltpu.make_async_copy(wgu_hbm.at[e], wgu_f32.at[s], sems.at[0, s]),
                pltpu.make_async_copy(wdn_hbm.at[e], wdn_f32.at[s], sems.at[1, s]))

    @pl.when(j == 0)
    def _():
        for c in weight_copies(be_ref[0], par_ref[0]):
            c.start()

    @pl.when(jnp.logical_and(valid, new_expert))
    def _():
        s = par_ref[j]
        for c in weight_copies(be_ref[j], s):
            c.wait()

        @pl.when(nxt_ref[j] >= 0)
        def _():
            for c in weight_copies(nxt_ref[j], 1 - s):
                c.start()

        wgu_bf[...] = wgu_f32[s].astype(_BF16)
        wdn_bf[...] = wdn_f32[s].astype(_BF16)

    @pl.when(valid)
    def _():
        x = _from_row_tiles(xs_ref, tm).astype(_BF16)
        gu = jnp.dot(x, wgu_bf[...], preferred_element_type=_F32) + bgu_ref[0]
        gate = jnp.minimum(gu[:, :D_FF], SWIGLU_LIMIT)
        up = jnp.clip(gu[:, D_FF:], -SWIGLU_LIMIT, SWIGLU_LIMIT)
        glu = gate * jax.nn.sigmoid(SWIGLU_ALPHA * gate)
        act = ((up + 1.0) * glu).astype(_BF16)
        y = jnp.dot(act, wdn_bf[...], preferred_element_type=_F32) + bdn_ref[0]
        _to_row_tiles(out_ref, y)

    @pl.when(j >= nv_ref[0])
    def _():
        out_ref[...] = jnp.zeros_like(out_ref)


def _experts(blk_e, n_valid, slot_par, next_e, xs, wgu, bgu, wdn, bdn):
    n_slots = xs.shape[0] // ROW_CHUNKS
    nb = n_slots // TM_EXPERT
    rows = TM_EXPERT * ROW_CHUNKS
    grid_spec = pltpu.PrefetchScalarGridSpec(
        num_scalar_prefetch=4,
        grid=(nb,),
        in_specs=[
            pl.BlockSpec((rows, LANES), lambda j, be, nv, par, nxt: (jnp.minimum(j, nv[0] - 1), 0)),
            pl.BlockSpec(memory_space=pl.ANY),
            pl.BlockSpec((1, 1, 2 * D_FF), lambda j, be, nv, par, nxt: (be[j], 0, 0)),
            pl.BlockSpec(memory_space=pl.ANY),
            pl.BlockSpec((1, 1, D_MODEL), lambda j, be, nv, par, nxt: (be[j], 0, 0)),
        ],
        out_specs=pl.BlockSpec((rows, LANES), lambda j, be, nv, par, nxt: (j, 0)),
        scratch_shapes=[
            pltpu.VMEM((2, D_MODEL, 2 * D_FF), _F32),
            pltpu.VMEM((2, D_FF, D_MODEL), _F32),
            pltpu.VMEM((D_MODEL, 2 * D_FF), _BF16),
            pltpu.VMEM((D_FF, D_MODEL), _BF16),
            pltpu.SemaphoreType.DMA((2, 2)),
        ],
    )
    return pl.pallas_call(
        _experts_kernel,
        grid_spec=grid_spec,
        out_shape=jax.ShapeDtypeStruct(xs.shape, _F32),
        compiler_params=pltpu.CompilerParams(
            dimension_semantics=("arbitrary",),
            vmem_limit_bytes=60 * 1024 * 1024),
        name="experts",
    )(blk_e, n_valid, slot_par, next_e, xs, wgu, bgu, wdn, bdn)


def _combine_kernel(dcur_ref, dnext_ref, yb_ref, x1rt_ref, gt_ref, p_ref, g2_ref, b2_ref, pw_ref, gw_ref,
                    gb_ref, g3_ref, b3_ref, out_ref, gbuf, sems):
    ts = TS_COMBINE
    i = pl.program_id(0)
    slot = i % 2

    def start_gather(d_ref, s):
        def body(t, carry):
            for k in range(TOP_K):
                src_row = d_ref[t * TOP_K + k]
                pltpu.make_async_copy(
                    yb_ref.at[pl.ds(pl.multiple_of(src_row * ROW_CHUNKS, ROW_CHUNKS), ROW_CHUNKS)],
                    gbuf.at[s, k, pl.ds(pl.multiple_of(t * ROW_CHUNKS, ROW_CHUNKS), ROW_CHUNKS)],
                    sems.at[s]).start(priority=k % 2)
            return carry
        lax.fori_loop(0, ts, body, 0, unroll=8)

    @pl.when(i == 0)
    def _():
        start_gather(dcur_ref, 0)

    @pl.when(i + 1 < pl.num_programs(0))
    def _():
        start_gather(dnext_ref, 1 - slot)

    for k in range(TOP_K):
        pltpu.make_async_copy(yb_ref.at[pl.ds(0, ts * ROW_CHUNKS)], gbuf.at[slot, k], sems.at[slot]).wait()

    x1 = _from_row_tiles(x1rt_ref, ts)
    ffn = jnp.zeros((ts, D_MODEL), _F32)
    for k in range(TOP_K):
        ffn = ffn + gt_ref[:, k:k + 1] * _from_row_tiles(gbuf.at[slot, k], ts)
    x2 = _layer_norm(DEEPNORM_ALPHA * x1 + ffn, g2_ref[...], b2_ref[...])
    z = jnp.dot(x2.astype(_BF16), gw_ref[...], preferred_element_type=_F32) + gb_ref[...]
    ple = jnp.dot(p_ref[...].astype(_BF16), pw_ref[...], preferred_element_type=_F32)
    x3 = _layer_norm(DEEPNORM_ALPHA * x2 + jax.nn.sigmoid(z) * ple, g3_ref[...], b3_ref[...])
    out_ref[...] = x3


def _combine(dest_flat, yb, x1rt, gates_t, p2d, g2, b2, pw, gw, gb, g3, b3):
    n = dest_flat.shape[0] // TOP_K
    ts = TS_COMBINE
    last = n // ts - 1
    full = lambda shape: pl.BlockSpec(shape, lambda i: (0,) * len(shape))
    return pl.pallas_call(
        _combine_kernel,
        grid=(n // ts,),
        in_specs=[
            pl.BlockSpec((TOP_K * ts,), lambda i: (i,), memory_space=pltpu.SMEM),
            pl.BlockSpec((TOP_K * ts,), lambda i: (jnp.minimum(i + 1, last),), memory_space=pltpu.SMEM),
            pl.BlockSpec(memory_space=pl.ANY),
            pl.BlockSpec((ts * ROW_CHUNKS, LANES), lambda i: (i, 0)),
            pl.BlockSpec((ts, TOP_K), lambda i: (i, 0)),
            pl.BlockSpec((ts, PLE_DIM), lambda i: (i, 0)),
            full(g2.shape), full(b2.shape), full(pw.shape), full(gw.shape), full(gb.shape),
            full(g3.shape), full(b3.shape),
        ],
        out_specs=pl.BlockSpec((ts, D_MODEL), lambda i: (i, 0)),
        out_shape=jax.ShapeDtypeStruct((n, D_MODEL), _F32),
        scratch_shapes=[
            pltpu.VMEM((2, TOP_K, ts * ROW_CHUNKS, LANES), _F32),
            pltpu.SemaphoreType.DMA((2,)),
        ],
        compiler_params=pltpu.CompilerParams(
            dimension_semantics=("arbitrary",),
            vmem_limit_bytes=48 * 1024 * 1024),
        name="combine",
    )(dest_flat, dest_flat, yb, x1rt, gates_t, p2d, g2, b2, pw, gw, gb, g3, b3)


def kernel(x, p, w_in, pool_mix, pool_scale, conv_w, w_out, ln1_g, ln1_b, router_w, router_b,
           w_gate_up, b_gate_up, w_down, b_down, ln2_g, ln2_b, ple_proj, ple_gate_w, ple_gate_b,
           ln3_g, ln3_b):
    assert DEPTH == 1 and x.shape[-1] == D_MODEL
    bsz, seq, d = x.shape
    n = bsz * seq
    assert seq % TS_MIX == 0 and n % TS_DISPATCH == 0 and n % TS_COMBINE == 0
    row = lambda v: v.reshape(1, -1)

    tri = jnp.triu(jnp.ones((TS_MIX, TS_MIX), _BF16))
    x1rt, ri, gates, counts = _mix_route(
        x, w_in[0].astype(_BF16), pool_mix[0].astype(_BF16), row(pool_scale[0]), conv_w[0],
        w_out[0].astype(_BF16), row(ln1_g[0]), row(ln1_b[0]),
        router_w[0].T.astype(_BF16), jnp.broadcast_to(router_b[0][:, None], (N_EXPERTS, LANES)), tri)

    tm = TM_EXPERT
    n_slots = n * TOP_K + N_EXPERTS * tm
    nb = n_slots // tm
    cnt = counts[:, 0]
    padded = ((cnt + tm - 1) // tm) * tm
    pad_end = jnp.cumsum(padded)
    pad_start = (pad_end - padded).astype(jnp.int32)
    n_valid = (pad_end[-1] // tm).astype(jnp.int32)
    tile_start = jnp.arange(nb, dtype=jnp.int32) * tm
    blk_e = jnp.minimum(jnp.sum(tile_start[:, None] >= pad_end[None, :], axis=1), N_EXPERTS - 1)
    blk_e = jnp.where(jnp.arange(nb) < n_valid, blk_e, blk_e[n_valid - 1]).astype(jnp.int32)

    last_tile = jnp.where(padded > 0, pad_end // tm - 1, -1).astype(jnp.int32)
    e_ids = jnp.arange(N_EXPERTS, dtype=jnp.int32)
    later = jnp.logical_and(e_ids[None, :] > e_ids[:, None], (padded > 0)[None, :])
    next_nonempty = jnp.min(jnp.where(later, e_ids[None, :], N_EXPERTS), axis=1)
    next_nonempty = jnp.where(next_nonempty == N_EXPERTS, -1, next_nonempty).astype(jnp.int32)
    next_e = next_nonempty[blk_e]
    switched = jnp.concatenate([jnp.zeros((1,), jnp.int32), (blk_e[1:] != blk_e[:-1]).astype(jnp.int32)])
    slot_par = (jnp.cumsum(switched) % 2).astype(jnp.int32)
    n_valid = n_valid.reshape(1)

    dest = _make_dest(pad_start, ri).T.reshape(-1)
    xs = _dispatch(last_tile, n_valid, dest, x1rt, n_slots)
    yb = _experts(blk_e, n_valid, slot_par, next_e, xs,
                  w_gate_up[0], b_gate_up[0][:, None, :], w_down[0], b_down[0][:, None, :])
    out = _combine(dest, yb, x1rt, gates[:TOP_K].T, p[0].reshape(n, PLE_DIM),
                   row(ln2_g[0]), row(ln2_b[0]), ple_proj[0].astype(_BF16),
                   ple_gate_w[0].astype(_BF16), row(ple_gate_b[0]), row(ln3_g[0]), row(ln3_b[0]))
    return out.reshape(bsz, seq, d)
```

```python
import functools

import jax
import jax.numpy as jnp
from jax import lax
from jax.experimental import pallas as pl
from jax.experimental.pallas import tpu as pltpu

D_MODEL = 1024
D_POOL = 512
D_CONV = 512
POOL_WINDOWS = (2, 4, 8, 16)
POOL_GROUP_DIM = 128
CONV_WIDTH = 3
D_IN_PROJ = D_POOL + 3 * D_CONV
N_EXPERTS = 32
TOP_K = 4
D_FF = 1024
SWIGLU_LIMIT = 7.0
SWIGLU_ALPHA = 1.702
PLE_DIM = 256
DEPTH = 1
DEEPNORM_ALPHA = (2.0 * DEPTH) ** 0.25
LN_EPS = 1e-5

LANES = 128
SUBLANES = 8
ROW_CHUNKS = D_MODEL // LANES
POOL_HALO = 16
CONV_HALO = 8

TS_MIX = 512
TS_DISPATCH = 1024
TM_EXPERT = 512
TS_COMBINE = 256

_BF16 = jnp.bfloat16
_F32 = jnp.float32


def _layer_norm(h, g, b):
    mu = jnp.mean(h, axis=-1, keepdims=True)
    c = h - mu
    var = jnp.mean(c * c, axis=-1, keepdims=True)
    return c * lax.rsqrt(var + LN_EPS) * g + b


def _row(ref, r):
    return ref.at[pl.ds(r, 1)]


def _mix_route_kernel(x_ref, w_in_ref, pmix_ref, pscale_ref, convw_ref, w_out_ref, g1_ref, b1_ref,
                      rwt_ref, rb_ref, tri_ref,
                      x1rt_ref, ri_ref, gate_ref, cnt_ref,
                      carry_v, carry_u, cnt_sc):
    b = pl.program_id(0)
    s = pl.program_id(1)
    ts = TS_MIX

    @pl.when(jnp.logical_and(b == 0, s == 0))
    def _():
        cnt_sc[...] = jnp.zeros_like(cnt_sc)

    @pl.when(s == 0)
    def _():
        carry_v[...] = jnp.zeros_like(carry_v)
        carry_u[...] = jnp.zeros_like(carry_u)

    xb = x_ref[0]
    proj = jnp.dot(xb.astype(_BF16), w_in_ref[...], preferred_element_type=_F32)
    vp = proj[:, :D_POOL]
    bg = proj[:, D_POOL:D_POOL + D_CONV]
    cg = proj[:, D_POOL + D_CONV:D_POOL + 2 * D_CONV]
    vc = proj[:, D_POOL + 2 * D_CONV:]

    ext = jnp.concatenate([carry_v[...], vp], axis=0)
    pos = lax.broadcasted_iota(jnp.int32, (ts, 1), 0) + s * ts
    mixed = []
    for g, w in enumerate(POOL_WINDOWS):
        lo, hi = g * POOL_GROUP_DIM, (g + 1) * POOL_GROUP_DIM
        acc = ext[:, lo:hi]
        sh = 1
        while sh < w:
            acc = acc + pltpu.roll(acc, sh, 0)
            sh *= 2
        cnt = jnp.minimum(pos + 1, w).astype(_F32)
        d = acc[POOL_HALO:] / cnt - vp[:, lo:hi]
        yg = jnp.dot(d.astype(_BF16), pmix_ref[g], preferred_element_type=_F32)
        mixed.append(yg * pscale_ref[:, lo:hi])
    carry_v[...] = vp[ts - POOL_HALO:]

    u = cg * vc
    extu = jnp.concatenate([carry_u[...], u], axis=0)
    u1 = pltpu.roll(extu, 1, 0)[CONV_HALO:]
    u2 = pltpu.roll(extu, 2, 0)[CONV_HALO:]
    yc = bg * (convw_ref[0:1, :] * u2 + convw_ref[1:2, :] * u1 + convw_ref[2:3, :] * u)
    carry_u[...] = u[ts - CONV_HALO:]
    mixed.append(yc)

    mix_in = jnp.concatenate(mixed, axis=1).astype(_BF16)
    mix = jnp.dot(mix_in, w_out_ref[...], preferred_element_type=_F32)
    x1 = _layer_norm(DEEPNORM_ALPHA * xb + mix, g1_ref[...], b1_ref[...])
    x1rt_ref[...] = x1

    logits = lax.dot_general(rwt_ref[...], x1.astype(_BF16), (((1,), (1,)), ((), ())),
                             preferred_element_type=_F32) + rb_ref[:, 0:1]
    eidx = lax.broadcasted_iota(jnp.int32, (N_EXPERTS, ts), 0).astype(_F32)
    vals, sels = [], []
    work = logits
    for k in range(TOP_K):
        m = jnp.max(work, axis=0, keepdims=True)
        first = jnp.min(jnp.where(work == m, eidx, float(N_EXPERTS)), axis=0, keepdims=True)
        sel = eidx == first
        work = jnp.where(sel, -jnp.inf, work)
        vals.append(m)
        sels.append(sel)
        ri_ref[k:k + 1, :] = first.astype(jnp.int32)
    exps = [jnp.exp(v - vals[0]) for v in vals]
    denom = exps[0] + exps[1] + exps[2] + exps[3]
    for k in range(TOP_K):
        gate_ref[k:k + 1, :] = exps[k] / denom
    gate_ref[TOP_K:, :] = jnp.zeros((SUBLANES - TOP_K, ts), _F32)

    chosen = jnp.zeros((N_EXPERTS, ts), _F32)
    for sel in sels:
        chosen = chosen + sel.astype(_F32)
    incl = jnp.dot(chosen.astype(_BF16), tri_ref[...], preferred_element_type=_F32)
    base = cnt_sc[:, 0:1] + (incl - chosen)
    for k in range(TOP_K):
        rank = jnp.sum(jnp.where(sels[k], base, 0.0), axis=0, keepdims=True)
        ri_ref[TOP_K + k:TOP_K + k + 1, :] = rank.astype(jnp.int32)
    total = cnt_sc[...] + jnp.sum(chosen, axis=1, keepdims=True)
    cnt_sc[...] = total
    cnt_ref[...] = total.astype(jnp.int32)


def _mix_route(x, w_in, pmix, pscale, convw, w_out, g1, b1, rwt, rb, tri):
    bsz, seq, d = x.shape
    n = bsz * seq
    ns = seq // TS_MIX
    full = lambda shape: pl.BlockSpec(shape, lambda b, s: (0,) * len(shape))
    return pl.pallas_call(
        _mix_route_kernel,
        grid=(bsz, ns),
        in_specs=[
            pl.BlockSpec((1, TS_MIX, d), lambda b, s: (b, s, 0)),
            full(w_in.shape), full(pmix.shape), full(pscale.shape), full(convw.shape),
            full(w_out.shape), full(g1.shape), full(b1.shape), full(rwt.shape), full(rb.shape),
            full(tri.shape),
        ],
        out_specs=[
            pl.BlockSpec((TS_MIX, D_MODEL), lambda b, s: (b * ns + s, 0)),
            pl.BlockSpec((2 * TOP_K, TS_MIX), lambda b, s: (0, b * ns + s)),
            pl.BlockSpec((SUBLANES, TS_MIX), lambda b, s: (0, b * ns + s)),
            pl.BlockSpec((N_EXPERTS, LANES), lambda b, s: (0, 0)),
        ],
        out_shape=[
            jax.ShapeDtypeStruct((n, D_MODEL), _F32),
            jax.ShapeDtypeStruct((2 * TOP_K, n), jnp.int32),
            jax.ShapeDtypeStruct((SUBLANES, n), _F32),
            jax.ShapeDtypeStruct((N_EXPERTS, LANES), jnp.int32),
        ],
        scratch_shapes=[
            pltpu.VMEM((POOL_HALO, D_POOL), _F32),
            pltpu.VMEM((CONV_HALO, D_CONV), _F32),
            pltpu.VMEM((N_EXPERTS, LANES), _F32),
        ],
        compiler_params=pltpu.CompilerParams(
            dimension_semantics=("arbitrary", "arbitrary"),
            vmem_limit_bytes=48 * 1024 * 1024),
        name="mix_route",
    )(x, w_in, pmix, pscale, convw, w_out, g1, b1, rwt, rb, tri)


def _make_dest_kernel(start_ref, ri_ref, dest_ref):
    ei = ri_ref[0:TOP_K, :]
    acc = ri_ref[TOP_K:, :]
    for e in range(N_EXPERTS):
        acc = acc + jnp.where(ei == e, start_ref[e], 0)
    dest_ref[...] = acc


def _make_dest(pad_start, ri):
    n = ri.shape[1]
    return pl.pallas_call(
        _make_dest_kernel,
        in_specs=[pl.BlockSpec(memory_space=pltpu.SMEM),
                  pl.BlockSpec((2 * TOP_K, n), lambda: (0, 0))],
        out_specs=pl.BlockSpec((TOP_K, n), lambda: (0, 0)),
        out_shape=jax.ShapeDtypeStruct((TOP_K, n), jnp.int32),
        name="make_dest",
    )(pad_start, ri)


def _dispatch_kernel(last_ref, nv_ref, dest_ref, src_ref, out_ref, zbuf, sem, zsem):
    ts = TS_DISPATCH
    tile_rows = TM_EXPERT
    n_tiles = out_ref.shape[0] // tile_rows

    @pl.when(pl.program_id(0) == 0)
    def _():
        zbuf[...] = jnp.zeros_like(zbuf)

        def zero_tile(tile):
            return pltpu.make_async_copy(
                zbuf, out_ref.at[pl.ds(pl.multiple_of(tile * tile_rows, tile_rows), tile_rows)], zsem)

        def each_tile(fn):
            for e in range(N_EXPERTS):
                @pl.when(last_ref[e] >= 0)
                def _():
                    fn(zero_tile(last_ref[e]))

            def tail(j, carry):
                fn(zero_tile(j))
                return carry
            lax.fori_loop(nv_ref[0], n_tiles, tail, 0)

        each_tile(lambda c: c.start())
        each_tile(lambda c: c.wait())

    def body(t, carry):
        for k in range(TOP_K):
            pltpu.make_async_copy(_row(src_ref, t), _row(out_ref, dest_ref[t * TOP_K + k]),
                                  sem).start(priority=k % 2)
        return carry

    lax.fori_loop(0, ts, body, 0, unroll=8)
    for k in range(TOP_K):
        pltpu.make_async_copy(src_ref, out_ref.at[pl.ds(0, ts)], sem).wait()


def _dispatch(last_tile, n_valid, dest_flat, x1rt, n_slots):
    n = dest_flat.shape[0] // TOP_K
    grid_spec = pltpu.PrefetchScalarGridSpec(
        num_scalar_prefetch=2,
        grid=(n // TS_DISPATCH,),
        in_specs=[
            pl.BlockSpec((TOP_K * TS_DISPATCH,), lambda i, lt, nv: (i,), memory_space=pltpu.SMEM),
            pl.BlockSpec((TS_DISPATCH, D_MODEL), lambda i, lt, nv: (i, 0)),
        ],
        out_specs=pl.BlockSpec(memory_space=pl.ANY),
        scratch_shapes=[
            pltpu.VMEM((TM_EXPERT, D_MODEL), _F32),
            pltpu.SemaphoreType.DMA(()),
            pltpu.SemaphoreType.DMA(()),
        ],
    )
    return pl.pallas_call(
        _dispatch_kernel,
        grid_spec=grid_spec,
        out_shape=jax.ShapeDtypeStruct((n_slots, D_MODEL), _F32),
        compiler_params=pltpu.CompilerParams(dimension_semantics=("arbitrary",)),
        name="dispatch",
    )(last_tile, n_valid, dest_flat, x1rt)


def _experts_kernel(be_ref, nv_ref, par_ref, nxt_ref, xs_ref, wgu_hbm, bgu_ref, wdn_hbm, bdn_ref, out_ref,
                    wgu_f32, wdn_f32, wgu_bf, wdn_bf, sems):
    j = pl.program_id(0)
    tm = TM_EXPERT
    valid = j < nv_ref[0]
    new_expert = jnp.logical_or(j == 0, be_ref[j] != be_ref[jnp.maximum(j - 1, 0)])

    def weight_copies(e, s):
        return (pltpu.make_async_copy(wgu_hbm.at[e], wgu_f32.at[s], sems.at[0, s]),
                pltpu.make_async_copy(wdn_hbm.at[e], wdn_f32.at[s], sems.at[1, s]))

    @pl.when(j == 0)
    def _():
        for c in weight_copies(be_ref[0], par_ref[0]):
            c.start()

    @pl.when(jnp.logical_and(valid, new_expert))
    def _():
        s = par_ref[j]
        for c in weight_copies(be_ref[j], s):
            c.wait()

        @pl.when(nxt_ref[j] >= 0)
        def _():
            for c in weight_copies(nxt_ref[j], 1 - s):
                c.start()

        wgu_bf[...] = wgu_f32[s].astype(_BF16)
        wdn_bf[...] = wdn_f32[s].astype(_BF16)

    @pl.when(valid)
    def _():
        x = xs_ref[...].astype(_BF16)
        gu = jnp.dot(x, wgu_bf[...], preferred_element_type=_F32) + bgu_ref[0]
        gate = jnp.minimum(gu[:, :D_FF], SWIGLU_LIMIT)
        up = jnp.clip(gu[:, D_FF:], -SWIGLU_LIMIT, SWIGLU_LIMIT)
        glu = gate * jax.nn.sigmoid(SWIGLU_ALPHA * gate)
        act = ((up + 1.0) * glu).astype(_BF16)
        y = jnp.dot(act, wdn_bf[...], preferred_element_type=_F32) + bdn_ref[0]
        out_ref[...] = y

    @pl.when(j >= nv_ref[0])
    def _():
        out_ref[...] = jnp.zeros_like(out_ref)


def _experts(blk_e, n_valid, slot_par, next_e, xs, wgu, bgu, wdn, bdn):
    n_slots = xs.shape[0]
    nb = n_slots // TM_EXPERT
    grid_spec = pltpu.PrefetchScalarGridSpec(
        num_scalar_prefetch=4,
        grid=(nb,),
        in_specs=[
            pl.BlockSpec((TM_EXPERT, D_MODEL), lambda j, be, nv, par, nxt: (jnp.minimum(j, nv[0] - 1), 0)),
            pl.BlockSpec(memory_space=pl.ANY),
            pl.BlockSpec((1, 1, 2 * D_FF), lambda j, be, nv, par, nxt: (be[j], 0, 0)),
            pl.BlockSpec(memory_space=pl.ANY),
            pl.BlockSpec((1, 1, D_MODEL), lambda j, be, nv, par, nxt: (be[j], 0, 0)),
        ],
        out_specs=pl.BlockSpec((TM_EXPERT, D_MODEL), lambda j, be, nv, par, nxt: (j, 0)),
        scratch_shapes=[
            pltpu.VMEM((2, D_MODEL, 2 * D_FF), _F32),
            pltpu.VMEM((2, D_FF, D_MODEL), _F32),
            pltpu.VMEM((D_MODEL, 2 * D_FF), _BF16),
            pltpu.VMEM((D_FF, D_MODEL), _BF16),
            pltpu.SemaphoreType.DMA((2, 2)),
        ],
    )
    return pl.pallas_call(
        _experts_kernel,
        grid_spec=grid_spec,
        out_shape=jax.ShapeDtypeStruct(xs.shape, _F32),
        compiler_params=pltpu.CompilerParams(
            dimension_semantics=("arbitrary",),
            vmem_limit_bytes=60 * 1024 * 1024),
        name="experts",
    )(blk_e, n_valid, slot_par, next_e, xs, wgu, bgu, wdn, bdn)


def _combine_kernel(dcur_ref, dnext_ref, yb_ref, x1rt_ref, gt_ref, p_ref, g2_ref, b2_ref, pw_ref, gw_ref,
                    gb_ref, g3_ref, b3_ref, out_ref, gbuf, sems):
    ts = TS_COMBINE
    i = pl.program_id(0)
    slot = i % 2

    def start_gather(d_ref, s):
        def body(t, carry):
            for k in range(TOP_K):
                pltpu.make_async_copy(_row(yb_ref, d_ref[t * TOP_K + k]), _row(gbuf.at[s, k], t),
                                      sems.at[s]).start(priority=k % 2)
            return carry
        lax.fori_loop(0, ts, body, 0, unroll=8)

    @pl.when(i == 0)
    def _():
        start_gather(dcur_ref, 0)

    @pl.when(i + 1 < pl.num_programs(0))
    def _():
        start_gather(dnext_ref, 1 - slot)

    for k in range(TOP_K):
        pltpu.make_async_copy(yb_ref.at[pl.ds(0, ts)], gbuf.at[slot, k], sems.at[slot]).wait()

    x1 = x1rt_ref[...]
    ffn = jnp.zeros((ts, D_MODEL), _F32)
    for k in range(TOP_K):
        ffn = ffn + gt_ref[:, k:k + 1] * gbuf[slot, k]
    x2 = _layer_norm(DEEPNORM_ALPHA * x1 + ffn, g2_ref[...], b2_ref[...])
    z = jnp.dot(x2.astype(_BF16), gw_ref[...], preferred_element_type=_F32) + gb_ref[...]
    ple = jnp.dot(p_ref[...].astype(_BF16), pw_ref[...], preferred_element_type=_F32)
    x3 = _layer_norm(DEEPNORM_ALPHA * x2 + jax.nn.sigmoid(z) * ple, g3_ref[...], b3_ref[...])
    out_ref[...] = x3


def _combine(dest_flat, yb, x1rt, gates_t, p2d, g2, b2, pw, gw, gb, g3, b3):
    n = dest_flat.shape[0] // TOP_K
    ts = TS_COMBINE
    last = n // ts - 1
    full = lambda shape: pl.BlockSpec(shape, lambda i: (0,) * len(shape))
    return pl.pallas_call(
        _combine_kernel,
        grid=(n // ts,),
        in_specs=[
            pl.BlockSpec((TOP_K * ts,), lambda i: (i,), memory_space=pltpu.SMEM),
            pl.BlockSpec((TOP_K * ts,), lambda i: (jnp.minimum(i + 1, last),), memory_space=pltpu.SMEM),
            pl.BlockSpec(memory_space=pl.ANY),
            pl.BlockSpec((ts, D_MODEL), lambda i: (i, 0)),
            pl.BlockSpec((ts, TOP_K), lambda i: (i, 0)),
            pl.BlockSpec((ts, PLE_DIM), lambda i: (i, 0)),
            full(g2.shape), full(b2.shape), full(pw.shape), full(gw.shape), full(gb.shape),
            full(g3.shape), full(b3.shape),
        ],
        out_specs=pl.BlockSpec((ts, D_MODEL), lambda i: (i, 0)),
        out_shape=jax.ShapeDtypeStruct((n, D_MODEL), _F32),
        scratch_shapes=[
            pltpu.VMEM((2, TOP_K, ts, D_MODEL), _F32),
            pltpu.SemaphoreType.DMA((2,)),
        ],
        compiler_params=pltpu.CompilerParams(
            dimension_semantics=("arbitrary",),
            vmem_limit_bytes=48 * 1024 * 1024),
        name="combine",
    )(dest_flat, dest_flat, yb, x1rt, gates_t, p2d, g2, b2, pw, gw, gb, g3, b3)


def kernel(x, p, w_in, pool_mix, pool_scale, conv_w, w_out, ln1_g, ln1_b, router_w, router_b,
           w_gate_up, b_gate_up, w_down, b_down, ln2_g, ln2_b, ple_proj, ple_gate_w, ple_gate_b,
           ln3_g, ln3_b):
    assert DEPTH == 1 and x.shape[-1] == D_MODEL
    bsz, seq, d = x.shape
    n = bsz * seq
    assert seq % TS_MIX == 0 and n % TS_DISPATCH == 0 and n % TS_COMBINE == 0
    row = lambda v: v.reshape(1, -1)

    tri = jnp.triu(jnp.ones((TS_MIX, TS_MIX), _BF16))
    x1rt, ri, gates, counts = _mix_route(
        x, w_in[0].astype(_BF16), pool_mix[0].astype(_BF16), row(pool_scale[0]), conv_w[0],
        w_out[0].astype(_BF16), row(ln1_g[0]), row(ln1_b[0]),
        router_w[0].T.astype(_BF16), jnp.broadcast_to(router_b[0][:, None], (N_EXPERTS, LANES)), tri)

    tm = TM_EXPERT
    n_slots = n * TOP_K + N_EXPERTS * tm
    nb = n_slots // tm
    cnt = counts[:, 0]
    padded = ((cnt + tm - 1) // tm) * tm
    pad_end = jnp.cumsum(padded)
    pad_start = (pad_end - padded).astype(jnp.int32)
    n_valid = (pad_end[-1] // tm).astype(jnp.int32)
    tile_start = jnp.arange(nb, dtype=jnp.int32) * tm
    blk_e = jnp.minimum(jnp.sum(tile_start[:, None] >= pad_end[None, :], axis=1), N_EXPERTS - 1)
    blk_e = jnp.where(jnp.arange(nb) < n_valid, blk_e, blk_e[n_valid - 1]).astype(jnp.int32)

    last_tile = jnp.where(padded > 0, pad_end // tm - 1, -1).astype(jnp.int32)
    e_ids = jnp.arange(N_EXPERTS, dtype=jnp.int32)
    later = jnp.logical_and(e_ids[None, :] > e_ids[:, None], (padded > 0)[None, :])
    next_nonempty = jnp.min(jnp.where(later, e_ids[None, :], N_EXPERTS), axis=1)
    next_nonempty = jnp.where(next_nonempty == N_EXPERTS, -1, next_nonempty).astype(jnp.int32)
    next_e = next_nonempty[blk_e]
    switched = jnp.concatenate([jnp.zeros((1,), jnp.int32), (blk_e[1:] != blk_e[:-1]).astype(jnp.int32)])
    slot_par = (jnp.cumsum(switched) % 2).astype(jnp.int32)
    n_valid = n_valid.reshape(1)

    dest = _make_dest(pad_start, ri).T.reshape(-1)
    xs = _dispatch(last_tile, n_valid, dest, x1rt, n_slots)
    yb = _experts(blk_e, n_valid, slot_par, next_e, xs,
                  w_gate_up[0], b_gate_up[0][:, None, :], w_down[0], b_down[0][:, None, :])
    out = _combine(dest, yb, x1rt, gates[:TOP_K].T, p[0].reshape(n, PLE_DIM),
                   row(ln2_g[0]), row(ln2_b[0]), ple_proj[0].astype(_BF16),
                   ple_gate_w[0].astype(_BF16), row(ple_gate_b[0]), row(ln3_g[0]), row(ln3_b[0]))
    return out.reshape(bsz, seq, d)
```

```python
import functools

import jax
import jax.numpy as jnp
from jax import lax
from jax.experimental import pallas as pl
from jax.experimental.pallas import tpu as pltpu

D_MODEL = 1024
D_POOL = 512
D_CONV = 512
POOL_WINDOWS = (2, 4, 8, 16)
POOL_GROUP_DIM = 128
CONV_WIDTH = 3
D_IN_PROJ = D_POOL + 3 * D_CONV
N_EXPERTS = 32
TOP_K = 4
D_FF = 1024
SWIGLU_LIMIT = 7.0
SWIGLU_ALPHA = 1.702
PLE_DIM = 256
DEPTH = 1
DEEPNORM_ALPHA = (2.0 * DEPTH) ** 0.25
LN_EPS = 1e-5

LANES = 128
SUBLANES = 8
ROW_CHUNKS = D_MODEL // LANES
POOL_HALO = 16
CONV_HALO = 8
GATHER_PITCH = ROW_CHUNKS + 1

TS_MIX = 512
TS_DISPATCH = 1024
TM_EXPERT = 512
TS_COMBINE = 256

_BF16 = jnp.bfloat16
_F32 = jnp.float32


def _layer_norm(h, g, b):
    mu = jnp.mean(h, axis=-1, keepdims=True)
    c = h - mu
    var = jnp.mean(c * c, axis=-1, keepdims=True)
    return c * lax.rsqrt(var + LN_EPS) * g + b


def _to_row_tiles(ref, val):
    rows = val.shape[0]
    for c in range(ROW_CHUNKS):
        ref[pl.ds(c, rows, stride=ROW_CHUNKS), :] = val[:, c * LANES:(c + 1) * LANES]


def _from_row_tiles(ref, rows, pitch=ROW_CHUNKS):
    return jnp.concatenate(
        [ref[pl.ds(c, rows, stride=pitch), :] for c in range(ROW_CHUNKS)], axis=1)


def _row_tile(ref, r):
    return ref.at[pl.ds(pl.multiple_of(r * ROW_CHUNKS, ROW_CHUNKS), ROW_CHUNKS)]


def _mix_route_kernel(x_ref, w_in_ref, pmix_ref, pscale_ref, convw_ref, w_out_ref, g1_ref, b1_ref,
                      rwt_ref, rb_ref, tri_ref,
                      x1rt_ref, ri_ref, gate_ref, cnt_ref,
                      carry_v, carry_u, cnt_sc):
    b = pl.program_id(0)
    s = pl.program_id(1)
    ts = TS_MIX

    @pl.when(jnp.logical_and(b == 0, s == 0))
    def _():
        cnt_sc[...] = jnp.zeros_like(cnt_sc)

    @pl.when(s == 0)
    def _():
        carry_v[...] = jnp.zeros_like(carry_v)
        carry_u[...] = jnp.zeros_like(carry_u)

    xb = x_ref[0]
    proj = jnp.dot(xb.astype(_BF16), w_in_ref[...], preferred_element_type=_F32)
    vp = proj[:, :D_POOL]
    bg = proj[:, D_POOL:D_POOL + D_CONV]
    cg = proj[:, D_POOL + D_CONV:D_POOL + 2 * D_CONV]
    vc = proj[:, D_POOL + 2 * D_CONV:]

    ext = jnp.concatenate([carry_v[...], vp], axis=0)
    pos = lax.broadcasted_iota(jnp.int32, (ts, 1), 0) + s * ts
    mixed = []
    for g, w in enumerate(POOL_WINDOWS):
        lo, hi = g * POOL_GROUP_DIM, (g + 1) * POOL_GROUP_DIM
        acc = ext[:, lo:hi]
        sh = 1
        while sh < w:
            acc = acc + pltpu.roll(acc, sh, 0)
            sh *= 2
        cnt = jnp.minimum(pos + 1, w).astype(_F32)
        d = acc[POOL_HALO:] / cnt - vp[:, lo:hi]
        yg = jnp.dot(d.astype(_BF16), pmix_ref[g], preferred_element_type=_F32)
        mixed.append(yg * pscale_ref[:, lo:hi])
    carry_v[...] = vp[ts - POOL_HALO:]

    u = cg * vc
    extu = jnp.concatenate([carry_u[...], u], axis=0)
    u1 = pltpu.roll(extu, 1, 0)[CONV_HALO:]
    u2 = pltpu.roll(extu, 2, 0)[CONV_HALO:]
    yc = bg * (convw_ref[0:1, :] * u2 + convw_ref[1:2, :] * u1 + convw_ref[2:3, :] * u)
    carry_u[...] = u[ts - CONV_HALO:]
    mixed.append(yc)

    mix_in = jnp.concatenate(mixed, axis=1).astype(_BF16)
    mix = jnp.dot(mix_in, w_out_ref[...], preferred_element_type=_F32)
    x1 = _layer_norm(DEEPNORM_ALPHA * xb + mix, g1_ref[...], b1_ref[...])
    _to_row_tiles(x1rt_ref, x1)

    logits = lax.dot_general(rwt_ref[...], x1.astype(_BF16), (((1,), (1,)), ((), ())),
                             preferred_element_type=_F32) + rb_ref[:, 0:1]
    eidx = lax.broadcasted_iota(jnp.int32, (N_EXPERTS, ts), 0).astype(_F32)
    vals, sels = [], []
    work = logits
    for k in range(TOP_K):
        m = jnp.max(work, axis=0, keepdims=True)
        first = jnp.min(jnp.where(work == m, eidx, float(N_EXPERTS)), axis=0, keepdims=True)
        sel = eidx == first
        work = jnp.where(sel, -jnp.inf, work)
        vals.append(m)
        sels.append(sel)
        ri_ref[k:k + 1, :] = first.astype(jnp.int32)
    exps = [jnp.exp(v - vals[0]) for v in vals]
    denom = exps[0] + exps[1] + exps[2] + exps[3]
    for k in range(TOP_K):
        gate_ref[k:k + 1, :] = exps[k] / denom
    gate_ref[TOP_K:, :] = jnp.zeros((SUBLANES - TOP_K, ts), _F32)

    chosen = jnp.zeros((N_EXPERTS, ts), _F32)
    for sel in sels:
        chosen = chosen + sel.astype(_F32)
    incl = jnp.dot(chosen.astype(_BF16), tri_ref[...], preferred_element_type=_F32)
    base = cnt_sc[:, 0:1] + (incl - chosen)
    for k in range(TOP_K):
        rank = jnp.sum(jnp.where(sels[k], base, 0.0), axis=0, keepdims=True)
        ri_ref[TOP_K + k:TOP_K + k + 1, :] = rank.astype(jnp.int32)
    total = cnt_sc[...] + jnp.sum(chosen, axis=1, keepdims=True)
    cnt_sc[...] = total
    cnt_ref[...] = total.astype(jnp.int32)


def _mix_route(x, w_in, pmix, pscale, convw, w_out, g1, b1, rwt, rb, tri):
    bsz, seq, d = x.shape
    n = bsz * seq
    ns = seq // TS_MIX
    full = lambda shape: pl.BlockSpec(shape, lambda b, s: (0,) * len(shape))
    return pl.pallas_call(
        _mix_route_kernel,
        grid=(bsz, ns),
        in_specs=[
            pl.BlockSpec((1, TS_MIX, d), lambda b, s: (b, s, 0)),
            full(w_in.shape), full(pmix.shape), full(pscale.shape), full(convw.shape),
            full(w_out.shape), full(g1.shape), full(b1.shape), full(rwt.shape), full(rb.shape),
            full(tri.shape),
        ],
        out_specs=[
            pl.BlockSpec((TS_MIX * ROW_CHUNKS, LANES), lambda b, s: (b * ns + s, 0)),
            pl.BlockSpec((2 * TOP_K, TS_MIX), lambda b, s: (0, b * ns + s)),
            pl.BlockSpec((SUBLANES, TS_MIX), lambda b, s: (0, b * ns + s)),
            pl.BlockSpec((N_EXPERTS, LANES), lambda b, s: (0, 0)),
        ],
        out_shape=[
            jax.ShapeDtypeStruct((n * ROW_CHUNKS, LANES), _F32),
            jax.ShapeDtypeStruct((2 * TOP_K, n), jnp.int32),
            jax.ShapeDtypeStruct((SUBLANES, n), _F32),
            jax.ShapeDtypeStruct((N_EXPERTS, LANES), jnp.int32),
        ],
        scratch_shapes=[
            pltpu.VMEM((POOL_HALO, D_POOL), _F32),
            pltpu.VMEM((CONV_HALO, D_CONV), _F32),
            pltpu.VMEM((N_EXPERTS, LANES), _F32),
        ],
        compiler_params=pltpu.CompilerParams(
            dimension_semantics=("arbitrary", "arbitrary"),
            vmem_limit_bytes=48 * 1024 * 1024),
        name="mix_route",
    )(x, w_in, pmix, pscale, convw, w_out, g1, b1, rwt, rb, tri)


def _make_dest_kernel(start_ref, ri_ref, dest_ref):
    ei = ri_ref[0:TOP_K, :]
    acc = ri_ref[TOP_K:, :]
    for e in range(N_EXPERTS):
        acc = acc + jnp.where(ei == e, start_ref[e], 0)
    dest_ref[...] = acc


def _make_dest(pad_start, ri):
    n = ri.shape[1]
    return pl.pallas_call(
        _make_dest_kernel,
        in_specs=[pl.BlockSpec(memory_space=pltpu.SMEM),
                  pl.BlockSpec((2 * TOP_K, n), lambda: (0, 0))],
        out_specs=pl.BlockSpec((TOP_K, n), lambda: (0, 0)),
        out_shape=jax.ShapeDtypeStruct((TOP_K, n), jnp.int32),
        name="make_dest",
    )(pad_start, ri)


def _dispatch_kernel(last_ref, nv_ref, dest_ref, src_ref, out_ref, zbuf, sem, zsem):
    ts = TS_DISPATCH
    tile_rows = TM_EXPERT * ROW_CHUNKS
    n_tiles = out_ref.shape[0] // tile_rows

    @pl.when(pl.program_id(0) == 0)
    def _():
        zbuf[...] = jnp.zeros_like(zbuf)

        def zero_tile(tile):
            return pltpu.make_async_copy(
                zbuf, out_ref.at[pl.ds(pl.multiple_of(tile * tile_rows, tile_rows), tile_rows)], zsem)

        def each_tile(fn):
            for e in range(N_EXPERTS):
                @pl.when(last_ref[e] >= 0)
                def _():
                    fn(zero_tile(last_ref[e]))

            def tail(j, carry):
                fn(zero_tile(j))
                return carry
            lax.fori_loop(nv_ref[0], n_tiles, tail, 0)

        each_tile(lambda c: c.start())
        each_tile(lambda c: c.wait())

    def body(t, carry):
        for k in range(TOP_K):
            pltpu.make_async_copy(_row_tile(src_ref, t), _row_tile(out_ref, dest_ref[t * TOP_K + k]),
                                  sem).start(priority=k % 2)
        return carry

    lax.fori_loop(0, ts, body, 0, unroll=8)
    for k in range(TOP_K):
        pltpu.make_async_copy(src_ref, out_ref.at[pl.ds(0, ts * ROW_CHUNKS)], sem).wait()


def _dispatch(last_tile, n_valid, dest_flat, x1rt, n_slots):
    n = dest_flat.shape[0] // TOP_K
    grid_spec = pltpu.PrefetchScalarGridSpec(
        num_scalar_prefetch=2,
        grid=(n // TS_DISPATCH,),
        in_specs=[
            pl.BlockSpec((TOP_K * TS_DISPATCH,), lambda i, lt, nv: (i,), memory_space=pltpu.SMEM),
            pl.BlockSpec((TS_DISPATCH * ROW_CHUNKS, LANES), lambda i, lt, nv: (i, 0)),
        ],
        out_specs=pl.BlockSpec(memory_space=pl.ANY),
        scratch_shapes=[
            pltpu.VMEM((TM_EXPERT * ROW_CHUNKS, LANES), _F32),
            pltpu.SemaphoreType.DMA(()),
            pltpu.SemaphoreType.DMA(()),
        ],
    )
    return pl.pallas_call(
        _dispatch_kernel,
        grid_spec=grid_spec,
        out_shape=jax.ShapeDtypeStruct((n_slots * ROW_CHUNKS, LANES), _F32),
        compiler_params=pltpu.CompilerParams(dimension_semantics=("arbitrary",)),
        name="dispatch",
    )(last_tile, n_valid, dest_flat, x1rt)


def _experts_kernel(be_ref, nv_ref, par_ref, nxt_ref, xs_ref, wgu_hbm, bgu_ref, wdn_hbm, bdn_ref, out_ref,
                    wgu_f32, wdn_f32, wgu_bf, wdn_bf, sems):
    j = pl.program_id(0)
    tm = TM_EXPERT
    valid = j < nv_ref[0]
    new_expert = jnp.logical_or(j == 0, be_ref[j] != be_ref[jnp.maximum(j - 1, 0)])

    def weight_copies(e, s):
        return (pltpu.make_async_copy(wgu_hbm.at[e], wgu_f32.at[s], sems.at[0, s]),
                pltpu.make_async_copy(wdn_hbm.at[e], wdn_f32.at[s], sems.at[1, s]))

    @pl.when(j == 0)
    def _():
        for c in weight_copies(be_ref[0], par_ref[0]):
            c.start()

    @pl.when(jnp.logical_and(valid, new_expert))
    def _():
        s = par_ref[j]
        for c in weight_copies(be_ref[j], s):
            c.wait()

        nxt = nxt_ref[be_ref[j]]

        @pl.when(nxt >= 0)
        def _():
            for c in weight_copies(nxt, 1 - s):
                c.start()

        wgu_bf[...] = wgu_f32[s].astype(_BF16)
        wdn_bf[...] = wdn_f32[s].astype(_BF16)

    @pl.when(valid)
    def _():
        x = _from_row_tiles(xs_ref, tm).astype(_BF16)
        gu = jnp.dot(x, wgu_bf[...], preferred_element_type=_F32) + bgu_ref[0]
        gate = jnp.minimum(gu[:, :D_FF], SWIGLU_LIMIT)
        up = jnp.clip(gu[:, D_FF:], -SWIGLU_LIMIT, SWIGLU_LIMIT)
        glu = gate * jax.nn.sigmoid(SWIGLU_ALPHA * gate)
        act = ((up + 1.0) * glu).astype(_BF16)
        y = jnp.dot(act, wdn_bf[...], preferred_element_type=_F32) + bdn_ref[0]
        _to_row_tiles(out_ref, y)

    @pl.when(j >= nv_ref[0])
    def _():
        out_ref[...] = jnp.zeros_like(out_ref)


def _experts(blk_e, n_valid, slot_par, next_e, xs, wgu, bgu, wdn, bdn):
    n_slots = xs.shape[0] // ROW_CHUNKS
    nb = n_slots // TM_EXPERT
    rows = TM_EXPERT * ROW_CHUNKS
    grid_spec = pltpu.PrefetchScalarGridSpec(
        num_scalar_prefetch=4,
        grid=(nb,),
        in_specs=[
            pl.BlockSpec((rows, LANES), lambda j, be, nv, par, nxt: (jnp.minimum(j, nv[0] - 1), 0)),
            pl.BlockSpec(memory_space=pl.ANY),
            pl.BlockSpec((1, 1, 2 * D_FF), lambda j, be, nv, par, nxt: (be[j], 0, 0)),
            pl.BlockSpec(memory_space=pl.ANY),
            pl.BlockSpec((1, 1, D_MODEL), lambda j, be, nv, par, nxt: (be[j], 0, 0)),
        ],
        out_specs=pl.BlockSpec((rows, LANES), lambda j, be, nv, par, nxt: (j, 0)),
        scratch_shapes=[
            pltpu.VMEM((2, D_MODEL, 2 * D_FF), _F32),
            pltpu.VMEM((2, D_FF, D_MODEL), _F32),
            pltpu.VMEM((D_MODEL, 2 * D_FF), _BF16),
            pltpu.VMEM((D_FF, D_MODEL), _BF16),
            pltpu.SemaphoreType.DMA((2, 2)),
        ],
    )
    return pl.pallas_call(
        _experts_kernel,
        grid_spec=grid_spec,
        out_shape=jax.ShapeDtypeStruct(xs.shape, _F32),
        compiler_params=pltpu.CompilerParams(
            dimension_semantics=("arbitrary",),
            vmem_limit_bytes=60 * 1024 * 1024),
        name="experts",
    )(blk_e, n_valid, slot_par, next_e, xs, wgu, bgu, wdn, bdn)


def _combine_kernel(dcur_ref, dnext_ref, yb_ref, x1rt_ref, gt_ref, p_ref, g2_ref, b2_ref, pw_ref, gw_ref,
                    gb_ref, g3_ref, b3_ref, out_ref, gbuf, sems):
    ts = TS_COMBINE
    i = pl.program_id(0)
    slot = i % 2

    def start_gather(d_ref, s):
        def body(t, carry):
            for k in range(TOP_K):
                pltpu.make_async_copy(_row_tile(yb_ref, d_ref[t * TOP_K + k]),
                                      gbuf.at[s, k, pl.ds(t * GATHER_PITCH, ROW_CHUNKS)],
                                      sems.at[s]).start(priority=k % 2)
            return carry
        lax.fori_loop(0, ts, body, 0, unroll=8)

    @pl.when(i == 0)
    def _():
        start_gather(dcur_ref, 0)

    @pl.when(i + 1 < pl.num_programs(0))
    def _():
        start_gather(dnext_ref, 1 - slot)

    for k in range(TOP_K):
        pltpu.make_async_copy(yb_ref.at[pl.ds(0, ts * ROW_CHUNKS)],
                              gbuf.at[slot, k, pl.ds(0, ts * ROW_CHUNKS)], sems.at[slot]).wait()

    x1 = _from_row_tiles(x1rt_ref, ts)
    ffn = jnp.zeros((ts, D_MODEL), _F32)
    for k in range(TOP_K):
        ffn = ffn + gt_ref[:, k:k + 1] * _from_row_tiles(gbuf.at[slot, k], ts, GATHER_PITCH)
    x2 = _layer_norm(DEEPNORM_ALPHA * x1 + ffn, g2_ref[...], b2_ref[...])
    z = jnp.dot(x2.astype(_BF16), gw_ref[...], preferred_element_type=_F32) + gb_ref[...]
    ple = jnp.dot(p_ref[...].astype(_BF16), pw_ref[...], preferred_element_type=_F32)
    x3 = _layer_norm(DEEPNORM_ALPHA * x2 + jax.nn.sigmoid(z) * ple, g3_ref[...], b3_ref[...])
    out_ref[...] = x3


def _combine(dest_flat, yb, x1rt, gates_t, p2d, g2, b2, pw, gw, gb, g3, b3):
    n = dest_flat.shape[0] // TOP_K
    ts = TS_COMBINE
    last = n // ts - 1
    full = lambda shape: pl.BlockSpec(shape, lambda i: (0,) * len(shape))
    return pl.pallas_call(
        _combine_kernel,
        grid=(n // ts,),
        in_specs=[
            pl.BlockSpec((TOP_K * ts,), lambda i: (i,), memory_space=pltpu.SMEM),
            pl.BlockSpec((TOP_K * ts,), lambda i: (jnp.minimum(i + 1, last),), memory_space=pltpu.SMEM),
            pl.BlockSpec(memory_space=pl.ANY),
            pl.BlockSpec((ts * ROW_CHUNKS, LANES), lambda i: (i, 0)),
            pl.BlockSpec((ts, TOP_K), lambda i: (i, 0)),
            pl.BlockSpec((ts, PLE_DIM), lambda i: (i, 0)),
            full(g2.shape), full(b2.shape), full(pw.shape), full(gw.shape), full(gb.shape),
            full(g3.shape), full(b3.shape),
        ],
        out_specs=pl.BlockSpec((ts, D_MODEL), lambda i: (i, 0)),
        out_shape=jax.ShapeDtypeStruct((n, D_MODEL), _F32),
        scratch_shapes=[
            pltpu.VMEM((2, TOP_K, ts * GATHER_PITCH, LANES), _F32),
            pltpu.SemaphoreType.DMA((2,)),
        ],
        compiler_params=pltpu.CompilerParams(
            dimension_semantics=("arbitrary",),
            vmem_limit_bytes=48 * 1024 * 1024),
        name="combine",
    )(dest_flat, dest_flat, yb, x1rt, gates_t, p2d, g2, b2, pw, gw, gb, g3, b3)


def kernel(x, p, w_in, pool_mix, pool_scale, conv_w, w_out, ln1_g, ln1_b, router_w, router_b,
           w_gate_up, b_gate_up, w_down, b_down, ln2_g, ln2_b, ple_proj, ple_gate_w, ple_gate_b,
           ln3_g, ln3_b):
    assert DEPTH == 1 and x.shape[-1] == D_MODEL
    bsz, seq, d = x.shape
    n = bsz * seq
    assert seq % TS_MIX == 0 and n % TS_DISPATCH == 0 and n % TS_COMBINE == 0
    row = lambda v: v.reshape(1, -1)

    tri = jnp.triu(jnp.ones((TS_MIX, TS_MIX), _BF16))
    x1rt, ri, gates, counts = _mix_route(
        x, w_in[0].astype(_BF16), pool_mix[0].astype(_BF16), row(pool_scale[0]), conv_w[0],
        w_out[0].astype(_BF16), row(ln1_g[0]), row(ln1_b[0]),
        router_w[0].T.astype(_BF16), jnp.broadcast_to(router_b[0][:, None], (N_EXPERTS, LANES)), tri)

    tm = TM_EXPERT
    n_slots = n * TOP_K + N_EXPERTS * tm
    nb = n_slots // tm
    cnt = counts[:, 0]
    padded = ((cnt + tm - 1) // tm) * tm
    pad_end = jnp.cumsum(padded)
    pad_start = (pad_end - padded).astype(jnp.int32)
    n_valid = (pad_end[-1] // tm).astype(jnp.int32)
    tile_start = jnp.arange(nb, dtype=jnp.int32) * tm
    blk_e = jnp.minimum(jnp.sum(tile_start[:, None] >= pad_end[None, :], axis=1), N_EXPERTS - 1)
    blk_e = jnp.where(jnp.arange(nb) < n_valid, blk_e, blk_e[n_valid - 1]).astype(jnp.int32)

    last_tile = jnp.where(padded > 0, pad_end // tm - 1, -1).astype(jnp.int32)
    e_ids = jnp.arange(N_EXPERTS, dtype=jnp.int32)
    later = jnp.logical_and(e_ids[None, :] > e_ids[:, None], (padded > 0)[None, :])
    next_nonempty = jnp.min(jnp.where(later, e_ids[None, :], N_EXPERTS), axis=1)
    next_e = jnp.where(next_nonempty == N_EXPERTS, -1, next_nonempty).astype(jnp.int32)
    switched = jnp.concatenate([jnp.zeros((1,), jnp.int32), (blk_e[1:] != blk_e[:-1]).astype(jnp.int32)])
    slot_par = (jnp.cumsum(switched) % 2).astype(jnp.int32)
    n_valid = n_valid.reshape(1)

    dest = _make_dest(pad_start, ri).T.reshape(-1)
    xs = _dispatch(last_tile, n_valid, dest, x1rt, n_slots)
    yb = _experts(blk_e, n_valid, slot_par, next_e, xs,
                  w_gate_up[0], b_gate_up[0][:, None, :], w_down[0], b_down[0][:, None, :])
    out = _combine(dest, yb, x1rt, gates[:TOP_K].T, p[0].reshape(n, PLE_DIM),
                   row(ln2_g[0]), row(ln2_b[0]), ple_proj[0].astype(_BF16),
                   ple_gate_w[0].astype(_BF16), row(ple_gate_b[0]), row(ln3_g[0]), row(ln3_b[0]))
    return out.reshape(bsz, seq, d)
```

```python
import functools

import jax
import jax.numpy as jnp
from jax import lax
from jax.experimental import pallas as pl
from jax.experimental.pallas import tpu as pltpu

D_MODEL = 1024
D_POOL = 512
D_CONV = 512
POOL_WINDOWS = (2, 4, 8, 16)
POOL_GROUP_DIM = 128
CONV_WIDTH = 3
D_IN_PROJ = D_POOL + 3 * D_CONV
N_EXPERTS = 32
TOP_K = 4
D_FF = 1024
SWIGLU_LIMIT = 7.0
SWIGLU_ALPHA = 1.702
PLE_DIM = 256
DEPTH = 1
DEEPNORM_ALPHA = (2.0 * DEPTH) ** 0.25
LN_EPS = 1e-5

LANES = 128
SUBLANES = 8
ROW_CHUNKS = D_MODEL // LANES
POOL_HALO = 16
CONV_HALO = 8
GATHER_PITCH = ROW_CHUNKS + 1

TS_MIX = 512
TS_DISPATCH = 2048
TM_EXPERT = 512
TS_COMBINE = 256

_BF16 = jnp.bfloat16
_F32 = jnp.float32


def _layer_norm(h, g, b):
    mu = jnp.mean(h, axis=-1, keepdims=True)
    c = h - mu
    var = jnp.mean(c * c, axis=-1, keepdims=True)
    return c * lax.rsqrt(var + LN_EPS) * g + b


def _to_row_tiles(ref, val):
    rows = val.shape[0]
    for c in range(ROW_CHUNKS):
        ref[pl.ds(c, rows, stride=ROW_CHUNKS), :] = val[:, c * LANES:(c + 1) * LANES]


def _from_row_tiles(ref, rows, pitch=ROW_CHUNKS):
    return jnp.concatenate(
        [ref[pl.ds(c, rows, stride=pitch), :] for c in range(ROW_CHUNKS)], axis=1)


def _row_tile(ref, r):
    return ref.at[pl.ds(pl.multiple_of(r * ROW_CHUNKS, ROW_CHUNKS), ROW_CHUNKS)]


def _mix_route_kernel(x_ref, w_in_ref, pmix_ref, pscale_ref, convw_ref, w_out_ref, g1_ref, b1_ref,
                      rwt_ref, rb_ref, tri_ref,
                      x1rt_ref, x1_ref, ri_ref, gate_ref, cnt_ref,
                      carry_v, carry_u, cnt_sc):
    b = pl.program_id(0)
    s = pl.program_id(1)
    ts = TS_MIX

    @pl.when(jnp.logical_and(b == 0, s == 0))
    def _():
        cnt_sc[...] = jnp.zeros_like(cnt_sc)

    @pl.when(s == 0)
    def _():
        carry_v[...] = jnp.zeros_like(carry_v)
        carry_u[...] = jnp.zeros_like(carry_u)

    xb = x_ref[0]
    proj = jnp.dot(xb.astype(_BF16), w_in_ref[...], preferred_element_type=_F32)
    vp = proj[:, :D_POOL]
    bg = proj[:, D_POOL:D_POOL + D_CONV]
    cg = proj[:, D_POOL + D_CONV:D_POOL + 2 * D_CONV]
    vc = proj[:, D_POOL + 2 * D_CONV:]

    ext = jnp.concatenate([carry_v[...], vp], axis=0)
    pos = lax.broadcasted_iota(jnp.int32, (ts, 1), 0) + s * ts
    mixed = []
    for g, w in enumerate(POOL_WINDOWS):
        lo, hi = g * POOL_GROUP_DIM, (g + 1) * POOL_GROUP_DIM
        acc = ext[:, lo:hi]
        sh = 1
        while sh < w:
            acc = acc + pltpu.roll(acc, sh, 0)
            sh *= 2
        cnt = jnp.minimum(pos + 1, w).astype(_F32)
        d = acc[POOL_HALO:] / cnt - vp[:, lo:hi]
        yg = jnp.dot(d.astype(_BF16), pmix_ref[g], preferred_element_type=_F32)
        mixed.append(yg * pscale_ref[:, lo:hi])
    carry_v[...] = vp[ts - POOL_HALO:]

    u = cg * vc
    extu = jnp.concatenate([carry_u[...], u], axis=0)
    u1 = pltpu.roll(extu, 1, 0)[CONV_HALO:]
    u2 = pltpu.roll(extu, 2, 0)[CONV_HALO:]
    yc = bg * (convw_ref[0:1, :] * u2 + convw_ref[1:2, :] * u1 + convw_ref[2:3, :] * u)
    carry_u[...] = u[ts - CONV_HALO:]
    mixed.append(yc)

    mix_in = jnp.concatenate(mixed, axis=1).astype(_BF16)
    mix = jnp.dot(mix_in, w_out_ref[...], preferred_element_type=_F32)
    x1 = _layer_norm(DEEPNORM_ALPHA * xb + mix, g1_ref[...], b1_ref[...])
    _to_row_tiles(x1rt_ref, x1)
    x1_ref[...] = x1

    logits = lax.dot_general(rwt_ref[...], x1.astype(_BF16), (((1,), (1,)), ((), ())),
                             preferred_element_type=_F32) + rb_ref[:, 0:1]
    eidx = lax.broadcasted_iota(jnp.int32, (N_EXPERTS, ts), 0).astype(_F32)
    vals, sels = [], []
    work = logits
    for k in range(TOP_K):
        m = jnp.max(work, axis=0, keepdims=True)
        first = jnp.min(jnp.where(work == m, eidx, float(N_EXPERTS)), axis=0, keepdims=True)
        sel = eidx == first
        work = jnp.where(sel, -jnp.inf, work)
        vals.append(m)
        sels.append(sel)
        ri_ref[k:k + 1, :] = first.astype(jnp.int32)
    exps = [jnp.exp(v - vals[0]) for v in vals]
    denom = exps[0] + exps[1] + exps[2] + exps[3]
    for k in range(TOP_K):
        gate_ref[k:k + 1, :] = exps[k] / denom
    gate_ref[TOP_K:, :] = jnp.zeros((SUBLANES - TOP_K, ts), _F32)

    chosen = jnp.zeros((N_EXPERTS, ts), _F32)
    for sel in sels:
        chosen = chosen + sel.astype(_F32)
    incl = jnp.dot(chosen.astype(_BF16), tri_ref[...], preferred_element_type=_F32)
    base = cnt_sc[:, 0:1] + (incl - chosen)
    for k in range(TOP_K):
        rank = jnp.sum(jnp.where(sels[k], base, 0.0), axis=0, keepdims=True)
        ri_ref[TOP_K + k:TOP_K + k + 1, :] = rank.astype(jnp.int32)
    total = cnt_sc[...] + jnp.sum(chosen, axis=1, keepdims=True)
    cnt_sc[...] = total
    cnt_ref[...] = total.astype(jnp.int32)


def _mix_route(x, w_in, pmix, pscale, convw, w_out, g1, b1, rwt, rb, tri):
    bsz, seq, d = x.shape
    n = bsz * seq
    ns = seq // TS_MIX
    full = lambda shape: pl.BlockSpec(shape, lambda b, s: (0,) * len(shape))
    return pl.pallas_call(
        _mix_route_kernel,
        grid=(bsz, ns),
        in_specs=[
            pl.BlockSpec((1, TS_MIX, d), lambda b, s: (b, s, 0)),
            full(w_in.shape), full(pmix.shape), full(pscale.shape), full(convw.shape),
            full(w_out.shape), full(g1.shape), full(b1.shape), full(rwt.shape), full(rb.shape),
            full(tri.shape),
        ],
        out_specs=[
            pl.BlockSpec((TS_MIX * ROW_CHUNKS, LANES), lambda b, s: (b * ns + s, 0)),
            pl.BlockSpec((TS_MIX, D_MODEL), lambda b, s: (b * ns + s, 0)),
            pl.BlockSpec((2 * TOP_K, TS_MIX), lambda b, s: (0, b * ns + s)),
            pl.BlockSpec((SUBLANES, TS_MIX), lambda b, s: (0, b * ns + s)),
            pl.BlockSpec((N_EXPERTS, LANES), lambda b, s: (0, 0)),
        ],
        out_shape=[
            jax.ShapeDtypeStruct((n * ROW_CHUNKS, LANES), _F32),
            jax.ShapeDtypeStruct((n, D_MODEL), _F32),
            jax.ShapeDtypeStruct((2 * TOP_K, n), jnp.int32),
            jax.ShapeDtypeStruct((SUBLANES, n), _F32),
            jax.ShapeDtypeStruct((N_EXPERTS, LANES), jnp.int32),
        ],
        scratch_shapes=[
            pltpu.VMEM((POOL_HALO, D_POOL), _F32),
            pltpu.VMEM((CONV_HALO, D_CONV), _F32),
            pltpu.VMEM((N_EXPERTS, LANES), _F32),
        ],
        compiler_params=pltpu.CompilerParams(
            dimension_semantics=("arbitrary", "arbitrary"),
            vmem_limit_bytes=48 * 1024 * 1024),
        name="mix_route",
    )(x, w_in, pmix, pscale, convw, w_out, g1, b1, rwt, rb, tri)


def _make_dest_kernel(start_ref, ri_ref, dest_ref):
    ei = ri_ref[0:TOP_K, :]
    acc = ri_ref[TOP_K:, :]
    for e in range(N_EXPERTS):
        acc = acc + jnp.where(ei == e, start_ref[e], 0)
    dest_ref[...] = acc


def _make_dest(pad_start, ri):
    n = ri.shape[1]
    return pl.pallas_call(
        _make_dest_kernel,
        in_specs=[pl.BlockSpec(memory_space=pltpu.SMEM),
                  pl.BlockSpec((2 * TOP_K, n), lambda: (0, 0))],
        out_specs=pl.BlockSpec((TOP_K, n), lambda: (0, 0)),
        out_shape=jax.ShapeDtypeStruct((TOP_K, n), jnp.int32),
        name="make_dest",
    )(pad_start, ri)


def _dispatch_kernel(last_ref, nv_ref, dest_ref, src_ref, out_ref, zbuf, sem, zsem):
    ts = TS_DISPATCH
    tile_rows = TM_EXPERT * ROW_CHUNKS
    n_tiles = out_ref.shape[0] // tile_rows

    @pl.when(pl.program_id(0) == 0)
    def _():
        zbuf[...] = jnp.zeros_like(zbuf)

        def zero_tile(tile):
            return pltpu.make_async_copy(
                zbuf, out_ref.at[pl.ds(pl.multiple_of(tile * tile_rows, tile_rows), tile_rows)], zsem)

        def each_tile(fn):
            for e in range(N_EXPERTS):
                @pl.when(last_ref[e] >= 0)
                def _():
                    fn(zero_tile(last_ref[e]))

            def tail(j, carry):
                fn(zero_tile(j))
                return carry
            lax.fori_loop(nv_ref[0], n_tiles, tail, 0)

        each_tile(lambda c: c.start())
        each_tile(lambda c: c.wait())

    def body(t, carry):
        for k in range(TOP_K):
            pltpu.make_async_copy(_row_tile(src_ref, t), _row_tile(out_ref, dest_ref[t * TOP_K + k]),
                                  sem).start(priority=k % 2)
        return carry

    lax.fori_loop(0, ts, body, 0, unroll=8)
    for k in range(TOP_K):
        pltpu.make_async_copy(src_ref, out_ref.at[pl.ds(0, ts * ROW_CHUNKS)], sem).wait()


def _dispatch(last_tile, n_valid, dest_flat, x1rt, n_slots):
    n = dest_flat.shape[0] // TOP_K
    grid_spec = pltpu.PrefetchScalarGridSpec(
        num_scalar_prefetch=2,
        grid=(n // TS_DISPATCH,),
        in_specs=[
            pl.BlockSpec((TOP_K * TS_DISPATCH,), lambda i, lt, nv: (i,), memory_space=pltpu.SMEM),
            pl.BlockSpec((TS_DISPATCH * ROW_CHUNKS, LANES), lambda i, lt, nv: (i, 0)),
        ],
        out_specs=pl.BlockSpec(memory_space=pl.ANY),
        scratch_shapes=[
            pltpu.VMEM((TM_EXPERT * ROW_CHUNKS, LANES), _F32),
            pltpu.SemaphoreType.DMA(()),
            pltpu.SemaphoreType.DMA(()),
        ],
    )
    return pl.pallas_call(
        _dispatch_kernel,
        grid_spec=grid_spec,
        out_shape=jax.ShapeDtypeStruct((n_slots * ROW_CHUNKS, LANES), _F32),
        compiler_params=pltpu.CompilerParams(dimension_semantics=("arbitrary",)),
        name="dispatch",
    )(last_tile, n_valid, dest_flat, x1rt)


def _experts_kernel(be_ref, nv_ref, par_ref, nxt_ref, xs_ref, wgu_hbm, bgu_ref, wdn_hbm, bdn_ref, out_ref,
                    wgu_f32, wdn_f32, wgu_bf, wdn_bf, sems):
    j = pl.program_id(0)
    tm = TM_EXPERT
    valid = j < nv_ref[0]
    new_expert = jnp.logical_or(j == 0, be_ref[j] != be_ref[jnp.maximum(j - 1, 0)])

    def weight_copies(e, s):
        return (pltpu.make_async_copy(wgu_hbm.at[e], wgu_f32.at[s], sems.at[0, s]),
                pltpu.make_async_copy(wdn_hbm.at[e], wdn_f32.at[s], sems.at[1, s]))

    @pl.when(j == 0)
    def _():
        for c in weight_copies(be_ref[0], par_ref[0]):
            c.start()

    @pl.when(jnp.logical_and(valid, new_expert))
    def _():
        s = par_ref[j]
        for c in weight_copies(be_ref[j], s):
            c.wait()

        nxt = nxt_ref[be_ref[j]]

        @pl.when(nxt >= 0)
        def _():
            for c in weight_copies(nxt, 1 - s):
                c.start()

        wgu_bf[...] = wgu_f32[s].astype(_BF16)
        wdn_bf[...] = wdn_f32[s].astype(_BF16)

    @pl.when(valid)
    def _():
        x = _from_row_tiles(xs_ref, tm).astype(_BF16)
        gu = jnp.dot(x, wgu_bf[...], preferred_element_type=_F32) + bgu_ref[0]
        gate = jnp.minimum(gu[:, :D_FF], SWIGLU_LIMIT)
        up = jnp.clip(gu[:, D_FF:], -SWIGLU_LIMIT, SWIGLU_LIMIT)
        glu = gate * jax.nn.sigmoid(SWIGLU_ALPHA * gate)
        act = ((up + 1.0) * glu).astype(_BF16)
        y = jnp.dot(act, wdn_bf[...], preferred_element_type=_F32) + bdn_ref[0]
        _to_row_tiles(out_ref, y)

    @pl.when(j >= nv_ref[0])
    def _():
        out_ref[...] = jnp.zeros_like(out_ref)


def _experts(blk_e, n_valid, slot_par, next_e, xs, wgu, bgu, wdn, bdn):
    n_slots = xs.shape[0] // ROW_CHUNKS
    nb = n_slots // TM_EXPERT
    rows = TM_EXPERT * ROW_CHUNKS
    grid_spec = pltpu.PrefetchScalarGridSpec(
        num_scalar_prefetch=4,
        grid=(nb,),
        in_specs=[
            pl.BlockSpec((rows, LANES), lambda j, be, nv, par, nxt: (jnp.minimum(j, nv[0] - 1), 0)),
            pl.BlockSpec(memory_space=pl.ANY),
            pl.BlockSpec((1, 1, 2 * D_FF), lambda j, be, nv, par, nxt: (be[j], 0, 0)),
            pl.BlockSpec(memory_space=pl.ANY),
            pl.BlockSpec((1, 1, D_MODEL), lambda j, be, nv, par, nxt: (be[j], 0, 0)),
        ],
        out_specs=pl.BlockSpec((rows, LANES), lambda j, be, nv, par, nxt: (j, 0)),
        scratch_shapes=[
            pltpu.VMEM((2, D_MODEL, 2 * D_FF), _F32),
            pltpu.VMEM((2, D_FF, D_MODEL), _F32),
            pltpu.VMEM((D_MODEL, 2 * D_FF), _BF16),
            pltpu.VMEM((D_FF, D_MODEL), _BF16),
            pltpu.SemaphoreType.DMA((2, 2)),
        ],
    )
    return pl.pallas_call(
        _experts_kernel,
        grid_spec=grid_spec,
        out_shape=jax.ShapeDtypeStruct(xs.shape, _F32),
        compiler_params=pltpu.CompilerParams(
            dimension_semantics=("arbitrary",),
            vmem_limit_bytes=60 * 1024 * 1024),
        name="experts",
    )(blk_e, n_valid, slot_par, next_e, xs, wgu, bgu, wdn, bdn)


def _combine_kernel(dcur_ref, dnext_ref, yb_ref, x1_ref, gt_ref, p_ref, g2_ref, b2_ref, pw_ref, gw_ref,
                    gb_ref, g3_ref, b3_ref, out_ref, gbuf, sems):
    ts = TS_COMBINE
    i = pl.program_id(0)
    slot = i % 2

    def start_gather(d_ref, s):
        def body(t, carry):
            for k in range(TOP_K):
                pltpu.make_async_copy(_row_tile(yb_ref, d_ref[t * TOP_K + k]),
                                      gbuf.at[s, k, pl.ds(t * GATHER_PITCH, ROW_CHUNKS)],
                                      sems.at[s]).start(priority=k % 2)
            return carry
        lax.fori_loop(0, ts, body, 0, unroll=8)

    @pl.when(i == 0)
    def _():
        start_gather(dcur_ref, 0)

    @pl.when(i + 1 < pl.num_programs(0))
    def _():
        start_gather(dnext_ref, 1 - slot)

    for k in range(TOP_K):
        pltpu.make_async_copy(yb_ref.at[pl.ds(0, ts * ROW_CHUNKS)],
                              gbuf.at[slot, k, pl.ds(0, ts * ROW_CHUNKS)], sems.at[slot]).wait()

    x1 = x1_ref[...]
    ffn = jnp.zeros((ts, D_MODEL), _F32)
    for k in range(TOP_K):
        ffn = ffn + gt_ref[:, k:k + 1] * _from_row_tiles(gbuf.at[slot, k], ts, GATHER_PITCH)
    x2 = _layer_norm(DEEPNORM_ALPHA * x1 + ffn, g2_ref[...], b2_ref[...])
    z = jnp.dot(x2.astype(_BF16), gw_ref[...], preferred_element_type=_F32) + gb_ref[...]
    ple = jnp.dot(p_ref[...].astype(_BF16), pw_ref[...], preferred_element_type=_F32)
    x3 = _layer_norm(DEEPNORM_ALPHA * x2 + jax.nn.sigmoid(z) * ple, g3_ref[...], b3_ref[...])
    out_ref[...] = x3


def _combine(dest_flat, yb, x1, gates_t, p2d, g2, b2, pw, gw, gb, g3, b3):
    n = dest_flat.shape[0] // TOP_K
    ts = TS_COMBINE
    last = n // ts - 1
    full = lambda shape: pl.BlockSpec(shape, lambda i: (0,) * len(shape))
    return pl.pallas_call(
        _combine_kernel,
        grid=(n // ts,),
        in_specs=[
            pl.BlockSpec((TOP_K * ts,), lambda i: (i,), memory_space=pltpu.SMEM),
            pl.BlockSpec((TOP_K * ts,), lambda i: (jnp.minimum(i + 1, last),), memory_space=pltpu.SMEM),
            pl.BlockSpec(memory_space=pl.ANY),
            pl.BlockSpec((ts, D_MODEL), lambda i: (i, 0)),
            pl.BlockSpec((ts, TOP_K), lambda i: (i, 0)),
            pl.BlockSpec((ts, PLE_DIM), lambda i: (i, 0)),
            full(g2.shape), full(b2.shape), full(pw.shape), full(gw.shape), full(gb.shape),
            full(g3.shape), full(b3.shape),
        ],
        out_specs=pl.BlockSpec((ts, D_MODEL), lambda i: (i, 0)),
        out_shape=jax.ShapeDtypeStruct((n, D_MODEL), _F32),
        scratch_shapes=[
            pltpu.VMEM((2, TOP_K, ts * GATHER_PITCH, LANES), _F32),
            pltpu.SemaphoreType.DMA((2,)),
        ],
        compiler_params=pltpu.CompilerParams(
            dimension_semantics=("arbitrary",),
            vmem_limit_bytes=48 * 1024 * 1024),
        name="combine",
    )(dest_flat, dest_flat, yb, x1, gates_t, p2d, g2, b2, pw, gw, gb, g3, b3)


def kernel(x, p, w_in, pool_mix, pool_scale, conv_w, w_out, ln1_g, ln1_b, router_w, router_b,
           w_gate_up, b_gate_up, w_down, b_down, ln2_g, ln2_b, ple_proj, ple_gate_w, ple_gate_b,
           ln3_g, ln3_b):
    assert DEPTH == 1 and x.shape[-1] == D_MODEL
    bsz, seq, d = x.shape
    n = bsz * seq
    assert seq % TS_MIX == 0 and n % TS_DISPATCH == 0 and n % TS_COMBINE == 0
    row = lambda v: v.reshape(1, -1)

    tri = jnp.triu(jnp.ones((TS_MIX, TS_MIX), _BF16))
    x1rt, x1, ri, gates, counts = _mix_route(
        x, w_in[0].astype(_BF16), pool_mix[0].astype(_BF16), row(pool_scale[0]), conv_w[0],
        w_out[0].astype(_BF16), row(ln1_g[0]), row(ln1_b[0]),
        router_w[0].T.astype(_BF16), jnp.broadcast_to(router_b[0][:, None], (N_EXPERTS, LANES)), tri)

    tm = TM_EXPERT
    n_slots = n * TOP_K + N_EXPERTS * tm
    nb = n_slots // tm
    cnt = counts[:, 0]
    padded = ((cnt + tm - 1) // tm) * tm
    pad_end = jnp.cumsum(padded)
    pad_start = (pad_end - padded).astype(jnp.int32)
    n_valid = (pad_end[-1] // tm).astype(jnp.int32)
    tile_start = jnp.arange(nb, dtype=jnp.int32) * tm
    blk_e = jnp.minimum(jnp.sum(tile_start[:, None] >= pad_end[None, :], axis=1), N_EXPERTS - 1)
    blk_e = jnp.where(jnp.arange(nb) < n_valid, blk_e, blk_e[n_valid - 1]).astype(jnp.int32)

    last_tile = jnp.where(padded > 0, pad_end // tm - 1, -1).astype(jnp.int32)
    e_ids = jnp.arange(N_EXPERTS, dtype=jnp.int32)
    later = jnp.logical_and(e_ids[None, :] > e_ids[:, None], (padded > 0)[None, :])
    next_nonempty = jnp.min(jnp.where(later, e_ids[None, :], N_EXPERTS), axis=1)
    next_e = jnp.where(next_nonempty == N_EXPERTS, -1, next_nonempty).astype(jnp.int32)
    switched = jnp.concatenate([jnp.zeros((1,), jnp.int32), (blk_e[1:] != blk_e[:-1]).astype(jnp.int32)])
    slot_par = (jnp.cumsum(switched) % 2).astype(jnp.int32)
    n_valid = n_valid.reshape(1)

    dest = _make_dest(pad_start, ri).T.reshape(-1)
    xs = _dispatch(last_tile, n_valid, dest, x1rt, n_slots)
    yb = _experts(blk_e, n_valid, slot_par, next_e, xs,
                  w_gate_up[0], b_gate_up[0][:, None, :], w_down[0], b_down[0][:, None, :])
    out = _combine(dest, yb, x1, gates[:TOP_K].T, p[0].reshape(n, PLE_DIM),
                   row(ln2_g[0]), row(ln2_b[0]), ple_proj[0].astype(_BF16),
                   ple_gate_w[0].astype(_BF16), row(ple_gate_b[0]), row(ln3_g[0]), row(ln3_b[0]))
    return out.reshape(bsz, seq, d)
```

```python
import functools

import jax
import jax.numpy as jnp
from jax import lax
from jax.experimental import pallas as pl
from jax.experimental.pallas import tpu as pltpu

D_MODEL = 1024
D_POOL = 512
D_CONV = 512
POOL_WINDOWS = (2, 4, 8, 16)
POOL_GROUP_DIM = 128
CONV_WIDTH = 3
D_IN_PROJ = D_POOL + 3 * D_CONV
N_EXPERTS = 32
TOP_K = 4
D_FF = 1024
SWIGLU_LIMIT = 7.0
SWIGLU_ALPHA = 1.702
PLE_DIM = 256
DEPTH = 1
DEEPNORM_ALPHA = (2.0 * DEPTH) ** 0.25
LN_EPS = 1e-5

LANES = 128
SUBLANES = 8
ROW_CHUNKS = D_MODEL // LANES
POOL_HALO = 16
CONV_HALO = 8
GATHER_PITCH = ROW_CHUNKS + 1

TS_MIX = 512
TS_DISPATCH = 2048
TM_EXPERT = 512
TS_COMBINE = 256

_BF16 = jnp.bfloat16
_F32 = jnp.float32


def _layer_norm(h, g, b):
    mu = jnp.mean(h, axis=-1, keepdims=True)
    c = h - mu
    var = jnp.mean(c * c, axis=-1, keepdims=True)
    return c * lax.rsqrt(var + LN_EPS) * g + b


def _to_row_tiles(ref, val):
    rows = val.shape[0]
    for c in range(ROW_CHUNKS):
        ref[pl.ds(c, rows, stride=ROW_CHUNKS), :] = val[:, c * LANES:(c + 1) * LANES]


def _from_row_tiles(ref, rows, pitch=ROW_CHUNKS):
    return jnp.concatenate(
        [ref[pl.ds(c, rows, stride=pitch), :] for c in range(ROW_CHUNKS)], axis=1)


def _row_tile(ref, r):
    return ref.at[pl.ds(pl.multiple_of(r * ROW_CHUNKS, ROW_CHUNKS), ROW_CHUNKS)]


def _mix_route_kernel(x_ref, w_in_ref, pmix_ref, pscale_ref, convw_ref, w_out_ref, g1_ref, b1_ref,
                      rwt_ref, rb_ref, tri_ref,
                      x1rt_ref, x1_ref, ri_ref, gate_ref, cnt_ref,
                      carry_v, carry_u, cnt_sc):
    b = pl.program_id(0)
    s = pl.program_id(1)
    ts = TS_MIX

    @pl.when(jnp.logical_and(b == 0, s == 0))
    def _():
        cnt_sc[...] = jnp.zeros_like(cnt_sc)

    @pl.when(s == 0)
    def _():
        carry_v[...] = jnp.zeros_like(carry_v)
        carry_u[...] = jnp.zeros_like(carry_u)

    xb = x_ref[0]
    proj = jnp.dot(xb.astype(_BF16), w_in_ref[...], preferred_element_type=_F32)
    vp = proj[:, :D_POOL]
    bg = proj[:, D_POOL:D_POOL + D_CONV]
    cg = proj[:, D_POOL + D_CONV:D_POOL + 2 * D_CONV]
    vc = proj[:, D_POOL + 2 * D_CONV:]

    ext = jnp.concatenate([carry_v[...], vp], axis=0)
    pos = lax.broadcasted_iota(jnp.int32, (ts, 1), 0) + s * ts
    mixed = []
    for g, w in enumerate(POOL_WINDOWS):
        lo, hi = g * POOL_GROUP_DIM, (g + 1) * POOL_GROUP_DIM
        acc = ext[:, lo:hi]
        sh = 1
        while sh < w:
            acc = acc + pltpu.roll(acc, sh, 0)
            sh *= 2
        cnt = jnp.minimum(pos + 1, w).astype(_F32)
        d = acc[POOL_HALO:] / cnt - vp[:, lo:hi]
        yg = jnp.dot(d.astype(_BF16), pmix_ref[g], preferred_element_type=_F32)
        mixed.append(yg * pscale_ref[:, lo:hi])
    carry_v[...] = vp[ts - POOL_HALO:]

    u = cg * vc
    extu = jnp.concatenate([carry_u[...], u], axis=0)
    u1 = pltpu.roll(extu, 1, 0)[CONV_HALO:]
    u2 = pltpu.roll(extu, 2, 0)[CONV_HALO:]
    yc = bg * (convw_ref[0:1, :] * u2 + convw_ref[1:2, :] * u1 + convw_ref[2:3, :] * u)
    carry_u[...] = u[ts - CONV_HALO:]
    mixed.append(yc)

    mix_in = jnp.concatenate(mixed, axis=1).astype(_BF16)
    mix = jnp.dot(mix_in, w_out_ref[...], preferred_element_type=_F32)
    x1 = _layer_norm(DEEPNORM_ALPHA * xb + mix, g1_ref[...], b1_ref[...])
    _to_row_tiles(x1rt_ref, x1)
    x1_ref[...] = x1

    logits = lax.dot_general(rwt_ref[...], x1.astype(_BF16), (((1,), (1,)), ((), ())),
                             preferred_element_type=_F32) + rb_ref[:, 0:1]
    eidx = lax.broadcasted_iota(jnp.int32, (N_EXPERTS, ts), 0).astype(_F32)
    vals, sels = [], []
    work = logits
    for k in range(TOP_K):
        m = jnp.max(work, axis=0, keepdims=True)
        first = jnp.min(jnp.where(work == m, eidx, float(N_EXPERTS)), axis=0, keepdims=True)
        sel = eidx == first
        work = jnp.where(sel, -jnp.inf, work)
        vals.append(m)
        sels.append(sel)
        ri_ref[k:k + 1, :] = first.astype(jnp.int32)
    exps = [jnp.exp(v - vals[0]) for v in vals]
    denom = exps[0] + exps[1] + exps[2] + exps[3]
    for k in range(TOP_K):
        gate_ref[k:k + 1, :] = exps[k] / denom
    gate_ref[TOP_K:, :] = jnp.zeros((SUBLANES - TOP_K, ts), _F32)

    chosen = jnp.zeros((N_EXPERTS, ts), _F32)
    for sel in sels:
        chosen = chosen + sel.astype(_F32)
    incl = jnp.dot(chosen.astype(_BF16), tri_ref[...], preferred_element_type=_F32)
    base = cnt_sc[:, 0:1] + (incl - chosen)
    for k in range(TOP_K):
        rank = jnp.sum(jnp.where(sels[k], base, 0.0), axis=0, keepdims=True)
        ri_ref[TOP_K + k:TOP_K + k + 1, :] = rank.astype(jnp.int32)
    total = cnt_sc[...] + jnp.sum(chosen, axis=1, keepdims=True)
    cnt_sc[...] = total
    cnt_ref[...] = total.astype(jnp.int32)


def _mix_route(x, w_in, pmix, pscale, convw, w_out, g1, b1, rwt, rb, tri):
    bsz, seq, d = x.shape
    n = bsz * seq
    ns = seq // TS_MIX
    full = lambda shape: pl.BlockSpec(shape, lambda b, s: (0,) * len(shape))
    return pl.pallas_call(
        _mix_route_kernel,
        grid=(bsz, ns),
        in_specs=[
            pl.BlockSpec((1, TS_MIX, d), lambda b, s: (b, s, 0)),
            full(w_in.shape), full(pmix.shape), full(pscale.shape), full(convw.shape),
            full(w_out.shape), full(g1.shape), full(b1.shape), full(rwt.shape), full(rb.shape),
            full(tri.shape),
        ],
        out_specs=[
            pl.BlockSpec((TS_MIX * ROW_CHUNKS, LANES), lambda b, s: (b * ns + s, 0)),
            pl.BlockSpec((TS_MIX, D_MODEL), lambda b, s: (b * ns + s, 0)),
            pl.BlockSpec((2 * TOP_K, TS_MIX), lambda b, s: (0, b * ns + s)),
            pl.BlockSpec((SUBLANES, TS_MIX), lambda b, s: (0, b * ns + s)),
            pl.BlockSpec((N_EXPERTS, LANES), lambda b, s: (0, 0)),
        ],
        out_shape=[
            jax.ShapeDtypeStruct((n * ROW_CHUNKS, LANES), _F32),
            jax.ShapeDtypeStruct((n, D_MODEL), _F32),
            jax.ShapeDtypeStruct((2 * TOP_K, n), jnp.int32),
            jax.ShapeDtypeStruct((SUBLANES, n), _F32),
            jax.ShapeDtypeStruct((N_EXPERTS, LANES), jnp.int32),
        ],
        scratch_shapes=[
            pltpu.VMEM((POOL_HALO, D_POOL), _F32),
            pltpu.VMEM((CONV_HALO, D_CONV), _F32),
            pltpu.VMEM((N_EXPERTS, LANES), _F32),
        ],
        compiler_params=pltpu.CompilerParams(
            dimension_semantics=("arbitrary", "arbitrary"),
            vmem_limit_bytes=48 * 1024 * 1024),
        name="mix_route",
    )(x, w_in, pmix, pscale, convw, w_out, g1, b1, rwt, rb, tri)


def _make_dest_kernel(start_ref, ri_ref, dest_ref):
    ei = ri_ref[0:TOP_K, :]
    acc = ri_ref[TOP_K:, :]
    for e in range(N_EXPERTS):
        acc = acc + jnp.where(ei == e, start_ref[e], 0)
    dest_ref[...] = acc


def _make_dest(pad_start, ri):
    n = ri.shape[1]
    return pl.pallas_call(
        _make_dest_kernel,
        in_specs=[pl.BlockSpec(memory_space=pltpu.SMEM),
                  pl.BlockSpec((2 * TOP_K, n), lambda: (0, 0))],
        out_specs=pl.BlockSpec((TOP_K, n), lambda: (0, 0)),
        out_shape=jax.ShapeDtypeStruct((TOP_K, n), jnp.int32),
        name="make_dest",
    )(pad_start, ri)


def _dispatch_kernel(last_ref, nv_ref, dest_ref, src_ref, out_ref, zbuf, sem, zsem):
    ts = TS_DISPATCH
    tile_rows = TM_EXPERT * ROW_CHUNKS
    n_tiles = out_ref.shape[0] // tile_rows

    @pl.when(pl.program_id(0) == 0)
    def _():
        zbuf[...] = jnp.zeros_like(zbuf)

        def zero_tile(tile):
            return pltpu.make_async_copy(
                zbuf, out_ref.at[pl.ds(pl.multiple_of(tile * tile_rows, tile_rows), tile_rows)], zsem)

        def each_tile(fn):
            for e in range(N_EXPERTS):
                @pl.when(last_ref[e] >= 0)
                def _():
                    fn(zero_tile(last_ref[e]))

            def tail(j, carry):
                fn(zero_tile(j))
                return carry
            lax.fori_loop(nv_ref[0], n_tiles, tail, 0)

        each_tile(lambda c: c.start())
        each_tile(lambda c: c.wait())

    def body(t, carry):
        for k in range(TOP_K):
            pltpu.make_async_copy(_row_tile(src_ref, t), _row_tile(out_ref, dest_ref[t * TOP_K + k]),
                                  sem).start(priority=k % 2)
        return carry

    lax.fori_loop(0, ts, body, 0, unroll=8)
    for k in range(TOP_K):
        pltpu.make_async_copy(src_ref, out_ref.at[pl.ds(0, ts * ROW_CHUNKS)], sem).wait()


def _dispatch(last_tile, n_valid, dest_flat, x1rt, n_slots):
    n = dest_flat.shape[0] // TOP_K
    grid_spec = pltpu.PrefetchScalarGridSpec(
        num_scalar_prefetch=2,
        grid=(n // TS_DISPATCH,),
        in_specs=[
            pl.BlockSpec((TOP_K * TS_DISPATCH,), lambda i, lt, nv: (i,), memory_space=pltpu.SMEM),
            pl.BlockSpec((TS_DISPATCH * ROW_CHUNKS, LANES), lambda i, lt, nv: (i, 0)),
        ],
        out_specs=pl.BlockSpec(memory_space=pl.ANY),
        scratch_shapes=[
            pltpu.VMEM((TM_EXPERT * ROW_CHUNKS, LANES), _F32),
            pltpu.SemaphoreType.DMA(()),
            pltpu.SemaphoreType.DMA(()),
        ],
    )
    return pl.pallas_call(
        _dispatch_kernel,
        grid_spec=grid_spec,
        out_shape=jax.ShapeDtypeStruct((n_slots * ROW_CHUNKS, LANES), _F32),
        compiler_params=pltpu.CompilerParams(dimension_semantics=("arbitrary",)),
        name="dispatch",
    )(last_tile, n_valid, dest_flat, x1rt)


def _experts_kernel(be_ref, nv_ref, par_ref, nxt_ref, xs_ref, wgu_hbm, bgu_ref, wdn_hbm, bdn_ref, out_ref,
                    wgu_f32, wdn_f32, wgu_bf, wdn_bf, sems):
    j = pl.program_id(0)
    tm = TM_EXPERT
    valid = j < nv_ref[0]
    new_expert = jnp.logical_or(j == 0, be_ref[j] != be_ref[jnp.maximum(j - 1, 0)])

    def weight_copies(e, s):
        return (pltpu.make_async_copy(wgu_hbm.at[e], wgu_f32.at[s], sems.at[0, s]),
                pltpu.make_async_copy(wdn_hbm.at[e], wdn_f32.at[s], sems.at[1, s]))

    @pl.when(j == 0)
    def _():
        for c in weight_copies(be_ref[0], par_ref[0]):
            c.start()

    @pl.when(jnp.logical_and(valid, new_expert))
    def _():
        s = par_ref[j]
        for c in weight_copies(be_ref[j], s):
            c.wait()

        nxt = nxt_ref[be_ref[j]]

        @pl.when(nxt >= 0)
        def _():
            for c in weight_copies(nxt, 1 - s):
                c.start()

        wgu_bf[...] = wgu_f32[s].astype(_BF16)
        wdn_bf[...] = wdn_f32[s].astype(_BF16)

    @pl.when(valid)
    def _():
        x = _from_row_tiles(xs_ref, tm).astype(_BF16)
        gu = jnp.dot(x, wgu_bf[...], preferred_element_type=_F32) + bgu_ref[0]
        gate = jnp.minimum(gu[:, :D_FF], SWIGLU_LIMIT)
        up = jnp.clip(gu[:, D_FF:], -SWIGLU_LIMIT, SWIGLU_LIMIT)
        glu = gate * jax.nn.sigmoid(SWIGLU_ALPHA * gate)
        act = ((up + 1.0) * glu).astype(_BF16)
        y = jnp.dot(act, wdn_bf[...], preferred_element_type=_F32) + bdn_ref[0]
        _to_row_tiles(out_ref, y)

    @pl.when(j >= nv_ref[0])
    def _():
        out_ref[...] = jnp.zeros_like(out_ref)


def _experts(blk_e, n_valid, slot_par, next_e, xs, wgu, bgu, wdn, bdn):
    n_slots = xs.shape[0] // ROW_CHUNKS
    nb = n_slots // TM_EXPERT
    rows = TM_EXPERT * ROW_CHUNKS
    grid_spec = pltpu.PrefetchScalarGridSpec(
        num_scalar_prefetch=4,
        grid=(nb,),
        in_specs=[
            pl.BlockSpec((rows, LANES), lambda j, be, nv, par, nxt: (jnp.minimum(j, nv[0] - 1), 0)),
            pl.BlockSpec(memory_space=pl.ANY),
            pl.BlockSpec((1, 1, 2 * D_FF), lambda j, be, nv, par, nxt: (be[j], 0, 0)),
            pl.BlockSpec(memory_space=pl.ANY),
            pl.BlockSpec((1, 1, D_MODEL), lambda j, be, nv, par, nxt: (be[j], 0, 0)),
        ],
        out_specs=pl.BlockSpec((rows, LANES), lambda j, be, nv, par, nxt: (j, 0)),
        scratch_shapes=[
            pltpu.VMEM((2, D_MODEL, 2 * D_FF), _F32),
            pltpu.VMEM((2, D_FF, D_MODEL), _F32),
            pltpu.VMEM((D_MODEL, 2 * D_FF), _BF16),
            pltpu.VMEM((D_FF, D_MODEL), _BF16),
            pltpu.SemaphoreType.DMA((2, 2)),
        ],
    )
    return pl.pallas_call(
        _experts_kernel,
        grid_spec=grid_spec,
        out_shape=jax.ShapeDtypeStruct(xs.shape, _F32),
        compiler_params=pltpu.CompilerParams(
            dimension_semantics=("arbitrary",),
            vmem_limit_bytes=60 * 1024 * 1024),
        name="experts",
    )(blk_e, n_valid, slot_par, next_e, xs, wgu, bgu, wdn, bdn)


def _combine_kernel(dcur_ref, dnext_ref, yb_ref, x1_ref, gt_ref, p_ref, g2_ref, b2_ref, pw_ref, gw_ref,
                    gb_ref, g3_ref, b3_ref, out_ref, gbuf, hbuf, sems):
    ts = TS_COMBINE
    grp = SUBLANES
    i = pl.program_id(0)
    slot = i % 2
    nxt = 1 - slot

    def start_rows(d_ref, s, t0):
        for j in range(grp):
            t = t0 + j
            for k in range(TOP_K):
                pltpu.make_async_copy(_row_tile(yb_ref, d_ref[t * TOP_K + k]),
                                      gbuf.at[s, k, pl.ds(t * GATHER_PITCH, ROW_CHUNKS)],
                                      sems.at[s]).start(priority=k % 2)

    def wait_tile(s):
        for k in range(TOP_K):
            pltpu.make_async_copy(yb_ref.at[pl.ds(0, ts * ROW_CHUNKS)],
                                  gbuf.at[s, k, pl.ds(0, ts * ROW_CHUNKS)], sems.at[s]).wait()

    @pl.when(i == 0)
    def _():
        def first(r, carry):
            start_rows(dcur_ref, 0, r * grp)
            return carry
        lax.fori_loop(0, ts // grp, first, 0)

    wait_tile(slot)

    def group(r, carry):
        t0 = pl.multiple_of(r * grp, grp)
        x1g = x1_ref[pl.ds(t0, grp), :]
        gts = gt_ref[pl.ds(t0, grp), :]
        ys = []
        for k in range(TOP_K):
            rows_k = gbuf.at[slot, k]
            ys.append(jnp.concatenate(
                [rows_k[pl.ds(t0 * GATHER_PITCH + c, grp, stride=GATHER_PITCH), :] for c in range(ROW_CHUNKS)],
                axis=1))
        start_rows(dnext_ref, nxt, t0)
        ffn = gts[:, 0:1] * ys[0]
        for k in range(1, TOP_K):
            ffn = ffn + gts[:, k:k + 1] * ys[k]
        hbuf[pl.ds(t0, grp), :] = DEEPNORM_ALPHA * x1g + ffn
        return carry

    lax.fori_loop(0, ts // grp, group, 0)

    @pl.when(i == pl.num_programs(0) - 1)
    def _():
        wait_tile(nxt)

    x2 = _layer_norm(hbuf[...], g2_ref[...], b2_ref[...])
    z = jnp.dot(x2.astype(_BF16), gw_ref[...], preferred_element_type=_F32) + gb_ref[...]
    ple = jnp.dot(p_ref[...].astype(_BF16), pw_ref[...], preferred_element_type=_F32)
    x3 = _layer_norm(DEEPNORM_ALPHA * x2 + jax.nn.sigmoid(z) * ple, g3_ref[...], b3_ref[...])
    out_ref[...] = x3


def _combine(dest_flat, yb, x1, gates_t, p2d, g2, b2, pw, gw, gb, g3, b3):
    n = dest_flat.shape[0] // TOP_K
    ts = TS_COMBINE
    last = n // ts - 1
    full = lambda shape: pl.BlockSpec(shape, lambda i: (0,) * len(shape))
    return pl.pallas_call(
        _combine_kernel,
        grid=(n // ts,),
        in_specs=[
            pl.BlockSpec((TOP_K * ts,), lambda i: (i,), memory_space=pltpu.SMEM),
            pl.BlockSpec((TOP_K * ts,), lambda i: (jnp.minimum(i + 1, last),), memory_space=pltpu.SMEM),
            pl.BlockSpec(memory_space=pl.ANY),
            pl.BlockSpec((ts, D_MODEL), lambda i: (i, 0)),
            pl.BlockSpec((ts, TOP_K), lambda i: (i, 0)),
            pl.BlockSpec((ts, PLE_DIM), lambda i: (i, 0)),
            full(g2.shape), full(b2.shape), full(pw.shape), full(gw.shape), full(gb.shape),
            full(g3.shape), full(b3.shape),
        ],
        out_specs=pl.BlockSpec((ts, D_MODEL), lambda i: (i, 0)),
        out_shape=jax.ShapeDtypeStruct((n, D_MODEL), _F32),
        scratch_shapes=[
            pltpu.VMEM((2, TOP_K, ts * GATHER_PITCH, LANES), _F32),
            pltpu.VMEM((ts, D_MODEL), _F32),
            pltpu.SemaphoreType.DMA((2,)),
        ],
        compiler_params=pltpu.CompilerParams(
            dimension_semantics=("arbitrary",),
            vmem_limit_bytes=48 * 1024 * 1024),
        name="combine",
    )(dest_flat, dest_flat, yb, x1, gates_t, p2d, g2, b2, pw, gw, gb, g3, b3)


def kernel(x, p, w_in, pool_mix, pool_scale, conv_w, w_out, ln1_g, ln1_b, router_w, router_b,
           w_gate_up, b_gate_up, w_down, b_down, ln2_g, ln2_b, ple_proj, ple_gate_w, ple_gate_b,
           ln3_g, ln3_b):
    assert DEPTH == 1 and x.shape[-1] == D_MODEL
    bsz, seq, d = x.shape
    n = bsz * seq
    assert seq % TS_MIX == 0 and n % TS_DISPATCH == 0 and n % TS_COMBINE == 0
    row = lambda v: v.reshape(1, -1)

    tri = jnp.triu(jnp.ones((TS_MIX, TS_MIX), _BF16))
    x1rt, x1, ri, gates, counts = _mix_route(
        x, w_in[0].astype(_BF16), pool_mix[0].astype(_BF16), row(pool_scale[0]), conv_w[0],
        w_out[0].astype(_BF16), row(ln1_g[0]), row(ln1_b[0]),
        router_w[0].T.astype(_BF16), jnp.broadcast_to(router_b[0][:, None], (N_EXPERTS, LANES)), tri)

    tm = TM_EXPERT
    n_slots = n * TOP_K + N_EXPERTS * tm
    nb = n_slots // tm
    cnt = counts[:, 0]
    padded = ((cnt + tm - 1) // tm) * tm
    pad_end = jnp.cumsum(padded)
    pad_start = (pad_end - padded).astype(jnp.int32)
    n_valid = (pad_end[-1] // tm).astype(jnp.int32)
    tile_start = jnp.arange(nb, dtype=jnp.int32) * tm
    blk_e = jnp.minimum(jnp.sum(tile_start[:, None] >= pad_end[None, :], axis=1), N_EXPERTS - 1)
    blk_e = jnp.where(jnp.arange(nb) < n_valid, blk_e, blk_e[n_valid - 1]).astype(jnp.int32)

    last_tile = jnp.where(padded > 0, pad_end // tm - 1, -1).astype(jnp.int32)
    e_ids = jnp.arange(N_EXPERTS, dtype=jnp.int32)
    later = jnp.logical_and(e_ids[None, :] > e_ids[:, None], (padded > 0)[None, :])
    next_nonempty = jnp.min(jnp.where(later, e_ids[None, :], N_EXPERTS), axis=1)
    next_e = jnp.where(next_nonempty == N_EXPERTS, -1, next_nonempty).astype(jnp.int32)
    switched = jnp.concatenate([jnp.zeros((1,), jnp.int32), (blk_e[1:] != blk_e[:-1]).astype(jnp.int32)])
    slot_par = (jnp.cumsum(switched) % 2).astype(jnp.int32)
    n_valid = n_valid.reshape(1)

    dest = _make_dest(pad_start, ri).T.reshape(-1)
    xs = _dispatch(last_tile, n_valid, dest, x1rt, n_slots)
    yb = _experts(blk_e, n_valid, slot_par, next_e, xs,
                  w_gate_up[0], b_gate_up[0][:, None, :], w_down[0], b_down[0][:, None, :])
    out = _combine(dest, yb, x1, gates[:TOP_K].T, p[0].reshape(n, PLE_DIM),
                   row(ln2_g[0]), row(ln2_b[0]), ple_proj[0].astype(_BF16),
                   ple_gate_w[0].astype(_BF16), row(ple_gate_b[0]), row(ln3_g[0]), row(ln3_b[0]))
    return out.reshape(bsz, seq, d)
```

```python
import functools

import jax
import jax.numpy as jnp
from jax import lax
from jax.experimental import pallas as pl
from jax.experimental.pallas import tpu as pltpu

D_MODEL = 1024
D_POOL = 512
D_CONV = 512
POOL_WINDOWS = (2, 4, 8, 16)
POOL_GROUP_DIM = 128
CONV_WIDTH = 3
D_IN_PROJ = D_POOL + 3 * D_CONV
N_EXPERTS = 32
TOP_K = 4
D_FF = 1024
SWIGLU_LIMIT = 7.0
SWIGLU_ALPHA = 1.702
PLE_DIM = 256
DEPTH = 1
DEEPNORM_ALPHA = (2.0 * DEPTH) ** 0.25
LN_EPS = 1e-5

LANES = 128
SUBLANES = 8
ROW_CHUNKS = D_MODEL // LANES
POOL_HALO = 16
CONV_HALO = 8
GATHER_PITCH = ROW_CHUNKS + 1

TS_MIX = 512
TS_DISPATCH = 2048
TM_EXPERT = 512
TS_COMBINE = 256
COMBINE_GROUP = 8

_BF16 = jnp.bfloat16
_F32 = jnp.float32


def _layer_norm(h, g, b):
    mu = jnp.mean(h, axis=-1, keepdims=True)
    c = h - mu
    var = jnp.mean(c * c, axis=-1, keepdims=True)
    return c * lax.rsqrt(var + LN_EPS) * g + b


def _to_row_tiles(ref, val, pitch=ROW_CHUNKS):
    rows = val.shape[0]
    for c in range(ROW_CHUNKS):
        ref[pl.ds(c, rows, stride=pitch), :] = val[:, c * LANES:(c + 1) * LANES]
    for c in range(ROW_CHUNKS, pitch):
        ref[pl.ds(c, rows, stride=pitch), :] = jnp.zeros((rows, LANES), val.dtype)


def _from_row_tiles(ref, rows, pitch=ROW_CHUNKS):
    return jnp.concatenate(
        [ref[pl.ds(c, rows, stride=pitch), :] for c in range(ROW_CHUNKS)], axis=1)


def _row_tile(ref, r, pitch=ROW_CHUNKS):
    if pitch == ROW_CHUNKS:
        return ref.at[pl.ds(pl.multiple_of(r * ROW_CHUNKS, ROW_CHUNKS), ROW_CHUNKS)]
    return ref.at[pl.ds(r * pitch, ROW_CHUNKS)]


def _mix_route_kernel(x_ref, w_in_ref, pmix_ref, pscale_ref, convw_ref, w_out_ref, g1_ref, b1_ref,
                      rwt_ref, rb_ref, tri_ref,
                      x1rt_ref, x1_ref, ri_ref, gate_ref, cnt_ref,
                      carry_v, carry_u, cnt_sc):
    b = pl.program_id(0)
    s = pl.program_id(1)
    ts = TS_MIX

    @pl.when(jnp.logical_and(b == 0, s == 0))
    def _():
        cnt_sc[...] = jnp.zeros_like(cnt_sc)

    @pl.when(s == 0)
    def _():
        carry_v[...] = jnp.zeros_like(carry_v)
        carry_u[...] = jnp.zeros_like(carry_u)

    xb = x_ref[0]
    proj = jnp.dot(xb.astype(_BF16), w_in_ref[...], preferred_element_type=_F32)
    vp = proj[:, :D_POOL]
    bg = proj[:, D_POOL:D_POOL + D_CONV]
    cg = proj[:, D_POOL + D_CONV:D_POOL + 2 * D_CONV]
    vc = proj[:, D_POOL + 2 * D_CONV:]

    ext = jnp.concatenate([carry_v[...], vp], axis=0)
    pos = lax.broadcasted_iota(jnp.int32, (ts, 1), 0) + s * ts
    mixed = []
    for g, w in enumerate(POOL_WINDOWS):
        lo, hi = g * POOL_GROUP_DIM, (g + 1) * POOL_GROUP_DIM
        acc = ext[:, lo:hi]
        sh = 1
        while sh < w:
            acc = acc + pltpu.roll(acc, sh, 0)
            sh *= 2
        cnt = jnp.minimum(pos + 1, w).astype(_F32)
        d = acc[POOL_HALO:] / cnt - vp[:, lo:hi]
        yg = jnp.dot(d.astype(_BF16), pmix_ref[g], preferred_element_type=_F32)
        mixed.append(yg * pscale_ref[:, lo:hi])
    carry_v[...] = vp[ts - POOL_HALO:]

    u = cg * vc
    extu = jnp.concatenate([carry_u[...], u], axis=0)
    u1 = pltpu.roll(extu, 1, 0)[CONV_HALO:]
    u2 = pltpu.roll(extu, 2, 0)[CONV_HALO:]
    yc = bg * (convw_ref[0:1, :] * u2 + convw_ref[1:2, :] * u1 + convw_ref[2:3, :] * u)
    carry_u[...] = u[ts - CONV_HALO:]
    mixed.append(yc)

    mix_in = jnp.concatenate(mixed, axis=1).astype(_BF16)
    mix = jnp.dot(mix_in, w_out_ref[...], preferred_element_type=_F32)
    x1 = _layer_norm(DEEPNORM_ALPHA * xb + mix, g1_ref[...], b1_ref[...])
    _to_row_tiles(x1rt_ref, x1)
    x1_ref[...] = x1

    logits = lax.dot_general(rwt_ref[...], x1.astype(_BF16), (((1,), (1,)), ((), ())),
                             preferred_element_type=_F32) + rb_ref[:, 0:1]
    eidx = lax.broadcasted_iota(jnp.int32, (N_EXPERTS, ts), 0).astype(_F32)
    vals, sels = [], []
    work = logits
    for k in range(TOP_K):
        m = jnp.max(work, axis=0, keepdims=True)
        first = jnp.min(jnp.where(work == m, eidx, float(N_EXPERTS)), axis=0, keepdims=True)
        sel = eidx == first
        work = jnp.where(sel, -jnp.inf, work)
        vals.append(m)
        sels.append(sel)
        ri_ref[k:k + 1, :] = first.astype(jnp.int32)
    exps = [jnp.exp(v - vals[0]) for v in vals]
    denom = exps[0] + exps[1] + exps[2] + exps[3]
    for k in range(TOP_K):
        gate_ref[k:k + 1, :] = exps[k] / denom
    gate_ref[TOP_K:, :] = jnp.zeros((SUBLANES - TOP_K, ts), _F32)

    chosen = jnp.zeros((N_EXPERTS, ts), _F32)
    for sel in sels:
        chosen = chosen + sel.astype(_F32)
    incl = jnp.dot(chosen.astype(_BF16), tri_ref[...], preferred_element_type=_F32)
    base = cnt_sc[:, 0:1] + (incl - chosen)
    for k in range(TOP_K):
        rank = jnp.sum(jnp.where(sels[k], base, 0.0), axis=0, keepdims=True)
        ri_ref[TOP_K + k:TOP_K + k + 1, :] = rank.astype(jnp.int32)
    total = cnt_sc[...] + jnp.sum(chosen, axis=1, keepdims=True)
    cnt_sc[...] = total
    cnt_ref[...] = total.astype(jnp.int32)


def _mix_route(x, w_in, pmix, pscale, convw, w_out, g1, b1, rwt, rb, tri):
    bsz, seq, d = x.shape
    n = bsz * seq
    ns = seq // TS_MIX
    full = lambda shape: pl.BlockSpec(shape, lambda b, s: (0,) * len(shape))
    return pl.pallas_call(
        _mix_route_kernel,
        grid=(bsz, ns),
        in_specs=[
            pl.BlockSpec((1, TS_MIX, d), lambda b, s: (b, s, 0)),
            full(w_in.shape), full(pmix.shape), full(pscale.shape), full(convw.shape),
            full(w_out.shape), full(g1.shape), full(b1.shape), full(rwt.shape), full(rb.shape),
            full(tri.shape),
        ],
        out_specs=[
            pl.BlockSpec((TS_MIX * ROW_CHUNKS, LANES), lambda b, s: (b * ns + s, 0)),
            pl.BlockSpec((TS_MIX, D_MODEL), lambda b, s: (b * ns + s, 0)),
            pl.BlockSpec((2 * TOP_K, TS_MIX), lambda b, s: (0, b * ns + s)),
            pl.BlockSpec((SUBLANES, TS_MIX), lambda b, s: (0, b * ns + s)),
            pl.BlockSpec((N_EXPERTS, LANES), lambda b, s: (0, 0)),
        ],
        out_shape=[
            jax.ShapeDtypeStruct((n * ROW_CHUNKS, LANES), _F32),
            jax.ShapeDtypeStruct((n, D_MODEL), _F32),
            jax.ShapeDtypeStruct((2 * TOP_K, n), jnp.int32),
            jax.ShapeDtypeStruct((SUBLANES, n), _F32),
            jax.ShapeDtypeStruct((N_EXPERTS, LANES), jnp.int32),
        ],
        scratch_shapes=[
            pltpu.VMEM((POOL_HALO, D_POOL), _F32),
            pltpu.VMEM((CONV_HALO, D_CONV), _F32),
            pltpu.VMEM((N_EXPERTS, LANES), _F32),
        ],
        compiler_params=pltpu.CompilerParams(
            dimension_semantics=("arbitrary", "arbitrary"),
            vmem_limit_bytes=48 * 1024 * 1024),
        name="mix_route",
    )(x, w_in, pmix, pscale, convw, w_out, g1, b1, rwt, rb, tri)


def _make_dest_kernel(start_ref, ri_ref, dest_ref):
    ei = ri_ref[0:TOP_K, :]
    acc = ri_ref[TOP_K:, :]
    for e in range(N_EXPERTS):
        acc = acc + jnp.where(ei == e, start_ref[e], 0)
    dest_ref[...] = acc


def _make_dest(pad_start, ri):
    n = ri.shape[1]
    return pl.pallas_call(
        _make_dest_kernel,
        in_specs=[pl.BlockSpec(memory_space=pltpu.SMEM),
                  pl.BlockSpec((2 * TOP_K, n), lambda: (0, 0))],
        out_specs=pl.BlockSpec((TOP_K, n), lambda: (0, 0)),
        out_shape=jax.ShapeDtypeStruct((TOP_K, n), jnp.int32),
        name="make_dest",
    )(pad_start, ri)


def _dispatch_kernel(last_ref, nv_ref, dest_ref, src_ref, out_ref, zbuf, sem, zsem):
    ts = TS_DISPATCH
    tile_rows = TM_EXPERT * ROW_CHUNKS
    n_tiles = out_ref.shape[0] // tile_rows

    @pl.when(pl.program_id(0) == 0)
    def _():
        zbuf[...] = jnp.zeros_like(zbuf)

        def zero_tile(tile):
            return pltpu.make_async_copy(
                zbuf, out_ref.at[pl.ds(pl.multiple_of(tile * tile_rows, tile_rows), tile_rows)], zsem)

        def each_tile(fn):
            for e in range(N_EXPERTS):
                @pl.when(last_ref[e] >= 0)
                def _():
                    fn(zero_tile(last_ref[e]))

            def tail(j, carry):
                fn(zero_tile(j))
                return carry
            lax.fori_loop(nv_ref[0], n_tiles, tail, 0)

        each_tile(lambda c: c.start())
        each_tile(lambda c: c.wait())

    def body(t, carry):
        for k in range(TOP_K):
            pltpu.make_async_copy(_row_tile(src_ref, t), _row_tile(out_ref, dest_ref[t * TOP_K + k]),
                                  sem).start(priority=k % 2)
        return carry

    lax.fori_loop(0, ts, body, 0, unroll=8)
    for k in range(TOP_K):
        pltpu.make_async_copy(src_ref, out_ref.at[pl.ds(0, ts * ROW_CHUNKS)], sem).wait()


def _dispatch(last_tile, n_valid, dest_flat, x1rt, n_slots):
    n = dest_flat.shape[0] // TOP_K
    grid_spec = pltpu.PrefetchScalarGridSpec(
        num_scalar_prefetch=2,
        grid=(n // TS_DISPATCH,),
        in_specs=[
            pl.BlockSpec((TOP_K * TS_DISPATCH,), lambda i, lt, nv: (i,), memory_space=pltpu.SMEM),
            pl.BlockSpec((TS_DISPATCH * ROW_CHUNKS, LANES), lambda i, lt, nv: (i, 0)),
        ],
        out_specs=pl.BlockSpec(memory_space=pl.ANY),
        scratch_shapes=[
            pltpu.VMEM((TM_EXPERT * ROW_CHUNKS, LANES), _F32),
            pltpu.SemaphoreType.DMA(()),
            pltpu.SemaphoreType.DMA(()),
        ],
    )
    return pl.pallas_call(
        _dispatch_kernel,
        grid_spec=grid_spec,
        out_shape=jax.ShapeDtypeStruct((n_slots * ROW_CHUNKS, LANES), _F32),
        compiler_params=pltpu.CompilerParams(dimension_semantics=("arbitrary",)),
        name="dispatch",
    )(last_tile, n_valid, dest_flat, x1rt)


def _experts_kernel(be_ref, nv_ref, par_ref, nxt_ref, xs_ref, wgu_hbm, bgu_ref, wdn_hbm, bdn_ref, out_ref,
                    wgu_f32, wdn_f32, wgu_bf, wdn_bf, sems):
    j = pl.program_id(0)
    tm = TM_EXPERT
    valid = j < nv_ref[0]
    new_expert = jnp.logical_or(j == 0, be_ref[j] != be_ref[jnp.maximum(j - 1, 0)])

    def weight_copies(e, s):
        return (pltpu.make_async_copy(wgu_hbm.at[e], wgu_f32.at[s], sems.at[0, s]),
                pltpu.make_async_copy(wdn_hbm.at[e], wdn_f32.at[s], sems.at[1, s]))

    @pl.when(j == 0)
    def _():
        for c in weight_copies(be_ref[0], par_ref[0]):
            c.start()

    @pl.when(jnp.logical_and(valid, new_expert))
    def _():
        s = par_ref[j]
        for c in weight_copies(be_ref[j], s):
            c.wait()

        nxt = nxt_ref[be_ref[j]]

        @pl.when(nxt >= 0)
        def _():
            for c in weight_copies(nxt, 1 - s):
                c.start()

        wgu_bf[...] = wgu_f32[s].astype(_BF16)
        wdn_bf[...] = wdn_f32[s].astype(_BF16)

    @pl.when(valid)
    def _():
        x = _from_row_tiles(xs_ref, tm).astype(_BF16)
        gu = jnp.dot(x, wgu_bf[...], preferred_element_type=_F32) + bgu_ref[0]
        gate = jnp.minimum(gu[:, :D_FF], SWIGLU_LIMIT)
        up = jnp.clip(gu[:, D_FF:], -SWIGLU_LIMIT, SWIGLU_LIMIT)
        glu = gate * jax.nn.sigmoid(SWIGLU_ALPHA * gate)
        act = ((up + 1.0) * glu).astype(_BF16)
        y = jnp.dot(act, wdn_bf[...], preferred_element_type=_F32) + bdn_ref[0]
        _to_row_tiles(out_ref, y, GATHER_PITCH)

    @pl.when(j >= nv_ref[0])
    def _():
        out_ref[...] = jnp.zeros_like(out_ref)


def _experts(blk_e, n_valid, slot_par, next_e, xs, wgu, bgu, wdn, bdn):
    n_slots = xs.shape[0] // ROW_CHUNKS
    nb = n_slots // TM_EXPERT
    rows = TM_EXPERT * ROW_CHUNKS
    grid_spec = pltpu.PrefetchScalarGridSpec(
        num_scalar_prefetch=4,
        grid=(nb,),
        in_specs=[
            pl.BlockSpec((rows, LANES), lambda j, be, nv, par, nxt: (jnp.minimum(j, nv[0] - 1), 0)),
            pl.BlockSpec(memory_space=pl.ANY),
            pl.BlockSpec((1, 1, 2 * D_FF), lambda j, be, nv, par, nxt: (be[j], 0, 0)),
            pl.BlockSpec(memory_space=pl.ANY),
            pl.BlockSpec((1, 1, D_MODEL), lambda j, be, nv, par, nxt: (be[j], 0, 0)),
        ],
        out_specs=pl.BlockSpec((TM_EXPERT * GATHER_PITCH, LANES), lambda j, be, nv, par, nxt: (j, 0)),
        scratch_shapes=[
            pltpu.VMEM((2, D_MODEL, 2 * D_FF), _F32),
            pltpu.VMEM((2, D_FF, D_MODEL), _F32),
            pltpu.VMEM((D_MODEL, 2 * D_FF), _BF16),
            pltpu.VMEM((D_FF, D_MODEL), _BF16),
            pltpu.SemaphoreType.DMA((2, 2)),
        ],
    )
    return pl.pallas_call(
        _experts_kernel,
        grid_spec=grid_spec,
        out_shape=jax.ShapeDtypeStruct((n_slots * GATHER_PITCH, LANES), _F32),
        compiler_params=pltpu.CompilerParams(
            dimension_semantics=("arbitrary",),
            vmem_limit_bytes=60 * 1024 * 1024),
        name="experts",
    )(blk_e, n_valid, slot_par, next_e, xs, wgu, bgu, wdn, bdn)


def _combine_kernel(dcur_ref, dnext_ref, yb_ref, x1_ref, gt_ref, p_ref, g2_ref, b2_ref, pw_ref, gw_ref,
                    gb_ref, g3_ref, b3_ref, out_ref, gbuf, hbuf, sems):
    ts = TS_COMBINE
    grp = COMBINE_GROUP
    i = pl.program_id(0)
    slot = i % 2
    nxt = 1 - slot

    def start_rows(d_ref, s, t0):
        for j in range(grp):
            t = t0 + j
            for k in range(TOP_K):
                pltpu.make_async_copy(_row_tile(yb_ref, d_ref[t * TOP_K + k], GATHER_PITCH),
                                      gbuf.at[s, k, pl.ds(t * GATHER_PITCH, ROW_CHUNKS)],
                                      sems.at[s]).start(priority=k % 2)

    def wait_tile(s):
        for k in range(TOP_K):
            pltpu.make_async_copy(yb_ref.at[pl.ds(0, ts * ROW_CHUNKS)],
                                  gbuf.at[s, k, pl.ds(0, ts * ROW_CHUNKS)], sems.at[s]).wait()

    @pl.when(i == 0)
    def _():
        def first(r, carry):
            start_rows(dcur_ref, 0, r * grp)
            return carry
        lax.fori_loop(0, ts // grp, first, 0)

    wait_tile(slot)

    def group(r, carry):
        t0 = pl.multiple_of(r * grp, grp)
        x1g = x1_ref[pl.ds(t0, grp), :]
        gts = gt_ref[pl.ds(t0, grp), :]
        ys = []
        for k in range(TOP_K):
            rows_k = gbuf.at[slot, k]
            ys.append(jnp.concatenate(
                [rows_k[pl.ds(t0 * GATHER_PITCH + c, grp, stride=GATHER_PITCH), :] for c in range(ROW_CHUNKS)],
                axis=1))
        start_rows(dnext_ref, nxt, t0)
        ffn = gts[:, 0:1] * ys[0]
        for k in range(1, TOP_K):
            ffn = ffn + gts[:, k:k + 1] * ys[k]
        hbuf[pl.ds(t0, grp), :] = DEEPNORM_ALPHA * x1g + ffn
        return carry

    lax.fori_loop(0, ts // grp, group, 0)

    @pl.when(i == pl.num_programs(0) - 1)
    def _():
        wait_tile(nxt)

    x2 = _layer_norm(hbuf[...], g2_ref[...], b2_ref[...])
    z = jnp.dot(x2.astype(_BF16), gw_ref[...], preferred_element_type=_F32) + gb_ref[...]
    ple = jnp.dot(p_ref[...].astype(_BF16), pw_ref[...], preferred_element_type=_F32)
    x3 = _layer_norm(DEEPNORM_ALPHA * x2 + jax.nn.sigmoid(z) * ple, g3_ref[...], b3_ref[...])
    out_ref[...] = x3


def _combine(dest_flat, yb, x1, gates_t, p2d, g2, b2, pw, gw, gb, g3, b3):
    n = dest_flat.shape[0] // TOP_K
    ts = TS_COMBINE
    last = n // ts - 1
    full = lambda shape: pl.BlockSpec(shape, lambda i: (0,) * len(shape))
    return pl.pallas_call(
        _combine_kernel,
        grid=(n // ts,),
        in_specs=[
            pl.BlockSpec((TOP_K * ts,), lambda i: (i,), memory_space=pltpu.SMEM),
            pl.BlockSpec((TOP_K * ts,), lambda i: (jnp.minimum(i + 1, last),), memory_space=pltpu.SMEM),
            pl.BlockSpec(memory_space=pl.ANY),
            pl.BlockSpec((ts, D_MODEL), lambda i: (i, 0)),
            pl.BlockSpec((ts, TOP_K), lambda i: (i, 0)),
            pl.BlockSpec((ts, PLE_DIM), lambda i: (i, 0)),
            full(g2.shape), full(b2.shape), full(pw.shape), full(gw.shape), full(gb.shape),
            full(g3.shape), full(b3.shape),
        ],
        out_specs=pl.BlockSpec((ts, D_MODEL), lambda i: (i, 0)),
        out_shape=jax.ShapeDtypeStruct((n, D_MODEL), _F32),
        scratch_shapes=[
            pltpu.VMEM((2, TOP_K, ts * GATHER_PITCH, LANES), _F32),
            pltpu.VMEM((ts, D_MODEL), _F32),
            pltpu.SemaphoreType.DMA((2,)),
        ],
        compiler_params=pltpu.CompilerParams(
            dimension_semantics=("arbitrary",),
            vmem_limit_bytes=48 * 1024 * 1024),
        name="combine",
    )(dest_flat, dest_flat, yb, x1, gates_t, p2d, g2, b2, pw, gw, gb, g3, b3)


def kernel(x, p, w_in, pool_mix, pool_scale, conv_w, w_out, ln1_g, ln1_b, router_w, router_b,
           w_gate_up, b_gate_up, w_down, b_down, ln2_g, ln2_b, ple_proj, ple_gate_w, ple_gate_b,
           ln3_g, ln3_b):
    assert DEPTH == 1 and x.shape[-1] == D_MODEL
    bsz, seq, d = x.shape
    n = bsz * seq
    assert seq % TS_MIX == 0 and n % TS_DISPATCH == 0 and n % TS_COMBINE == 0
    row = lambda v: v.reshape(1, -1)

    tri = jnp.triu(jnp.ones((TS_MIX, TS_MIX), _BF16))
    x1rt, x1, ri, gates, counts = _mix_route(
        x, w_in[0].astype(_BF16), pool_mix[0].astype(_BF16), row(pool_scale[0]), conv_w[0],
        w_out[0].astype(_BF16), row(ln1_g[0]), row(ln1_b[0]),
        router_w[0].T.astype(_BF16), jnp.broadcast_to(router_b[0][:, None], (N_EXPERTS, LANES)), tri)

    tm = TM_EXPERT
    n_slots = n * TOP_K + N_EXPERTS * tm
    nb = n_slots // tm
    cnt = counts[:, 0]
    padded = ((cnt + tm - 1) // tm) * tm
    pad_end = jnp.cumsum(padded)
    pad_start = (pad_end - padded).astype(jnp.int32)
    n_valid = (pad_end[-1] // tm).astype(jnp.int32)
    tile_start = jnp.arange(nb, dtype=jnp.int32) * tm
    blk_e = jnp.minimum(jnp.sum(tile_start[:, None] >= pad_end[None, :], axis=1), N_EXPERTS - 1)
    blk_e = jnp.where(jnp.arange(nb) < n_valid, blk_e, blk_e[n_valid - 1]).astype(jnp.int32)

    last_tile = jnp.where(padded > 0, pad_end // tm - 1, -1).astype(jnp.int32)
    e_ids = jnp.arange(N_EXPERTS, dtype=jnp.int32)
    later = jnp.logical_and(e_ids[None, :] > e_ids[:, None], (padded > 0)[None, :])
    next_nonempty = jnp.min(jnp.where(later, e_ids[None, :], N_EXPERTS), axis=1)
    next_e = jnp.where(next_nonempty == N_EXPERTS, -1, next_nonempty).astype(jnp.int32)
    switched = jnp.concatenate([jnp.zeros((1,), jnp.int32), (blk_e[1:] != blk_e[:-1]).astype(jnp.int32)])
    slot_par = (jnp.cumsum(switched) % 2).astype(jnp.int32)
    n_valid = n_valid.reshape(1)

    dest = _make_dest(pad_start, ri).T.reshape(-1)
    xs = _dispatch(last_tile, n_valid, dest, x1rt, n_slots)
    yb = _experts(blk_e, n_valid, slot_par, next_e, xs,
                  w_gate_up[0], b_gate_up[0][:, None, :], w_down[0], b_down[0][:, None, :])
    out = _combine(dest, yb, x1, gates[:TOP_K].T, p[0].reshape(n, PLE_DIM),
                   row(ln2_g[0]), row(ln2_b[0]), ple_proj[0].astype(_BF16),
                   ple_gate_w[0].astype(_BF16), row(ple_gate_b[0]), row(ln3_g[0]), row(ln3_b[0]))
    return out.reshape(bsz, seq, d)
```

```python
import functools

import jax
import jax.numpy as jnp
from jax import lax
from jax.experimental import pallas as pl
from jax.experimental.pallas import tpu as pltpu

D_MODEL = 1024
D_POOL = 512
D_CONV = 512
POOL_WINDOWS = (2, 4, 8, 16)
POOL_GROUP_DIM = 128
CONV_WIDTH = 3
D_IN_PROJ = D_POOL + 3 * D_CONV
N_EXPERTS = 32
TOP_K = 4
D_FF = 1024
SWIGLU_LIMIT = 7.0
SWIGLU_ALPHA = 1.702
PLE_DIM = 256
DEPTH = 1
DEEPNORM_ALPHA = (2.0 * DEPTH) ** 0.25
LN_EPS = 1e-5

LANES = 128
SUBLANES = 8
ROW_CHUNKS = D_MODEL // LANES
POOL_HALO = 16
CONV_HALO = 8
GATHER_PITCH = ROW_CHUNKS + 1

TS_MIX = 512
TS_DISPATCH = 2048
TM_EXPERT = 512
TS_COMBINE = 256
COMBINE_GROUP = 8

_BF16 = jnp.bfloat16
_F32 = jnp.float32


def _layer_norm(h, g, b):
    mu = jnp.mean(h, axis=-1, keepdims=True)
    c = h - mu
    var = jnp.mean(c * c, axis=-1, keepdims=True)
    return c * lax.rsqrt(var + LN_EPS) * g + b


def _to_row_tiles(ref, val, pitch=ROW_CHUNKS):
    rows = val.shape[0]
    for c in range(ROW_CHUNKS):
        ref[pl.ds(c, rows, stride=pitch), :] = val[:, c * LANES:(c + 1) * LANES]
    for c in range(ROW_CHUNKS, pitch):
        ref[pl.ds(c, rows, stride=pitch), :] = jnp.zeros((rows, LANES), val.dtype)


def _from_row_tiles(ref, rows, pitch=ROW_CHUNKS):
    return jnp.concatenate(
        [ref[pl.ds(c, rows, stride=pitch), :] for c in range(ROW_CHUNKS)], axis=1)


def _row_tile(ref, r, pitch=ROW_CHUNKS):
    if pitch == ROW_CHUNKS:
        return ref.at[pl.ds(pl.multiple_of(r * ROW_CHUNKS, ROW_CHUNKS), ROW_CHUNKS)]
    return ref.at[pl.ds(r * pitch, ROW_CHUNKS)]


def _mix_route_kernel(x_ref, w_in_ref, pmix_ref, pscale_ref, convw_ref, w_out_ref, g1_ref, b1_ref,
                      rwt_ref, rb_ref, tri_ref,
                      x1rt_ref, x1_ref, ri_ref, gate_ref, cnt_ref,
                      carry_v, carry_u, cnt_sc):
    b = pl.program_id(0)
    s = pl.program_id(1)
    ts = TS_MIX

    @pl.when(jnp.logical_and(b == 0, s == 0))
    def _():
        cnt_sc[...] = jnp.zeros_like(cnt_sc)

    @pl.when(s == 0)
    def _():
        carry_v[...] = jnp.zeros_like(carry_v)
        carry_u[...] = jnp.zeros_like(carry_u)

    xb = x_ref[0]
    proj = jnp.dot(xb.astype(_BF16), w_in_ref[...], preferred_element_type=_F32)
    vp = proj[:, :D_POOL]
    bg = proj[:, D_POOL:D_POOL + D_CONV]
    cg = proj[:, D_POOL + D_CONV:D_POOL + 2 * D_CONV]
    vc = proj[:, D_POOL + 2 * D_CONV:]

    ext = jnp.concatenate([carry_v[...], vp], axis=0)
    pos = lax.broadcasted_iota(jnp.int32, (ts, 1), 0) + s * ts
    mixed = []
    for g, w in enumerate(POOL_WINDOWS):
        lo, hi = g * POOL_GROUP_DIM, (g + 1) * POOL_GROUP_DIM
        acc = ext[:, lo:hi]
        sh = 1
        while sh < w:
            acc = acc + pltpu.roll(acc, sh, 0)
            sh *= 2
        cnt = jnp.minimum(pos + 1, w).astype(_F32)
        d = acc[POOL_HALO:] / cnt - vp[:, lo:hi]
        yg = jnp.dot(d.astype(_BF16), pmix_ref[g], preferred_element_type=_F32)
        mixed.append(yg * pscale_ref[:, lo:hi])
    carry_v[...] = vp[ts - POOL_HALO:]

    u = cg * vc
    extu = jnp.concatenate([carry_u[...], u], axis=0)
    u1 = pltpu.roll(extu, 1, 0)[CONV_HALO:]
    u2 = pltpu.roll(extu, 2, 0)[CONV_HALO:]
    yc = bg * (convw_ref[0:1, :] * u2 + convw_ref[1:2, :] * u1 + convw_ref[2:3, :] * u)
    carry_u[...] = u[ts - CONV_HALO:]
    mixed.append(yc)

    mix_in = jnp.concatenate(mixed, axis=1).astype(_BF16)
    mix = jnp.dot(mix_in, w_out_ref[...], preferred_element_type=_F32)
    x1 = _layer_norm(DEEPNORM_ALPHA * xb + mix, g1_ref[...], b1_ref[...])
    _to_row_tiles(x1rt_ref, x1, GATHER_PITCH)
    x1_ref[...] = x1

    logits = lax.dot_general(rwt_ref[...], x1.astype(_BF16), (((1,), (1,)), ((), ())),
                             preferred_element_type=_F32) + rb_ref[:, 0:1]
    eidx = lax.broadcasted_iota(jnp.int32, (N_EXPERTS, ts), 0).astype(_F32)
    vals, sels = [], []
    work = logits
    for k in range(TOP_K):
        m = jnp.max(work, axis=0, keepdims=True)
        first = jnp.min(jnp.where(work == m, eidx, float(N_EXPERTS)), axis=0, keepdims=True)
        sel = eidx == first
        work = jnp.where(sel, -jnp.inf, work)
        vals.append(m)
        sels.append(sel)
        ri_ref[k:k + 1, :] = first.astype(jnp.int32)
    exps = [jnp.exp(v - vals[0]) for v in vals]
    denom = exps[0] + exps[1] + exps[2] + exps[3]
    for k in range(TOP_K):
        gate_ref[k:k + 1, :] = exps[k] / denom
    gate_ref[TOP_K:, :] = jnp.zeros((SUBLANES - TOP_K, ts), _F32)

    chosen = jnp.zeros((N_EXPERTS, ts), _F32)
    for sel in sels:
        chosen = chosen + sel.astype(_F32)
    incl = jnp.dot(chosen.astype(_BF16), tri_ref[...], preferred_element_type=_F32)
    base = cnt_sc[:, 0:1] + (incl - chosen)
    for k in range(TOP_K):
        rank = jnp.sum(jnp.where(sels[k], base, 0.0), axis=0, keepdims=True)
        ri_ref[TOP_K + k:TOP_K + k + 1, :] = rank.astype(jnp.int32)
    total = cnt_sc[...] + jnp.sum(chosen, axis=1, keepdims=True)
    cnt_sc[...] = total
    cnt_ref[...] = total.astype(jnp.int32)


def _mix_route(x, w_in, pmix, pscale, convw, w_out, g1, b1, rwt, rb, tri):
    bsz, seq, d = x.shape
    n = bsz * seq
    ns = seq // TS_MIX
    full = lambda shape: pl.BlockSpec(shape, lambda b, s: (0,) * len(shape))
    return pl.pallas_call(
        _mix_route_kernel,
        grid=(bsz, ns),
        in_specs=[
            pl.BlockSpec((1, TS_MIX, d), lambda b, s: (b, s, 0)),
            full(w_in.shape), full(pmix.shape), full(pscale.shape), full(convw.shape),
            full(w_out.shape), full(g1.shape), full(b1.shape), full(rwt.shape), full(rb.shape),
            full(tri.shape),
        ],
        out_specs=[
            pl.BlockSpec((TS_MIX * GATHER_PITCH, LANES), lambda b, s: (b * ns + s, 0)),
            pl.BlockSpec((TS_MIX, D_MODEL), lambda b, s: (b * ns + s, 0)),
            pl.BlockSpec((2 * TOP_K, TS_MIX), lambda b, s: (0, b * ns + s)),
            pl.BlockSpec((SUBLANES, TS_MIX), lambda b, s: (0, b * ns + s)),
            pl.BlockSpec((N_EXPERTS, LANES), lambda b, s: (0, 0)),
        ],
        out_shape=[
            jax.ShapeDtypeStruct((n * GATHER_PITCH, LANES), _F32),
            jax.ShapeDtypeStruct((n, D_MODEL), _F32),
            jax.ShapeDtypeStruct((2 * TOP_K, n), jnp.int32),
            jax.ShapeDtypeStruct((SUBLANES, n), _F32),
            jax.ShapeDtypeStruct((N_EXPERTS, LANES), jnp.int32),
        ],
        scratch_shapes=[
            pltpu.VMEM((POOL_HALO, D_POOL), _F32),
            pltpu.VMEM((CONV_HALO, D_CONV), _F32),
            pltpu.VMEM((N_EXPERTS, LANES), _F32),
        ],
        compiler_params=pltpu.CompilerParams(
            dimension_semantics=("arbitrary", "arbitrary"),
            vmem_limit_bytes=48 * 1024 * 1024),
        name="mix_route",
    )(x, w_in, pmix, pscale, convw, w_out, g1, b1, rwt, rb, tri)


def _make_dest_kernel(start_ref, ri_ref, dest_ref):
    ei = ri_ref[0:TOP_K, :]
    acc = ri_ref[TOP_K:, :]
    for e in range(N_EXPERTS):
        acc = acc + jnp.where(ei == e, start_ref[e], 0)
    dest_ref[...] = acc


def _make_dest(pad_start, ri):
    n = ri.shape[1]
    return pl.pallas_call(
        _make_dest_kernel,
        in_specs=[pl.BlockSpec(memory_space=pltpu.SMEM),
                  pl.BlockSpec((2 * TOP_K, n), lambda: (0, 0))],
        out_specs=pl.BlockSpec((TOP_K, n), lambda: (0, 0)),
        out_shape=jax.ShapeDtypeStruct((TOP_K, n), jnp.int32),
        name="make_dest",
    )(pad_start, ri)


def _dispatch_kernel(last_ref, nv_ref, dest_ref, src_ref, out_ref, zbuf, sem, zsem):
    ts = TS_DISPATCH
    tile_rows = TM_EXPERT * GATHER_PITCH
    n_tiles = out_ref.shape[0] // tile_rows

    @pl.when(pl.program_id(0) == 0)
    def _():
        zbuf[...] = jnp.zeros_like(zbuf)

        def zero_tile(tile):
            return pltpu.make_async_copy(
                zbuf, out_ref.at[pl.ds(pl.multiple_of(tile * tile_rows, tile_rows), tile_rows)], zsem)

        def each_tile(fn):
            for e in range(N_EXPERTS):
                @pl.when(last_ref[e] >= 0)
                def _():
                    fn(zero_tile(last_ref[e]))

            def tail(j, carry):
                fn(zero_tile(j))
                return carry
            lax.fori_loop(nv_ref[0], n_tiles, tail, 0)

        each_tile(lambda c: c.start())
        each_tile(lambda c: c.wait())

    def slot_rows(ref, r):
        return ref.at[pl.ds(r * GATHER_PITCH, GATHER_PITCH)]

    def body(t, carry):
        for k in range(TOP_K):
            pltpu.make_async_copy(slot_rows(src_ref, t), slot_rows(out_ref, dest_ref[t * TOP_K + k]),
                                  sem).start(priority=k % 2)
        return carry

    lax.fori_loop(0, ts, body, 0, unroll=8)
    for k in range(TOP_K):
        pltpu.make_async_copy(src_ref, out_ref.at[pl.ds(0, ts * GATHER_PITCH)], sem).wait()


def _dispatch(last_tile, n_valid, dest_flat, x1rt, n_slots):
    n = dest_flat.shape[0] // TOP_K
    grid_spec = pltpu.PrefetchScalarGridSpec(
        num_scalar_prefetch=2,
        grid=(n // TS_DISPATCH,),
        in_specs=[
            pl.BlockSpec((TOP_K * TS_DISPATCH,), lambda i, lt, nv: (i,), memory_space=pltpu.SMEM),
            pl.BlockSpec((TS_DISPATCH * GATHER_PITCH, LANES), lambda i, lt, nv: (i, 0)),
        ],
        out_specs=pl.BlockSpec(memory_space=pl.ANY),
        scratch_shapes=[
            pltpu.VMEM((TM_EXPERT * GATHER_PITCH, LANES), _F32),
            pltpu.SemaphoreType.DMA(()),
            pltpu.SemaphoreType.DMA(()),
        ],
    )
    return pl.pallas_call(
        _dispatch_kernel,
        grid_spec=grid_spec,
        out_shape=jax.ShapeDtypeStruct((n_slots * GATHER_PITCH, LANES), _F32),
        compiler_params=pltpu.CompilerParams(
            dimension_semantics=("arbitrary",),
            vmem_limit_bytes=40 * 1024 * 1024),
        name="dispatch",
    )(last_tile, n_valid, dest_flat, x1rt)


def _experts_kernel(be_ref, nv_ref, par_ref, nxt_ref, xs_ref, wgu_hbm, bgu_ref, wdn_hbm, bdn_ref, out_ref,
                    wgu_f32, wdn_f32, wgu_bf, wdn_bf, sems):
    j = pl.program_id(0)
    tm = TM_EXPERT
    valid = j < nv_ref[0]
    new_expert = jnp.logical_or(j == 0, be_ref[j] != be_ref[jnp.maximum(j - 1, 0)])

    def weight_copies(e, s):
        return (pltpu.make_async_copy(wgu_hbm.at[e], wgu_f32.at[s], sems.at[0, s]),
                pltpu.make_async_copy(wdn_hbm.at[e], wdn_f32.at[s], sems.at[1, s]))

    @pl.when(j == 0)
    def _():
        for c in weight_copies(be_ref[0], par_ref[0]):
            c.start()

    @pl.when(jnp.logical_and(valid, new_expert))
    def _():
        s = par_ref[j]
        for c in weight_copies(be_ref[j], s):
            c.wait()

        nxt = nxt_ref[be_ref[j]]

        @pl.when(nxt >= 0)
        def _():
            for c in weight_copies(nxt, 1 - s):
                c.start()

        wgu_bf[...] = wgu_f32[s].astype(_BF16)
        wdn_bf[...] = wdn_f32[s].astype(_BF16)

    @pl.when(valid)
    def _():
        x = _from_row_tiles(xs_ref, tm, GATHER_PITCH).astype(_BF16)
        gu = jnp.dot(x, wgu_bf[...], preferred_element_type=_F32) + bgu_ref[0]
        gate = jnp.minimum(gu[:, :D_FF], SWIGLU_LIMIT)
        up = jnp.clip(gu[:, D_FF:], -SWIGLU_LIMIT, SWIGLU_LIMIT)
        glu = gate * jax.nn.sigmoid(SWIGLU_ALPHA * gate)
        act = ((up + 1.0) * glu).astype(_BF16)
        y = jnp.dot(act, wdn_bf[...], preferred_element_type=_F32) + bdn_ref[0]
        _to_row_tiles(out_ref, y, GATHER_PITCH)

    @pl.when(j >= nv_ref[0])
    def _():
        out_ref[...] = jnp.zeros_like(out_ref)


def _experts(blk_e, n_valid, slot_par, next_e, xs, wgu, bgu, wdn, bdn):
    n_slots = xs.shape[0] // GATHER_PITCH
    nb = n_slots // TM_EXPERT
    rows = TM_EXPERT * GATHER_PITCH
    grid_spec = pltpu.PrefetchScalarGridSpec(
        num_scalar_prefetch=4,
        grid=(nb,),
        in_specs=[
            pl.BlockSpec((rows, LANES), lambda j, be, nv, par, nxt: (jnp.minimum(j, nv[0] - 1), 0)),
            pl.BlockSpec(memory_space=pl.ANY),
            pl.BlockSpec((1, 1, 2 * D_FF), lambda j, be, nv, par, nxt: (be[j], 0, 0)),
            pl.BlockSpec(memory_space=pl.ANY),
            pl.BlockSpec((1, 1, D_MODEL), lambda j, be, nv, par, nxt: (be[j], 0, 0)),
        ],
        out_specs=pl.BlockSpec((TM_EXPERT * GATHER_PITCH, LANES), lambda j, be, nv, par, nxt: (j, 0)),
        scratch_shapes=[
            pltpu.VMEM((2, D_MODEL, 2 * D_FF), _F32),
            pltpu.VMEM((2, D_FF, D_MODEL), _F32),
            pltpu.VMEM((D_MODEL, 2 * D_FF), _BF16),
            pltpu.VMEM((D_FF, D_MODEL), _BF16),
            pltpu.SemaphoreType.DMA((2, 2)),
        ],
    )
    return pl.pallas_call(
        _experts_kernel,
        grid_spec=grid_spec,
        out_shape=jax.ShapeDtypeStruct((n_slots * GATHER_PITCH, LANES), _F32),
        compiler_params=pltpu.CompilerParams(
            dimension_semantics=("arbitrary",),
            vmem_limit_bytes=60 * 1024 * 1024),
        name="experts",
    )(blk_e, n_valid, slot_par, next_e, xs, wgu, bgu, wdn, bdn)


def _combine_kernel(dcur_ref, dnext_ref, yb_ref, x1_ref, gt_ref, p_ref, g2_ref, b2_ref, pw_ref, gw_ref,
                    gb_ref, g3_ref, b3_ref, out_ref, gbuf, hbuf, sems):
    ts = TS_COMBINE
    grp = COMBINE_GROUP
    i = pl.program_id(0)
    slot = i % 2
    nxt = 1 - slot

    def start_rows(d_ref, s, t0):
        for j in range(grp):
            t = t0 + j
            for k in range(TOP_K):
                pltpu.make_async_copy(_row_tile(yb_ref, d_ref[t * TOP_K + k], GATHER_PITCH),
                                      gbuf.at[s, k, pl.ds(t * GATHER_PITCH, ROW_CHUNKS)],
                                      sems.at[s]).start(priority=k % 2)

    def wait_tile(s):
        for k in range(TOP_K):
            pltpu.make_async_copy(yb_ref.at[pl.ds(0, ts * ROW_CHUNKS)],
                                  gbuf.at[s, k, pl.ds(0, ts * ROW_CHUNKS)], sems.at[s]).wait()

    @pl.when(i == 0)
    def _():
        def first(r, carry):
            start_rows(dcur_ref, 0, r * grp)
            return carry
        lax.fori_loop(0, ts // grp, first, 0)

    wait_tile(slot)

    def group(r, carry):
        t0 = pl.multiple_of(r * grp, grp)
        x1g = x1_ref[pl.ds(t0, grp), :]
        gts = gt_ref[pl.ds(t0, grp), :]
        ys = []
        for k in range(TOP_K):
            rows_k = gbuf.at[slot, k]
            ys.append(jnp.concatenate(
                [rows_k[pl.ds(t0 * GATHER_PITCH + c, grp, stride=GATHER_PITCH), :] for c in range(ROW_CHUNKS)],
                axis=1))
        start_rows(dnext_ref, nxt, t0)
        ffn = gts[:, 0:1] * ys[0]
        for k in range(1, TOP_K):
            ffn = ffn + gts[:, k:k + 1] * ys[k]
        hbuf[pl.ds(t0, grp), :] = DEEPNORM_ALPHA * x1g + ffn
        return carry

    lax.fori_loop(0, ts // grp, group, 0)

    @pl.when(i == pl.num_programs(0) - 1)
    def _():
        wait_tile(nxt)

    x2 = _layer_norm(hbuf[...], g2_ref[...], b2_ref[...])
    z = jnp.dot(x2.astype(_BF16), gw_ref[...], preferred_element_type=_F32) + gb_ref[...]
    ple = jnp.dot(p_ref[...].astype(_BF16), pw_ref[...], preferred_element_type=_F32)
    x3 = _layer_norm(DEEPNORM_ALPHA * x2 + jax.nn.sigmoid(z) * ple, g3_ref[...], b3_ref[...])
    out_ref[...] = x3


def _combine(dest_flat, yb, x1, gates_t, p2d, g2, b2, pw, gw, gb, g3, b3):
    n = dest_flat.shape[0] // TOP_K
    ts = TS_COMBINE
    last = n // ts - 1
    full = lambda shape: pl.BlockSpec(shape, lambda i: (0,) * len(shape))
    return pl.pallas_call(
        _combine_kernel,
        grid=(n // ts,),
        in_specs=[
            pl.BlockSpec((TOP_K * ts,), lambda i: (i,), memory_space=pltpu.SMEM),
            pl.BlockSpec((TOP_K * ts,), lambda i: (jnp.minimum(i + 1, last),), memory_space=pltpu.SMEM),
            pl.BlockSpec(memory_space=pl.ANY),
            pl.BlockSpec((ts, D_MODEL), lambda i: (i, 0)),
            pl.BlockSpec((ts, TOP_K), lambda i: (i, 0)),
            pl.BlockSpec((ts, PLE_DIM), lambda i: (i, 0)),
            full(g2.shape), full(b2.shape), full(pw.shape), full(gw.shape), full(gb.shape),
            full(g3.shape), full(b3.shape),
        ],
        out_specs=pl.BlockSpec((ts, D_MODEL), lambda i: (i, 0)),
        out_shape=jax.ShapeDtypeStruct((n, D_MODEL), _F32),
        scratch_shapes=[
            pltpu.VMEM((2, TOP_K, ts * GATHER_PITCH, LANES), _F32),
            pltpu.VMEM((ts, D_MODEL), _F32),
            pltpu.SemaphoreType.DMA((2,)),
        ],
        compiler_params=pltpu.CompilerParams(
            dimension_semantics=("arbitrary",),
            vmem_limit_bytes=48 * 1024 * 1024),
        name="combine",
    )(dest_flat, dest_flat, yb, x1, gates_t, p2d, g2, b2, pw, gw, gb, g3, b3)


def kernel(x, p, w_in, pool_mix, pool_scale, conv_w, w_out, ln1_g, ln1_b, router_w, router_b,
           w_gate_up, b_gate_up, w_down, b_down, ln2_g, ln2_b, ple_proj, ple_gate_w, ple_gate_b,
           ln3_g, ln3_b):
    assert DEPTH == 1 and x.shape[-1] == D_MODEL
    bsz, seq, d = x.shape
    n = bsz * seq
    assert seq % TS_MIX == 0 and n % TS_DISPATCH == 0 and n % TS_COMBINE == 0
    row = lambda v: v.reshape(1, -1)

    tri = jnp.triu(jnp.ones((TS_MIX, TS_MIX), _BF16))
    x1rt, x1, ri, gates, counts = _mix_route(
        x, w_in[0].astype(_BF16), pool_mix[0].astype(_BF16), row(pool_scale[0]), conv_w[0],
        w_out[0].astype(_BF16), row(ln1_g[0]), row(ln1_b[0]),
        router_w[0].T.astype(_BF16), jnp.broadcast_to(router_b[0][:, None], (N_EXPERTS, LANES)), tri)

    tm = TM_EXPERT
    n_slots = n * TOP_K + N_EXPERTS * tm
    nb = n_slots // tm
    cnt = counts[:, 0]
    padded = ((cnt + tm - 1) // tm) * tm
    pad_end = jnp.cumsum(padded)
    pad_start = (pad_end - padded).astype(jnp.int32)
    n_valid = (pad_end[-1] // tm).astype(jnp.int32)
    tile_start = jnp.arange(nb, dtype=jnp.int32) * tm
    blk_e = jnp.minimum(jnp.sum(tile_start[:, None] >= pad_end[None, :], axis=1), N_EXPERTS - 1)
    blk_e = jnp.where(jnp.arange(nb) < n_valid, blk_e, blk_e[n_valid - 1]).astype(jnp.int32)

    last_tile = jnp.where(padded > 0, pad_end // tm - 1, -1).astype(jnp.int32)
    e_ids = jnp.arange(N_EXPERTS, dtype=jnp.int32)
    later = jnp.logical_and(e_ids[None, :] > e_ids[:, None], (padded > 0)[None, :])
    next_nonempty = jnp.min(jnp.where(later, e_ids[None, :], N_EXPERTS), axis=1)
    next_e = jnp.where(next_nonempty == N_EXPERTS, -1, next_nonempty).astype(jnp.int32)
    switched = jnp.concatenate([jnp.zeros((1,), jnp.int32), (blk_e[1:] != blk_e[:-1]).astype(jnp.int32)])
    slot_par = (jnp.cumsum(switched) % 2).astype(jnp.int32)
    n_valid = n_valid.reshape(1)

    dest = _make_dest(pad_start, ri).T.reshape(-1)
    xs = _dispatch(last_tile, n_valid, dest, x1rt, n_slots)
    yb = _experts(blk_e, n_valid, slot_par, next_e, xs,
                  w_gate_up[0], b_gate_up[0][:, None, :], w_down[0], b_down[0][:, None, :])
    out = _combine(dest, yb, x1, gates[:TOP_K].T, p[0].reshape(n, PLE_DIM),
                   row(ln2_g[0]), row(ln2_b[0]), ple_proj[0].astype(_BF16),
                   ple_gate_w[0].astype(_BF16), row(ple_gate_b[0]), row(ln3_g[0]), row(ln3_b[0]))
    return out.reshape(bsz, seq, d)
```

```python
import functools

import jax
import jax.numpy as jnp
from jax import lax
from jax.experimental import pallas as pl
from jax.experimental.pallas import tpu as pltpu

D_MODEL = 1024
D_POOL = 512
D_CONV = 512
POOL_WINDOWS = (2, 4, 8, 16)
POOL_GROUP_DIM = 128
CONV_WIDTH = 3
D_IN_PROJ = D_POOL + 3 * D_CONV
N_EXPERTS = 32
TOP_K = 4
D_FF = 1024
SWIGLU_LIMIT = 7.0
SWIGLU_ALPHA = 1.702
PLE_DIM = 256
DEPTH = 1
DEEPNORM_ALPHA = (2.0 * DEPTH) ** 0.25
LN_EPS = 1e-5

LANES = 128
SUBLANES = 8
ROW_CHUNKS = D_MODEL // LANES
POOL_HALO = 16
CONV_HALO = 8
GATHER_PITCH = ROW_CHUNKS + 1

TS_MIX = 512
TS_DISPATCH = 2048
TM_EXPERT = 512
TS_COMBINE = 256
COMBINE_GROUP = 8

_BF16 = jnp.bfloat16
_F32 = jnp.float32


def _layer_norm(h, g, b):
    mu = jnp.mean(h, axis=-1, keepdims=True)
    c = h - mu
    var = jnp.mean(c * c, axis=-1, keepdims=True)
    return c * lax.rsqrt(var + LN_EPS) * g + b


def _to_row_tiles(ref, val, pitch=ROW_CHUNKS):
    rows = val.shape[0]
    for c in range(ROW_CHUNKS):
        ref[pl.ds(c, rows, stride=pitch), :] = val[:, c * LANES:(c + 1) * LANES]
    for c in range(ROW_CHUNKS, pitch):
        ref[pl.ds(c, rows, stride=pitch), :] = jnp.zeros((rows, LANES), val.dtype)


def _from_row_tiles(ref, rows, pitch=ROW_CHUNKS):
    return jnp.concatenate(
        [ref[pl.ds(c, rows, stride=pitch), :] for c in range(ROW_CHUNKS)], axis=1)


def _row_tile(ref, r, pitch=ROW_CHUNKS):
    if pitch == ROW_CHUNKS:
        return ref.at[pl.ds(pl.multiple_of(r * ROW_CHUNKS, ROW_CHUNKS), ROW_CHUNKS)]
    return ref.at[pl.ds(r * pitch, ROW_CHUNKS)]


def _mix_route_kernel(x_ref, w_in_ref, pmix_ref, pscale_ref, convw_ref, w_out_ref, g1_ref, b1_ref,
                      rwt_ref, rb_ref, tri_ref, low_ref,
                      x1rt_ref, x1_ref, dest_ref, gate_ref, cnt_ref, tab_ref,
                      carry_v, carry_u, cnt_sc, used_sc, cur_sc, tab_sc):
    b = pl.program_id(0)
    s = pl.program_id(1)
    ts = TS_MIX

    @pl.when(jnp.logical_and(b == 0, s == 0))
    def _():
        cnt_sc[...] = jnp.zeros_like(cnt_sc)
        used_sc[...] = jnp.zeros_like(used_sc)
        cur_sc[...] = jnp.zeros_like(cur_sc)
        tab_sc[...] = jnp.zeros_like(tab_sc)

    @pl.when(s == 0)
    def _():
        carry_v[...] = jnp.zeros_like(carry_v)
        carry_u[...] = jnp.zeros_like(carry_u)

    xb = x_ref[0]
    proj = jnp.dot(xb.astype(_BF16), w_in_ref[...], preferred_element_type=_F32)
    vp = proj[:, :D_POOL]
    bg = proj[:, D_POOL:D_POOL + D_CONV]
    cg = proj[:, D_POOL + D_CONV:D_POOL + 2 * D_CONV]
    vc = proj[:, D_POOL + 2 * D_CONV:]

    ext = jnp.concatenate([carry_v[...], vp], axis=0)
    pos = lax.broadcasted_iota(jnp.int32, (ts, 1), 0) + s * ts
    mixed = []
    for g, w in enumerate(POOL_WINDOWS):
        lo, hi = g * POOL_GROUP_DIM, (g + 1) * POOL_GROUP_DIM
        acc = ext[:, lo:hi]
        sh = 1
        while sh < w:
            acc = acc + pltpu.roll(acc, sh, 0)
            sh *= 2
        cnt = jnp.minimum(pos + 1, w).astype(_F32)
        d = acc[POOL_HALO:] / cnt - vp[:, lo:hi]
        yg = jnp.dot(d.astype(_BF16), pmix_ref[g], preferred_element_type=_F32)
        mixed.append(yg * pscale_ref[:, lo:hi])
    carry_v[...] = vp[ts - POOL_HALO:]

    u = cg * vc
    extu = jnp.concatenate([carry_u[...], u], axis=0)
    u1 = pltpu.roll(extu, 1, 0)[CONV_HALO:]
    u2 = pltpu.roll(extu, 2, 0)[CONV_HALO:]
    yc = bg * (convw_ref[0:1, :] * u2 + convw_ref[1:2, :] * u1 + convw_ref[2:3, :] * u)
    carry_u[...] = u[ts - CONV_HALO:]
    mixed.append(yc)

    mix_in = jnp.concatenate(mixed, axis=1).astype(_BF16)
    mix = jnp.dot(mix_in, w_out_ref[...], preferred_element_type=_F32)
    x1 = _layer_norm(DEEPNORM_ALPHA * xb + mix, g1_ref[...], b1_ref[...])
    _to_row_tiles(x1rt_ref, x1, GATHER_PITCH)
    x1_ref[...] = x1

    logits = lax.dot_general(rwt_ref[...], x1.astype(_BF16), (((1,), (1,)), ((), ())),
                             preferred_element_type=_F32) + rb_ref[:, 0:1]
    eidx = lax.broadcasted_iota(jnp.int32, (N_EXPERTS, ts), 0).astype(_F32)
    vals, sels = [], []
    work = logits
    for k in range(TOP_K):
        m = jnp.max(work, axis=0, keepdims=True)
        first = jnp.min(jnp.where(work == m, eidx, float(N_EXPERTS)), axis=0, keepdims=True)
        sel = eidx == first
        work = jnp.where(sel, -jnp.inf, work)
        vals.append(m)
        sels.append(sel)
    exps = [jnp.exp(v - vals[0]) for v in vals]
    denom = exps[0] + exps[1] + exps[2] + exps[3]
    for k in range(TOP_K):
        gate_ref[k:k + 1, :] = exps[k] / denom
    gate_ref[TOP_K:, :] = jnp.zeros((SUBLANES - TOP_K, ts), _F32)

    tm = float(TM_EXPERT)
    chosen = jnp.zeros((N_EXPERTS, ts), _F32)
    for sel in sels:
        chosen = chosen + sel.astype(_F32)
    incl = jnp.dot(chosen.astype(_BF16), tri_ref[...], preferred_element_type=_F32)
    cnt_b = cnt_sc[:, 0:1]
    cnt_a = cnt_b + jnp.sum(chosen, axis=1, keepdims=True)
    q_b = jnp.floor((cnt_b + (tm - 1.0)) * (1.0 / tm))
    q_a = jnp.floor((cnt_a + (tm - 1.0)) * (1.0 / tm))
    n_new = q_a - q_b
    before = jnp.dot(low_ref[...], jnp.broadcast_to(n_new, (N_EXPERTS, LANES)).astype(_BF16),
                     preferred_element_type=_F32)[:, 0:1]
    used = used_sc[:, 0:1]
    new_tile = used + before
    cur_tile = cur_sc[:, 0:1]
    rank = cnt_b + (incl - chosen)
    page = jnp.floor(rank * (1.0 / tm))
    slot = jnp.where(page < q_b, cur_tile, new_tile) * tm + (rank - page * tm)
    for k in range(TOP_K):
        dest_ref[k:k + 1, :] = jnp.sum(jnp.where(sels[k], slot, 0.0), axis=0, keepdims=True).astype(jnp.int32)
    dest_ref[TOP_K:, :] = jnp.zeros((SUBLANES - TOP_K, ts), jnp.int32)

    took = n_new > 0.0
    lane = lax.broadcasted_iota(jnp.int32, (N_EXPERTS, LANES), 1).astype(_F32)
    table = jnp.where(jnp.logical_and(lane == q_b, took), new_tile, tab_sc[...])
    tab_sc[...] = table
    cur_sc[...] = jnp.broadcast_to(jnp.where(took, new_tile, cur_tile), (N_EXPERTS, LANES))
    used_sc[...] = jnp.broadcast_to(used + jnp.sum(n_new, axis=0, keepdims=True), (N_EXPERTS, LANES))
    cnt_sc[...] = jnp.broadcast_to(cnt_a, (N_EXPERTS, LANES))
    cnt_ref[...] = jnp.broadcast_to(cnt_a, (N_EXPERTS, LANES)).astype(jnp.int32)
    tab_ref[...] = table.astype(jnp.int32)


def _mix_route(x, w_in, pmix, pscale, convw, w_out, g1, b1, rwt, rb, tri, low):
    bsz, seq, d = x.shape
    n = bsz * seq
    ns = seq // TS_MIX
    assert n // TM_EXPERT <= LANES
    full = lambda shape: pl.BlockSpec(shape, lambda b, s: (0,) * len(shape))
    state = pltpu.VMEM((N_EXPERTS, LANES), _F32)
    return pl.pallas_call(
        _mix_route_kernel,
        grid=(bsz, ns),
        in_specs=[
            pl.BlockSpec((1, TS_MIX, d), lambda b, s: (b, s, 0)),
            full(w_in.shape), full(pmix.shape), full(pscale.shape), full(convw.shape),
            full(w_out.shape), full(g1.shape), full(b1.shape), full(rwt.shape), full(rb.shape),
            full(tri.shape), full(low.shape),
        ],
        out_specs=[
            pl.BlockSpec((TS_MIX * GATHER_PITCH, LANES), lambda b, s: (b * ns + s, 0)),
            pl.BlockSpec((TS_MIX, D_MODEL), lambda b, s: (b * ns + s, 0)),
            pl.BlockSpec((SUBLANES, TS_MIX), lambda b, s: (0, b * ns + s)),
            pl.BlockSpec((SUBLANES, TS_MIX), lambda b, s: (0, b * ns + s)),
            pl.BlockSpec((N_EXPERTS, LANES), lambda b, s: (0, 0)),
            pl.BlockSpec((N_EXPERTS, LANES), lambda b, s: (0, 0)),
        ],
        out_shape=[
            jax.ShapeDtypeStruct((n * GATHER_PITCH, LANES), _F32),
            jax.ShapeDtypeStruct((n, D_MODEL), _F32),
            jax.ShapeDtypeStruct((SUBLANES, n), jnp.int32),
            jax.ShapeDtypeStruct((SUBLANES, n), _F32),
            jax.ShapeDtypeStruct((N_EXPERTS, LANES), jnp.int32),
            jax.ShapeDtypeStruct((N_EXPERTS, LANES), jnp.int32),
        ],
        scratch_shapes=[
            pltpu.VMEM((POOL_HALO, D_POOL), _F32),
            pltpu.VMEM((CONV_HALO, D_CONV), _F32),
            state, state, state, state,
        ],
        compiler_params=pltpu.CompilerParams(
            dimension_semantics=("arbitrary", "arbitrary"),
            vmem_limit_bytes=48 * 1024 * 1024),
        name="mix_route",
    )(x, w_in, pmix, pscale, convw, w_out, g1, b1, rwt, rb, tri, low)


def _dispatch_kernel(last_ref, nv_ref, dest_ref, src_ref, out_ref, zbuf, sem, zsem):
    ts = TS_DISPATCH
    tile_rows = TM_EXPERT * GATHER_PITCH
    n_tiles = out_ref.shape[0] // tile_rows

    @pl.when(pl.program_id(0) == 0)
    def _():
        zbuf[...] = jnp.zeros_like(zbuf)

        def zero_tile(tile):
            return pltpu.make_async_copy(
                zbuf, out_ref.at[pl.ds(pl.multiple_of(tile * tile_rows, tile_rows), tile_rows)], zsem)

        def each_tile(fn):
            for e in range(N_EXPERTS):
                @pl.when(last_ref[e] >= 0)
                def _():
                    fn(zero_tile(last_ref[e]))

            def tail(j, carry):
                fn(zero_tile(j))
                return carry
            lax.fori_loop(nv_ref[0], n_tiles, tail, 0)

        each_tile(lambda c: c.start())
        each_tile(lambda c: c.wait())

    def slot_rows(ref, r):
        return ref.at[pl.ds(r * GATHER_PITCH, GATHER_PITCH)]

    def body(t, carry):
        for k in range(TOP_K):
            pltpu.make_async_copy(slot_rows(src_ref, t), slot_rows(out_ref, dest_ref[t * TOP_K + k]),
                                  sem).start(priority=k % 2)
        return carry

    lax.fori_loop(0, ts, body, 0, unroll=8)
    for k in range(TOP_K):
        pltpu.make_async_copy(src_ref, out_ref.at[pl.ds(0, ts * GATHER_PITCH)], sem).wait()


def _dispatch(last_tile, n_valid, dest_flat, x1rt, n_slots):
    n = dest_flat.shape[0] // TOP_K
    grid_spec = pltpu.PrefetchScalarGridSpec(
        num_scalar_prefetch=2,
        grid=(n // TS_DISPATCH,),
        in_specs=[
            pl.BlockSpec((TOP_K * TS_DISPATCH,), lambda i, lt, nv: (i,), memory_space=pltpu.SMEM),
            pl.BlockSpec((TS_DISPATCH * GATHER_PITCH, LANES), lambda i, lt, nv: (i, 0)),
        ],
        out_specs=pl.BlockSpec(memory_space=pl.ANY),
        scratch_shapes=[
            pltpu.VMEM((TM_EXPERT * GATHER_PITCH, LANES), _F32),
            pltpu.SemaphoreType.DMA(()),
            pltpu.SemaphoreType.DMA(()),
        ],
    )
    return pl.pallas_call(
        _dispatch_kernel,
        grid_spec=grid_spec,
        out_shape=jax.ShapeDtypeStruct((n_slots * GATHER_PITCH, LANES), _F32),
        compiler_params=pltpu.CompilerParams(
            dimension_semantics=("arbitrary",),
            vmem_limit_bytes=40 * 1024 * 1024),
        name="dispatch",
    )(last_tile, n_valid, dest_flat, x1rt)


def _experts_kernel(be_ref, nv_ref, par_ref, nxt_ref, phys_ref, xs_ref, wgu_hbm, bgu_ref, wdn_hbm, bdn_ref, out_ref,
                    wgu_f32, wdn_f32, wgu_bf, wdn_bf, sems):
    j = pl.program_id(0)
    tm = TM_EXPERT
    valid = j < nv_ref[0]
    new_expert = jnp.logical_or(j == 0, be_ref[j] != be_ref[jnp.maximum(j - 1, 0)])

    def weight_copies(e, s):
        return (pltpu.make_async_copy(wgu_hbm.at[e], wgu_f32.at[s], sems.at[0, s]),
                pltpu.make_async_copy(wdn_hbm.at[e], wdn_f32.at[s], sems.at[1, s]))

    @pl.when(j == 0)
    def _():
        for c in weight_copies(be_ref[0], par_ref[0]):
            c.start()

    @pl.when(jnp.logical_and(valid, new_expert))
    def _():
        s = par_ref[j]
        for c in weight_copies(be_ref[j], s):
            c.wait()

        nxt = nxt_ref[be_ref[j]]

        @pl.when(nxt >= 0)
        def _():
            for c in weight_copies(nxt, 1 - s):
                c.start()

        wgu_bf[...] = wgu_f32[s].astype(_BF16)
        wdn_bf[...] = wdn_f32[s].astype(_BF16)

    @pl.when(valid)
    def _():
        x = _from_row_tiles(xs_ref, tm, GATHER_PITCH).astype(_BF16)
        gu = jnp.dot(x, wgu_bf[...], preferred_element_type=_F32) + bgu_ref[0]
        gate = jnp.minimum(gu[:, :D_FF], SWIGLU_LIMIT)
        up = jnp.clip(gu[:, D_FF:], -SWIGLU_LIMIT, SWIGLU_LIMIT)
        glu = gate * jax.nn.sigmoid(SWIGLU_ALPHA * gate)
        act = ((up + 1.0) * glu).astype(_BF16)
        y = jnp.dot(act, wdn_bf[...], preferred_element_type=_F32) + bdn_ref[0]
        _to_row_tiles(out_ref, y, GATHER_PITCH)

    @pl.when(j >= nv_ref[0])
    def _():
        out_ref[...] = jnp.zeros_like(out_ref)


def _experts(blk_e, n_valid, slot_par, next_e, phys, xs, wgu, bgu, wdn, bdn):
    n_slots = xs.shape[0] // GATHER_PITCH
    nb = n_slots // TM_EXPERT
    rows = TM_EXPERT * GATHER_PITCH
    grid_spec = pltpu.PrefetchScalarGridSpec(
        num_scalar_prefetch=5,
        grid=(nb,),
        in_specs=[
            pl.BlockSpec((rows, LANES), lambda j, be, nv, par, nxt, ph: (ph[jnp.minimum(j, nv[0] - 1)], 0)),
            pl.BlockSpec(memory_space=pl.ANY),
            pl.BlockSpec((1, 1, 2 * D_FF), lambda j, be, nv, par, nxt, ph: (be[j], 0, 0)),
            pl.BlockSpec(memory_space=pl.ANY),
            pl.BlockSpec((1, 1, D_MODEL), lambda j, be, nv, par, nxt, ph: (be[j], 0, 0)),
        ],
        out_specs=pl.BlockSpec((rows, LANES), lambda j, be, nv, par, nxt, ph: (ph[j], 0)),
        scratch_shapes=[
            pltpu.VMEM((2, D_MODEL, 2 * D_FF), _F32),
            pltpu.VMEM((2, D_FF, D_MODEL), _F32),
            pltpu.VMEM((D_MODEL, 2 * D_FF), _BF16),
            pltpu.VMEM((D_FF, D_MODEL), _BF16),
            pltpu.SemaphoreType.DMA((2, 2)),
        ],
    )
    return pl.pallas_call(
        _experts_kernel,
        grid_spec=grid_spec,
        out_shape=jax.ShapeDtypeStruct((n_slots * GATHER_PITCH, LANES), _F32),
        compiler_params=pltpu.CompilerParams(
            dimension_semantics=("arbitrary",),
            vmem_limit_bytes=60 * 1024 * 1024),
        name="experts",
    )(blk_e, n_valid, slot_par, next_e, phys, xs, wgu, bgu, wdn, bdn)


def _combine_kernel(dcur_ref, dnext_ref, yb_ref, x1_ref, gt_ref, p_ref, g2_ref, b2_ref, pw_ref, gw_ref,
                    gb_ref, g3_ref, b3_ref, out_ref, gbuf, hbuf, sems):
    ts = TS_COMBINE
    grp = COMBINE_GROUP
    i = pl.program_id(0)
    slot = i % 2
    nxt = 1 - slot

    def start_rows(d_ref, s, t0):
        for j in range(grp):
            t = t0 + j
            for k in range(TOP_K):
                pltpu.make_async_copy(_row_tile(yb_ref, d_ref[t * TOP_K + k], GATHER_PITCH),
                                      gbuf.at[s, k, pl.ds(t * GATHER_PITCH, ROW_CHUNKS)],
                                      sems.at[s]).start(priority=k % 2)

    def wait_tile(s):
        for k in range(TOP_K):
            pltpu.make_async_copy(yb_ref.at[pl.ds(0, ts * ROW_CHUNKS)],
                                  gbuf.at[s, k, pl.ds(0, ts * ROW_CHUNKS)], sems.at[s]).wait()

    @pl.when(i == 0)
    def _():
        def first(r, carry):
            start_rows(dcur_ref, 0, r * grp)
            return carry
        lax.fori_loop(0, ts // grp, first, 0)

    wait_tile(slot)

    def group(r, carry):
        t0 = pl.multiple_of(r * grp, grp)
        x1g = x1_ref[pl.ds(t0, grp), :]
        gts = gt_ref[pl.ds(t0, grp), :]
        ys = []
        for k in range(TOP_K):
            rows_k = gbuf.at[slot, k]
            ys.append(jnp.concatenate(
                [rows_k[pl.ds(t0 * GATHER_PITCH + c, grp, stride=GATHER_PITCH), :] for c in range(ROW_CHUNKS)],
                axis=1))
        start_rows(dnext_ref, nxt, t0)
        ffn = gts[:, 0:1] * ys[0]
        for k in range(1, TOP_K):
            ffn = ffn + gts[:, k:k + 1] * ys[k]
        hbuf[pl.ds(t0, grp), :] = DEEPNORM_ALPHA * x1g + ffn
        return carry

    lax.fori_loop(0, ts // grp, group, 0)

    @pl.when(i == pl.num_programs(0) - 1)
    def _():
        wait_tile(nxt)

    x2 = _layer_norm(hbuf[...], g2_ref[...], b2_ref[...])
    z = jnp.dot(x2.astype(_BF16), gw_ref[...], preferred_element_type=_F32) + gb_ref[...]
    ple = jnp.dot(p_ref[...].astype(_BF16), pw_ref[...], preferred_element_type=_F32)
    x3 = _layer_norm(DEEPNORM_ALPHA * x2 + jax.nn.sigmoid(z) * ple, g3_ref[...], b3_ref[...])
    out_ref[...] = x3


def _combine(dest_flat, yb, x1, gates_t, p2d, g2, b2, pw, gw, gb, g3, b3):
    n = dest_flat.shape[0] // TOP_K
    ts = TS_COMBINE
    last = n // ts - 1
    full = lambda shape: pl.BlockSpec(shape, lambda i: (0,) * len(shape))
    return pl.pallas_call(
        _combine_kernel,
        grid=(n // ts,),
        in_specs=[
            pl.BlockSpec((TOP_K * ts,), lambda i: (i,), memory_space=pltpu.SMEM),
            pl.BlockSpec((TOP_K * ts,), lambda i: (jnp.minimum(i + 1, last),), memory_space=pltpu.SMEM),
            pl.BlockSpec(memory_space=pl.ANY),
            pl.BlockSpec((ts, D_MODEL), lambda i: (i, 0)),
            pl.BlockSpec((ts, TOP_K), lambda i: (i, 0)),
            pl.BlockSpec((ts, PLE_DIM), lambda i: (i, 0)),
            full(g2.shape), full(b2.shape), full(pw.shape), full(gw.shape), full(gb.shape),
            full(g3.shape), full(b3.shape),
        ],
        out_specs=pl.BlockSpec((ts, D_MODEL), lambda i: (i, 0)),
        out_shape=jax.ShapeDtypeStruct((n, D_MODEL), _F32),
        scratch_shapes=[
            pltpu.VMEM((2, TOP_K, ts * GATHER_PITCH, LANES), _F32),
            pltpu.VMEM((ts, D_MODEL), _F32),
            pltpu.SemaphoreType.DMA((2,)),
        ],
        compiler_params=pltpu.CompilerParams(
            dimension_semantics=("arbitrary",),
            vmem_limit_bytes=48 * 1024 * 1024),
        name="combine",
    )(dest_flat, dest_flat, yb, x1, gates_t, p2d, g2, b2, pw, gw, gb, g3, b3)


def kernel(x, p, w_in, pool_mix, pool_scale, conv_w, w_out, ln1_g, ln1_b, router_w, router_b,
           w_gate_up, b_gate_up, w_down, b_down, ln2_g, ln2_b, ple_proj, ple_gate_w, ple_gate_b,
           ln3_g, ln3_b):
    assert DEPTH == 1 and x.shape[-1] == D_MODEL
    bsz, seq, d = x.shape
    n = bsz * seq
    assert seq % TS_MIX == 0 and n % TS_DISPATCH == 0 and n % TS_COMBINE == 0
    row = lambda v: v.reshape(1, -1)

    tri = jnp.triu(jnp.ones((TS_MIX, TS_MIX), _BF16))
    low = jnp.tril(jnp.ones((N_EXPERTS, N_EXPERTS), _BF16), k=-1)
    x1rt, x1, dest, gates, counts, table = _mix_route(
        x, w_in[0].astype(_BF16), pool_mix[0].astype(_BF16), row(pool_scale[0]), conv_w[0],
        w_out[0].astype(_BF16), row(ln1_g[0]), row(ln1_b[0]),
        router_w[0].T.astype(_BF16), jnp.broadcast_to(router_b[0][:, None], (N_EXPERTS, LANES)), tri, low)

    tm = TM_EXPERT
    n_slots = n * TOP_K + N_EXPERTS * tm
    nb = n_slots // tm
    cnt = counts[:, 0]
    tiles_e = (cnt + tm - 1) // tm
    tile_end = jnp.cumsum(tiles_e)
    n_valid = tile_end[-1].astype(jnp.int32)
    steps = jnp.arange(nb, dtype=jnp.int32)
    blk_e = jnp.minimum(jnp.sum(steps[:, None] >= tile_end[None, :], axis=1), N_EXPERTS - 1)
    blk_e = jnp.where(steps < n_valid, blk_e, blk_e[n_valid - 1]).astype(jnp.int32)
    within = steps - (tile_end - tiles_e)[blk_e]
    phys = jnp.where(steps < n_valid, table[blk_e, jnp.clip(within, 0, LANES - 1)], steps).astype(jnp.int32)

    padded = tiles_e
    last_tile = jnp.where(tiles_e > 0, table[jnp.arange(N_EXPERTS), jnp.maximum(tiles_e - 1, 0)], -1)
    last_tile = last_tile.astype(jnp.int32)
    e_ids = jnp.arange(N_EXPERTS, dtype=jnp.int32)
    later = jnp.logical_and(e_ids[None, :] > e_ids[:, None], (padded > 0)[None, :])
    next_nonempty = jnp.min(jnp.where(later, e_ids[None, :], N_EXPERTS), axis=1)
    next_e = jnp.where(next_nonempty == N_EXPERTS, -1, next_nonempty).astype(jnp.int32)
    switched = jnp.concatenate([jnp.zeros((1,), jnp.int32), (blk_e[1:] != blk_e[:-1]).astype(jnp.int32)])
    slot_par = (jnp.cumsum(switched) % 2).astype(jnp.int32)
    n_valid = n_valid.reshape(1)

    dest = dest[:TOP_K].T.reshape(-1)
    xs = _dispatch(last_tile, n_valid, dest, x1rt, n_slots)
    yb = _experts(blk_e, n_valid, slot_par, next_e, phys, xs,
                  w_gate_up[0], b_gate_up[0][:, None, :], w_down[0], b_down[0][:, None, :])
    out = _combine(dest, yb, x1, gates[:TOP_K].T, p[0].reshape(n, PLE_DIM),
                   row(ln2_g[0]), row(ln2_b[0]), ple_proj[0].astype(_BF16),
                   ple_gate_w[0].astype(_BF16), row(ple_gate_b[0]), row(ln3_g[0]), row(ln3_b[0]))
    return out.reshape(bsz, seq, d)
```

```python
import functools

import jax
import jax.numpy as jnp
from jax import lax
from jax.experimental import pallas as pl
from jax.experimental.pallas import tpu as pltpu

D_MODEL = 1024
D_POOL = 512
D_CONV = 512
POOL_WINDOWS = (2, 4, 8, 16)
POOL_GROUP_DIM = 128
CONV_WIDTH = 3
D_IN_PROJ = D_POOL + 3 * D_CONV
N_EXPERTS = 32
TOP_K = 4
D_FF = 1024
SWIGLU_LIMIT = 7.0
SWIGLU_ALPHA = 1.702
PLE_DIM = 256
DEPTH = 1
DEEPNORM_ALPHA = (2.0 * DEPTH) ** 0.25
LN_EPS = 1e-5

LANES = 128
SUBLANES = 8
ROW_CHUNKS = D_MODEL // LANES
POOL_HALO = 16
CONV_HALO = 8
GATHER_PITCH = ROW_CHUNKS + 1

TS_MIX = 512
TM_EXPERT = 512
TS_COMBINE = 256
COMBINE_GROUP = 8

_BF16 = jnp.bfloat16
_F32 = jnp.float32


def _layer_norm(h, g, b):
    mu = jnp.mean(h, axis=-1, keepdims=True)
    c = h - mu
    var = jnp.mean(c * c, axis=-1, keepdims=True)
    return c * lax.rsqrt(var + LN_EPS) * g + b


def _to_row_tiles(ref, val, pitch=ROW_CHUNKS):
    rows = val.shape[0]
    for c in range(ROW_CHUNKS):
        ref[pl.ds(c, rows, stride=pitch), :] = val[:, c * LANES:(c + 1) * LANES]
    for c in range(ROW_CHUNKS, pitch):
        ref[pl.ds(c, rows, stride=pitch), :] = jnp.zeros((rows, LANES), val.dtype)


def _from_row_tiles(ref, rows, pitch=ROW_CHUNKS):
    return jnp.concatenate(
        [ref[pl.ds(c, rows, stride=pitch), :] for c in range(ROW_CHUNKS)], axis=1)


def _row_tile(ref, r, pitch=ROW_CHUNKS):
    if pitch == ROW_CHUNKS:
        return ref.at[pl.ds(pl.multiple_of(r * ROW_CHUNKS, ROW_CHUNKS), ROW_CHUNKS)]
    return ref.at[pl.ds(r * pitch, ROW_CHUNKS)]


def _mix_route_kernel(ns, x_ref, w_in_ref, pmix_ref, pscale_ref, convw_ref, w_out_ref, g1_ref, b1_ref,
                      rwt_ref, rb_ref, tri_ref, low_ref,
                      x1_ref, dest_ref, gate_ref, cnt_ref, tab_ref, xs_ref,
                      xbuf, dsm, carry_v, carry_u, cnt_sc, used_sc, cur_sc, tab_sc, meta_v, meta_s, zbuf,
                      row_sems, dest_sem):
    g = pl.program_id(0)
    n_tiles = pl.num_programs(0) - 1
    ts = TS_MIX
    par = lax.rem(g, 2)
    tile_rows = ts * GATHER_PITCH

    def wait_rows(q):
        for _ in range(TOP_K):
            pltpu.make_async_copy(xbuf.at[q], xs_ref.at[pl.ds(0, tile_rows)], row_sems.at[q]).wait()

    @pl.when(g >= 2)
    def _():
        wait_rows(par)

    @pl.when(g < n_tiles)
    def _():
        _mix_route_tile(lax.rem(g, ns), g == 0, par, x_ref, w_in_ref, pmix_ref, pscale_ref, convw_ref, w_out_ref,
                        g1_ref, b1_ref, rwt_ref, rb_ref, tri_ref, low_ref, x1_ref, dest_ref, gate_ref, cnt_ref,
                        tab_ref, xbuf, dsm, carry_v, carry_u, cnt_sc, used_sc, cur_sc, tab_sc, dest_sem)

    @pl.when(g >= 1)
    def _():
        q = 1 - par
        base = q * (TOP_K * ts)

        def body(t, carry):
            for k in range(TOP_K):
                slot = dsm[base + k * ts + t]
                pltpu.make_async_copy(xbuf.at[q, pl.ds(t * GATHER_PITCH, GATHER_PITCH)],
                                      xs_ref.at[pl.ds(slot * GATHER_PITCH, GATHER_PITCH)],
                                      row_sems.at[q]).start(priority=k % 2)
            return carry
        lax.fori_loop(0, ts, body, 0, unroll=8)

        @pl.when(g == n_tiles)
        def _():
            wait_rows(q)
            _seal_slots(xs_ref, cnt_sc, tab_sc, meta_v, meta_s, zbuf, dest_sem)


def _seal_slots(xs_ref, cnt_sc, tab_sc, meta_v, meta_s, zbuf, sem):
    tm = TM_EXPERT
    tile_rows = tm * GATHER_PITCH
    n_tiles = xs_ref.shape[0] // tile_rows
    meta_v[0] = cnt_sc[...].astype(jnp.int32)
    meta_v[1] = tab_sc[...].astype(jnp.int32)
    to_scalar = pltpu.make_async_copy(meta_v, meta_s, sem)
    to_scalar.start()
    to_scalar.wait()
    zbuf[...] = jnp.zeros_like(zbuf)
    pieces = [1 << b for b in reversed(range(tm.bit_length() - 1))]

    def each_copy(fn):
        owned = 0
        for e in range(N_EXPERTS):
            cnt = meta_s[0, e, 0]
            tiles = lax.shift_right_logical(cnt + (tm - 1), tm.bit_length() - 1)
            owned = owned + tiles
            last = meta_s[1, e, jnp.maximum(tiles - 1, 0)]
            fill = cnt - jnp.maximum(tiles - 1, 0) * tm
            spare = tm - fill
            pos = last * tm + fill
            for piece in pieces:
                take = jnp.logical_and(tiles > 0, (spare & piece) != 0)

                @pl.when(take)
                def _():
                    fn(pltpu.make_async_copy(
                        zbuf.at[pl.ds(0, piece * GATHER_PITCH)],
                        xs_ref.at[pl.ds(pos * GATHER_PITCH, piece * GATHER_PITCH)], sem))
                pos = pos + jnp.where((spare & piece) != 0, piece, 0)

        def unowned(j, carry):
            fn(pltpu.make_async_copy(
                zbuf, xs_ref.at[pl.ds(pl.multiple_of(j * tile_rows, tile_rows), tile_rows)], sem))
            return carry
        lax.fori_loop(owned, n_tiles, unowned, 0)

    each_copy(lambda c: c.start())
    each_copy(lambda c: c.wait())


def _mix_route_tile(s, first, par, x_ref, w_in_ref, pmix_ref, pscale_ref, convw_ref, w_out_ref, g1_ref, b1_ref,
                    rwt_ref, rb_ref, tri_ref, low_ref, x1_ref, dest_ref, gate_ref, cnt_ref, tab_ref,
                    xbuf, dsm, carry_v, carry_u, cnt_sc, used_sc, cur_sc, tab_sc, dest_sem):
    ts = TS_MIX

    @pl.when(first)
    def _():
        cnt_sc[...] = jnp.zeros_like(cnt_sc)
        used_sc[...] = jnp.zeros_like(used_sc)
        cur_sc[...] = jnp.zeros_like(cur_sc)
        tab_sc[...] = jnp.zeros_like(tab_sc)

    @pl.when(s == 0)
    def _():
        carry_v[...] = jnp.zeros_like(carry_v)
        carry_u[...] = jnp.zeros_like(carry_u)

    xb = x_ref[0]
    proj = jnp.dot(xb.astype(_BF16), w_in_ref[...], preferred_element_type=_F32)
    vp = proj[:, :D_POOL]
    bg = proj[:, D_POOL:D_POOL + D_CONV]
    cg = proj[:, D_POOL + D_CONV:D_POOL + 2 * D_CONV]
    vc = proj[:, D_POOL + 2 * D_CONV:]

    ext = jnp.concatenate([carry_v[...], vp], axis=0)
    pos = lax.broadcasted_iota(jnp.int32, (ts, 1), 0) + s * ts
    mixed = []
    for g, w in enumerate(POOL_WINDOWS):
        lo, hi = g * POOL_GROUP_DIM, (g + 1) * POOL_GROUP_DIM
        acc = ext[:, lo:hi]
        sh = 1
        while sh < w:
            acc = acc + pltpu.roll(acc, sh, 0)
            sh *= 2
        cnt = jnp.minimum(pos + 1, w).astype(_F32)
        d = acc[POOL_HALO:] / cnt - vp[:, lo:hi]
        yg = jnp.dot(d.astype(_BF16), pmix_ref[g], preferred_element_type=_F32)
        mixed.append(yg * pscale_ref[:, lo:hi])
    carry_v[...] = vp[ts - POOL_HALO:]

    u = cg * vc
    extu = jnp.concatenate([carry_u[...], u], axis=0)
    u1 = pltpu.roll(extu, 1, 0)[CONV_HALO:]
    u2 = pltpu.roll(extu, 2, 0)[CONV_HALO:]
    yc = bg * (convw_ref[0:1, :] * u2 + convw_ref[1:2, :] * u1 + convw_ref[2:3, :] * u)
    carry_u[...] = u[ts - CONV_HALO:]
    mixed.append(yc)

    mix_in = jnp.concatenate(mixed, axis=1).astype(_BF16)
    mix = jnp.dot(mix_in, w_out_ref[...], preferred_element_type=_F32)
    x1 = _layer_norm(DEEPNORM_ALPHA * xb + mix, g1_ref[...], b1_ref[...])
    _to_row_tiles(xbuf.at[par], x1, GATHER_PITCH)
    x1_ref[...] = x1

    logits = lax.dot_general(rwt_ref[...], x1.astype(_BF16), (((1,), (1,)), ((), ())),
                             preferred_element_type=_F32) + rb_ref[:, 0:1]
    eidx = lax.broadcasted_iota(jnp.int32, (N_EXPERTS, ts), 0).astype(_F32)
    vals, sels = [], []
    work = logits
    for k in range(TOP_K):
        m = jnp.max(work, axis=0, keepdims=True)
        first = jnp.min(jnp.where(work == m, eidx, float(N_EXPERTS)), axis=0, keepdims=True)
        sel = eidx == first
        work = jnp.where(sel, -jnp.inf, work)
        vals.append(m)
        sels.append(sel)
    exps = [jnp.exp(v - vals[0]) for v in vals]
    denom = exps[0] + exps[1] + exps[2] + exps[3]
    for k in range(TOP_K):
        gate_ref[k:k + 1, :] = exps[k] / denom
    gate_ref[TOP_K:, :] = jnp.zeros((SUBLANES - TOP_K, ts), _F32)

    tm = float(TM_EXPERT)
    chosen = jnp.zeros((N_EXPERTS, ts), _F32)
    for sel in sels:
        chosen = chosen + sel.astype(_F32)
    incl = jnp.dot(chosen.astype(_BF16), tri_ref[...], preferred_element_type=_F32)
    cnt_b = cnt_sc[:, 0:1]
    cnt_a = cnt_b + jnp.sum(chosen, axis=1, keepdims=True)
    q_b = jnp.floor((cnt_b + (tm - 1.0)) * (1.0 / tm))
    q_a = jnp.floor((cnt_a + (tm - 1.0)) * (1.0 / tm))
    n_new = q_a - q_b
    before = jnp.dot(low_ref[...], jnp.broadcast_to(n_new, (N_EXPERTS, LANES)).astype(_BF16),
                     preferred_element_type=_F32)[:, 0:1]
    used = used_sc[:, 0:1]
    new_tile = used + before
    cur_tile = cur_sc[:, 0:1]
    rank = cnt_b + (incl - chosen)
    page = jnp.floor(rank * (1.0 / tm))
    slot = jnp.where(page < q_b, cur_tile, new_tile) * tm + (rank - page * tm)
    for k in range(TOP_K):
        dest_ref[k:k + 1, :] = jnp.sum(jnp.where(sels[k], slot, 0.0), axis=0, keepdims=True).astype(jnp.int32)
    dest_ref[TOP_K:, :] = jnp.zeros((SUBLANES - TOP_K, ts), jnp.int32)

    took = n_new > 0.0
    lane = lax.broadcasted_iota(jnp.int32, (N_EXPERTS, LANES), 1).astype(_F32)
    table = jnp.where(jnp.logical_and(lane == q_b, took), new_tile, tab_sc[...])
    tab_sc[...] = table
    cur_sc[...] = jnp.broadcast_to(jnp.where(took, new_tile, cur_tile), (N_EXPERTS, LANES))
    used_sc[...] = jnp.broadcast_to(used + jnp.sum(n_new, axis=0, keepdims=True), (N_EXPERTS, LANES))
    cnt_sc[...] = jnp.broadcast_to(cnt_a, (N_EXPERTS, LANES))
    cnt_ref[...] = jnp.broadcast_to(cnt_a, (N_EXPERTS, LANES)).astype(jnp.int32)
    tab_ref[...] = table.astype(jnp.int32)

    to_scalar = [pltpu.make_async_copy(dest_ref.at[k], dsm.at[pl.ds((par * TOP_K + k) * ts, ts)], dest_sem)
                 for k in range(TOP_K)]
    for c in to_scalar:
        c.start()
    for c in to_scalar:
        c.wait()


def _mix_route(x, w_in, pmix, pscale, convw, w_out, g1, b1, rwt, rb, tri, low, n_slots):
    bsz, seq, d = x.shape
    n = bsz * seq
    ns = seq // TS_MIX
    n_tiles = bsz * ns
    assert n // TM_EXPERT <= LANES
    full = lambda shape: pl.BlockSpec(shape, lambda g: (0,) * len(shape))
    tile = lambda g: jnp.minimum(g, n_tiles - 1)
    state = pltpu.VMEM((N_EXPERTS, LANES), _F32)
    return pl.pallas_call(
        functools.partial(_mix_route_kernel, ns),
        grid=(n_tiles + 1,),
        in_specs=[
            pl.BlockSpec((1, TS_MIX, d), lambda g: (tile(g) // ns, tile(g) % ns, 0)),
            full(w_in.shape), full(pmix.shape), full(pscale.shape), full(convw.shape),
            full(w_out.shape), full(g1.shape), full(b1.shape), full(rwt.shape), full(rb.shape),
            full(tri.shape), full(low.shape),
        ],
        out_specs=[
            pl.BlockSpec((TS_MIX, D_MODEL), lambda g: (tile(g), 0)),
            pl.BlockSpec((SUBLANES, TS_MIX), lambda g: (0, tile(g))),
            pl.BlockSpec((SUBLANES, TS_MIX), lambda g: (0, tile(g))),
            pl.BlockSpec((N_EXPERTS, LANES), lambda g: (0, 0)),
            pl.BlockSpec((N_EXPERTS, LANES), lambda g: (0, 0)),
            pl.BlockSpec(memory_space=pl.ANY),
        ],
        out_shape=[
            jax.ShapeDtypeStruct((n, D_MODEL), _F32),
            jax.ShapeDtypeStruct((SUBLANES, n), jnp.int32),
            jax.ShapeDtypeStruct((SUBLANES, n), _F32),
            jax.ShapeDtypeStruct((N_EXPERTS, LANES), jnp.int32),
            jax.ShapeDtypeStruct((N_EXPERTS, LANES), jnp.int32),
            jax.ShapeDtypeStruct((n_slots * GATHER_PITCH, LANES), _F32),
        ],
        scratch_shapes=[
            pltpu.VMEM((2, TS_MIX * GATHER_PITCH, LANES), _F32),
            pltpu.SMEM((2 * TOP_K * TS_MIX,), jnp.int32),
            pltpu.VMEM((POOL_HALO, D_POOL), _F32),
            pltpu.VMEM((CONV_HALO, D_CONV), _F32),
            state, state, state, state,
            pltpu.VMEM((2, N_EXPERTS, LANES), jnp.int32),
            pltpu.SMEM((2, N_EXPERTS, LANES), jnp.int32),
            pltpu.VMEM((TM_EXPERT * GATHER_PITCH, LANES), _F32),
            pltpu.SemaphoreType.DMA((2,)),
            pltpu.SemaphoreType.DMA(()),
        ],
        compiler_params=pltpu.CompilerParams(
            dimension_semantics=("arbitrary",),
            vmem_limit_bytes=48 * 1024 * 1024),
        name="mix_route",
    )(x, w_in, pmix, pscale, convw, w_out, g1, b1, rwt, rb, tri, low)


def _experts_kernel(be_ref, nv_ref, par_ref, nxt_ref, phys_ref, xs_ref, wgu_hbm, bgu_ref, wdn_hbm, bdn_ref, out_ref,
                    wgu_f32, wdn_f32, wgu_bf, wdn_bf, sems):
    j = pl.program_id(0)
    tm = TM_EXPERT
    valid = j < nv_ref[0]
    new_expert = jnp.logical_or(j == 0, be_ref[j] != be_ref[jnp.maximum(j - 1, 0)])

    def weight_copies(e, s):
        return (pltpu.make_async_copy(wgu_hbm.at[e], wgu_f32.at[s], sems.at[0, s]),
                pltpu.make_async_copy(wdn_hbm.at[e], wdn_f32.at[s], sems.at[1, s]))

    @pl.when(j == 0)
    def _():
        for c in weight_copies(be_ref[0], par_ref[0]):
            c.start()

    @pl.when(jnp.logical_and(valid, new_expert))
    def _():
        s = par_ref[j]
        for c in weight_copies(be_ref[j], s):
            c.wait()

        nxt = nxt_ref[be_ref[j]]

        @pl.when(nxt >= 0)
        def _():
            for c in weight_copies(nxt, 1 - s):
                c.start()

        wgu_bf[...] = wgu_f32[s].astype(_BF16)
        wdn_bf[...] = wdn_f32[s].astype(_BF16)

    @pl.when(valid)
    def _():
        x = _from_row_tiles(xs_ref, tm, GATHER_PITCH).astype(_BF16)
        gu = jnp.dot(x, wgu_bf[...], preferred_element_type=_F32) + bgu_ref[0]
        gate = jnp.minimum(gu[:, :D_FF], SWIGLU_LIMIT)
        up = jnp.clip(gu[:, D_FF:], -SWIGLU_LIMIT, SWIGLU_LIMIT)
        glu = gate * jax.nn.sigmoid(SWIGLU_ALPHA * gate)
        act = ((up + 1.0) * glu).astype(_BF16)
        y = jnp.dot(act, wdn_bf[...], preferred_element_type=_F32) + bdn_ref[0]
        _to_row_tiles(out_ref, y, GATHER_PITCH)

    @pl.when(j >= nv_ref[0])
    def _():
        out_ref[...] = jnp.zeros_like(out_ref)


def _experts(blk_e, n_valid, slot_par, next_e, phys, xs, wgu, bgu, wdn, bdn):
    n_slots = xs.shape[0] // GATHER_PITCH
    nb = n_slots // TM_EXPERT
    rows = TM_EXPERT * GATHER_PITCH
    grid_spec = pltpu.PrefetchScalarGridSpec(
        num_scalar_prefetch=5,
        grid=(nb,),
        in_specs=[
            pl.BlockSpec((rows, LANES), lambda j, be, nv, par, nxt, ph: (ph[jnp.minimum(j, nv[0] - 1)], 0)),
            pl.BlockSpec(memory_space=pl.ANY),
            pl.BlockSpec((1, 1, 2 * D_FF), lambda j, be, nv, par, nxt, ph: (be[j], 0, 0)),
            pl.BlockSpec(memory_space=pl.ANY),
            pl.BlockSpec((1, 1, D_MODEL), lambda j, be, nv, par, nxt, ph: (be[j], 0, 0)),
        ],
        out_specs=pl.BlockSpec((rows, LANES), lambda j, be, nv, par, nxt, ph: (ph[j], 0)),
        scratch_shapes=[
            pltpu.VMEM((2, D_MODEL, 2 * D_FF), _F32),
            pltpu.VMEM((2, D_FF, D_MODEL), _F32),
            pltpu.VMEM((D_MODEL, 2 * D_FF), _BF16),
            pltpu.VMEM((D_FF, D_MODEL), _BF16),
            pltpu.SemaphoreType.DMA((2, 2)),
        ],
    )
    return pl.pallas_call(
        _experts_kernel,
        grid_spec=grid_spec,
        out_shape=jax.ShapeDtypeStruct((n_slots * GATHER_PITCH, LANES), _F32),
        compiler_params=pltpu.CompilerParams(
            dimension_semantics=("arbitrary",),
            vmem_limit_bytes=60 * 1024 * 1024),
        name="experts",
    )(blk_e, n_valid, slot_par, next_e, phys, xs, wgu, bgu, wdn, bdn)


def _combine_kernel(dcur_ref, dnext_ref, yb_ref, x1_ref, gt_ref, p_ref, g2_ref, b2_ref, pw_ref, gw_ref,
                    gb_ref, g3_ref, b3_ref, out_ref, gbuf, hbuf, sems):
    ts = TS_COMBINE
    grp = COMBINE_GROUP
    i = pl.program_id(0)
    slot = i % 2
    nxt = 1 - slot

    def start_rows(d_ref, s, t0):
        for j in range(grp):
            t = t0 + j
            for k in range(TOP_K):
                pltpu.make_async_copy(_row_tile(yb_ref, d_ref[t * TOP_K + k], GATHER_PITCH),
                                      gbuf.at[s, k, pl.ds(t * GATHER_PITCH, ROW_CHUNKS)],
                                      sems.at[s]).start(priority=k % 2)

    def wait_tile(s):
        for k in range(TOP_K):
            pltpu.make_async_copy(yb_ref.at[pl.ds(0, ts * ROW_CHUNKS)],
                                  gbuf.at[s, k, pl.ds(0, ts * ROW_CHUNKS)], sems.at[s]).wait()

    @pl.when(i == 0)
    def _():
        def first(r, carry):
            start_rows(dcur_ref, 0, r * grp)
            return carry
        lax.fori_loop(0, ts // grp, first, 0)

    wait_tile(slot)

    def group(r, carry):
        t0 = pl.multiple_of(r * grp, grp)
        x1g = x1_ref[pl.ds(t0, grp), :]
        gts = gt_ref[pl.ds(t0, grp), :]
        ys = []
        for k in range(TOP_K):
            rows_k = gbuf.at[slot, k]
            ys.append(jnp.concatenate(
                [rows_k[pl.ds(t0 * GATHER_PITCH + c, grp, stride=GATHER_PITCH), :] for c in range(ROW_CHUNKS)],
                axis=1))
        start_rows(dnext_ref, nxt, t0)
        ffn = gts[:, 0:1] * ys[0]
        for k in range(1, TOP_K):
            ffn = ffn + gts[:, k:k + 1] * ys[k]
        hbuf[pl.ds(t0, grp), :] = DEEPNORM_ALPHA * x1g + ffn
        return carry

    lax.fori_loop(0, ts // grp, group, 0)

    @pl.when(i == pl.num_programs(0) - 1)
    def _():
        wait_tile(nxt)

    x2 = _layer_norm(hbuf[...], g2_ref[...], b2_ref[...])
    z = jnp.dot(x2.astype(_BF16), gw_ref[...], preferred_element_type=_F32) + gb_ref[...]
    ple = jnp.dot(p_ref[...].astype(_BF16), pw_ref[...], preferred_element_type=_F32)
    x3 = _layer_norm(DEEPNORM_ALPHA * x2 + jax.nn.sigmoid(z) * ple, g3_ref[...], b3_ref[...])
    out_ref[...] = x3


def _combine(dest_flat, yb, x1, gates_t, p2d, g2, b2, pw, gw, gb, g3, b3):
    n = dest_flat.shape[0] // TOP_K
    ts = TS_COMBINE
    last = n // ts - 1
    full = lambda shape: pl.BlockSpec(shape, lambda i: (0,) * len(shape))
    return pl.pallas_call(
        _combine_kernel,
        grid=(n // ts,),
        in_specs=[
            pl.BlockSpec((TOP_K * ts,), lambda i: (i,), memory_space=pltpu.SMEM),
            pl.BlockSpec((TOP_K * ts,), lambda i: (jnp.minimum(i + 1, last),), memory_space=pltpu.SMEM),
            pl.BlockSpec(memory_space=pl.ANY),
            pl.BlockSpec((ts, D_MODEL), lambda i: (i, 0)),
            pl.BlockSpec((ts, TOP_K), lambda i: (i, 0)),
            pl.BlockSpec((ts, PLE_DIM), lambda i: (i, 0)),
            full(g2.shape), full(b2.shape), full(pw.shape), full(gw.shape), full(gb.shape),
            full(g3.shape), full(b3.shape),
        ],
        out_specs=pl.BlockSpec((ts, D_MODEL), lambda i: (i, 0)),
        out_shape=jax.ShapeDtypeStruct((n, D_MODEL), _F32),
        scratch_shapes=[
            pltpu.VMEM((2, TOP_K, ts * GATHER_PITCH, LANES), _F32),
            pltpu.VMEM((ts, D_MODEL), _F32),
            pltpu.SemaphoreType.DMA((2,)),
        ],
        compiler_params=pltpu.CompilerParams(
            dimension_semantics=("arbitrary",),
            vmem_limit_bytes=48 * 1024 * 1024),
        name="combine",
    )(dest_flat, dest_flat, yb, x1, gates_t, p2d, g2, b2, pw, gw, gb, g3, b3)


def kernel(x, p, w_in, pool_mix, pool_scale, conv_w, w_out, ln1_g, ln1_b, router_w, router_b,
           w_gate_up, b_gate_up, w_down, b_down, ln2_g, ln2_b, ple_proj, ple_gate_w, ple_gate_b,
           ln3_g, ln3_b):
    assert DEPTH == 1 and x.shape[-1] == D_MODEL
    bsz, seq, d = x.shape
    n = bsz * seq
    assert seq % TS_MIX == 0 and n % TS_COMBINE == 0
    row = lambda v: v.reshape(1, -1)

    tri = jnp.triu(jnp.ones((TS_MIX, TS_MIX), _BF16))
    low = jnp.tril(jnp.ones((N_EXPERTS, N_EXPERTS), _BF16), k=-1)
    tm = TM_EXPERT
    n_slots = n * TOP_K + N_EXPERTS * tm
    nb = n_slots // tm
    x1, dest, gates, counts, table, xs = _mix_route(
        x, w_in[0].astype(_BF16), pool_mix[0].astype(_BF16), row(pool_scale[0]), conv_w[0],
        w_out[0].astype(_BF16), row(ln1_g[0]), row(ln1_b[0]),
        router_w[0].T.astype(_BF16), jnp.broadcast_to(router_b[0][:, None], (N_EXPERTS, LANES)), tri, low,
        n_slots)

    cnt = counts[:, 0]
    tiles_e = (cnt + tm - 1) // tm
    tile_end = jnp.cumsum(tiles_e)
    n_valid = tile_end[-1].astype(jnp.int32)
    steps = jnp.arange(nb, dtype=jnp.int32)
    blk_e = jnp.minimum(jnp.sum(steps[:, None] >= tile_end[None, :], axis=1), N_EXPERTS - 1)
    blk_e = jnp.where(steps < n_valid, blk_e, blk_e[n_valid - 1]).astype(jnp.int32)
    within = steps - (tile_end - tiles_e)[blk_e]
    phys = jnp.where(steps < n_valid, table[blk_e, jnp.clip(within, 0, LANES - 1)], steps).astype(jnp.int32)

    padded = tiles_e
    e_ids = jnp.arange(N_EXPERTS, dtype=jnp.int32)
    later = jnp.logical_and(e_ids[None, :] > e_ids[:, None], (padded > 0)[None, :])
    next_nonempty = jnp.min(jnp.where(later, e_ids[None, :], N_EXPERTS), axis=1)
    next_e = jnp.where(next_nonempty == N_EXPERTS, -1, next_nonempty).astype(jnp.int32)
    switched = jnp.concatenate([jnp.zeros((1,), jnp.int32), (blk_e[1:] != blk_e[:-1]).astype(jnp.int32)])
    slot_par = (jnp.cumsum(switched) % 2).astype(jnp.int32)
    n_valid = n_valid.reshape(1)

    dest = dest[:TOP_K].T.reshape(-1)
    yb = _experts(blk_e, n_valid, slot_par, next_e, phys, xs,
                  w_gate_up[0], b_gate_up[0][:, None, :], w_down[0], b_down[0][:, None, :])
    out = _combine(dest, yb, x1, gates[:TOP_K].T, p[0].reshape(n, PLE_DIM),
                   row(ln2_g[0]), row(ln2_b[0]), ple_proj[0].astype(_BF16),
                   ple_gate_w[0].astype(_BF16), row(ple_gate_b[0]), row(ln3_g[0]), row(ln3_b[0]))
    return out.reshape(bsz, seq, d)
```

```python
import functools

import jax
import jax.numpy as jnp
from jax import lax
from jax.experimental import pallas as pl
from jax.experimental.pallas import tpu as pltpu

D_MODEL = 1024
D_POOL = 512
D_CONV = 512
POOL_WINDOWS = (2, 4, 8, 16)
POOL_GROUP_DIM = 128
CONV_WIDTH = 3
D_IN_PROJ = D_POOL + 3 * D_CONV
N_EXPERTS = 32
TOP_K = 4
D_FF = 1024
SWIGLU_LIMIT = 7.0
SWIGLU_ALPHA = 1.702
PLE_DIM = 256
DEPTH = 1
DEEPNORM_ALPHA = (2.0 * DEPTH) ** 0.25
LN_EPS = 1e-5

LANES = 128
SUBLANES = 8
ROW_CHUNKS = D_MODEL // LANES
POOL_HALO = 16
CONV_HALO = 8
GATHER_PITCH = ROW_CHUNKS + 1

TS_MIX = 512
TM_EXPERT = 512
TS_COMBINE = 256
COMBINE_GROUP = 8

_BF16 = jnp.bfloat16
_F32 = jnp.float32


def _layer_norm(h, g, b):
    mu = jnp.mean(h, axis=-1, keepdims=True)
    c = h - mu
    var = jnp.mean(c * c, axis=-1, keepdims=True)
    return c * lax.rsqrt(var + LN_EPS) * g + b


def _to_row_tiles(ref, val, pitch=ROW_CHUNKS):
    rows = val.shape[0]
    for c in range(ROW_CHUNKS):
        ref[pl.ds(c, rows, stride=pitch), :] = val[:, c * LANES:(c + 1) * LANES]
    for c in range(ROW_CHUNKS, pitch):
        ref[pl.ds(c, rows, stride=pitch), :] = jnp.zeros((rows, LANES), val.dtype)


def _from_row_tiles(ref, rows, pitch=ROW_CHUNKS):
    return jnp.concatenate(
        [ref[pl.ds(c, rows, stride=pitch), :] for c in range(ROW_CHUNKS)], axis=1)


def _row_tile(ref, r, pitch=ROW_CHUNKS):
    if pitch == ROW_CHUNKS:
        return ref.at[pl.ds(pl.multiple_of(r * ROW_CHUNKS, ROW_CHUNKS), ROW_CHUNKS)]
    return ref.at[pl.ds(r * pitch, ROW_CHUNKS)]


def _mix_route_kernel(ns, x_ref, w_in_ref, pmix_ref, pscale_ref, convw_ref, w_out_ref, g1_ref, b1_ref,
                      rwt_ref, rb_ref, tri_ref, low_ref,
                      x1_ref, dest_ref, gate_ref, cnt_ref, tab_ref, xs_ref,
                      xbuf, dsm, carry_v, carry_u, cnt_sc, used_sc, cur_sc, tab_sc, meta_v, meta_s, zbuf,
                      row_sems, dest_sem):
    g = pl.program_id(0)
    n_tiles = pl.num_programs(0) - 1
    ts = TS_MIX
    par = lax.rem(g, 2)
    tile_rows = ts * GATHER_PITCH

    def wait_rows(q):
        for _ in range(TOP_K):
            pltpu.make_async_copy(xbuf.at[q], xs_ref.at[pl.ds(0, tile_rows)], row_sems.at[q]).wait()

    @pl.when(g >= 2)
    def _():
        wait_rows(par)

    @pl.when(g >= 1)
    def _():
        q = 1 - par
        base = q * (TOP_K * ts)

        def body(t, carry):
            for k in range(TOP_K):
                slot = dsm[base + k * ts + t]
                pltpu.make_async_copy(xbuf.at[q, pl.ds(t * GATHER_PITCH, GATHER_PITCH)],
                                      xs_ref.at[pl.ds(slot * GATHER_PITCH, GATHER_PITCH)],
                                      row_sems.at[q]).start(priority=k % 2)
            return carry
        lax.fori_loop(0, ts, body, 0, unroll=8)

    @pl.when(g < n_tiles)
    def _():
        _mix_route_tile(lax.rem(g, ns), g == 0, par, x_ref, w_in_ref, pmix_ref, pscale_ref, convw_ref, w_out_ref,
                        g1_ref, b1_ref, rwt_ref, rb_ref, tri_ref, low_ref, x1_ref, dest_ref, gate_ref, cnt_ref,
                        tab_ref, xbuf, dsm, carry_v, carry_u, cnt_sc, used_sc, cur_sc, tab_sc, dest_sem)

    @pl.when(g == n_tiles)
    def _():
        wait_rows(1 - par)
        _seal_slots(xs_ref, cnt_sc, tab_sc, meta_v, meta_s, zbuf, dest_sem)


def _seal_slots(xs_ref, cnt_sc, tab_sc, meta_v, meta_s, zbuf, sem):
    tm = TM_EXPERT
    tile_rows = tm * GATHER_PITCH
    n_tiles = xs_ref.shape[0] // tile_rows
    meta_v[0] = cnt_sc[...].astype(jnp.int32)
    meta_v[1] = tab_sc[...].astype(jnp.int32)
    to_scalar = pltpu.make_async_copy(meta_v, meta_s, sem)
    to_scalar.start()
    to_scalar.wait()
    zbuf[...] = jnp.zeros_like(zbuf)
    pieces = [1 << b for b in reversed(range(tm.bit_length() - 1))]

    def each_copy(fn):
        owned = 0
        for e in range(N_EXPERTS):
            cnt = meta_s[0, e, 0]
            tiles = lax.shift_right_logical(cnt + (tm - 1), tm.bit_length() - 1)
            owned = owned + tiles
            last = meta_s[1, e, jnp.maximum(tiles - 1, 0)]
            fill = cnt - jnp.maximum(tiles - 1, 0) * tm
            spare = tm - fill
            pos = last * tm + fill
            for piece in pieces:
                take = jnp.logical_and(tiles > 0, (spare & piece) != 0)

                @pl.when(take)
                def _():
                    fn(pltpu.make_async_copy(
                        zbuf.at[pl.ds(0, piece * GATHER_PITCH)],
                        xs_ref.at[pl.ds(pos * GATHER_PITCH, piece * GATHER_PITCH)], sem))
                pos = pos + jnp.where((spare & piece) != 0, piece, 0)

        def unowned(j, carry):
            fn(pltpu.make_async_copy(
                zbuf, xs_ref.at[pl.ds(pl.multiple_of(j * tile_rows, tile_rows), tile_rows)], sem))
            return carry
        lax.fori_loop(owned, n_tiles, unowned, 0)

    each_copy(lambda c: c.start())
    each_copy(lambda c: c.wait())


def _mix_route_tile(s, first, par, x_ref, w_in_ref, pmix_ref, pscale_ref, convw_ref, w_out_ref, g1_ref, b1_ref,
                    rwt_ref, rb_ref, tri_ref, low_ref, x1_ref, dest_ref, gate_ref, cnt_ref, tab_ref,
                    xbuf, dsm, carry_v, carry_u, cnt_sc, used_sc, cur_sc, tab_sc, dest_sem):
    ts = TS_MIX

    @pl.when(first)
    def _():
        cnt_sc[...] = jnp.zeros_like(cnt_sc)
        used_sc[...] = jnp.zeros_like(used_sc)
        cur_sc[...] = jnp.zeros_like(cur_sc)
        tab_sc[...] = jnp.zeros_like(tab_sc)

    @pl.when(s == 0)
    def _():
        carry_v[...] = jnp.zeros_like(carry_v)
        carry_u[...] = jnp.zeros_like(carry_u)

    xb = x_ref[0]
    proj = jnp.dot(xb.astype(_BF16), w_in_ref[...], preferred_element_type=_F32)
    vp = proj[:, :D_POOL]
    bg = proj[:, D_POOL:D_POOL + D_CONV]
    cg = proj[:, D_POOL + D_CONV:D_POOL + 2 * D_CONV]
    vc = proj[:, D_POOL + 2 * D_CONV:]

    ext = jnp.concatenate([carry_v[...], vp], axis=0)
    pos = lax.broadcasted_iota(jnp.int32, (ts, 1), 0) + s * ts
    mixed = []
    for g, w in enumerate(POOL_WINDOWS):
        lo, hi = g * POOL_GROUP_DIM, (g + 1) * POOL_GROUP_DIM
        acc = ext[:, lo:hi]
        sh = 1
        while sh < w:
            acc = acc + pltpu.roll(acc, sh, 0)
            sh *= 2
        cnt = jnp.minimum(pos + 1, w).astype(_F32)
        d = acc[POOL_HALO:] / cnt - vp[:, lo:hi]
        yg = jnp.dot(d.astype(_BF16), pmix_ref[g], preferred_element_type=_F32)
        mixed.append(yg * pscale_ref[:, lo:hi])
    carry_v[...] = vp[ts - POOL_HALO:]

    u = cg * vc
    extu = jnp.concatenate([carry_u[...], u], axis=0)
    u1 = pltpu.roll(extu, 1, 0)[CONV_HALO:]
    u2 = pltpu.roll(extu, 2, 0)[CONV_HALO:]
    yc = bg * (convw_ref[0:1, :] * u2 + convw_ref[1:2, :] * u1 + convw_ref[2:3, :] * u)
    carry_u[...] = u[ts - CONV_HALO:]
    mixed.append(yc)

    mix_in = jnp.concatenate(mixed, axis=1).astype(_BF16)
    mix = jnp.dot(mix_in, w_out_ref[...], preferred_element_type=_F32)
    x1 = _layer_norm(DEEPNORM_ALPHA * xb + mix, g1_ref[...], b1_ref[...])
    _to_row_tiles(xbuf.at[par], x1, GATHER_PITCH)
    x1_ref[...] = x1

    logits = lax.dot_general(rwt_ref[...], x1.astype(_BF16), (((1,), (1,)), ((), ())),
                             preferred_element_type=_F32) + rb_ref[:, 0:1]
    eidx = lax.broadcasted_iota(jnp.int32, (N_EXPERTS, ts), 0).astype(_F32)
    vals, sels = [], []
    work = logits
    for k in range(TOP_K):
        m = jnp.max(work, axis=0, keepdims=True)
        first = jnp.min(jnp.where(work == m, eidx, float(N_EXPERTS)), axis=0, keepdims=True)
        sel = eidx == first
        work = jnp.where(sel, -jnp.inf, work)
        vals.append(m)
        sels.append(sel)
    exps = [jnp.exp(v - vals[0]) for v in vals]
    denom = exps[0] + exps[1] + exps[2] + exps[3]
    for k in range(TOP_K):
        gate_ref[k:k + 1, :] = exps[k] / denom
    gate_ref[TOP_K:, :] = jnp.zeros((SUBLANES - TOP_K, ts), _F32)

    tm = float(TM_EXPERT)
    chosen = jnp.zeros((N_EXPERTS, ts), _F32)
    for sel in sels:
        chosen = chosen + sel.astype(_F32)
    incl = jnp.dot(chosen.astype(_BF16), tri_ref[...], preferred_element_type=_F32)
    cnt_b = cnt_sc[:, 0:1]
    cnt_a = cnt_b + jnp.sum(chosen, axis=1, keepdims=True)
    q_b = jnp.floor((cnt_b + (tm - 1.0)) * (1.0 / tm))
    q_a = jnp.floor((cnt_a + (tm - 1.0)) * (1.0 / tm))
    n_new = q_a - q_b
    before = jnp.dot(low_ref[...], jnp.broadcast_to(n_new, (N_EXPERTS, LANES)).astype(_BF16),
                     preferred_element_type=_F32)[:, 0:1]
    used = used_sc[:, 0:1]
    new_tile = used + before
    cur_tile = cur_sc[:, 0:1]
    rank = cnt_b + (incl - chosen)
    page = jnp.floor(rank * (1.0 / tm))
    slot = jnp.where(page < q_b, cur_tile, new_tile) * tm + (rank - page * tm)
    for k in range(TOP_K):
        dest_ref[k:k + 1, :] = jnp.sum(jnp.where(sels[k], slot, 0.0), axis=0, keepdims=True).astype(jnp.int32)
    dest_ref[TOP_K:, :] = jnp.zeros((SUBLANES - TOP_K, ts), jnp.int32)

    took = n_new > 0.0
    lane = lax.broadcasted_iota(jnp.int32, (N_EXPERTS, LANES), 1).astype(_F32)
    table = jnp.where(jnp.logical_and(lane == q_b, took), new_tile, tab_sc[...])
    tab_sc[...] = table
    cur_sc[...] = jnp.broadcast_to(jnp.where(took, new_tile, cur_tile), (N_EXPERTS, LANES))
    used_sc[...] = jnp.broadcast_to(used + jnp.sum(n_new, axis=0, keepdims=True), (N_EXPERTS, LANES))
    cnt_sc[...] = jnp.broadcast_to(cnt_a, (N_EXPERTS, LANES))
    cnt_ref[...] = jnp.broadcast_to(cnt_a, (N_EXPERTS, LANES)).astype(jnp.int32)
    tab_ref[...] = table.astype(jnp.int32)

    to_scalar = [pltpu.make_async_copy(dest_ref.at[k], dsm.at[pl.ds((par * TOP_K + k) * ts, ts)], dest_sem)
                 for k in range(TOP_K)]
    for c in to_scalar:
        c.start()
    for c in to_scalar:
        c.wait()


def _mix_route(x, w_in, pmix, pscale, convw, w_out, g1, b1, rwt, rb, tri, low, n_slots):
    bsz, seq, d = x.shape
    n = bsz * seq
    ns = seq // TS_MIX
    n_tiles = bsz * ns
    assert n // TM_EXPERT <= LANES
    full = lambda shape: pl.BlockSpec(shape, lambda g: (0,) * len(shape))
    tile = lambda g: jnp.minimum(g, n_tiles - 1)
    state = pltpu.VMEM((N_EXPERTS, LANES), _F32)
    return pl.pallas_call(
        functools.partial(_mix_route_kernel, ns),
        grid=(n_tiles + 1,),
        in_specs=[
            pl.BlockSpec((1, TS_MIX, d), lambda g: (tile(g) // ns, tile(g) % ns, 0)),
            full(w_in.shape), full(pmix.shape), full(pscale.shape), full(convw.shape),
            full(w_out.shape), full(g1.shape), full(b1.shape), full(rwt.shape), full(rb.shape),
            full(tri.shape), full(low.shape),
        ],
        out_specs=[
            pl.BlockSpec((TS_MIX, D_MODEL), lambda g: (tile(g), 0)),
            pl.BlockSpec((SUBLANES, TS_MIX), lambda g: (0, tile(g))),
            pl.BlockSpec((SUBLANES, TS_MIX), lambda g: (0, tile(g))),
            pl.BlockSpec((N_EXPERTS, LANES), lambda g: (0, 0)),
            pl.BlockSpec((N_EXPERTS, LANES), lambda g: (0, 0)),
            pl.BlockSpec(memory_space=pl.ANY),
        ],
        out_shape=[
            jax.ShapeDtypeStruct((n, D_MODEL), _F32),
            jax.ShapeDtypeStruct((SUBLANES, n), jnp.int32),
            jax.ShapeDtypeStruct((SUBLANES, n), _F32),
            jax.ShapeDtypeStruct((N_EXPERTS, LANES), jnp.int32),
            jax.ShapeDtypeStruct((N_EXPERTS, LANES), jnp.int32),
            jax.ShapeDtypeStruct((n_slots * GATHER_PITCH, LANES), _F32),
        ],
        scratch_shapes=[
            pltpu.VMEM((2, TS_MIX * GATHER_PITCH, LANES), _F32),
            pltpu.SMEM((2 * TOP_K * TS_MIX,), jnp.int32),
            pltpu.VMEM((POOL_HALO, D_POOL), _F32),
            pltpu.VMEM((CONV_HALO, D_CONV), _F32),
            state, state, state, state,
            pltpu.VMEM((2, N_EXPERTS, LANES), jnp.int32),
            pltpu.SMEM((2, N_EXPERTS, LANES), jnp.int32),
            pltpu.VMEM((TM_EXPERT * GATHER_PITCH, LANES), _F32),
            pltpu.SemaphoreType.DMA((2,)),
            pltpu.SemaphoreType.DMA(()),
        ],
        compiler_params=pltpu.CompilerParams(
            dimension_semantics=("arbitrary",),
            vmem_limit_bytes=48 * 1024 * 1024),
        name="mix_route",
    )(x, w_in, pmix, pscale, convw, w_out, g1, b1, rwt, rb, tri, low)


def _experts_kernel(be_ref, nv_ref, par_ref, nxt_ref, phys_ref, xs_ref, wgu_hbm, bgu_ref, wdn_hbm, bdn_ref, out_ref,
                    wgu_f32, wdn_f32, wgu_bf, wdn_bf, sems):
    j = pl.program_id(0)
    tm = TM_EXPERT
    valid = j < nv_ref[0]
    new_expert = jnp.logical_or(j == 0, be_ref[j] != be_ref[jnp.maximum(j - 1, 0)])

    def weight_copies(e, s):
        return (pltpu.make_async_copy(wgu_hbm.at[e], wgu_f32.at[s], sems.at[0, s]),
                pltpu.make_async_copy(wdn_hbm.at[e], wdn_f32.at[s], sems.at[1, s]))

    @pl.when(j == 0)
    def _():
        for c in weight_copies(be_ref[0], par_ref[0]):
            c.start()

    @pl.when(jnp.logical_and(valid, new_expert))
    def _():
        s = par_ref[j]
        for c in weight_copies(be_ref[j], s):
            c.wait()

        nxt = nxt_ref[be_ref[j]]

        @pl.when(nxt >= 0)
        def _():
            for c in weight_copies(nxt, 1 - s):
                c.start()

        wgu_bf[...] = wgu_f32[s].astype(_BF16)
        wdn_bf[...] = wdn_f32[s].astype(_BF16)

    @pl.when(valid)
    def _():
        x = _from_row_tiles(xs_ref, tm, GATHER_PITCH).astype(_BF16)
        gu = jnp.dot(x, wgu_bf[...], preferred_element_type=_F32) + bgu_ref[0]
        gate = jnp.minimum(gu[:, :D_FF], SWIGLU_LIMIT)
        up = jnp.clip(gu[:, D_FF:], -SWIGLU_LIMIT, SWIGLU_LIMIT)
        glu = gate * jax.nn.sigmoid(SWIGLU_ALPHA * gate)
        act = ((up + 1.0) * glu).astype(_BF16)
        y = jnp.dot(act, wdn_bf[...], preferred_element_type=_F32) + bdn_ref[0]
        _to_row_tiles(out_ref, y, GATHER_PITCH)

    @pl.when(j >= nv_ref[0])
    def _():
        out_ref[...] = jnp.zeros_like(out_ref)


def _experts(blk_e, n_valid, slot_par, next_e, phys, xs, wgu, bgu, wdn, bdn):
    n_slots = xs.shape[0] // GATHER_PITCH
    nb = n_slots // TM_EXPERT
    rows = TM_EXPERT * GATHER_PITCH
    grid_spec = pltpu.PrefetchScalarGridSpec(
        num_scalar_prefetch=5,
        grid=(nb,),
        in_specs=[
            pl.BlockSpec((rows, LANES), lambda j, be, nv, par, nxt, ph: (ph[jnp.minimum(j, nv[0] - 1)], 0)),
            pl.BlockSpec(memory_space=pl.ANY),
            pl.BlockSpec((1, 1, 2 * D_FF), lambda j, be, nv, par, nxt, ph: (be[j], 0, 0)),
            pl.BlockSpec(memory_space=pl.ANY),
            pl.BlockSpec((1, 1, D_MODEL), lambda j, be, nv, par, nxt, ph: (be[j], 0, 0)),
        ],
        out_specs=pl.BlockSpec((rows, LANES), lambda j, be, nv, par, nxt, ph: (ph[j], 0)),
        scratch_shapes=[
            pltpu.VMEM((2, D_MODEL, 2 * D_FF), _F32),
            pltpu.VMEM((2, D_FF, D_MODEL), _F32),
            pltpu.VMEM((D_MODEL, 2 * D_FF), _BF16),
            pltpu.VMEM((D_FF, D_MODEL), _BF16),
            pltpu.SemaphoreType.DMA((2, 2)),
        ],
    )
    return pl.pallas_call(
        _experts_kernel,
        grid_spec=grid_spec,
        out_shape=jax.ShapeDtypeStruct((n_slots * GATHER_PITCH, LANES), _F32),
        compiler_params=pltpu.CompilerParams(
            dimension_semantics=("arbitrary",),
            vmem_limit_bytes=60 * 1024 * 1024),
        name="experts",
    )(blk_e, n_valid, slot_par, next_e, phys, xs, wgu, bgu, wdn, bdn)


def _combine_kernel(dcur_ref, dnext_ref, yb_ref, x1_ref, gt_ref, p_ref, g2_ref, b2_ref, pw_ref, gw_ref,
                    gb_ref, g3_ref, b3_ref, out_ref, gbuf, hbuf, sems):
    ts = TS_COMBINE
    grp = COMBINE_GROUP
    i = pl.program_id(0)
    slot = i % 2
    nxt = 1 - slot

    def start_rows(d_ref, s, t0):
        for j in range(grp):
            t = t0 + j
            for k in range(TOP_K):
                pltpu.make_async_copy(_row_tile(yb_ref, d_ref[t * TOP_K + k], GATHER_PITCH),
                                      gbuf.at[s, k, pl.ds(t * GATHER_PITCH, ROW_CHUNKS)],
                                      sems.at[s]).start(priority=k % 2)

    def wait_tile(s):
        for k in range(TOP_K):
            pltpu.make_async_copy(yb_ref.at[pl.ds(0, ts * ROW_CHUNKS)],
                                  gbuf.at[s, k, pl.ds(0, ts * ROW_CHUNKS)], sems.at[s]).wait()

    @pl.when(i == 0)
    def _():
        def first(r, carry):
            start_rows(dcur_ref, 0, r * grp)
            return carry
        lax.fori_loop(0, ts // grp, first, 0)

    wait_tile(slot)

    def group(r, carry):
        t0 = pl.multiple_of(r * grp, grp)
        x1g = x1_ref[pl.ds(t0, grp), :]
        gts = gt_ref[pl.ds(t0, grp), :]
        ys = []
        for k in range(TOP_K):
            rows_k = gbuf.at[slot, k]
            ys.append(jnp.concatenate(
                [rows_k[pl.ds(t0 * GATHER_PITCH + c, grp, stride=GATHER_PITCH), :] for c in range(ROW_CHUNKS)],
                axis=1))
        start_rows(dnext_ref, nxt, t0)
        ffn = gts[:, 0:1] * ys[0]
        for k in range(1, TOP_K):
            ffn = ffn + gts[:, k:k + 1] * ys[k]
        hbuf[pl.ds(t0, grp), :] = DEEPNORM_ALPHA * x1g + ffn
        return carry

    lax.fori_loop(0, ts // grp, group, 0)

    @pl.when(i == pl.num_programs(0) - 1)
    def _():
        wait_tile(nxt)

    x2 = _layer_norm(hbuf[...], g2_ref[...], b2_ref[...])
    z = jnp.dot(x2.astype(_BF16), gw_ref[...], preferred_element_type=_F32) + gb_ref[...]
    ple = jnp.dot(p_ref[...].astype(_BF16), pw_ref[...], preferred_element_type=_F32)
    x3 = _layer_norm(DEEPNORM_ALPHA * x2 + jax.nn.sigmoid(z) * ple, g3_ref[...], b3_ref[...])
    out_ref[...] = x3


def _combine(dest_flat, yb, x1, gates_t, p2d, g2, b2, pw, gw, gb, g3, b3):
    n = dest_flat.shape[0] // TOP_K
    ts = TS_COMBINE
    last = n // ts - 1
    full = lambda shape: pl.BlockSpec(shape, lambda i: (0,) * len(shape))
    return pl.pallas_call(
        _combine_kernel,
        grid=(n // ts,),
        in_specs=[
            pl.BlockSpec((TOP_K * ts,), lambda i: (i,), memory_space=pltpu.SMEM),
            pl.BlockSpec((TOP_K * ts,), lambda i: (jnp.minimum(i + 1, last),), memory_space=pltpu.SMEM),
            pl.BlockSpec(memory_space=pl.ANY),
            pl.BlockSpec((ts, D_MODEL), lambda i: (i, 0)),
            pl.BlockSpec((ts, TOP_K), lambda i: (i, 0)),
            pl.BlockSpec((ts, PLE_DIM), lambda i: (i, 0)),
            full(g2.shape), full(b2.shape), full(pw.shape), full(gw.shape), full(gb.shape),
            full(g3.shape), full(b3.shape),
        ],
        out_specs=pl.BlockSpec((ts, D_MODEL), lambda i: (i, 0)),
        out_shape=jax.ShapeDtypeStruct((n, D_MODEL), _F32),
        scratch_shapes=[
            pltpu.VMEM((2, TOP_K, ts * GATHER_PITCH, LANES), _F32),
            pltpu.VMEM((ts, D_MODEL), _F32),
            pltpu.SemaphoreType.DMA((2,)),
        ],
        compiler_params=pltpu.CompilerParams(
            dimension_semantics=("arbitrary",),
            vmem_limit_bytes=48 * 1024 * 1024),
        name="combine",
    )(dest_flat, dest_flat, yb, x1, gates_t, p2d, g2, b2, pw, gw, gb, g3, b3)


def kernel(x, p, w_in, pool_mix, pool_scale, conv_w, w_out, ln1_g, ln1_b, router_w, router_b,
           w_gate_up, b_gate_up, w_down, b_down, ln2_g, ln2_b, ple_proj, ple_gate_w, ple_gate_b,
           ln3_g, ln3_b):
    assert DEPTH == 1 and x.shape[-1] == D_MODEL
    bsz, seq, d = x.shape
    n = bsz * seq
    assert seq % TS_MIX == 0 and n % TS_COMBINE == 0
    row = lambda v: v.reshape(1, -1)

    tri = jnp.triu(jnp.ones((TS_MIX, TS_MIX), _BF16))
    low = jnp.tril(jnp.ones((N_EXPERTS, N_EXPERTS), _BF16), k=-1)
    tm = TM_EXPERT
    n_slots = n * TOP_K + N_EXPERTS * tm
    nb = n_slots // tm
    x1, dest, gates, counts, table, xs = _mix_route(
        x, w_in[0].astype(_BF16), pool_mix[0].astype(_BF16), row(pool_scale[0]), conv_w[0],
        w_out[0].astype(_BF16), row(ln1_g[0]), row(ln1_b[0]),
        router_w[0].T.astype(_BF16), jnp.broadcast_to(router_b[0][:, None], (N_EXPERTS, LANES)), tri, low,
        n_slots)

    cnt = counts[:, 0]
    tiles_e = (cnt + tm - 1) // tm
    tile_end = jnp.cumsum(tiles_e)
    n_valid = tile_end[-1].astype(jnp.int32)
    steps = jnp.arange(nb, dtype=jnp.int32)
    blk_e = jnp.minimum(jnp.sum(steps[:, None] >= tile_end[None, :], axis=1), N_EXPERTS - 1)
    blk_e = jnp.where(steps < n_valid, blk_e, blk_e[n_valid - 1]).astype(jnp.int32)
    within = steps - (tile_end - tiles_e)[blk_e]
    phys = jnp.where(steps < n_valid, table[blk_e, jnp.clip(within, 0, LANES - 1)], steps).astype(jnp.int32)

    padded = tiles_e
    e_ids = jnp.arange(N_EXPERTS, dtype=jnp.int32)
    later = jnp.logical_and(e_ids[None, :] > e_ids[:, None], (padded > 0)[None, :])
    next_nonempty = jnp.min(jnp.where(later, e_ids[None, :], N_EXPERTS), axis=1)
    next_e = jnp.where(next_nonempty == N_EXPERTS, -1, next_nonempty).astype(jnp.int32)
    switched = jnp.concatenate([jnp.zeros((1,), jnp.int32), (blk_e[1:] != blk_e[:-1]).astype(jnp.int32)])
    slot_par = (jnp.cumsum(switched) % 2).astype(jnp.int32)
    n_valid = n_valid.reshape(1)

    dest = dest[:TOP_K].T.reshape(-1)
    yb = _experts(blk_e, n_valid, slot_par, next_e, phys, xs,
                  w_gate_up[0], b_gate_up[0][:, None, :], w_down[0], b_down[0][:, None, :])
    out = _combine(dest, yb, x1, gates[:TOP_K].T, p[0].reshape(n, PLE_DIM),
                   row(ln2_g[0]), row(ln2_b[0]), ple_proj[0].astype(_BF16),
                   ple_gate_w[0].astype(_BF16), row(ple_gate_b[0]), row(ln3_g[0]), row(ln3_b[0]))
    return out.reshape(bsz, seq, d)
```

```python
import functools

import jax
import jax.numpy as jnp
from jax import lax
from jax.experimental import pallas as pl
from jax.experimental.pallas import tpu as pltpu

D_MODEL = 1024
D_POOL = 512
D_CONV = 512
POOL_WINDOWS = (2, 4, 8, 16)
POOL_GROUP_DIM = 128
CONV_WIDTH = 3
D_IN_PROJ = D_POOL + 3 * D_CONV
N_EXPERTS = 32
TOP_K = 4
D_FF = 1024
SWIGLU_LIMIT = 7.0
SWIGLU_ALPHA = 1.702
PLE_DIM = 256
DEPTH = 1
DEEPNORM_ALPHA = (2.0 * DEPTH) ** 0.25
LN_EPS = 1e-5

LANES = 128
SUBLANES = 8
ROW_CHUNKS = D_MODEL // LANES
POOL_HALO = 16
CONV_HALO = 8
GATHER_PITCH = ROW_CHUNKS + 1

TS_MIX = 512
TM_EXPERT = 512
TS_COMBINE = 256
COMBINE_GROUP = 8

_BF16 = jnp.bfloat16
_F32 = jnp.float32


def _layer_norm(h, g, b):
    mu = jnp.mean(h, axis=-1, keepdims=True)
    c = h - mu
    var = jnp.mean(c * c, axis=-1, keepdims=True)
    return c * lax.rsqrt(var + LN_EPS) * g + b


def _to_row_tiles(ref, val, pitch=ROW_CHUNKS):
    rows = val.shape[0]
    for c in range(ROW_CHUNKS):
        ref[pl.ds(c, rows, stride=pitch), :] = val[:, c * LANES:(c + 1) * LANES]
    for c in range(ROW_CHUNKS, pitch):
        ref[pl.ds(c, rows, stride=pitch), :] = jnp.zeros((rows, LANES), val.dtype)


def _from_row_tiles(ref, rows, pitch=ROW_CHUNKS):
    return jnp.concatenate(
        [ref[pl.ds(c, rows, stride=pitch), :] for c in range(ROW_CHUNKS)], axis=1)


def _row_tile(ref, r, pitch=ROW_CHUNKS):
    if pitch == ROW_CHUNKS:
        return ref.at[pl.ds(pl.multiple_of(r * ROW_CHUNKS, ROW_CHUNKS), ROW_CHUNKS)]
    return ref.at[pl.ds(r * pitch, ROW_CHUNKS)]


def _mix_route_kernel(ns, x_ref, w_in_ref, pmix_ref, pscale_ref, convw_ref, w_out_ref, g1_ref, b1_ref,
                      rwt_ref, rb_ref, tri_ref, low_ref,
                      x1_ref, dest_ref, gate_ref, cnt_ref, tab_ref, xs_ref,
                      xbuf, dsm, carry_v, carry_u, cnt_sc, used_sc, cur_sc, tab_sc, meta_v, meta_s, zbuf,
                      row_sems, dest_sem):
    g = pl.program_id(0)
    n_tiles = pl.num_programs(0) - 1
    ts = TS_MIX
    par = lax.rem(g, 2)
    tile_rows = ts * GATHER_PITCH

    def wait_rows(q):
        for _ in range(TOP_K):
            pltpu.make_async_copy(xbuf.at[q], xs_ref.at[pl.ds(0, tile_rows)], row_sems.at[q]).wait()

    @pl.when(g >= 2)
    def _():
        wait_rows(par)

    @pl.when(g >= 1)
    def _():
        q = 1 - par
        base = q * (TOP_K * ts)
        for c in _slots_to_scalar(dest_ref, dsm, q, dest_sem):
            c.wait()

        def body(t, carry):
            for k in range(TOP_K):
                slot = dsm[base + k * ts + t]
                pltpu.make_async_copy(xbuf.at[q, pl.ds(t * GATHER_PITCH, GATHER_PITCH)],
                                      xs_ref.at[pl.ds(slot * GATHER_PITCH, GATHER_PITCH)],
                                      row_sems.at[q]).start(priority=k % 2)
            return carry
        lax.fori_loop(0, ts, body, 0, unroll=8)

    @pl.when(g < n_tiles)
    def _():
        _mix_route_tile(lax.rem(g, ns), g == 0, par, x_ref, w_in_ref, pmix_ref, pscale_ref, convw_ref, w_out_ref,
                        g1_ref, b1_ref, rwt_ref, rb_ref, tri_ref, low_ref, x1_ref, dest_ref, gate_ref, cnt_ref,
                        tab_ref, xbuf, dsm, carry_v, carry_u, cnt_sc, used_sc, cur_sc, tab_sc, dest_sem)

    @pl.when(g == n_tiles)
    def _():
        wait_rows(1 - par)
        _seal_slots(xs_ref, cnt_sc, tab_sc, meta_v, meta_s, zbuf, dest_sem)


def _slots_to_scalar(dest_ref, dsm, q, sem):
    ts = TS_MIX
    return [pltpu.make_async_copy(dest_ref.at[k], dsm.at[pl.ds((q * TOP_K + k) * ts, ts)], sem)
            for k in range(TOP_K)]


def _seal_slots(xs_ref, cnt_sc, tab_sc, meta_v, meta_s, zbuf, sem):
    tm = TM_EXPERT
    tile_rows = tm * GATHER_PITCH
    n_tiles = xs_ref.shape[0] // tile_rows
    meta_v[0] = cnt_sc[...].astype(jnp.int32)
    meta_v[1] = tab_sc[...].astype(jnp.int32)
    to_scalar = pltpu.make_async_copy(meta_v, meta_s, sem)
    to_scalar.start()
    to_scalar.wait()
    zbuf[...] = jnp.zeros_like(zbuf)
    pieces = [1 << b for b in reversed(range(tm.bit_length() - 1))]

    def each_copy(fn):
        owned = 0
        for e in range(N_EXPERTS):
            cnt = meta_s[0, e, 0]
            tiles = lax.shift_right_logical(cnt + (tm - 1), tm.bit_length() - 1)
            owned = owned + tiles
            last = meta_s[1, e, jnp.maximum(tiles - 1, 0)]
            fill = cnt - jnp.maximum(tiles - 1, 0) * tm
            spare = tm - fill
            pos = last * tm + fill
            for piece in pieces:
                take = jnp.logical_and(tiles > 0, (spare & piece) != 0)

                @pl.when(take)
                def _():
                    fn(pltpu.make_async_copy(
                        zbuf.at[pl.ds(0, piece * GATHER_PITCH)],
                        xs_ref.at[pl.ds(pos * GATHER_PITCH, piece * GATHER_PITCH)], sem))
                pos = pos + jnp.where((spare & piece) != 0, piece, 0)

        def unowned(j, carry):
            fn(pltpu.make_async_copy(
                zbuf, xs_ref.at[pl.ds(pl.multiple_of(j * tile_rows, tile_rows), tile_rows)], sem))
            return carry
        lax.fori_loop(owned, n_tiles, unowned, 0)

    each_copy(lambda c: c.start())
    each_copy(lambda c: c.wait())


def _mix_route_tile(s, first, par, x_ref, w_in_ref, pmix_ref, pscale_ref, convw_ref, w_out_ref, g1_ref, b1_ref,
                    rwt_ref, rb_ref, tri_ref, low_ref, x1_ref, dest_ref, gate_ref, cnt_ref, tab_ref,
                    xbuf, dsm, carry_v, carry_u, cnt_sc, used_sc, cur_sc, tab_sc, dest_sem):
    ts = TS_MIX

    @pl.when(first)
    def _():
        cnt_sc[...] = jnp.zeros_like(cnt_sc)
        used_sc[...] = jnp.zeros_like(used_sc)
        cur_sc[...] = jnp.zeros_like(cur_sc)
        tab_sc[...] = jnp.zeros_like(tab_sc)

    @pl.when(s == 0)
    def _():
        carry_v[...] = jnp.zeros_like(carry_v)
        carry_u[...] = jnp.zeros_like(carry_u)

    xb = x_ref[0]
    proj = jnp.dot(xb.astype(_BF16), w_in_ref[...], preferred_element_type=_F32)
    vp = proj[:, :D_POOL]
    bg = proj[:, D_POOL:D_POOL + D_CONV]
    cg = proj[:, D_POOL + D_CONV:D_POOL + 2 * D_CONV]
    vc = proj[:, D_POOL + 2 * D_CONV:]

    ext = jnp.concatenate([carry_v[...], vp], axis=0)
    pos = lax.broadcasted_iota(jnp.int32, (ts, 1), 0) + s * ts
    mixed = []
    for g, w in enumerate(POOL_WINDOWS):
        lo, hi = g * POOL_GROUP_DIM, (g + 1) * POOL_GROUP_DIM
        acc = ext[:, lo:hi]
        sh = 1
        while sh < w:
            acc = acc + pltpu.roll(acc, sh, 0)
            sh *= 2
        cnt = jnp.minimum(pos + 1, w).astype(_F32)
        d = acc[POOL_HALO:] / cnt - vp[:, lo:hi]
        yg = jnp.dot(d.astype(_BF16), pmix_ref[g], preferred_element_type=_F32)
        mixed.append(yg * pscale_ref[:, lo:hi])
    carry_v[...] = vp[ts - POOL_HALO:]

    u = cg * vc
    extu = jnp.concatenate([carry_u[...], u], axis=0)
    u1 = pltpu.roll(extu, 1, 0)[CONV_HALO:]
    u2 = pltpu.roll(extu, 2, 0)[CONV_HALO:]
    yc = bg * (convw_ref[0:1, :] * u2 + convw_ref[1:2, :] * u1 + convw_ref[2:3, :] * u)
    carry_u[...] = u[ts - CONV_HALO:]
    mixed.append(yc)

    mix_in = jnp.concatenate(mixed, axis=1).astype(_BF16)
    mix = jnp.dot(mix_in, w_out_ref[...], preferred_element_type=_F32)
    x1 = _layer_norm(DEEPNORM_ALPHA * xb + mix, g1_ref[...], b1_ref[...])
    _to_row_tiles(xbuf.at[par], x1, GATHER_PITCH)
    x1_ref[...] = x1

    logits = lax.dot_general(rwt_ref[...], x1.astype(_BF16), (((1,), (1,)), ((), ())),
                             preferred_element_type=_F32) + rb_ref[:, 0:1]
    eidx = lax.broadcasted_iota(jnp.int32, (N_EXPERTS, ts), 0).astype(_F32)
    vals, sels = [], []
    work = logits
    for k in range(TOP_K):
        m = jnp.max(work, axis=0, keepdims=True)
        first = jnp.min(jnp.where(work == m, eidx, float(N_EXPERTS)), axis=0, keepdims=True)
        sel = eidx == first
        work = jnp.where(sel, -jnp.inf, work)
        vals.append(m)
        sels.append(sel)
    exps = [jnp.exp(v - vals[0]) for v in vals]
    denom = exps[0] + exps[1] + exps[2] + exps[3]
    for k in range(TOP_K):
        gate_ref[k:k + 1, :] = exps[k] / denom
    gate_ref[TOP_K:, :] = jnp.zeros((SUBLANES - TOP_K, ts), _F32)

    tm = float(TM_EXPERT)
    chosen = jnp.zeros((N_EXPERTS, ts), _F32)
    for sel in sels:
        chosen = chosen + sel.astype(_F32)
    incl = jnp.dot(chosen.astype(_BF16), tri_ref[...], preferred_element_type=_F32)
    cnt_b = cnt_sc[:, 0:1]
    cnt_a = cnt_b + jnp.sum(chosen, axis=1, keepdims=True)
    q_b = jnp.floor((cnt_b + (tm - 1.0)) * (1.0 / tm))
    q_a = jnp.floor((cnt_a + (tm - 1.0)) * (1.0 / tm))
    n_new = q_a - q_b
    before = jnp.dot(low_ref[...], jnp.broadcast_to(n_new, (N_EXPERTS, LANES)).astype(_BF16),
                     preferred_element_type=_F32)[:, 0:1]
    used = used_sc[:, 0:1]
    new_tile = used + before
    cur_tile = cur_sc[:, 0:1]
    rank = cnt_b + (incl - chosen)
    page = jnp.floor(rank * (1.0 / tm))
    slot = jnp.where(page < q_b, cur_tile, new_tile) * tm + (rank - page * tm)
    for k in range(TOP_K):
        dest_ref[k:k + 1, :] = jnp.sum(jnp.where(sels[k], slot, 0.0), axis=0, keepdims=True).astype(jnp.int32)
    dest_ref[TOP_K:, :] = jnp.zeros((SUBLANES - TOP_K, ts), jnp.int32)

    took = n_new > 0.0
    lane = lax.broadcasted_iota(jnp.int32, (N_EXPERTS, LANES), 1).astype(_F32)
    table = jnp.where(jnp.logical_and(lane == q_b, took), new_tile, tab_sc[...])
    tab_sc[...] = table
    cur_sc[...] = jnp.broadcast_to(jnp.where(took, new_tile, cur_tile), (N_EXPERTS, LANES))
    used_sc[...] = jnp.broadcast_to(used + jnp.sum(n_new, axis=0, keepdims=True), (N_EXPERTS, LANES))
    cnt_sc[...] = jnp.broadcast_to(cnt_a, (N_EXPERTS, LANES))
    cnt_ref[...] = jnp.broadcast_to(cnt_a, (N_EXPERTS, LANES)).astype(jnp.int32)
    tab_ref[...] = table.astype(jnp.int32)

    for c in _slots_to_scalar(dest_ref, dsm, par, dest_sem):
        c.start()


def _mix_route(x, w_in, pmix, pscale, convw, w_out, g1, b1, rwt, rb, tri, low, n_slots):
    bsz, seq, d = x.shape
    n = bsz * seq
    ns = seq // TS_MIX
    n_tiles = bsz * ns
    assert n // TM_EXPERT <= LANES
    full = lambda shape: pl.BlockSpec(shape, lambda g: (0,) * len(shape))
    tile = lambda g: jnp.minimum(g, n_tiles - 1)
    state = pltpu.VMEM((N_EXPERTS, LANES), _F32)
    return pl.pallas_call(
        functools.partial(_mix_route_kernel, ns),
        grid=(n_tiles + 1,),
        in_specs=[
            pl.BlockSpec((1, TS_MIX, d), lambda g: (tile(g) // ns, tile(g) % ns, 0)),
            full(w_in.shape), full(pmix.shape), full(pscale.shape), full(convw.shape),
            full(w_out.shape), full(g1.shape), full(b1.shape), full(rwt.shape), full(rb.shape),
            full(tri.shape), full(low.shape),
        ],
        out_specs=[
            pl.BlockSpec((TS_MIX, D_MODEL), lambda g: (tile(g), 0)),
            pl.BlockSpec((SUBLANES, TS_MIX), lambda g: (0, tile(g))),
            pl.BlockSpec((SUBLANES, TS_MIX), lambda g: (0, tile(g))),
            pl.BlockSpec((N_EXPERTS, LANES), lambda g: (0, 0)),
            pl.BlockSpec((N_EXPERTS, LANES), lambda g: (0, 0)),
            pl.BlockSpec(memory_space=pl.ANY),
        ],
        out_shape=[
            jax.ShapeDtypeStruct((n, D_MODEL), _F32),
            jax.ShapeDtypeStruct((SUBLANES, n), jnp.int32),
            jax.ShapeDtypeStruct((SUBLANES, n), _F32),
            jax.ShapeDtypeStruct((N_EXPERTS, LANES), jnp.int32),
            jax.ShapeDtypeStruct((N_EXPERTS, LANES), jnp.int32),
            jax.ShapeDtypeStruct((n_slots * GATHER_PITCH, LANES), _F32),
        ],
        scratch_shapes=[
            pltpu.VMEM((2, TS_MIX * GATHER_PITCH, LANES), _F32),
            pltpu.SMEM((2 * TOP_K * TS_MIX,), jnp.int32),
            pltpu.VMEM((POOL_HALO, D_POOL), _F32),
            pltpu.VMEM((CONV_HALO, D_CONV), _F32),
            state, state, state, state,
            pltpu.VMEM((2, N_EXPERTS, LANES), jnp.int32),
            pltpu.SMEM((2, N_EXPERTS, LANES), jnp.int32),
            pltpu.VMEM((TM_EXPERT * GATHER_PITCH, LANES), _F32),
            pltpu.SemaphoreType.DMA((2,)),
            pltpu.SemaphoreType.DMA(()),
        ],
        compiler_params=pltpu.CompilerParams(
            dimension_semantics=("arbitrary",),
            vmem_limit_bytes=48 * 1024 * 1024),
        name="mix_route",
    )(x, w_in, pmix, pscale, convw, w_out, g1, b1, rwt, rb, tri, low)


def _experts_kernel(be_ref, nv_ref, par_ref, nxt_ref, phys_ref, xs_ref, wgu_hbm, bgu_ref, wdn_hbm, bdn_ref, out_ref,
                    wgu_f32, wdn_f32, wgu_bf, wdn_bf, sems):
    j = pl.program_id(0)
    tm = TM_EXPERT
    valid = j < nv_ref[0]
    new_expert = jnp.logical_or(j == 0, be_ref[j] != be_ref[jnp.maximum(j - 1, 0)])

    def weight_copies(e, s):
        return (pltpu.make_async_copy(wgu_hbm.at[e], wgu_f32.at[s], sems.at[0, s]),
                pltpu.make_async_copy(wdn_hbm.at[e], wdn_f32.at[s], sems.at[1, s]))

    @pl.when(j == 0)
    def _():
        for c in weight_copies(be_ref[0], par_ref[0]):
            c.start()

    @pl.when(jnp.logical_and(valid, new_expert))
    def _():
        s = par_ref[j]
        for c in weight_copies(be_ref[j], s):
            c.wait()

        nxt = nxt_ref[be_ref[j]]

        @pl.when(nxt >= 0)
        def _():
            for c in weight_copies(nxt, 1 - s):
                c.start()

        wgu_bf[...] = wgu_f32[s].astype(_BF16)
        wdn_bf[...] = wdn_f32[s].astype(_BF16)

    @pl.when(valid)
    def _():
        x = _from_row_tiles(xs_ref, tm, GATHER_PITCH).astype(_BF16)
        gu = jnp.dot(x, wgu_bf[...], preferred_element_type=_F32) + bgu_ref[0]
        gate = jnp.minimum(gu[:, :D_FF], SWIGLU_LIMIT)
        up = jnp.clip(gu[:, D_FF:], -SWIGLU_LIMIT, SWIGLU_LIMIT)
        glu = gate * jax.nn.sigmoid(SWIGLU_ALPHA * gate)
        act = ((up + 1.0) * glu).astype(_BF16)
        y = jnp.dot(act, wdn_bf[...], preferred_element_type=_F32) + bdn_ref[0]
        _to_row_tiles(out_ref, y, GATHER_PITCH)

    @pl.when(j >= nv_ref[0])
    def _():
        out_ref[...] = jnp.zeros_like(out_ref)


def _experts(blk_e, n_valid, slot_par, next_e, phys, xs, wgu, bgu, wdn, bdn):
    n_slots = xs.shape[0] // GATHER_PITCH
    nb = n_slots // TM_EXPERT
    rows = TM_EXPERT * GATHER_PITCH
    grid_spec = pltpu.PrefetchScalarGridSpec(
        num_scalar_prefetch=5,
        grid=(nb,),
        in_specs=[
            pl.BlockSpec((rows, LANES), lambda j, be, nv, par, nxt, ph: (ph[jnp.minimum(j, nv[0] - 1)], 0)),
            pl.BlockSpec(memory_space=pl.ANY),
            pl.BlockSpec((1, 1, 2 * D_FF), lambda j, be, nv, par, nxt, ph: (be[j], 0, 0)),
            pl.BlockSpec(memory_space=pl.ANY),
            pl.BlockSpec((1, 1, D_MODEL), lambda j, be, nv, par, nxt, ph: (be[j], 0, 0)),
        ],
        out_specs=pl.BlockSpec((rows, LANES), lambda j, be, nv, par, nxt, ph: (ph[j], 0)),
        scratch_shapes=[
            pltpu.VMEM((2, D_MODEL, 2 * D_FF), _F32),
            pltpu.VMEM((2, D_FF, D_MODEL), _F32),
            pltpu.VMEM((D_MODEL, 2 * D_FF), _BF16),
            pltpu.VMEM((D_FF, D_MODEL), _BF16),
            pltpu.SemaphoreType.DMA((2, 2)),
        ],
    )
    return pl.pallas_call(
        _experts_kernel,
        grid_spec=grid_spec,
        out_shape=jax.ShapeDtypeStruct((n_slots * GATHER_PITCH, LANES), _F32),
        compiler_params=pltpu.CompilerParams(
            dimension_semantics=("arbitrary",),
            vmem_limit_bytes=60 * 1024 * 1024),
        name="experts",
    )(blk_e, n_valid, slot_par, next_e, phys, xs, wgu, bgu, wdn, bdn)


def _combine_kernel(dcur_ref, dnext_ref, yb_ref, x1_ref, gt_ref, p_ref, g2_ref, b2_ref, pw_ref, gw_ref,
                    gb_ref, g3_ref, b3_ref, out_ref, gbuf, hbuf, sems):
    ts = TS_COMBINE
    grp = COMBINE_GROUP
    i = pl.program_id(0)
    slot = i % 2
    nxt = 1 - slot

    def start_rows(d_ref, s, t0):
        for j in range(grp):
            t = t0 + j
            for k in range(TOP_K):
                pltpu.make_async_copy(_row_tile(yb_ref, d_ref[t * TOP_K + k], GATHER_PITCH),
                                      gbuf.at[s, k, pl.ds(t * GATHER_PITCH, ROW_CHUNKS)],
                                      sems.at[s]).start(priority=k % 2)

    def wait_tile(s):
        for k in range(TOP_K):
            pltpu.make_async_copy(yb_ref.at[pl.ds(0, ts * ROW_CHUNKS)],
                                  gbuf.at[s, k, pl.ds(0, ts * ROW_CHUNKS)], sems.at[s]).wait()

    @pl.when(i == 0)
    def _():
        def first(r, carry):
            start_rows(dcur_ref, 0, r * grp)
            return carry
        lax.fori_loop(0, ts // grp, first, 0)

    wait_tile(slot)

    def group(r, carry):
        t0 = pl.multiple_of(r * grp, grp)
        x1g = x1_ref[pl.ds(t0, grp), :]
        gts = gt_ref[pl.ds(t0, grp), :]
        ys = []
        for k in range(TOP_K):
            rows_k = gbuf.at[slot, k]
            ys.append(jnp.concatenate(
                [rows_k[pl.ds(t0 * GATHER_PITCH + c, grp, stride=GATHER_PITCH), :] for c in range(ROW_CHUNKS)],
                axis=1))
        start_rows(dnext_ref, nxt, t0)
        ffn = gts[:, 0:1] * ys[0]
        for k in range(1, TOP_K):
            ffn = ffn + gts[:, k:k + 1] * ys[k]
        hbuf[pl.ds(t0, grp), :] = DEEPNORM_ALPHA * x1g + ffn
        return carry

    lax.fori_loop(0, ts // grp, group, 0)

    @pl.when(i == pl.num_programs(0) - 1)
    def _():
        wait_tile(nxt)

    x2 = _layer_norm(hbuf[...], g2_ref[...], b2_ref[...])
    z = jnp.dot(x2.astype(_BF16), gw_ref[...], preferred_element_type=_F32) + gb_ref[...]
    ple = jnp.dot(p_ref[...].astype(_BF16), pw_ref[...], preferred_element_type=_F32)
    x3 = _layer_norm(DEEPNORM_ALPHA * x2 + jax.nn.sigmoid(z) * ple, g3_ref[...], b3_ref[...])
    out_ref[...] = x3


def _combine(dest_flat, yb, x1, gates_t, p2d, g2, b2, pw, gw, gb, g3, b3):
    n = dest_flat.shape[0] // TOP_K
    ts = TS_COMBINE
    last = n // ts - 1
    full = lambda shape: pl.BlockSpec(shape, lambda i: (0,) * len(shape))
    return pl.pallas_call(
        _combine_kernel,
        grid=(n // ts,),
        in_specs=[
            pl.BlockSpec((TOP_K * ts,), lambda i: (i,), memory_space=pltpu.SMEM),
            pl.BlockSpec((TOP_K * ts,), lambda i: (jnp.minimum(i + 1, last),), memory_space=pltpu.SMEM),
            pl.BlockSpec(memory_space=pl.ANY),
            pl.BlockSpec((ts, D_MODEL), lambda i: (i, 0)),
            pl.BlockSpec((ts, TOP_K), lambda i: (i, 0)),
            pl.BlockSpec((ts, PLE_DIM), lambda i: (i, 0)),
            full(g2.shape), full(b2.shape), full(pw.shape), full(gw.shape), full(gb.shape),
            full(g3.shape), full(b3.shape),
        ],
        out_specs=pl.BlockSpec((ts, D_MODEL), lambda i: (i, 0)),
        out_shape=jax.ShapeDtypeStruct((n, D_MODEL), _F32),
        scratch_shapes=[
            pltpu.VMEM((2, TOP_K, ts * GATHER_PITCH, LANES), _F32),
            pltpu.VMEM((ts, D_MODEL), _F32),
            pltpu.SemaphoreType.DMA((2,)),
        ],
        compiler_params=pltpu.CompilerParams(
            dimension_semantics=("arbitrary",),
            vmem_limit_bytes=48 * 1024 * 1024),
        name="combine",
    )(dest_flat, dest_flat, yb, x1, gates_t, p2d, g2, b2, pw, gw, gb, g3, b3)


def kernel(x, p, w_in, pool_mix, pool_scale, conv_w, w_out, ln1_g, ln1_b, router_w, router_b,
           w_gate_up, b_gate_up, w_down, b_down, ln2_g, ln2_b, ple_proj, ple_gate_w, ple_gate_b,
           ln3_g, ln3_b):
    assert DEPTH == 1 and x.shape[-1] == D_MODEL
    bsz, seq, d = x.shape
    n = bsz * seq
    assert seq % TS_MIX == 0 and n % TS_COMBINE == 0
    row = lambda v: v.reshape(1, -1)

    tri = jnp.triu(jnp.ones((TS_MIX, TS_MIX), _BF16))
    low = jnp.tril(jnp.ones((N_EXPERTS, N_EXPERTS), _BF16), k=-1)
    tm = TM_EXPERT
    n_slots = n * TOP_K + N_EXPERTS * tm
    nb = n_slots // tm
    x1, dest, gates, counts, table, xs = _mix_route(
        x, w_in[0].astype(_BF16), pool_mix[0].astype(_BF16), row(pool_scale[0]), conv_w[0],
        w_out[0].astype(_BF16), row(ln1_g[0]), row(ln1_b[0]),
        router_w[0].T.astype(_BF16), jnp.broadcast_to(router_b[0][:, None], (N_EXPERTS, LANES)), tri, low,
        n_slots)

    cnt = counts[:, 0]
    tiles_e = (cnt + tm - 1) // tm
    tile_end = jnp.cumsum(tiles_e)
    n_valid = tile_end[-1].astype(jnp.int32)
    steps = jnp.arange(nb, dtype=jnp.int32)
    blk_e = jnp.minimum(jnp.sum(steps[:, None] >= tile_end[None, :], axis=1), N_EXPERTS - 1)
    blk_e = jnp.where(steps < n_valid, blk_e, blk_e[n_valid - 1]).astype(jnp.int32)
    within = steps - (tile_end - tiles_e)[blk_e]
    phys = jnp.where(steps < n_valid, table[blk_e, jnp.clip(within, 0, LANES - 1)], steps).astype(jnp.int32)

    padded = tiles_e
    e_ids = jnp.arange(N_EXPERTS, dtype=jnp.int32)
    later = jnp.logical_and(e_ids[None, :] > e_ids[:, None], (padded > 0)[None, :])
    next_nonempty = jnp.min(jnp.where(later, e_ids[None, :], N_EXPERTS), axis=1)
    next_e = jnp.where(next_nonempty == N_EXPERTS, -1, next_nonempty).astype(jnp.int32)
    switched = jnp.concatenate([jnp.zeros((1,), jnp.int32), (blk_e[1:] != blk_e[:-1]).astype(jnp.int32)])
    slot_par = (jnp.cumsum(switched) % 2).astype(jnp.int32)
    n_valid = n_valid.reshape(1)

    dest = dest[:TOP_K].T.reshape(-1)
    yb = _experts(blk_e, n_valid, slot_par, next_e, phys, xs,
                  w_gate_up[0], b_gate_up[0][:, None, :], w_down[0], b_down[0][:, None, :])
    out = _combine(dest, yb, x1, gates[:TOP_K].T, p[0].reshape(n, PLE_DIM),
                   row(ln2_g[0]), row(ln2_b[0]), ple_proj[0].astype(_BF16),
                   ple_gate_w[0].astype(_BF16), row(ple_gate_b[0]), row(ln3_g[0]), row(ln3_b[0]))
    return out.reshape(bsz, seq, d)
```

```python
import functools

import jax
import jax.numpy as jnp
from jax import lax
from jax.experimental import pallas as pl
from jax.experimental.pallas import tpu as pltpu

D_MODEL = 1024
D_POOL = 512
D_CONV = 512
POOL_WINDOWS = (2, 4, 8, 16)
POOL_GROUP_DIM = 128
CONV_WIDTH = 3
D_IN_PROJ = D_POOL + 3 * D_CONV
N_EXPERTS = 32
TOP_K = 4
D_FF = 1024
SWIGLU_LIMIT = 7.0
SWIGLU_ALPHA = 1.702
PLE_DIM = 256
DEPTH = 1
DEEPNORM_ALPHA = (2.0 * DEPTH) ** 0.25
LN_EPS = 1e-5

LANES = 128
SUBLANES = 8
ROW_CHUNKS = D_MODEL // LANES
POOL_HALO = 16
CONV_HALO = 8
GATHER_PITCH = ROW_CHUNKS + 1

TS_MIX = 512
TM_EXPERT = 512
TS_COMBINE = 256
COMBINE_GROUP = 8

_BF16 = jnp.bfloat16
_F32 = jnp.float32


def _layer_norm(h, g, b):
    mu = jnp.mean(h, axis=-1, keepdims=True)
    c = h - mu
    var = jnp.mean(c * c, axis=-1, keepdims=True)
    return c * lax.rsqrt(var + LN_EPS) * g + b


def _to_row_tiles(ref, val, pitch=ROW_CHUNKS):
    rows = val.shape[0]
    for c in range(ROW_CHUNKS):
        ref[pl.ds(c, rows, stride=pitch), :] = val[:, c * LANES:(c + 1) * LANES]
    for c in range(ROW_CHUNKS, pitch):
        ref[pl.ds(c, rows, stride=pitch), :] = jnp.zeros((rows, LANES), val.dtype)


def _from_row_tiles(ref, rows, pitch=ROW_CHUNKS):
    return jnp.concatenate(
        [ref[pl.ds(c, rows, stride=pitch), :] for c in range(ROW_CHUNKS)], axis=1)


def _row_tile(ref, r, pitch=ROW_CHUNKS):
    if pitch == ROW_CHUNKS:
        return ref.at[pl.ds(pl.multiple_of(r * ROW_CHUNKS, ROW_CHUNKS), ROW_CHUNKS)]
    return ref.at[pl.ds(r * pitch, ROW_CHUNKS)]


def _mix_route_kernel(ns, x_ref, w_in_ref, pmix_ref, pscale_ref, convw_ref, w_out_ref, g1_ref, b1_ref,
                      rwt_ref, rb_ref, tri_ref, low_ref,
                      x1_ref, dest_ref, gate_ref, cnt_ref, tab_ref, xs_ref,
                      xbuf, dvm, dsm, carry_v, carry_u, cnt_sc, used_sc, cur_sc, tab_sc, meta_v, meta_s, zbuf,
                      row_sems, dest_sem):
    g = pl.program_id(0)
    n_tiles = pl.num_programs(0) - 1
    ts = TS_MIX
    par = lax.rem(g, 2)
    tile_rows = ts * GATHER_PITCH

    def wait_rows(q):
        for _ in range(TOP_K):
            pltpu.make_async_copy(xbuf.at[q], xs_ref.at[pl.ds(0, tile_rows)], row_sems.at[q]).wait()

    @pl.when(g >= 2)
    def _():
        wait_rows(par)

    @pl.when(g >= 1)
    def _():
        q = 1 - par
        base = q * (TOP_K * ts)
        for c in _slots_to_scalar(dvm, dsm, q, dest_sem):
            c.wait()

        def body(t, carry):
            for k in range(TOP_K):
                slot = dsm[base + k * ts + t]
                pltpu.make_async_copy(xbuf.at[q, pl.ds(t * GATHER_PITCH, GATHER_PITCH)],
                                      xs_ref.at[pl.ds(slot * GATHER_PITCH, GATHER_PITCH)],
                                      row_sems.at[q]).start(priority=k % 2)
            return carry
        lax.fori_loop(0, ts, body, 0, unroll=8)

    @pl.when(g < n_tiles)
    def _():
        _mix_route_tile(lax.rem(g, ns), g == 0, par, x_ref, w_in_ref, pmix_ref, pscale_ref, convw_ref, w_out_ref,
                        g1_ref, b1_ref, rwt_ref, rb_ref, tri_ref, low_ref, x1_ref, dest_ref, gate_ref, cnt_ref,
                        tab_ref, xbuf, dvm, dsm, carry_v, carry_u, cnt_sc, used_sc, cur_sc, tab_sc, dest_sem)

    @pl.when(g == n_tiles)
    def _():
        wait_rows(1 - par)
        _seal_slots(xs_ref, cnt_sc, tab_sc, meta_v, meta_s, zbuf, dest_sem)


def _slots_to_scalar(dvm, dsm, q, sem):
    ts = TS_MIX
    return [pltpu.make_async_copy(dvm.at[q, k], dsm.at[pl.ds((q * TOP_K + k) * ts, ts)], sem)
            for k in range(TOP_K)]


def _seal_slots(xs_ref, cnt_sc, tab_sc, meta_v, meta_s, zbuf, sem):
    tm = TM_EXPERT
    tile_rows = tm * GATHER_PITCH
    n_tiles = xs_ref.shape[0] // tile_rows
    meta_v[0] = cnt_sc[...].astype(jnp.int32)
    meta_v[1] = tab_sc[...].astype(jnp.int32)
    to_scalar = pltpu.make_async_copy(meta_v, meta_s, sem)
    to_scalar.start()
    to_scalar.wait()
    zbuf[...] = jnp.zeros_like(zbuf)
    pieces = [1 << b for b in reversed(range(tm.bit_length() - 1))]

    def each_copy(fn):
        owned = 0
        for e in range(N_EXPERTS):
            cnt = meta_s[0, e, 0]
            tiles = lax.shift_right_logical(cnt + (tm - 1), tm.bit_length() - 1)
            owned = owned + tiles
            last = meta_s[1, e, jnp.maximum(tiles - 1, 0)]
            fill = cnt - jnp.maximum(tiles - 1, 0) * tm
            spare = tm - fill
            pos = last * tm + fill
            for piece in pieces:
                take = jnp.logical_and(tiles > 0, (spare & piece) != 0)

                @pl.when(take)
                def _():
                    fn(pltpu.make_async_copy(
                        zbuf.at[pl.ds(0, piece * GATHER_PITCH)],
                        xs_ref.at[pl.ds(pos * GATHER_PITCH, piece * GATHER_PITCH)], sem))
                pos = pos + jnp.where((spare & piece) != 0, piece, 0)

        def unowned(j, carry):
            fn(pltpu.make_async_copy(
                zbuf, xs_ref.at[pl.ds(pl.multiple_of(j * tile_rows, tile_rows), tile_rows)], sem))
            return carry
        lax.fori_loop(owned, n_tiles, unowned, 0)

    each_copy(lambda c: c.start())
    each_copy(lambda c: c.wait())


def _mix_route_tile(s, first, par, x_ref, w_in_ref, pmix_ref, pscale_ref, convw_ref, w_out_ref, g1_ref, b1_ref,
                    rwt_ref, rb_ref, tri_ref, low_ref, x1_ref, dest_ref, gate_ref, cnt_ref, tab_ref,
                    xbuf, dvm, dsm, carry_v, carry_u, cnt_sc, used_sc, cur_sc, tab_sc, dest_sem):
    ts = TS_MIX

    @pl.when(first)
    def _():
        cnt_sc[...] = jnp.zeros_like(cnt_sc)
        used_sc[...] = jnp.zeros_like(used_sc)
        cur_sc[...] = jnp.zeros_like(cur_sc)
        tab_sc[...] = jnp.zeros_like(tab_sc)

    @pl.when(s == 0)
    def _():
        carry_v[...] = jnp.zeros_like(carry_v)
        carry_u[...] = jnp.zeros_like(carry_u)

    xb = x_ref[0]
    proj = jnp.dot(xb.astype(_BF16), w_in_ref[...], preferred_element_type=_F32)
    vp = proj[:, :D_POOL]
    bg = proj[:, D_POOL:D_POOL + D_CONV]
    cg = proj[:, D_POOL + D_CONV:D_POOL + 2 * D_CONV]
    vc = proj[:, D_POOL + 2 * D_CONV:]

    ext = jnp.concatenate([carry_v[...], vp], axis=0)
    pos = lax.broadcasted_iota(jnp.int32, (ts, 1), 0) + s * ts
    mixed = []
    for g, w in enumerate(POOL_WINDOWS):
        lo, hi = g * POOL_GROUP_DIM, (g + 1) * POOL_GROUP_DIM
        acc = ext[:, lo:hi]
        sh = 1
        while sh < w:
            acc = acc + pltpu.roll(acc, sh, 0)
            sh *= 2
        cnt = jnp.minimum(pos + 1, w).astype(_F32)
        d = acc[POOL_HALO:] / cnt - vp[:, lo:hi]
        yg = jnp.dot(d.astype(_BF16), pmix_ref[g], preferred_element_type=_F32)
        mixed.append(yg * pscale_ref[:, lo:hi])
    carry_v[...] = vp[ts - POOL_HALO:]

    u = cg * vc
    extu = jnp.concatenate([carry_u[...], u], axis=0)
    u1 = pltpu.roll(extu, 1, 0)[CONV_HALO:]
    u2 = pltpu.roll(extu, 2, 0)[CONV_HALO:]
    yc = bg * (convw_ref[0:1, :] * u2 + convw_ref[1:2, :] * u1 + convw_ref[2:3, :] * u)
    carry_u[...] = u[ts - CONV_HALO:]
    mixed.append(yc)

    mix_in = jnp.concatenate(mixed, axis=1).astype(_BF16)
    mix = jnp.dot(mix_in, w_out_ref[...], preferred_element_type=_F32)
    x1 = _layer_norm(DEEPNORM_ALPHA * xb + mix, g1_ref[...], b1_ref[...])
    _to_row_tiles(xbuf.at[par], x1, GATHER_PITCH)
    x1_ref[...] = x1

    logits = lax.dot_general(rwt_ref[...], x1.astype(_BF16), (((1,), (1,)), ((), ())),
                             preferred_element_type=_F32) + rb_ref[:, 0:1]
    eidx = lax.broadcasted_iota(jnp.int32, (N_EXPERTS, ts), 0).astype(_F32)
    vals, sels = [], []
    work = logits
    for k in range(TOP_K):
        m = jnp.max(work, axis=0, keepdims=True)
        first = jnp.min(jnp.where(work == m, eidx, float(N_EXPERTS)), axis=0, keepdims=True)
        sel = eidx == first
        work = jnp.where(sel, -jnp.inf, work)
        vals.append(m)
        sels.append(sel)
    exps = [jnp.exp(v - vals[0]) for v in vals]
    denom = exps[0] + exps[1] + exps[2] + exps[3]
    for k in range(TOP_K):
        gate_ref[k:k + 1, :] = exps[k] / denom
    gate_ref[TOP_K:, :] = jnp.zeros((SUBLANES - TOP_K, ts), _F32)

    tm = float(TM_EXPERT)
    chosen = jnp.zeros((N_EXPERTS, ts), _F32)
    for sel in sels:
        chosen = chosen + sel.astype(_F32)
    incl = jnp.dot(chosen.astype(_BF16), tri_ref[...], preferred_element_type=_F32)
    cnt_b = cnt_sc[:, 0:1]
    cnt_a = cnt_b + jnp.sum(chosen, axis=1, keepdims=True)
    q_b = jnp.floor((cnt_b + (tm - 1.0)) * (1.0 / tm))
    q_a = jnp.floor((cnt_a + (tm - 1.0)) * (1.0 / tm))
    n_new = q_a - q_b
    before = jnp.dot(low_ref[...], jnp.broadcast_to(n_new, (N_EXPERTS, LANES)).astype(_BF16),
                     preferred_element_type=_F32)[:, 0:1]
    used = used_sc[:, 0:1]
    new_tile = used + before
    cur_tile = cur_sc[:, 0:1]
    rank = cnt_b + (incl - chosen)
    page = jnp.floor(rank * (1.0 / tm))
    slot = jnp.where(page < q_b, cur_tile, new_tile) * tm + (rank - page * tm)
    for k in range(TOP_K):
        dest_k = jnp.sum(jnp.where(sels[k], slot, 0.0), axis=0, keepdims=True).astype(jnp.int32)
        dest_ref[k:k + 1, :] = dest_k
        dvm[par, k:k + 1, :] = dest_k
    dest_ref[TOP_K:, :] = jnp.zeros((SUBLANES - TOP_K, ts), jnp.int32)

    took = n_new > 0.0
    lane = lax.broadcasted_iota(jnp.int32, (N_EXPERTS, LANES), 1).astype(_F32)
    table = jnp.where(jnp.logical_and(lane == q_b, took), new_tile, tab_sc[...])
    tab_sc[...] = table
    cur_sc[...] = jnp.broadcast_to(jnp.where(took, new_tile, cur_tile), (N_EXPERTS, LANES))
    used_sc[...] = jnp.broadcast_to(used + jnp.sum(n_new, axis=0, keepdims=True), (N_EXPERTS, LANES))
    cnt_sc[...] = jnp.broadcast_to(cnt_a, (N_EXPERTS, LANES))
    cnt_ref[...] = jnp.broadcast_to(cnt_a, (N_EXPERTS, LANES)).astype(jnp.int32)
    tab_ref[...] = table.astype(jnp.int32)

    for c in _slots_to_scalar(dvm, dsm, par, dest_sem):
        c.start()


def _mix_route(x, w_in, pmix, pscale, convw, w_out, g1, b1, rwt, rb, tri, low, n_slots):
    bsz, seq, d = x.shape
    n = bsz * seq
    ns = seq // TS_MIX
    n_tiles = bsz * ns
    assert n // TM_EXPERT <= LANES
    full = lambda shape: pl.BlockSpec(shape, lambda g: (0,) * len(shape))
    tile = lambda g: jnp.minimum(g, n_tiles - 1)
    state = pltpu.VMEM((N_EXPERTS, LANES), _F32)
    return pl.pallas_call(
        functools.partial(_mix_route_kernel, ns),
        grid=(n_tiles + 1,),
        in_specs=[
            pl.BlockSpec((1, TS_MIX, d), lambda g: (tile(g) // ns, tile(g) % ns, 0)),
            full(w_in.shape), full(pmix.shape), full(pscale.shape), full(convw.shape),
            full(w_out.shape), full(g1.shape), full(b1.shape), full(rwt.shape), full(rb.shape),
            full(tri.shape), full(low.shape),
        ],
        out_specs=[
            pl.BlockSpec((TS_MIX, D_MODEL), lambda g: (tile(g), 0)),
            pl.BlockSpec((SUBLANES, TS_MIX), lambda g: (0, tile(g))),
            pl.BlockSpec((SUBLANES, TS_MIX), lambda g: (0, tile(g))),
            pl.BlockSpec((N_EXPERTS, LANES), lambda g: (0, 0)),
            pl.BlockSpec((N_EXPERTS, LANES), lambda g: (0, 0)),
            pl.BlockSpec(memory_space=pl.ANY),
        ],
        out_shape=[
            jax.ShapeDtypeStruct((n, D_MODEL), _F32),
            jax.ShapeDtypeStruct((SUBLANES, n), jnp.int32),
            jax.ShapeDtypeStruct((SUBLANES, n), _F32),
            jax.ShapeDtypeStruct((N_EXPERTS, LANES), jnp.int32),
            jax.ShapeDtypeStruct((N_EXPERTS, LANES), jnp.int32),
            jax.ShapeDtypeStruct((n_slots * GATHER_PITCH, LANES), _F32),
        ],
        scratch_shapes=[
            pltpu.VMEM((2, TS_MIX * GATHER_PITCH, LANES), _F32),
            pltpu.VMEM((2, SUBLANES, TS_MIX), jnp.int32),
            pltpu.SMEM((2 * TOP_K * TS_MIX,), jnp.int32),
            pltpu.VMEM((POOL_HALO, D_POOL), _F32),
            pltpu.VMEM((CONV_HALO, D_CONV), _F32),
            state, state, state, state,
            pltpu.VMEM((2, N_EXPERTS, LANES), jnp.int32),
            pltpu.SMEM((2, N_EXPERTS, LANES), jnp.int32),
            pltpu.VMEM((TM_EXPERT * GATHER_PITCH, LANES), _F32),
            pltpu.SemaphoreType.DMA((2,)),
            pltpu.SemaphoreType.DMA(()),
        ],
        compiler_params=pltpu.CompilerParams(
            dimension_semantics=("arbitrary",),
            vmem_limit_bytes=48 * 1024 * 1024),
        name="mix_route",
    )(x, w_in, pmix, pscale, convw, w_out, g1, b1, rwt, rb, tri, low)


def _experts_kernel(be_ref, nv_ref, par_ref, nxt_ref, phys_ref, xs_ref, wgu_hbm, bgu_ref, wdn_hbm, bdn_ref, out_ref,
                    wgu_f32, wdn_f32, wgu_bf, wdn_bf, sems):
    j = pl.program_id(0)
    tm = TM_EXPERT
    valid = j < nv_ref[0]
    new_expert = jnp.logical_or(j == 0, be_ref[j] != be_ref[jnp.maximum(j - 1, 0)])

    def weight_copies(e, s):
        return (pltpu.make_async_copy(wgu_hbm.at[e], wgu_f32.at[s], sems.at[0, s]),
                pltpu.make_async_copy(wdn_hbm.at[e], wdn_f32.at[s], sems.at[1, s]))

    @pl.when(j == 0)
    def _():
        for c in weight_copies(be_ref[0], par_ref[0]):
            c.start()

    @pl.when(jnp.logical_and(valid, new_expert))
    def _():
        s = par_ref[j]
        for c in weight_copies(be_ref[j], s):
            c.wait()

        nxt = nxt_ref[be_ref[j]]

        @pl.when(nxt >= 0)
        def _():
            for c in weight_copies(nxt, 1 - s):
                c.start()

        wgu_bf[...] = wgu_f32[s].astype(_BF16)
        wdn_bf[...] = wdn_f32[s].astype(_BF16)

    @pl.when(valid)
    def _():
        x = _from_row_tiles(xs_ref, tm, GATHER_PITCH).astype(_BF16)
        gu = jnp.dot(x, wgu_bf[...], preferred_element_type=_F32) + bgu_ref[0]
        gate = jnp.minimum(gu[:, :D_FF], SWIGLU_LIMIT)
        up = jnp.clip(gu[:, D_FF:], -SWIGLU_LIMIT, SWIGLU_LIMIT)
        glu = gate * jax.nn.sigmoid(SWIGLU_ALPHA * gate)
        act = ((up + 1.0) * glu).astype(_BF16)
        y = jnp.dot(act, wdn_bf[...], preferred_element_type=_F32) + bdn_ref[0]
        _to_row_tiles(out_ref, y, GATHER_PITCH)

    @pl.when(j >= nv_ref[0])
    def _():
        out_ref[...] = jnp.zeros_like(out_ref)


def _experts(blk_e, n_valid, slot_par, next_e, phys, xs, wgu, bgu, wdn, bdn):
    n_slots = xs.shape[0] // GATHER_PITCH
    nb = n_slots // TM_EXPERT
    rows = TM_EXPERT * GATHER_PITCH
    grid_spec = pltpu.PrefetchScalarGridSpec(
        num_scalar_prefetch=5,
        grid=(nb,),
        in_specs=[
            pl.BlockSpec((rows, LANES), lambda j, be, nv, par, nxt, ph: (ph[jnp.minimum(j, nv[0] - 1)], 0)),
            pl.BlockSpec(memory_space=pl.ANY),
            pl.BlockSpec((1, 1, 2 * D_FF), lambda j, be, nv, par, nxt, ph: (be[j], 0, 0)),
            pl.BlockSpec(memory_space=pl.ANY),
            pl.BlockSpec((1, 1, D_MODEL), lambda j, be, nv, par, nxt, ph: (be[j], 0, 0)),
        ],
        out_specs=pl.BlockSpec((rows, LANES), lambda j, be, nv, par, nxt, ph: (ph[j], 0)),
        scratch_shapes=[
            pltpu.VMEM((2, D_MODEL, 2 * D_FF), _F32),
            pltpu.VMEM((2, D_FF, D_MODEL), _F32),
            pltpu.VMEM((D_MODEL, 2 * D_FF), _BF16),
            pltpu.VMEM((D_FF, D_MODEL), _BF16),
            pltpu.SemaphoreType.DMA((2, 2)),
        ],
    )
    return pl.pallas_call(
        _experts_kernel,
        grid_spec=grid_spec,
        out_shape=jax.ShapeDtypeStruct((n_slots * GATHER_PITCH, LANES), _F32),
        compiler_params=pltpu.CompilerParams(
            dimension_semantics=("arbitrary",),
            vmem_limit_bytes=60 * 1024 * 1024),
        name="experts",
    )(blk_e, n_valid, slot_par, next_e, phys, xs, wgu, bgu, wdn, bdn)


def _combine_kernel(dcur_ref, dnext_ref, yb_ref, x1_ref, gt_ref, p_ref, g2_ref, b2_ref, pw_ref, gw_ref,
                    gb_ref, g3_ref, b3_ref, out_ref, gbuf, hbuf, sems):
    ts = TS_COMBINE
    grp = COMBINE_GROUP
    i = pl.program_id(0)
    slot = i % 2
    nxt = 1 - slot

    def start_rows(d_ref, s, t0):
        for j in range(grp):
            t = t0 + j
            for k in range(TOP_K):
                pltpu.make_async_copy(_row_tile(yb_ref, d_ref[t * TOP_K + k], GATHER_PITCH),
                                      gbuf.at[s, k, pl.ds(t * GATHER_PITCH, ROW_CHUNKS)],
                                      sems.at[s]).start(priority=k % 2)

    def wait_tile(s):
        for k in range(TOP_K):
            pltpu.make_async_copy(yb_ref.at[pl.ds(0, ts * ROW_CHUNKS)],
                                  gbuf.at[s, k, pl.ds(0, ts * ROW_CHUNKS)], sems.at[s]).wait()

    @pl.when(i == 0)
    def _():
        def first(r, carry):
            start_rows(dcur_ref, 0, r * grp)
            return carry
        lax.fori_loop(0, ts // grp, first, 0)

    wait_tile(slot)

    def group(r, carry):
        t0 = pl.multiple_of(r * grp, grp)
        x1g = x1_ref[pl.ds(t0, grp), :]
        gts = gt_ref[pl.ds(t0, grp), :]
        ys = []
        for k in range(TOP_K):
            rows_k = gbuf.at[slot, k]
            ys.append(jnp.concatenate(
                [rows_k[pl.ds(t0 * GATHER_PITCH + c, grp, stride=GATHER_PITCH), :] for c in range(ROW_CHUNKS)],
                axis=1))
        start_rows(dnext_ref, nxt, t0)
        ffn = gts[:, 0:1] * ys[0]
        for k in range(1, TOP_K):
            ffn = ffn + gts[:, k:k + 1] * ys[k]
        hbuf[pl.ds(t0, grp), :] = DEEPNORM_ALPHA * x1g + ffn
        return carry

    lax.fori_loop(0, ts // grp, group, 0)

    @pl.when(i == pl.num_programs(0) - 1)
    def _():
        wait_tile(nxt)

    x2 = _layer_norm(hbuf[...], g2_ref[...], b2_ref[...])
    z = jnp.dot(x2.astype(_BF16), gw_ref[...], preferred_element_type=_F32) + gb_ref[...]
    ple = jnp.dot(p_ref[...].astype(_BF16), pw_ref[...], preferred_element_type=_F32)
    x3 = _layer_norm(DEEPNORM_ALPHA * x2 + jax.nn.sigmoid(z) * ple, g3_ref[...], b3_ref[...])
    out_ref[...] = x3


def _combine(dest_flat, yb, x1, gates_t, p2d, g2, b2, pw, gw, gb, g3, b3):
    n = dest_flat.shape[0] // TOP_K
    ts = TS_COMBINE
    last = n // ts - 1
    full = lambda shape: pl.BlockSpec(shape, lambda i: (0,) * len(shape))
    return pl.pallas_call(
        _combine_kernel,
        grid=(n // ts,),
        in_specs=[
            pl.BlockSpec((TOP_K * ts,), lambda i: (i,), memory_space=pltpu.SMEM),
            pl.BlockSpec((TOP_K * ts,), lambda i: (jnp.minimum(i + 1, last),), memory_space=pltpu.SMEM),
            pl.BlockSpec(memory_space=pl.ANY),
            pl.BlockSpec((ts, D_MODEL), lambda i: (i, 0)),
            pl.BlockSpec((ts, TOP_K), lambda i: (i, 0)),
            pl.BlockSpec((ts, PLE_DIM), lambda i: (i, 0)),
            full(g2.shape), full(b2.shape), full(pw.shape), full(gw.shape), full(gb.shape),
            full(g3.shape), full(b3.shape),
        ],
        out_specs=pl.BlockSpec((ts, D_MODEL), lambda i: (i, 0)),
        out_shape=jax.ShapeDtypeStruct((n, D_MODEL), _F32),
        scratch_shapes=[
            pltpu.VMEM((2, TOP_K, ts * GATHER_PITCH, LANES), _F32),
            pltpu.VMEM((ts, D_MODEL), _F32),
            pltpu.SemaphoreType.DMA((2,)),
        ],
        compiler_params=pltpu.CompilerParams(
            dimension_semantics=("arbitrary",),
            vmem_limit_bytes=48 * 1024 * 1024),
        name="combine",
    )(dest_flat, dest_flat, yb, x1, gates_t, p2d, g2, b2, pw, gw, gb, g3, b3)


def kernel(x, p, w_in, pool_mix, pool_scale, conv_w, w_out, ln1_g, ln1_b, router_w, router_b,
           w_gate_up, b_gate_up, w_down, b_down, ln2_g, ln2_b, ple_proj, ple_gate_w, ple_gate_b,
           ln3_g, ln3_b):
    assert DEPTH == 1 and x.shape[-1] == D_MODEL
    bsz, seq, d = x.shape
    n = bsz * seq
    assert seq % TS_MIX == 0 and n % TS_COMBINE == 0
    row = lambda v: v.reshape(1, -1)

    tri = jnp.triu(jnp.ones((TS_MIX, TS_MIX), _BF16))
    low = jnp.tril(jnp.ones((N_EXPERTS, N_EXPERTS), _BF16), k=-1)
    tm = TM_EXPERT
    n_slots = n * TOP_K + N_EXPERTS * tm
    nb = n_slots // tm
    x1, dest, gates, counts, table, xs = _mix_route(
        x, w_in[0].astype(_BF16), pool_mix[0].astype(_BF16), row(pool_scale[0]), conv_w[0],
        w_out[0].astype(_BF16), row(ln1_g[0]), row(ln1_b[0]),
        router_w[0].T.astype(_BF16), jnp.broadcast_to(router_b[0][:, None], (N_EXPERTS, LANES)), tri, low,
        n_slots)

    cnt = counts[:, 0]
    tiles_e = (cnt + tm - 1) // tm
    tile_end = jnp.cumsum(tiles_e)
    n_valid = tile_end[-1].astype(jnp.int32)
    steps = jnp.arange(nb, dtype=jnp.int32)
    blk_e = jnp.minimum(jnp.sum(steps[:, None] >= tile_end[None, :], axis=1), N_EXPERTS - 1)
    blk_e = jnp.where(steps < n_valid, blk_e, blk_e[n_valid - 1]).astype(jnp.int32)
    within = steps - (tile_end - tiles_e)[blk_e]
    phys = jnp.where(steps < n_valid, table[blk_e, jnp.clip(within, 0, LANES - 1)], steps).astype(jnp.int32)

    padded = tiles_e
    e_ids = jnp.arange(N_EXPERTS, dtype=jnp.int32)
    later = jnp.logical_and(e_ids[None, :] > e_ids[:, None], (padded > 0)[None, :])
    next_nonempty = jnp.min(jnp.where(later, e_ids[None, :], N_EXPERTS), axis=1)
    next_e = jnp.where(next_nonempty == N_EXPERTS, -1, next_nonempty).astype(jnp.int32)
    switched = jnp.concatenate([jnp.zeros((1,), jnp.int32), (blk_e[1:] != blk_e[:-1]).astype(jnp.int32)])
    slot_par = (jnp.cumsum(switched) % 2).astype(jnp.int32)
    n_valid = n_valid.reshape(1)

    dest = dest[:TOP_K].T.reshape(-1)
    yb = _experts(blk_e, n_valid, slot_par, next_e, phys, xs,
                  w_gate_up[0], b_gate_up[0][:, None, :], w_down[0], b_down[0][:, None, :])
    out = _combine(dest, yb, x1, gates[:TOP_K].T, p[0].reshape(n, PLE_DIM),
                   row(ln2_g[0]), row(ln2_b[0]), ple_proj[0].astype(_BF16),
                   ple_gate_w[0].astype(_BF16), row(ple_gate_b[0]), row(ln3_g[0]), row(ln3_b[0]))
    return out.reshape(bsz, seq, d)
```

```python
import functools

import jax
import jax.numpy as jnp
from jax import lax
from jax.experimental import pallas as pl
from jax.experimental.pallas import tpu as pltpu

D_MODEL = 1024
D_POOL = 512
D_CONV = 512
POOL_WINDOWS = (2, 4, 8, 16)
POOL_GROUP_DIM = 128
CONV_WIDTH = 3
D_IN_PROJ = D_POOL + 3 * D_CONV
N_EXPERTS = 32
TOP_K = 4
D_FF = 1024
SWIGLU_LIMIT = 7.0
SWIGLU_ALPHA = 1.702
PLE_DIM = 256
DEPTH = 1
DEEPNORM_ALPHA = (2.0 * DEPTH) ** 0.25
LN_EPS = 1e-5

LANES = 128
SUBLANES = 8
ROW_CHUNKS = D_MODEL // LANES
POOL_HALO = 16
CONV_HALO = 8
GATHER_PITCH = ROW_CHUNKS + 1

TS_MIX = 512
TM_EXPERT = 512
TS_COMBINE = 256
COMBINE_GROUP = 8

_BF16 = jnp.bfloat16
_F32 = jnp.float32


def _layer_norm(h, g, b):
    mu = jnp.mean(h, axis=-1, keepdims=True)
    c = h - mu
    var = jnp.mean(c * c, axis=-1, keepdims=True)
    return c * lax.rsqrt(var + LN_EPS) * g + b


def _to_row_tiles(ref, val, pitch=ROW_CHUNKS):
    rows = val.shape[0]
    for c in range(ROW_CHUNKS):
        ref[pl.ds(c, rows, stride=pitch), :] = val[:, c * LANES:(c + 1) * LANES]
    for c in range(ROW_CHUNKS, pitch):
        ref[pl.ds(c, rows, stride=pitch), :] = jnp.zeros((rows, LANES), val.dtype)


def _from_row_tiles(ref, rows, pitch=ROW_CHUNKS):
    return jnp.concatenate(
        [ref[pl.ds(c, rows, stride=pitch), :] for c in range(ROW_CHUNKS)], axis=1)


def _row_tile(ref, r, pitch=ROW_CHUNKS):
    if pitch == ROW_CHUNKS:
        return ref.at[pl.ds(pl.multiple_of(r * ROW_CHUNKS, ROW_CHUNKS), ROW_CHUNKS)]
    return ref.at[pl.ds(r * pitch, ROW_CHUNKS)]


def _mix_route_kernel(ns, x_ref, w_in_ref, pmix_ref, pscale_ref, convw_ref, w_out_ref, g1_ref, b1_ref,
                      rwt_ref, rb_ref, tri_ref, low_ref,
                      x1_ref, dest_ref, gate_ref, cnt_ref, tab_ref, xs_ref,
                      xbuf, cbuf, rbuf, dvm, dsm, carry_v, carry_u, cnt_sc, used_sc, cur_sc, tab_sc,
                      meta_v, meta_s, zbuf, row_sems, dest_sem):
    g = pl.program_id(0)
    n_tiles = pl.num_programs(0) - 1
    ts = TS_MIX
    grp = SUBLANES
    par = lax.rem(g, 2)
    buf = lax.rem(g, 3)
    prev = lax.rem(g + 2, 3)
    tile_rows = ts * GATHER_PITCH

    def wait_rows(q):
        for _ in range(TOP_K):
            pltpu.make_async_copy(xbuf.at[q], xs_ref.at[pl.ds(0, tile_rows)], row_sems.at[q]).wait()

    @pl.when(g >= 3)
    def _():
        wait_rows(buf)

    @pl.when(g >= 1)
    def _():
        for c in _slots_to_scalar(dvm, dsm, 1 - par, dest_sem):
            c.wait()

    @pl.when(g < n_tiles)
    def _():
        _mix_tile(lax.rem(g, ns), g == 0, x_ref, w_in_ref, pmix_ref, pscale_ref, convw_ref, w_out_ref,
                  cbuf, rbuf, carry_v, carry_u, cnt_sc, used_sc, cur_sc, tab_sc)

    base = (1 - par) * (TOP_K * ts)

    def start_rows(t0):
        for j in range(grp):
            t = t0 + j
            for k in range(TOP_K):
                slot = dsm[base + k * ts + t]
                pltpu.make_async_copy(xbuf.at[prev, pl.ds(t * GATHER_PITCH, GATHER_PITCH)],
                                      xs_ref.at[pl.ds(slot * GATHER_PITCH, GATHER_PITCH)],
                                      row_sems.at[prev]).start(priority=k % 2)

    def row_loop(finish, send):
        gain = g1_ref[...]
        bias = b1_ref[...]

        def group(r, carry):
            t0 = pl.multiple_of(r * grp, grp)
            if finish:
                c = cbuf[pl.ds(t0, grp), :]
                rstd = rbuf[pl.ds(t0, grp), :]
            if send:
                start_rows(t0)
            if finish:
                rows = xbuf.at[buf]
                for ch in range(ROW_CHUNKS):
                    lanes = slice(ch * LANES, (ch + 1) * LANES)
                    x1 = c[:, lanes] * rstd * gain[:, lanes] + bias[:, lanes]
                    x1_ref[pl.ds(t0, grp), lanes] = x1
                    rows[pl.ds(t0 * GATHER_PITCH + ch, grp, stride=GATHER_PITCH), :] = x1
                rows[pl.ds(t0 * GATHER_PITCH + ROW_CHUNKS, grp, stride=GATHER_PITCH), :] = jnp.zeros((grp, LANES), _F32)
            return carry
        lax.fori_loop(0, ts // grp, group, 0)

    @pl.when(g == 0)
    def _():
        row_loop(finish=True, send=False)

    @pl.when(jnp.logical_and(g >= 1, g < n_tiles))
    def _():
        row_loop(finish=True, send=True)

    @pl.when(g == n_tiles)
    def _():
        row_loop(finish=False, send=True)

    @pl.when(g < n_tiles)
    def _():
        _route_tile(par, x1_ref, rwt_ref, rb_ref, tri_ref, low_ref, dest_ref, gate_ref, cnt_ref, tab_ref,
                    dvm, dsm, cnt_sc, used_sc, cur_sc, tab_sc, dest_sem)

    @pl.when(g == n_tiles)
    def _():
        wait_rows(lax.rem(g + 1, 3))
        wait_rows(prev)
        _seal_slots(xs_ref, cnt_sc, tab_sc, meta_v, meta_s, zbuf, dest_sem)


def _slots_to_scalar(dvm, dsm, q, sem):
    ts = TS_MIX
    return [pltpu.make_async_copy(dvm.at[q, k], dsm.at[pl.ds((q * TOP_K + k) * ts, ts)], sem)
            for k in range(TOP_K)]


def _seal_slots(xs_ref, cnt_sc, tab_sc, meta_v, meta_s, zbuf, sem):
    tm = TM_EXPERT
    tile_rows = tm * GATHER_PITCH
    n_tiles = xs_ref.shape[0] // tile_rows
    meta_v[0] = cnt_sc[...].astype(jnp.int32)
    meta_v[1] = tab_sc[...].astype(jnp.int32)
    to_scalar = pltpu.make_async_copy(meta_v, meta_s, sem)
    to_scalar.start()
    to_scalar.wait()
    zbuf[...] = jnp.zeros_like(zbuf)
    pieces = [1 << b for b in reversed(range(tm.bit_length() - 1))]

    def each_copy(fn):
        owned = 0
        for e in range(N_EXPERTS):
            cnt = meta_s[0, e, 0]
            tiles = lax.shift_right_logical(cnt + (tm - 1), tm.bit_length() - 1)
            owned = owned + tiles
            last = meta_s[1, e, jnp.maximum(tiles - 1, 0)]
            fill = cnt - jnp.maximum(tiles - 1, 0) * tm
            spare = tm - fill
            pos = last * tm + fill
            for piece in pieces:
                take = jnp.logical_and(tiles > 0, (spare & piece) != 0)

                @pl.when(take)
                def _():
                    fn(pltpu.make_async_copy(
                        zbuf.at[pl.ds(0, piece * GATHER_PITCH)],
                        xs_ref.at[pl.ds(pos * GATHER_PITCH, piece * GATHER_PITCH)], sem))
                pos = pos + jnp.where((spare & piece) != 0, piece, 0)

        def unowned(j, carry):
            fn(pltpu.make_async_copy(
                zbuf, xs_ref.at[pl.ds(pl.multiple_of(j * tile_rows, tile_rows), tile_rows)], sem))
            return carry
        lax.fori_loop(owned, n_tiles, unowned, 0)

    each_copy(lambda c: c.start())
    each_copy(lambda c: c.wait())


def _mix_tile(s, first, x_ref, w_in_ref, pmix_ref, pscale_ref, convw_ref, w_out_ref, cbuf, rbuf,
              carry_v, carry_u, cnt_sc, used_sc, cur_sc, tab_sc):
    ts = TS_MIX

    @pl.when(first)
    def _():
        cnt_sc[...] = jnp.zeros_like(cnt_sc)
        used_sc[...] = jnp.zeros_like(used_sc)
        cur_sc[...] = jnp.zeros_like(cur_sc)
        tab_sc[...] = jnp.zeros_like(tab_sc)

    @pl.when(s == 0)
    def _():
        carry_v[...] = jnp.zeros_like(carry_v)
        carry_u[...] = jnp.zeros_like(carry_u)

    xb = x_ref[0]
    proj = jnp.dot(xb.astype(_BF16), w_in_ref[...], preferred_element_type=_F32)
    vp = proj[:, :D_POOL]
    bg = proj[:, D_POOL:D_POOL + D_CONV]
    cg = proj[:, D_POOL + D_CONV:D_POOL + 2 * D_CONV]
    vc = proj[:, D_POOL + 2 * D_CONV:]

    ext = jnp.concatenate([carry_v[...], vp], axis=0)
    pos = lax.broadcasted_iota(jnp.int32, (ts, 1), 0) + s * ts
    mixed = []
    for g, w in enumerate(POOL_WINDOWS):
        lo, hi = g * POOL_GROUP_DIM, (g + 1) * POOL_GROUP_DIM
        acc = ext[:, lo:hi]
        sh = 1
        while sh < w:
            acc = acc + pltpu.roll(acc, sh, 0)
            sh *= 2
        cnt = jnp.minimum(pos + 1, w).astype(_F32)
        d = acc[POOL_HALO:] / cnt - vp[:, lo:hi]
        yg = jnp.dot(d.astype(_BF16), pmix_ref[g], preferred_element_type=_F32)
        mixed.append(yg * pscale_ref[:, lo:hi])
    carry_v[...] = vp[ts - POOL_HALO:]

    u = cg * vc
    extu = jnp.concatenate([carry_u[...], u], axis=0)
    u1 = pltpu.roll(extu, 1, 0)[CONV_HALO:]
    u2 = pltpu.roll(extu, 2, 0)[CONV_HALO:]
    yc = bg * (convw_ref[0:1, :] * u2 + convw_ref[1:2, :] * u1 + convw_ref[2:3, :] * u)
    carry_u[...] = u[ts - CONV_HALO:]
    mixed.append(yc)

    mix_in = jnp.concatenate(mixed, axis=1).astype(_BF16)
    mix = jnp.dot(mix_in, w_out_ref[...], preferred_element_type=_F32)
    h = DEEPNORM_ALPHA * xb + mix
    c = h - jnp.mean(h, axis=-1, keepdims=True)
    cbuf[...] = c
    rbuf[...] = jnp.broadcast_to(lax.rsqrt(jnp.mean(c * c, axis=-1, keepdims=True) + LN_EPS), rbuf.shape)


def _route_tile(par, x1_ref, rwt_ref, rb_ref, tri_ref, low_ref, dest_ref, gate_ref, cnt_ref, tab_ref,
                dvm, dsm, cnt_sc, used_sc, cur_sc, tab_sc, dest_sem):
    ts = TS_MIX
    x1 = x1_ref[...]

    logits = lax.dot_general(rwt_ref[...], x1.astype(_BF16), (((1,), (1,)), ((), ())),
                             preferred_element_type=_F32) + rb_ref[:, 0:1]
    eidx = lax.broadcasted_iota(jnp.int32, (N_EXPERTS, ts), 0).astype(_F32)
    vals, sels = [], []
    work = logits
    for k in range(TOP_K):
        m = jnp.max(work, axis=0, keepdims=True)
        first = jnp.min(jnp.where(work == m, eidx, float(N_EXPERTS)), axis=0, keepdims=True)
        sel = eidx == first
        work = jnp.where(sel, -jnp.inf, work)
        vals.append(m)
        sels.append(sel)
    exps = [jnp.exp(v - vals[0]) for v in vals]
    denom = exps[0] + exps[1] + exps[2] + exps[3]
    for k in range(TOP_K):
        gate_ref[k:k + 1, :] = exps[k] / denom
    gate_ref[TOP_K:, :] = jnp.zeros((SUBLANES - TOP_K, ts), _F32)

    tm = float(TM_EXPERT)
    chosen = jnp.zeros((N_EXPERTS, ts), _F32)
    for sel in sels:
        chosen = chosen + sel.astype(_F32)
    incl = jnp.dot(chosen.astype(_BF16), tri_ref[...], preferred_element_type=_F32)
    cnt_b = cnt_sc[:, 0:1]
    cnt_a = cnt_b + jnp.sum(chosen, axis=1, keepdims=True)
    q_b = jnp.floor((cnt_b + (tm - 1.0)) * (1.0 / tm))
    q_a = jnp.floor((cnt_a + (tm - 1.0)) * (1.0 / tm))
    n_new = q_a - q_b
    before = jnp.dot(low_ref[...], jnp.broadcast_to(n_new, (N_EXPERTS, LANES)).astype(_BF16),
                     preferred_element_type=_F32)[:, 0:1]
    used = used_sc[:, 0:1]
    new_tile = used + before
    cur_tile = cur_sc[:, 0:1]
    rank = cnt_b + (incl - chosen)
    page = jnp.floor(rank * (1.0 / tm))
    slot = jnp.where(page < q_b, cur_tile, new_tile) * tm + (rank - page * tm)
    for k in range(TOP_K):
        dest_k = jnp.sum(jnp.where(sels[k], slot, 0.0), axis=0, keepdims=True).astype(jnp.int32)
        dest_ref[k:k + 1, :] = dest_k
        dvm[par, k:k + 1, :] = dest_k
    dest_ref[TOP_K:, :] = jnp.zeros((SUBLANES - TOP_K, ts), jnp.int32)

    took = n_new > 0.0
    lane = lax.broadcasted_iota(jnp.int32, (N_EXPERTS, LANES), 1).astype(_F32)
    table = jnp.where(jnp.logical_and(lane == q_b, took), new_tile, tab_sc[...])
    tab_sc[...] = table
    cur_sc[...] = jnp.broadcast_to(jnp.where(took, new_tile, cur_tile), (N_EXPERTS, LANES))
    used_sc[...] = jnp.broadcast_to(used + jnp.sum(n_new, axis=0, keepdims=True), (N_EXPERTS, LANES))
    cnt_sc[...] = jnp.broadcast_to(cnt_a, (N_EXPERTS, LANES))
    cnt_ref[...] = jnp.broadcast_to(cnt_a, (N_EXPERTS, LANES)).astype(jnp.int32)
    tab_ref[...] = table.astype(jnp.int32)

    for c in _slots_to_scalar(dvm, dsm, par, dest_sem):
        c.start()


def _mix_route(x, w_in, pmix, pscale, convw, w_out, g1, b1, rwt, rb, tri, low, n_slots):
    bsz, seq, d = x.shape
    n = bsz * seq
    ns = seq // TS_MIX
    n_tiles = bsz * ns
    assert n // TM_EXPERT <= LANES
    full = lambda shape: pl.BlockSpec(shape, lambda g: (0,) * len(shape))
    tile = lambda g: jnp.minimum(g, n_tiles - 1)
    state = pltpu.VMEM((N_EXPERTS, LANES), _F32)
    return pl.pallas_call(
        functools.partial(_mix_route_kernel, ns),
        grid=(n_tiles + 1,),
        in_specs=[
            pl.BlockSpec((1, TS_MIX, d), lambda g: (tile(g) // ns, tile(g) % ns, 0)),
            full(w_in.shape), full(pmix.shape), full(pscale.shape), full(convw.shape),
            full(w_out.shape), full(g1.shape), full(b1.shape), full(rwt.shape), full(rb.shape),
            full(tri.shape), full(low.shape),
        ],
        out_specs=[
            pl.BlockSpec((TS_MIX, D_MODEL), lambda g: (tile(g), 0)),
            pl.BlockSpec((SUBLANES, TS_MIX), lambda g: (0, tile(g))),
            pl.BlockSpec((SUBLANES, TS_MIX), lambda g: (0, tile(g))),
            pl.BlockSpec((N_EXPERTS, LANES), lambda g: (0, 0)),
            pl.BlockSpec((N_EXPERTS, LANES), lambda g: (0, 0)),
            pl.BlockSpec(memory_space=pl.ANY),
        ],
        out_shape=[
            jax.ShapeDtypeStruct((n, D_MODEL), _F32),
            jax.ShapeDtypeStruct((SUBLANES, n), jnp.int32),
            jax.ShapeDtypeStruct((SUBLANES, n), _F32),
            jax.ShapeDtypeStruct((N_EXPERTS, LANES), jnp.int32),
            jax.ShapeDtypeStruct((N_EXPERTS, LANES), jnp.int32),
            jax.ShapeDtypeStruct((n_slots * GATHER_PITCH, LANES), _F32),
        ],
        scratch_shapes=[
            pltpu.VMEM((3, TS_MIX * GATHER_PITCH, LANES), _F32),
            pltpu.VMEM((TS_MIX, D_MODEL), _F32),
            pltpu.VMEM((TS_MIX, LANES), _F32),
            pltpu.VMEM((2, SUBLANES, TS_MIX), jnp.int32),
            pltpu.SMEM((2 * TOP_K * TS_MIX,), jnp.int32),
            pltpu.VMEM((POOL_HALO, D_POOL), _F32),
            pltpu.VMEM((CONV_HALO, D_CONV), _F32),
            state, state, state, state,
            pltpu.VMEM((2, N_EXPERTS, LANES), jnp.int32),
            pltpu.SMEM((2, N_EXPERTS, LANES), jnp.int32),
            pltpu.VMEM((TM_EXPERT * GATHER_PITCH, LANES), _F32),
            pltpu.SemaphoreType.DMA((3,)),
            pltpu.SemaphoreType.DMA(()),
        ],
        compiler_params=pltpu.CompilerParams(
            dimension_semantics=("arbitrary",),
            vmem_limit_bytes=48 * 1024 * 1024),
        name="mix_route",
    )(x, w_in, pmix, pscale, convw, w_out, g1, b1, rwt, rb, tri, low)


def _experts_kernel(be_ref, nv_ref, par_ref, nxt_ref, phys_ref, xs_ref, wgu_hbm, bgu_ref, wdn_hbm, bdn_ref, out_ref,
                    wgu_f32, wdn_f32, wgu_bf, wdn_bf, sems):
    j = pl.program_id(0)
    tm = TM_EXPERT
    valid = j < nv_ref[0]
    new_expert = jnp.logical_or(j == 0, be_ref[j] != be_ref[jnp.maximum(j - 1, 0)])

    def weight_copies(e, s):
        return (pltpu.make_async_copy(wgu_hbm.at[e], wgu_f32.at[s], sems.at[0, s]),
                pltpu.make_async_copy(wdn_hbm.at[e], wdn_f32.at[s], sems.at[1, s]))

    @pl.when(j == 0)
    def _():
        for c in weight_copies(be_ref[0], par_ref[0]):
            c.start()

    @pl.when(jnp.logical_and(valid, new_expert))
    def _():
        s = par_ref[j]
        for c in weight_copies(be_ref[j], s):
            c.wait()

        nxt = nxt_ref[be_ref[j]]

        @pl.when(nxt >= 0)
        def _():
            for c in weight_copies(nxt, 1 - s):
                c.start()

        wgu_bf[...] = wgu_f32[s].astype(_BF16)
        wdn_bf[...] = wdn_f32[s].astype(_BF16)

    @pl.when(valid)
    def _():
        x = _from_row_tiles(xs_ref, tm, GATHER_PITCH).astype(_BF16)
        gu = jnp.dot(x, wgu_bf[...], preferred_element_type=_F32) + bgu_ref[0]
        gate = jnp.minimum(gu[:, :D_FF], SWIGLU_LIMIT)
        up = jnp.clip(gu[:, D_FF:], -SWIGLU_LIMIT, SWIGLU_LIMIT)
        glu = gate * jax.nn.sigmoid(SWIGLU_ALPHA * gate)
        act = ((up + 1.0) * glu).astype(_BF16)
        y = jnp.dot(act, wdn_bf[...], preferred_element_type=_F32) + bdn_ref[0]
        _to_row_tiles(out_ref, y, GATHER_PITCH)

    @pl.when(j >= nv_ref[0])
    def _():
        out_ref[...] = jnp.zeros_like(out_ref)


def _experts(blk_e, n_valid, slot_par, next_e, phys, xs, wgu, bgu, wdn, bdn):
    n_slots = xs.shape[0] // GATHER_PITCH
    nb = n_slots // TM_EXPERT
    rows = TM_EXPERT * GATHER_PITCH
    grid_spec = pltpu.PrefetchScalarGridSpec(
        num_scalar_prefetch=5,
        grid=(nb,),
        in_specs=[
            pl.BlockSpec((rows, LANES), lambda j, be, nv, par, nxt, ph: (ph[jnp.minimum(j, nv[0] - 1)], 0)),
            pl.BlockSpec(memory_space=pl.ANY),
            pl.BlockSpec((1, 1, 2 * D_FF), lambda j, be, nv, par, nxt, ph: (be[j], 0, 0)),
            pl.BlockSpec(memory_space=pl.ANY),
            pl.BlockSpec((1, 1, D_MODEL), lambda j, be, nv, par, nxt, ph: (be[j], 0, 0)),
        ],
        out_specs=pl.BlockSpec((rows, LANES), lambda j, be, nv, par, nxt, ph: (ph[j], 0)),
        scratch_shapes=[
            pltpu.VMEM((2, D_MODEL, 2 * D_FF), _F32),
            pltpu.VMEM((2, D_FF, D_MODEL), _F32),
            pltpu.VMEM((D_MODEL, 2 * D_FF), _BF16),
            pltpu.VMEM((D_FF, D_MODEL), _BF16),
            pltpu.SemaphoreType.DMA((2, 2)),
        ],
    )
    return pl.pallas_call(
        _experts_kernel,
        grid_spec=grid_spec,
        out_shape=jax.ShapeDtypeStruct((n_slots * GATHER_PITCH, LANES), _F32),
        compiler_params=pltpu.CompilerParams(
            dimension_semantics=("arbitrary",),
            vmem_limit_bytes=60 * 1024 * 1024),
        name="experts",
    )(blk_e, n_valid, slot_par, next_e, phys, xs, wgu, bgu, wdn, bdn)


def _combine_kernel(dcur_ref, dnext_ref, yb_ref, x1_ref, gt_ref, p_ref, g2_ref, b2_ref, pw_ref, gw_ref,
                    gb_ref, g3_ref, b3_ref, out_ref, gbuf, hbuf, sems):
    ts = TS_COMBINE
    grp = COMBINE_GROUP
    i = pl.program_id(0)
    slot = i % 2
    nxt = 1 - slot

    def start_rows(d_ref, s, t0):
        for j in range(grp):
            t = t0 + j
            for k in range(TOP_K):
                pltpu.make_async_copy(_row_tile(yb_ref, d_ref[t * TOP_K + k], GATHER_PITCH),
                                      gbuf.at[s, k, pl.ds(t * GATHER_PITCH, ROW_CHUNKS)],
                                      sems.at[s]).start(priority=k % 2)

    def wait_tile(s):
        for k in range(TOP_K):
            pltpu.make_async_copy(yb_ref.at[pl.ds(0, ts * ROW_CHUNKS)],
                                  gbuf.at[s, k, pl.ds(0, ts * ROW_CHUNKS)], sems.at[s]).wait()

    @pl.when(i == 0)
    def _():
        def first(r, carry):
            start_rows(dcur_ref, 0, r * grp)
            return carry
        lax.fori_loop(0, ts // grp, first, 0)

    wait_tile(slot)

    def group(r, carry):
        t0 = pl.multiple_of(r * grp, grp)
        x1g = x1_ref[pl.ds(t0, grp), :]
        gts = gt_ref[pl.ds(t0, grp), :]
        ys = []
        for k in range(TOP_K):
            rows_k = gbuf.at[slot, k]
            ys.append(jnp.concatenate(
                [rows_k[pl.ds(t0 * GATHER_PITCH + c, grp, stride=GATHER_PITCH), :] for c in range(ROW_CHUNKS)],
                axis=1))
        start_rows(dnext_ref, nxt, t0)
        ffn = gts[:, 0:1] * ys[0]
        for k in range(1, TOP_K):
            ffn = ffn + gts[:, k:k + 1] * ys[k]
        hbuf[pl.ds(t0, grp), :] = DEEPNORM_ALPHA * x1g + ffn
        return carry

    lax.fori_loop(0, ts // grp, group, 0)

    @pl.when(i == pl.num_programs(0) - 1)
    def _():
        wait_tile(nxt)

    x2 = _layer_norm(hbuf[...], g2_ref[...], b2_ref[...])
    z = jnp.dot(x2.astype(_BF16), gw_ref[...], preferred_element_type=_F32) + gb_ref[...]
    ple = jnp.dot(p_ref[...].astype(_BF16), pw_ref[...], preferred_element_type=_F32)
    x3 = _layer_norm(DEEPNORM_ALPHA * x2 + jax.nn.sigmoid(z) * ple, g3_ref[...], b3_ref[...])
    out_ref[...] = x3


def _combine(dest_flat, yb, x1, gates_t, p2d, g2, b2, pw, gw, gb, g3, b3):
    n = dest_flat.shape[0] // TOP_K
    ts = TS_COMBINE
    last = n // ts - 1
    full = lambda shape: pl.BlockSpec(shape, lambda i: (0,) * len(shape))
    return pl.pallas_call(
        _combine_kernel,
        grid=(n // ts,),
        in_specs=[
            pl.BlockSpec((TOP_K * ts,), lambda i: (i,), memory_space=pltpu.SMEM),
            pl.BlockSpec((TOP_K * ts,), lambda i: (jnp.minimum(i + 1, last),), memory_space=pltpu.SMEM),
            pl.BlockSpec(memory_space=pl.ANY),
            pl.BlockSpec((ts, D_MODEL), lambda i: (i, 0)),
            pl.BlockSpec((ts, TOP_K), lambda i: (i, 0)),
            pl.BlockSpec((ts, PLE_DIM), lambda i: (i, 0)),
            full(g2.shape), full(b2.shape), full(pw.shape), full(gw.shape), full(gb.shape),
            full(g3.shape), full(b3.shape),
        ],
        out_specs=pl.BlockSpec((ts, D_MODEL), lambda i: (i, 0)),
        out_shape=jax.ShapeDtypeStruct((n, D_MODEL), _F32),
        scratch_shapes=[
            pltpu.VMEM((2, TOP_K, ts * GATHER_PITCH, LANES), _F32),
            pltpu.VMEM((ts, D_MODEL), _F32),
            pltpu.SemaphoreType.DMA((2,)),
        ],
        compiler_params=pltpu.CompilerParams(
            dimension_semantics=("arbitrary",),
            vmem_limit_bytes=48 * 1024 * 1024),
        name="combine",
    )(dest_flat, dest_flat, yb, x1, gates_t, p2d, g2, b2, pw, gw, gb, g3, b3)


def kernel(x, p, w_in, pool_mix, pool_scale, conv_w, w_out, ln1_g, ln1_b, router_w, router_b,
           w_gate_up, b_gate_up, w_down, b_down, ln2_g, ln2_b, ple_proj, ple_gate_w, ple_gate_b,
           ln3_g, ln3_b):
    assert DEPTH == 1 and x.shape[-1] == D_MODEL
    bsz, seq, d = x.shape
    n = bsz * seq
    assert seq % TS_MIX == 0 and n % TS_COMBINE == 0
    row = lambda v: v.reshape(1, -1)

    tri = jnp.triu(jnp.ones((TS_MIX, TS_MIX), _BF16))
    low = jnp.tril(jnp.ones((N_EXPERTS, N_EXPERTS), _BF16), k=-1)
    tm = TM_EXPERT
    n_slots = n * TOP_K + N_EXPERTS * tm
    nb = n_slots // tm
    x1, dest, gates, counts, table, xs = _mix_route(
        x, w_in[0].astype(_BF16), pool_mix[0].astype(_BF16), row(pool_scale[0]), conv_w[0],
        w_out[0].astype(_BF16), row(ln1_g[0]), row(ln1_b[0]),
        router_w[0].T.astype(_BF16), jnp.broadcast_to(router_b[0][:, None], (N_EXPERTS, LANES)), tri, low,
        n_slots)

    cnt = counts[:, 0]
    tiles_e = (cnt + tm - 1) // tm
    tile_end = jnp.cumsum(tiles_e)
    n_valid = tile_end[-1].astype(jnp.int32)
    steps = jnp.arange(nb, dtype=jnp.int32)
    blk_e = jnp.minimum(jnp.sum(steps[:, None] >= tile_end[None, :], axis=1), N_EXPERTS - 1)
    blk_e = jnp.where(steps < n_valid, blk_e, blk_e[n_valid - 1]).astype(jnp.int32)
    within = steps - (tile_end - tiles_e)[blk_e]
    phys = jnp.where(steps < n_valid, table[blk_e, jnp.clip(within, 0, LANES - 1)], steps).astype(jnp.int32)

    padded = tiles_e
    e_ids = jnp.arange(N_EXPERTS, dtype=jnp.int32)
    later = jnp.logical_and(e_ids[None, :] > e_ids[:, None], (padded > 0)[None, :])
    next_nonempty = jnp.min(jnp.where(later, e_ids[None, :], N_EXPERTS), axis=1)
    next_e = jnp.where(next_nonempty == N_EXPERTS, -1, next_nonempty).astype(jnp.int32)
    switched = jnp.concatenate([jnp.zeros((1,), jnp.int32), (blk_e[1:] != blk_e[:-1]).astype(jnp.int32)])
    slot_par = (jnp.cumsum(switched) % 2).astype(jnp.int32)
    n_valid = n_valid.reshape(1)

    dest = dest[:TOP_K].T.reshape(-1)
    yb = _experts(blk_e, n_valid, slot_par, next_e, phys, xs,
                  w_gate_up[0], b_gate_up[0][:, None, :], w_down[0], b_down[0][:, None, :])
    out = _combine(dest, yb, x1, gates[:TOP_K].T, p[0].reshape(n, PLE_DIM),
                   row(ln2_g[0]), row(ln2_b[0]), ple_proj[0].astype(_BF16),
                   ple_gate_w[0].astype(_BF16), row(ple_gate_b[0]), row(ln3_g[0]), row(ln3_b[0]))
    return out.reshape(bsz, seq, d)
```

```python
import functools

import jax
import jax.numpy as jnp
from jax import lax
from jax.experimental import pallas as pl
from jax.experimental.pallas import tpu as pltpu

D_MODEL = 1024
D_POOL = 512
D_CONV = 512
POOL_WINDOWS = (2, 4, 8, 16)
POOL_GROUP_DIM = 128
CONV_WIDTH = 3
D_IN_PROJ = D_POOL + 3 * D_CONV
N_EXPERTS = 32
TOP_K = 4
D_FF = 1024
SWIGLU_LIMIT = 7.0
SWIGLU_ALPHA = 1.702
PLE_DIM = 256
DEPTH = 1
DEEPNORM_ALPHA = (2.0 * DEPTH) ** 0.25
LN_EPS = 1e-5

LANES = 128
SUBLANES = 8
ROW_CHUNKS = D_MODEL // LANES
POOL_HALO = 16
CONV_HALO = 8
GATHER_PITCH = ROW_CHUNKS + 1

TS_MIX = 512
TM_EXPERT = 512
TS_COMBINE = 256
COMBINE_GROUP = 8
EARLY_SEND_NUM, EARLY_SEND_DEN = 3, 4

_BF16 = jnp.bfloat16
_F32 = jnp.float32


def _layer_norm(h, g, b):
    mu = jnp.mean(h, axis=-1, keepdims=True)
    c = h - mu
    var = jnp.mean(c * c, axis=-1, keepdims=True)
    return c * lax.rsqrt(var + LN_EPS) * g + b


def _to_row_tiles(ref, val, pitch=ROW_CHUNKS):
    rows = val.shape[0]
    for c in range(ROW_CHUNKS):
        ref[pl.ds(c, rows, stride=pitch), :] = val[:, c * LANES:(c + 1) * LANES]
    for c in range(ROW_CHUNKS, pitch):
        ref[pl.ds(c, rows, stride=pitch), :] = jnp.zeros((rows, LANES), val.dtype)


def _from_row_tiles(ref, rows, pitch=ROW_CHUNKS):
    return jnp.concatenate(
        [ref[pl.ds(c, rows, stride=pitch), :] for c in range(ROW_CHUNKS)], axis=1)


def _row_tile(ref, r, pitch=ROW_CHUNKS):
    if pitch == ROW_CHUNKS:
        return ref.at[pl.ds(pl.multiple_of(r * ROW_CHUNKS, ROW_CHUNKS), ROW_CHUNKS)]
    return ref.at[pl.ds(r * pitch, ROW_CHUNKS)]


def _mix_route_kernel(ns, x_ref, w_in_ref, pmix_ref, pscale_ref, convw_ref, w_out_ref, g1_ref, b1_ref,
                      rwt_ref, rb_ref, tri_ref, low_ref,
                      x1_ref, dest_ref, gate_ref, cnt_ref, tab_ref, xs_ref,
                      xbuf, cbuf, rbuf, dvm, dsm, carry_v, carry_u, cnt_sc, used_sc, cur_sc, tab_sc,
                      meta_v, meta_s, zbuf, row_sems, dest_sem):
    g = pl.program_id(0)
    n_tiles = pl.num_programs(0) - 1
    ts = TS_MIX
    grp = SUBLANES
    par = lax.rem(g, 2)
    buf = lax.rem(g, 3)
    prev = lax.rem(g + 2, 3)
    tile_rows = ts * GATHER_PITCH

    def wait_rows(q):
        for _ in range(TOP_K):
            pltpu.make_async_copy(xbuf.at[q], xs_ref.at[pl.ds(0, tile_rows)], row_sems.at[q]).wait()

    @pl.when(g >= 3)
    def _():
        wait_rows(buf)

    base = (1 - par) * (TOP_K * ts)

    def start_rows(t0):
        for j in range(grp):
            t = t0 + j
            for k in range(TOP_K):
                slot = dsm[base + k * ts + t]
                pltpu.make_async_copy(xbuf.at[prev, pl.ds(t * GATHER_PITCH, GATHER_PITCH)],
                                      xs_ref.at[pl.ds(slot * GATHER_PITCH, GATHER_PITCH)],
                                      row_sems.at[prev]).start(priority=k % 2)

    def row_loop(finish, send, lo, hi):
        gain = g1_ref[...]
        bias = b1_ref[...]

        def group(r, carry):
            t0 = pl.multiple_of(r * grp, grp)
            if finish:
                c = cbuf[pl.ds(t0, grp), :]
                rstd = rbuf[pl.ds(t0, grp), :]
            if send:
                start_rows(t0)
            if finish:
                rows = xbuf.at[buf]
                for ch in range(ROW_CHUNKS):
                    lanes = slice(ch * LANES, (ch + 1) * LANES)
                    x1 = c[:, lanes] * rstd * gain[:, lanes] + bias[:, lanes]
                    x1_ref[pl.ds(t0, grp), lanes] = x1
                    rows[pl.ds(t0 * GATHER_PITCH + ch, grp, stride=GATHER_PITCH), :] = x1
                rows[pl.ds(t0 * GATHER_PITCH + ROW_CHUNKS, grp, stride=GATHER_PITCH), :] = jnp.zeros((grp, LANES), _F32)
            return carry
        lax.fori_loop(lo, hi, group, 0)

    n_groups = ts // grp
    early = (n_groups * EARLY_SEND_NUM) // EARLY_SEND_DEN

    @pl.when(g >= 1)
    def _():
        for c in _slots_to_scalar(dvm, dsm, 1 - par, dest_sem):
            c.wait()
        row_loop(False, True, 0, early)

    @pl.when(g < n_tiles)
    def _():
        _mix_tile(lax.rem(g, ns), g == 0, x_ref, w_in_ref, pmix_ref, pscale_ref, convw_ref, w_out_ref,
                  cbuf, rbuf, carry_v, carry_u, cnt_sc, used_sc, cur_sc, tab_sc)

    @pl.when(g == 0)
    def _():
        row_loop(True, False, 0, n_groups)

    @pl.when(jnp.logical_and(g >= 1, g < n_tiles))
    def _():
        row_loop(True, False, 0, early)
        row_loop(True, True, early, n_groups)

    @pl.when(g == n_tiles)
    def _():
        row_loop(False, True, early, n_groups)

    @pl.when(g < n_tiles)
    def _():
        _route_tile(par, x1_ref, rwt_ref, rb_ref, tri_ref, low_ref, dest_ref, gate_ref, cnt_ref, tab_ref,
                    dvm, dsm, cnt_sc, used_sc, cur_sc, tab_sc, dest_sem)

    @pl.when(g == n_tiles)
    def _():
        wait_rows(lax.rem(g + 1, 3))
        wait_rows(prev)
        _seal_slots(xs_ref, cnt_sc, tab_sc, meta_v, meta_s, zbuf, dest_sem)


def _slots_to_scalar(dvm, dsm, q, sem):
    ts = TS_MIX
    return [pltpu.make_async_copy(dvm.at[q, k], dsm.at[pl.ds((q * TOP_K + k) * ts, ts)], sem)
            for k in range(TOP_K)]


def _seal_slots(xs_ref, cnt_sc, tab_sc, meta_v, meta_s, zbuf, sem):
    tm = TM_EXPERT
    tile_rows = tm * GATHER_PITCH
    n_tiles = xs_ref.shape[0] // tile_rows
    meta_v[0] = cnt_sc[...].astype(jnp.int32)
    meta_v[1] = tab_sc[...].astype(jnp.int32)
    to_scalar = pltpu.make_async_copy(meta_v, meta_s, sem)
    to_scalar.start()
    to_scalar.wait()
    zbuf[...] = jnp.zeros_like(zbuf)
    pieces = [1 << b for b in reversed(range(tm.bit_length() - 1))]

    def each_copy(fn):
        owned = 0
        for e in range(N_EXPERTS):
            cnt = meta_s[0, e, 0]
            tiles = lax.shift_right_logical(cnt + (tm - 1), tm.bit_length() - 1)
            owned = owned + tiles
            last = meta_s[1, e, jnp.maximum(tiles - 1, 0)]
            fill = cnt - jnp.maximum(tiles - 1, 0) * tm
            spare = tm - fill
            pos = last * tm + fill
            for piece in pieces:
                take = jnp.logical_and(tiles > 0, (spare & piece) != 0)

                @pl.when(take)
                def _():
                    fn(pltpu.make_async_copy(
                        zbuf.at[pl.ds(0, piece * GATHER_PITCH)],
                        xs_ref.at[pl.ds(pos * GATHER_PITCH, piece * GATHER_PITCH)], sem))
                pos = pos + jnp.where((spare & piece) != 0, piece, 0)

        def unowned(j, carry):
            fn(pltpu.make_async_copy(
                zbuf, xs_ref.at[pl.ds(pl.multiple_of(j * tile_rows, tile_rows), tile_rows)], sem))
            return carry
        lax.fori_loop(owned, n_tiles, unowned, 0)

    each_copy(lambda c: c.start())
    each_copy(lambda c: c.wait())


def _mix_tile(s, first, x_ref, w_in_ref, pmix_ref, pscale_ref, convw_ref, w_out_ref, cbuf, rbuf,
              carry_v, carry_u, cnt_sc, used_sc, cur_sc, tab_sc):
    ts = TS_MIX

    @pl.when(first)
    def _():
        cnt_sc[...] = jnp.zeros_like(cnt_sc)
        used_sc[...] = jnp.zeros_like(used_sc)
        cur_sc[...] = jnp.zeros_like(cur_sc)
        tab_sc[...] = jnp.zeros_like(tab_sc)

    @pl.when(s == 0)
    def _():
        carry_v[...] = jnp.zeros_like(carry_v)
        carry_u[...] = jnp.zeros_like(carry_u)

    xb = x_ref[0]
    proj = jnp.dot(xb.astype(_BF16), w_in_ref[...], preferred_element_type=_F32)
    vp = proj[:, :D_POOL]
    bg = proj[:, D_POOL:D_POOL + D_CONV]
    cg = proj[:, D_POOL + D_CONV:D_POOL + 2 * D_CONV]
    vc = proj[:, D_POOL + 2 * D_CONV:]

    ext = jnp.concatenate([carry_v[...], vp], axis=0)
    pos = lax.broadcasted_iota(jnp.int32, (ts, 1), 0) + s * ts
    mixed = []
    for g, w in enumerate(POOL_WINDOWS):
        lo, hi = g * POOL_GROUP_DIM, (g + 1) * POOL_GROUP_DIM
        acc = ext[:, lo:hi]
        sh = 1
        while sh < w:
            acc = acc + pltpu.roll(acc, sh, 0)
            sh *= 2
        cnt = jnp.minimum(pos + 1, w).astype(_F32)
        d = acc[POOL_HALO:] / cnt - vp[:, lo:hi]
        yg = jnp.dot(d.astype(_BF16), pmix_ref[g], preferred_element_type=_F32)
        mixed.append(yg * pscale_ref[:, lo:hi])
    carry_v[...] = vp[ts - POOL_HALO:]

    u = cg * vc
    extu = jnp.concatenate([carry_u[...], u], axis=0)
    u1 = pltpu.roll(extu, 1, 0)[CONV_HALO:]
    u2 = pltpu.roll(extu, 2, 0)[CONV_HALO:]
    yc = bg * (convw_ref[0:1, :] * u2 + convw_ref[1:2, :] * u1 + convw_ref[2:3, :] * u)
    carry_u[...] = u[ts - CONV_HALO:]
    mixed.append(yc)

    mix_in = jnp.concatenate(mixed, axis=1).astype(_BF16)
    mix = jnp.dot(mix_in, w_out_ref[...], preferred_element_type=_F32)
    h = DEEPNORM_ALPHA * xb + mix
    c = h - jnp.mean(h, axis=-1, keepdims=True)
    cbuf[...] = c
    rbuf[...] = jnp.broadcast_to(lax.rsqrt(jnp.mean(c * c, axis=-1, keepdims=True) + LN_EPS), rbuf.shape)


def _route_tile(par, x1_ref, rwt_ref, rb_ref, tri_ref, low_ref, dest_ref, gate_ref, cnt_ref, tab_ref,
                dvm, dsm, cnt_sc, used_sc, cur_sc, tab_sc, dest_sem):
    ts = TS_MIX
    x1 = x1_ref[...]

    logits = lax.dot_general(rwt_ref[...], x1.astype(_BF16), (((1,), (1,)), ((), ())),
                             preferred_element_type=_F32) + rb_ref[:, 0:1]
    eidx = lax.broadcasted_iota(jnp.int32, (N_EXPERTS, ts), 0).astype(_F32)
    vals, sels = [], []
    work = logits
    for k in range(TOP_K):
        m = jnp.max(work, axis=0, keepdims=True)
        first = jnp.min(jnp.where(work == m, eidx, float(N_EXPERTS)), axis=0, keepdims=True)
        sel = eidx == first
        work = jnp.where(sel, -jnp.inf, work)
        vals.append(m)
        sels.append(sel)
    exps = [jnp.exp(v - vals[0]) for v in vals]
    denom = exps[0] + exps[1] + exps[2] + exps[3]
    for k in range(TOP_K):
        gate_ref[k:k + 1, :] = exps[k] / denom
    gate_ref[TOP_K:, :] = jnp.zeros((SUBLANES - TOP_K, ts), _F32)

    tm = float(TM_EXPERT)
    chosen = jnp.zeros((N_EXPERTS, ts), _F32)
    for sel in sels:
        chosen = chosen + sel.astype(_F32)
    incl = jnp.dot(chosen.astype(_BF16), tri_ref[...], preferred_element_type=_F32)
    cnt_b = cnt_sc[:, 0:1]
    cnt_a = cnt_b + jnp.sum(chosen, axis=1, keepdims=True)
    q_b = jnp.floor((cnt_b + (tm - 1.0)) * (1.0 / tm))
    q_a = jnp.floor((cnt_a + (tm - 1.0)) * (1.0 / tm))
    n_new = q_a - q_b
    before = jnp.dot(low_ref[...], jnp.broadcast_to(n_new, (N_EXPERTS, LANES)).astype(_BF16),
                     preferred_element_type=_F32)[:, 0:1]
    used = used_sc[:, 0:1]
    new_tile = used + before
    cur_tile = cur_sc[:, 0:1]
    rank = cnt_b + (incl - chosen)
    page = jnp.floor(rank * (1.0 / tm))
    slot = jnp.where(page < q_b, cur_tile, new_tile) * tm + (rank - page * tm)
    for k in range(TOP_K):
        dest_k = jnp.sum(jnp.where(sels[k], slot, 0.0), axis=0, keepdims=True).astype(jnp.int32)
        dest_ref[k:k + 1, :] = dest_k
        dvm[par, k:k + 1, :] = dest_k
    dest_ref[TOP_K:, :] = jnp.zeros((SUBLANES - TOP_K, ts), jnp.int32)

    took = n_new > 0.0
    lane = lax.broadcasted_iota(jnp.int32, (N_EXPERTS, LANES), 1).astype(_F32)
    table = jnp.where(jnp.logical_and(lane == q_b, took), new_tile, tab_sc[...])
    tab_sc[...] = table
    cur_sc[...] = jnp.broadcast_to(jnp.where(took, new_tile, cur_tile), (N_EXPERTS, LANES))
    used_sc[...] = jnp.broadcast_to(used + jnp.sum(n_new, axis=0, keepdims=True), (N_EXPERTS, LANES))
    cnt_sc[...] = jnp.broadcast_to(cnt_a, (N_EXPERTS, LANES))
    cnt_ref[...] = jnp.broadcast_to(cnt_a, (N_EXPERTS, LANES)).astype(jnp.int32)
    tab_ref[...] = table.astype(jnp.int32)

    for c in _slots_to_scalar(dvm, dsm, par, dest_sem):
        c.start()


def _mix_route(x, w_in, pmix, pscale, convw, w_out, g1, b1, rwt, rb, tri, low, n_slots):
    bsz, seq, d = x.shape
    n = bsz * seq
    ns = seq // TS_MIX
    n_tiles = bsz * ns
    assert n // TM_EXPERT <= LANES
    full = lambda shape: pl.BlockSpec(shape, lambda g: (0,) * len(shape))
    tile = lambda g: jnp.minimum(g, n_tiles - 1)
    state = pltpu.VMEM((N_EXPERTS, LANES), _F32)
    return pl.pallas_call(
        functools.partial(_mix_route_kernel, ns),
        grid=(n_tiles + 1,),
        in_specs=[
            pl.BlockSpec((1, TS_MIX, d), lambda g: (tile(g) // ns, tile(g) % ns, 0)),
            full(w_in.shape), full(pmix.shape), full(pscale.shape), full(convw.shape),
            full(w_out.shape), full(g1.shape), full(b1.shape), full(rwt.shape), full(rb.shape),
            full(tri.shape), full(low.shape),
        ],
        out_specs=[
            pl.BlockSpec((TS_MIX, D_MODEL), lambda g: (tile(g), 0)),
            pl.BlockSpec((SUBLANES, TS_MIX), lambda g: (0, tile(g))),
            pl.BlockSpec((SUBLANES, TS_MIX), lambda g: (0, tile(g))),
            pl.BlockSpec((N_EXPERTS, LANES), lambda g: (0, 0)),
            pl.BlockSpec((N_EXPERTS, LANES), lambda g: (0, 0)),
            pl.BlockSpec(memory_space=pl.ANY),
        ],
        out_shape=[
            jax.ShapeDtypeStruct((n, D_MODEL), _F32),
            jax.ShapeDtypeStruct((SUBLANES, n), jnp.int32),
            jax.ShapeDtypeStruct((SUBLANES, n), _F32),
            jax.ShapeDtypeStruct((N_EXPERTS, LANES), jnp.int32),
            jax.ShapeDtypeStruct((N_EXPERTS, LANES), jnp.int32),
            jax.ShapeDtypeStruct((n_slots * GATHER_PITCH, LANES), _F32),
        ],
        scratch_shapes=[
            pltpu.VMEM((3, TS_MIX * GATHER_PITCH, LANES), _F32),
            pltpu.VMEM((TS_MIX, D_MODEL), _F32),
            pltpu.VMEM((TS_MIX, LANES), _F32),
            pltpu.VMEM((2, SUBLANES, TS_MIX), jnp.int32),
            pltpu.SMEM((2 * TOP_K * TS_MIX,), jnp.int32),
            pltpu.VMEM((POOL_HALO, D_POOL), _F32),
            pltpu.VMEM((CONV_HALO, D_CONV), _F32),
            state, state, state, state,
            pltpu.VMEM((2, N_EXPERTS, LANES), jnp.int32),
            pltpu.SMEM((2, N_EXPERTS, LANES), jnp.int32),
            pltpu.VMEM((TM_EXPERT * GATHER_PITCH, LANES), _F32),
            pltpu.SemaphoreType.DMA((3,)),
            pltpu.SemaphoreType.DMA(()),
        ],
        compiler_params=pltpu.CompilerParams(
            dimension_semantics=("arbitrary",),
            vmem_limit_bytes=48 * 1024 * 1024),
        name="mix_route",
    )(x, w_in, pmix, pscale, convw, w_out, g1, b1, rwt, rb, tri, low)


def _experts_kernel(be_ref, nv_ref, par_ref, nxt_ref, phys_ref, xs_ref, wgu_hbm, bgu_ref, wdn_hbm, bdn_ref, out_ref,
                    wgu_f32, wdn_f32, wgu_bf, wdn_bf, sems):
    j = pl.program_id(0)
    tm = TM_EXPERT
    valid = j < nv_ref[0]
    new_expert = jnp.logical_or(j == 0, be_ref[j] != be_ref[jnp.maximum(j - 1, 0)])

    def weight_copies(e, s):
        return (pltpu.make_async_copy(wgu_hbm.at[e], wgu_f32.at[s], sems.at[0, s]),
                pltpu.make_async_copy(wdn_hbm.at[e], wdn_f32.at[s], sems.at[1, s]))

    @pl.when(j == 0)
    def _():
        for c in weight_copies(be_ref[0], par_ref[0]):
            c.start()

    @pl.when(jnp.logical_and(valid, new_expert))
    def _():
        s = par_ref[j]
        for c in weight_copies(be_ref[j], s):
            c.wait()

        nxt = nxt_ref[be_ref[j]]

        @pl.when(nxt >= 0)
        def _():
            for c in weight_copies(nxt, 1 - s):
                c.start()

        wgu_bf[...] = wgu_f32[s].astype(_BF16)
        wdn_bf[...] = wdn_f32[s].astype(_BF16)

    @pl.when(valid)
    def _():
        x = _from_row_tiles(xs_ref, tm, GATHER_PITCH).astype(_BF16)
        gu = jnp.dot(x, wgu_bf[...], preferred_element_type=_F32) + bgu_ref[0]
        gate = jnp.minimum(gu[:, :D_FF], SWIGLU_LIMIT)
        up = jnp.clip(gu[:, D_FF:], -SWIGLU_LIMIT, SWIGLU_LIMIT)
        glu = gate * jax.nn.sigmoid(SWIGLU_ALPHA * gate)
        act = ((up + 1.0) * glu).astype(_BF16)
        y = jnp.dot(act, wdn_bf[...], preferred_element_type=_F32) + bdn_ref[0]
        _to_row_tiles(out_ref, y, GATHER_PITCH)

    @pl.when(j >= nv_ref[0])
    def _():
        out_ref[...] = jnp.zeros_like(out_ref)


def _experts(blk_e, n_valid, slot_par, next_e, phys, xs, wgu, bgu, wdn, bdn):
    n_slots = xs.shape[0] // GATHER_PITCH
    nb = n_slots // TM_EXPERT
    rows = TM_EXPERT * GATHER_PITCH
    grid_spec = pltpu.PrefetchScalarGridSpec(
        num_scalar_prefetch=5,
        grid=(nb,),
        in_specs=[
            pl.BlockSpec((rows, LANES), lambda j, be, nv, par, nxt, ph: (ph[jnp.minimum(j, nv[0] - 1)], 0)),
            pl.BlockSpec(memory_space=pl.ANY),
            pl.BlockSpec((1, 1, 2 * D_FF), lambda j, be, nv, par, nxt, ph: (be[j], 0, 0)),
            pl.BlockSpec(memory_space=pl.ANY),
            pl.BlockSpec((1, 1, D_MODEL), lambda j, be, nv, par, nxt, ph: (be[j], 0, 0)),
        ],
        out_specs=pl.BlockSpec((rows, LANES), lambda j, be, nv, par, nxt, ph: (ph[j], 0)),
        scratch_shapes=[
            pltpu.VMEM((2, D_MODEL, 2 * D_FF), _F32),
            pltpu.VMEM((2, D_FF, D_MODEL), _F32),
            pltpu.VMEM((D_MODEL, 2 * D_FF), _BF16),
            pltpu.VMEM((D_FF, D_MODEL), _BF16),
            pltpu.SemaphoreType.DMA((2, 2)),
        ],
    )
    return pl.pallas_call(
        _experts_kernel,
        grid_spec=grid_spec,
        out_shape=jax.ShapeDtypeStruct((n_slots * GATHER_PITCH, LANES), _F32),
        compiler_params=pltpu.CompilerParams(
            dimension_semantics=("arbitrary",),
            vmem_limit_bytes=60 * 1024 * 1024),
        name="experts",
    )(blk_e, n_valid, slot_par, next_e, phys, xs, wgu, bgu, wdn, bdn)


def _combine_kernel(dcur_ref, dnext_ref, yb_ref, x1_ref, gt_ref, p_ref, g2_ref, b2_ref, pw_ref, gw_ref,
                    gb_ref, g3_ref, b3_ref, out_ref, gbuf, hbuf, sems):
    ts = TS_COMBINE
    grp = COMBINE_GROUP
    i = pl.program_id(0)
    slot = i % 2
    nxt = 1 - slot

    def start_rows(d_ref, s, t0):
        for j in range(grp):
            t = t0 + j
            for k in range(TOP_K):
                pltpu.make_async_copy(_row_tile(yb_ref, d_ref[t * TOP_K + k], GATHER_PITCH),
                                      gbuf.at[s, k, pl.ds(t * GATHER_PITCH, ROW_CHUNKS)],
                                      sems.at[s]).start(priority=k % 2)

    def wait_tile(s):
        for k in range(TOP_K):
            pltpu.make_async_copy(yb_ref.at[pl.ds(0, ts * ROW_CHUNKS)],
                                  gbuf.at[s, k, pl.ds(0, ts * ROW_CHUNKS)], sems.at[s]).wait()

    @pl.when(i == 0)
    def _():
        def first(r, carry):
            start_rows(dcur_ref, 0, r * grp)
            return carry
        lax.fori_loop(0, ts // grp, first, 0)

    wait_tile(slot)

    def group(r, carry):
        t0 = pl.multiple_of(r * grp, grp)
        x1g = x1_ref[pl.ds(t0, grp), :]
        gts = gt_ref[pl.ds(t0, grp), :]
        ys = []
        for k in range(TOP_K):
            rows_k = gbuf.at[slot, k]
            ys.append(jnp.concatenate(
                [rows_k[pl.ds(t0 * GATHER_PITCH + c, grp, stride=GATHER_PITCH), :] for c in range(ROW_CHUNKS)],
                axis=1))
        start_rows(dnext_ref, nxt, t0)
        ffn = gts[:, 0:1] * ys[0]
        for k in range(1, TOP_K):
            ffn = ffn + gts[:, k:k + 1] * ys[k]
        hbuf[pl.ds(t0, grp), :] = DEEPNORM_ALPHA * x1g + ffn
        return carry

    lax.fori_loop(0, ts // grp, group, 0)

    @pl.when(i == pl.num_programs(0) - 1)
    def _():
        wait_tile(nxt)

    x2 = _layer_norm(hbuf[...], g2_ref[...], b2_ref[...])
    z = jnp.dot(x2.astype(_BF16), gw_ref[...], preferred_element_type=_F32) + gb_ref[...]
    ple = jnp.dot(p_ref[...].astype(_BF16), pw_ref[...], preferred_element_type=_F32)
    x3 = _layer_norm(DEEPNORM_ALPHA * x2 + jax.nn.sigmoid(z) * ple, g3_ref[...], b3_ref[...])
    out_ref[...] = x3


def _combine(dest_flat, yb, x1, gates_t, p2d, g2, b2, pw, gw, gb, g3, b3):
    n = dest_flat.shape[0] // TOP_K
    ts = TS_COMBINE
    last = n // ts - 1
    full = lambda shape: pl.BlockSpec(shape, lambda i: (0,) * len(shape))
    return pl.pallas_call(
        _combine_kernel,
        grid=(n // ts,),
        in_specs=[
            pl.BlockSpec((TOP_K * ts,), lambda i: (i,), memory_space=pltpu.SMEM),
            pl.BlockSpec((TOP_K * ts,), lambda i: (jnp.minimum(i + 1, last),), memory_space=pltpu.SMEM),
            pl.BlockSpec(memory_space=pl.ANY),
            pl.BlockSpec((ts, D_MODEL), lambda i: (i, 0)),
            pl.BlockSpec((ts, TOP_K), lambda i: (i, 0)),
            pl.BlockSpec((ts, PLE_DIM), lambda i: (i, 0)),
            full(g2.shape), full(b2.shape), full(pw.shape), full(gw.shape), full(gb.shape),
            full(g3.shape), full(b3.shape),
        ],
        out_specs=pl.BlockSpec((ts, D_MODEL), lambda i: (i, 0)),
        out_shape=jax.ShapeDtypeStruct((n, D_MODEL), _F32),
        scratch_shapes=[
            pltpu.VMEM((2, TOP_K, ts * GATHER_PITCH, LANES), _F32),
            pltpu.VMEM((ts, D_MODEL), _F32),
            pltpu.SemaphoreType.DMA((2,)),
        ],
        compiler_params=pltpu.CompilerParams(
            dimension_semantics=("arbitrary",),
            vmem_limit_bytes=48 * 1024 * 1024),
        name="combine",
    )(dest_flat, dest_flat, yb, x1, gates_t, p2d, g2, b2, pw, gw, gb, g3, b3)


def kernel(x, p, w_in, pool_mix, pool_scale, conv_w, w_out, ln1_g, ln1_b, router_w, router_b,
           w_gate_up, b_gate_up, w_down, b_down, ln2_g, ln2_b, ple_proj, ple_gate_w, ple_gate_b,
           ln3_g, ln3_b):
    assert DEPTH == 1 and x.shape[-1] == D_MODEL
    bsz, seq, d = x.shape
    n = bsz * seq
    assert seq % TS_MIX == 0 and n % TS_COMBINE == 0
    row = lambda v: v.reshape(1, -1)

    tri = jnp.triu(jnp.ones((TS_MIX, TS_MIX), _BF16))
    low = jnp.tril(jnp.ones((N_EXPERTS, N_EXPERTS), _BF16), k=-1)
    tm = TM_EXPERT
    n_slots = n * TOP_K + N_EXPERTS * tm
    nb = n_slots // tm
    x1, dest, gates, counts, table, xs = _mix_route(
        x, w_in[0].astype(_BF16), pool_mix[0].astype(_BF16), row(pool_scale[0]), conv_w[0],
        w_out[0].astype(_BF16), row(ln1_g[0]), row(ln1_b[0]),
        router_w[0].T.astype(_BF16), jnp.broadcast_to(router_b[0][:, None], (N_EXPERTS, LANES)), tri, low,
        n_slots)

    cnt = counts[:, 0]
    tiles_e = (cnt + tm - 1) // tm
    tile_end = jnp.cumsum(tiles_e)
    n_valid = tile_end[-1].astype(jnp.int32)
    steps = jnp.arange(nb, dtype=jnp.int32)
    blk_e = jnp.minimum(jnp.sum(steps[:, None] >= tile_end[None, :], axis=1), N_EXPERTS - 1)
    blk_e = jnp.where(steps < n_valid, blk_e, blk_e[n_valid - 1]).astype(jnp.int32)
    within = steps - (tile_end - tiles_e)[blk_e]
    phys = jnp.where(steps < n_valid, table[blk_e, jnp.clip(within, 0, LANES - 1)], steps).astype(jnp.int32)

    padded = tiles_e
    e_ids = jnp.arange(N_EXPERTS, dtype=jnp.int32)
    later = jnp.logical_and(e_ids[None, :] > e_ids[:, None], (padded > 0)[None, :])
    next_nonempty = jnp.min(jnp.where(later, e_ids[None, :], N_EXPERTS), axis=1)
    next_e = jnp.where(next_nonempty == N_EXPERTS, -1, next_nonempty).astype(jnp.int32)
    switched = jnp.concatenate([jnp.zeros((1,), jnp.int32), (blk_e[1:] != blk_e[:-1]).astype(jnp.int32)])
    slot_par = (jnp.cumsum(switched) % 2).astype(jnp.int32)
    n_valid = n_valid.reshape(1)

    dest = dest[:TOP_K].T.reshape(-1)
    yb = _experts(blk_e, n_valid, slot_par, next_e, phys, xs,
                  w_gate_up[0], b_gate_up[0][:, None, :], w_down[0], b_down[0][:, None, :])
    out = _combine(dest, yb, x1, gates[:TOP_K].T, p[0].reshape(n, PLE_DIM),
                   row(ln2_g[0]), row(ln2_b[0]), ple_proj[0].astype(_BF16),
                   ple_gate_w[0].astype(_BF16), row(ple_gate_b[0]), row(ln3_g[0]), row(ln3_b[0]))
    return out.reshape(bsz, seq, d)
```

```python
import functools

import jax
import jax.numpy as jnp
from jax import lax
from jax.experimental import pallas as pl
from jax.experimental.pallas import tpu as pltpu

D_MODEL = 1024
D_POOL = 512
D_CONV = 512
POOL_WINDOWS = (2, 4, 8, 16)
POOL_GROUP_DIM = 128
CONV_WIDTH = 3
D_IN_PROJ = D_POOL + 3 * D_CONV
N_EXPERTS = 32
TOP_K = 4
D_FF = 1024
SWIGLU_LIMIT = 7.0
SWIGLU_ALPHA = 1.702
PLE_DIM = 256
DEPTH = 1
DEEPNORM_ALPHA = (2.0 * DEPTH) ** 0.25
LN_EPS = 1e-5

LANES = 128
SUBLANES = 8
ROW_CHUNKS = D_MODEL // LANES
POOL_HALO = 16
CONV_HALO = 8
GATHER_PITCH = ROW_CHUNKS + 1

TS_MIX = 512
TM_EXPERT = 512
TS_COMBINE = 256
COMBINE_GROUP = 8

_BF16 = jnp.bfloat16
_F32 = jnp.float32


def _layer_norm(h, g, b):
    mu = jnp.mean(h, axis=-1, keepdims=True)
    c = h - mu
    var = jnp.mean(c * c, axis=-1, keepdims=True)
    return c * lax.rsqrt(var + LN_EPS) * g + b


def _to_row_tiles(ref, val, pitch=ROW_CHUNKS):
    rows = val.shape[0]
    for c in range(ROW_CHUNKS):
        ref[pl.ds(c, rows, stride=pitch), :] = val[:, c * LANES:(c + 1) * LANES]
    for c in range(ROW_CHUNKS, pitch):
        ref[pl.ds(c, rows, stride=pitch), :] = jnp.zeros((rows, LANES), val.dtype)


def _from_row_tiles(ref, rows, pitch=ROW_CHUNKS):
    return jnp.concatenate(
        [ref[pl.ds(c, rows, stride=pitch), :] for c in range(ROW_CHUNKS)], axis=1)


def _row_tile(ref, r, pitch=ROW_CHUNKS):
    if pitch == ROW_CHUNKS:
        return ref.at[pl.ds(pl.multiple_of(r * ROW_CHUNKS, ROW_CHUNKS), ROW_CHUNKS)]
    return ref.at[pl.ds(r * pitch, ROW_CHUNKS)]


def _mix_route_kernel(ns, x_ref, w_in_ref, pmix_ref, pscale_ref, convw_ref, w_out_ref, g1_ref, b1_ref,
                      rwt_ref, rb_ref, tri_ref, low_ref,
                      x1_ref, dest_ref, gate_ref, cnt_ref, tab_ref, xs_ref,
                      xbuf, cbuf, rbuf, dvm, dsm, carry_v, carry_u, cnt_sc, used_sc, cur_sc, tab_sc,
                      meta_v, meta_s, zbuf, row_sems, dest_sem):
    g = pl.program_id(0)
    n_tiles = pl.num_programs(0) - 1
    ts = TS_MIX
    grp = SUBLANES
    par = lax.rem(g, 2)
    buf = lax.rem(g, 3)
    prev = lax.rem(g + 2, 3)
    tile_rows = ts * GATHER_PITCH

    def wait_rows(q):
        for _ in range(TOP_K):
            pltpu.make_async_copy(xbuf.at[q], xs_ref.at[pl.ds(0, tile_rows)], row_sems.at[q]).wait()

    @pl.when(g >= 3)
    def _():
        wait_rows(buf)

    @pl.when(g >= 1)
    def _():
        for c in _slots_to_scalar(dvm, dsm, 1 - par, dest_sem):
            c.wait()

    @pl.when(g < n_tiles)
    def _():
        _mix_tile(lax.rem(g, ns), g == 0, x_ref, w_in_ref, pmix_ref, pscale_ref, convw_ref, w_out_ref,
                  cbuf, rbuf, carry_v, carry_u, cnt_sc, used_sc, cur_sc, tab_sc)

    base = (1 - par) * (TOP_K * ts)

    def start_rows(t0):
        for j in range(grp):
            t = t0 + j
            for k in range(TOP_K):
                slot = dsm[base + k * ts + t]
                pltpu.make_async_copy(xbuf.at[prev, pl.ds(t * GATHER_PITCH, GATHER_PITCH)],
                                      xs_ref.at[pl.ds(slot * GATHER_PITCH, GATHER_PITCH)],
                                      row_sems.at[prev]).start(priority=k % 2)

    def row_loop(finish, send):
        gain = g1_ref[...]
        bias = b1_ref[...]

        def group(r, carry):
            t0 = pl.multiple_of(r * grp, grp)
            if finish:
                c = cbuf[pl.ds(t0, grp), :]
                rstd = rbuf[pl.ds(t0, grp), :]
            if send:
                start_rows(t0)
            if finish:
                rows = xbuf.at[buf]
                for ch in range(ROW_CHUNKS):
                    lanes = slice(ch * LANES, (ch + 1) * LANES)
                    x1 = c[:, lanes] * rstd * gain[:, lanes] + bias[:, lanes]
                    x1_ref[pl.ds(t0, grp), lanes] = x1
                    rows[pl.ds(t0 * GATHER_PITCH + ch, grp, stride=GATHER_PITCH), :] = x1
                rows[pl.ds(t0 * GATHER_PITCH + ROW_CHUNKS, grp, stride=GATHER_PITCH), :] = jnp.zeros((grp, LANES), _F32)
            return carry
        lax.fori_loop(0, ts // grp, group, 0)

    @pl.when(g == 0)
    def _():
        row_loop(finish=True, send=False)

    @pl.when(jnp.logical_and(g >= 1, g < n_tiles))
    def _():
        row_loop(finish=True, send=True)

    @pl.when(g == n_tiles)
    def _():
        row_loop(finish=False, send=True)

    @pl.when(g < n_tiles)
    def _():
        _route_tile(par, x1_ref, rwt_ref, rb_ref, tri_ref, low_ref, dest_ref, gate_ref, cnt_ref, tab_ref,
                    dvm, dsm, cnt_sc, used_sc, cur_sc, tab_sc, dest_sem)

    @pl.when(g == n_tiles)
    def _():
        wait_rows(lax.rem(g + 1, 3))
        wait_rows(prev)
        _seal_slots(xs_ref, cnt_sc, tab_sc, meta_v, meta_s, zbuf, dest_sem)


def _slots_to_scalar(dvm, dsm, q, sem):
    ts = TS_MIX
    return [pltpu.make_async_copy(dvm.at[q, k], dsm.at[pl.ds((q * TOP_K + k) * ts, ts)], sem)
            for k in range(TOP_K)]


def _seal_slots(xs_ref, cnt_sc, tab_sc, meta_v, meta_s, zbuf, sem):
    tm = TM_EXPERT
    tile_rows = tm * GATHER_PITCH
    n_tiles = xs_ref.shape[0] // tile_rows
    meta_v[0] = cnt_sc[...].astype(jnp.int32)
    meta_v[1] = tab_sc[...].astype(jnp.int32)
    to_scalar = pltpu.make_async_copy(meta_v, meta_s, sem)
    to_scalar.start()
    to_scalar.wait()
    zbuf[...] = jnp.zeros_like(zbuf)
    pieces = [1 << b for b in reversed(range(tm.bit_length() - 1))]

    def each_copy(fn):
        owned = 0
        for e in range(N_EXPERTS):
            cnt = meta_s[0, e, 0]
            tiles = lax.shift_right_logical(cnt + (tm - 1), tm.bit_length() - 1)
            owned = owned + tiles
            last = meta_s[1, e, jnp.maximum(tiles - 1, 0)]
            fill = cnt - jnp.maximum(tiles - 1, 0) * tm
            spare = tm - fill
            pos = last * tm + fill
            for piece in pieces:
                take = jnp.logical_and(tiles > 0, (spare & piece) != 0)

                @pl.when(take)
                def _():
                    fn(pltpu.make_async_copy(
                        zbuf.at[pl.ds(0, piece * GATHER_PITCH)],
                        xs_ref.at[pl.ds(pos * GATHER_PITCH, piece * GATHER_PITCH)], sem))
                pos = pos + jnp.where((spare & piece) != 0, piece, 0)

        def unowned(j, carry):
            fn(pltpu.make_async_copy(
                zbuf, xs_ref.at[pl.ds(pl.multiple_of(j * tile_rows, tile_rows), tile_rows)], sem))
            return carry
        lax.fori_loop(owned, n_tiles, unowned, 0)

    each_copy(lambda c: c.start())
    each_copy(lambda c: c.wait())


def _mix_tile(s, first, x_ref, w_in_ref, pmix_ref, pscale_ref, convw_ref, w_out_ref, cbuf, rbuf,
              carry_v, carry_u, cnt_sc, used_sc, cur_sc, tab_sc):
    ts = TS_MIX

    @pl.when(first)
    def _():
        cnt_sc[...] = jnp.zeros_like(cnt_sc)
        used_sc[...] = jnp.zeros_like(used_sc)
        cur_sc[...] = jnp.zeros_like(cur_sc)
        tab_sc[...] = jnp.zeros_like(tab_sc)

    @pl.when(s == 0)
    def _():
        carry_v[...] = jnp.zeros_like(carry_v)
        carry_u[...] = jnp.zeros_like(carry_u)

    xb = x_ref[0]
    proj = jnp.dot(xb.astype(_BF16), w_in_ref[...], preferred_element_type=_F32)
    vp = proj[:, :D_POOL]
    bg = proj[:, D_POOL:D_POOL + D_CONV]
    cg = proj[:, D_POOL + D_CONV:D_POOL + 2 * D_CONV]
    vc = proj[:, D_POOL + 2 * D_CONV:]

    ext = jnp.concatenate([carry_v[...], vp], axis=0)
    pos = lax.broadcasted_iota(jnp.int32, (ts, 1), 0) + s * ts
    mixed = []
    for g, w in enumerate(POOL_WINDOWS):
        lo, hi = g * POOL_GROUP_DIM, (g + 1) * POOL_GROUP_DIM
        acc = ext[:, lo:hi]
        sh = 1
        while sh < w:
            acc = acc + pltpu.roll(acc, sh, 0)
            sh *= 2
        cnt = jnp.minimum(pos + 1, w).astype(_F32)
        d = acc[POOL_HALO:] / cnt - vp[:, lo:hi]
        yg = jnp.dot(d.astype(_BF16), pmix_ref[g], preferred_element_type=_F32)
        mixed.append(yg * pscale_ref[:, lo:hi])
    carry_v[...] = vp[ts - POOL_HALO:]

    u = cg * vc
    extu = jnp.concatenate([carry_u[...], u], axis=0)
    u1 = pltpu.roll(extu, 1, 0)[CONV_HALO:]
    u2 = pltpu.roll(extu, 2, 0)[CONV_HALO:]
    yc = bg * (convw_ref[0:1, :] * u2 + convw_ref[1:2, :] * u1 + convw_ref[2:3, :] * u)
    carry_u[...] = u[ts - CONV_HALO:]
    mixed.append(yc)

    mix_in = jnp.concatenate(mixed, axis=1).astype(_BF16)
    mix = jnp.dot(mix_in, w_out_ref[...], preferred_element_type=_F32)
    h = DEEPNORM_ALPHA * xb + mix
    c = h - jnp.mean(h, axis=-1, keepdims=True)
    cbuf[...] = c
    rbuf[...] = jnp.broadcast_to(lax.rsqrt(jnp.mean(c * c, axis=-1, keepdims=True) + LN_EPS), rbuf.shape)


def _route_tile(par, x1_ref, rwt_ref, rb_ref, tri_ref, low_ref, dest_ref, gate_ref, cnt_ref, tab_ref,
                dvm, dsm, cnt_sc, used_sc, cur_sc, tab_sc, dest_sem):
    ts = TS_MIX
    x1 = x1_ref[...]

    logits = lax.dot_general(rwt_ref[...], x1.astype(_BF16), (((1,), (1,)), ((), ())),
                             preferred_element_type=_F32) + rb_ref[:, 0:1]
    eidx = lax.broadcasted_iota(jnp.int32, (N_EXPERTS, ts), 0).astype(_F32)
    vals, sels = [], []
    work = logits
    for k in range(TOP_K):
        m = jnp.max(work, axis=0, keepdims=True)
        first = jnp.min(jnp.where(work == m, eidx, float(N_EXPERTS)), axis=0, keepdims=True)
        sel = eidx == first
        work = jnp.where(sel, -jnp.inf, work)
        vals.append(m)
        sels.append(sel)
    exps = [jnp.exp(v - vals[0]) for v in vals]
    denom = exps[0] + exps[1] + exps[2] + exps[3]
    for k in range(TOP_K):
        gate_ref[k:k + 1, :] = exps[k] / denom
    gate_ref[TOP_K:, :] = jnp.zeros((SUBLANES - TOP_K, ts), _F32)

    tm = float(TM_EXPERT)
    chosen = jnp.zeros((N_EXPERTS, ts), _F32)
    for sel in sels:
        chosen = chosen + sel.astype(_F32)
    incl = jnp.dot(chosen.astype(_BF16), tri_ref[...], preferred_element_type=_F32)
    cnt_b = cnt_sc[:, 0:1]
    cnt_a = cnt_b + jnp.sum(chosen, axis=1, keepdims=True)
    q_b = jnp.floor((cnt_b + (tm - 1.0)) * (1.0 / tm))
    q_a = jnp.floor((cnt_a + (tm - 1.0)) * (1.0 / tm))
    n_new = q_a - q_b
    before = jnp.dot(low_ref[...], jnp.broadcast_to(n_new, (N_EXPERTS, LANES)).astype(_BF16),
                     preferred_element_type=_F32)[:, 0:1]
    used = used_sc[:, 0:1]
    new_tile = used + before
    cur_tile = cur_sc[:, 0:1]
    rank = cnt_b + (incl - chosen)
    page = jnp.floor(rank * (1.0 / tm))
    slot = jnp.where(page < q_b, cur_tile, new_tile) * tm + (rank - page * tm)
    for k in range(TOP_K):
        dest_k = jnp.sum(jnp.where(sels[k], slot, 0.0), axis=0, keepdims=True).astype(jnp.int32)
        dest_ref[k:k + 1, :] = dest_k
        dvm[par, k:k + 1, :] = dest_k
    dest_ref[TOP_K:, :] = jnp.zeros((SUBLANES - TOP_K, ts), jnp.int32)

    took = n_new > 0.0
    lane = lax.broadcasted_iota(jnp.int32, (N_EXPERTS, LANES), 1).astype(_F32)
    table = jnp.where(jnp.logical_and(lane == q_b, took), new_tile, tab_sc[...])
    tab_sc[...] = table
    cur_sc[...] = jnp.broadcast_to(jnp.where(took, new_tile, cur_tile), (N_EXPERTS, LANES))
    used_sc[...] = jnp.broadcast_to(used + jnp.sum(n_new, axis=0, keepdims=True), (N_EXPERTS, LANES))
    cnt_sc[...] = jnp.broadcast_to(cnt_a, (N_EXPERTS, LANES))
    cnt_ref[...] = jnp.broadcast_to(cnt_a, (N_EXPERTS, LANES)).astype(jnp.int32)
    tab_ref[...] = table.astype(jnp.int32)

    for c in _slots_to_scalar(dvm, dsm, par, dest_sem):
        c.start()


def _mix_route(x, w_in, pmix, pscale, convw, w_out, g1, b1, rwt, rb, tri, low, n_slots):
    bsz, seq, d = x.shape
    n = bsz * seq
    ns = seq // TS_MIX
    n_tiles = bsz * ns
    assert n // TM_EXPERT <= LANES
    full = lambda shape: pl.BlockSpec(shape, lambda g: (0,) * len(shape))
    tile = lambda g: jnp.minimum(g, n_tiles - 1)
    state = pltpu.VMEM((N_EXPERTS, LANES), _F32)
    return pl.pallas_call(
        functools.partial(_mix_route_kernel, ns),
        grid=(n_tiles + 1,),
        in_specs=[
            pl.BlockSpec((1, TS_MIX, d), lambda g: (tile(g) // ns, tile(g) % ns, 0)),
            full(w_in.shape), full(pmix.shape), full(pscale.shape), full(convw.shape),
            full(w_out.shape), full(g1.shape), full(b1.shape), full(rwt.shape), full(rb.shape),
            full(tri.shape), full(low.shape),
        ],
        out_specs=[
            pl.BlockSpec((TS_MIX, D_MODEL), lambda g: (tile(g), 0)),
            pl.BlockSpec((SUBLANES, TS_MIX), lambda g: (0, tile(g))),
            pl.BlockSpec((SUBLANES, TS_MIX), lambda g: (0, tile(g))),
            pl.BlockSpec((N_EXPERTS, LANES), lambda g: (0, 0)),
            pl.BlockSpec((N_EXPERTS, LANES), lambda g: (0, 0)),
            pl.BlockSpec(memory_space=pl.ANY),
        ],
        out_shape=[
            jax.ShapeDtypeStruct((n, D_MODEL), _F32),
            jax.ShapeDtypeStruct((SUBLANES, n), jnp.int32),
            jax.ShapeDtypeStruct((SUBLANES, n), _F32),
            jax.ShapeDtypeStruct((N_EXPERTS, LANES), jnp.int32),
            jax.ShapeDtypeStruct((N_EXPERTS, LANES), jnp.int32),
            jax.ShapeDtypeStruct((n_slots * GATHER_PITCH, LANES), _F32),
        ],
        scratch_shapes=[
            pltpu.VMEM((3, TS_MIX * GATHER_PITCH, LANES), _F32),
            pltpu.VMEM((TS_MIX, D_MODEL), _F32),
            pltpu.VMEM((TS_MIX, LANES), _F32),
            pltpu.VMEM((2, SUBLANES, TS_MIX), jnp.int32),
            pltpu.SMEM((2 * TOP_K * TS_MIX,), jnp.int32),
            pltpu.VMEM((POOL_HALO, D_POOL), _F32),
            pltpu.VMEM((CONV_HALO, D_CONV), _F32),
            state, state, state, state,
            pltpu.VMEM((2, N_EXPERTS, LANES), jnp.int32),
            pltpu.SMEM((2, N_EXPERTS, LANES), jnp.int32),
            pltpu.VMEM((TM_EXPERT * GATHER_PITCH, LANES), _F32),
            pltpu.SemaphoreType.DMA((3,)),
            pltpu.SemaphoreType.DMA(()),
        ],
        compiler_params=pltpu.CompilerParams(
            dimension_semantics=("arbitrary",),
            vmem_limit_bytes=48 * 1024 * 1024),
        name="mix_route",
    )(x, w_in, pmix, pscale, convw, w_out, g1, b1, rwt, rb, tri, low)


def _experts_kernel(be_ref, nv_ref, par_ref, nxt_ref, tab_ref, first_ref, cnt_ref, xs_ref, wgu_hbm, bgu_ref,
                    wdn_hbm, bdn_ref, out_ref, wgu_f32, wdn_f32, wgu_bf, wdn_bf, sems):
    del tab_ref
    j = pl.program_id(0)
    tm = TM_EXPERT
    valid = j < nv_ref[0]
    e_j = be_ref[j]
    rows_used = cnt_ref[e_j] - (j - first_ref[e_j]) * tm
    new_expert = jnp.logical_or(j == 0, be_ref[j] != be_ref[jnp.maximum(j - 1, 0)])

    def weight_copies(e, s):
        return (pltpu.make_async_copy(wgu_hbm.at[e], wgu_f32.at[s], sems.at[0, s]),
                pltpu.make_async_copy(wdn_hbm.at[e], wdn_f32.at[s], sems.at[1, s]))

    @pl.when(j == 0)
    def _():
        for c in weight_copies(be_ref[0], par_ref[0]):
            c.start()

    @pl.when(jnp.logical_and(valid, new_expert))
    def _():
        s = par_ref[j]
        for c in weight_copies(be_ref[j], s):
            c.wait()

        nxt = nxt_ref[be_ref[j]]

        @pl.when(nxt >= 0)
        def _():
            for c in weight_copies(nxt, 1 - s):
                c.start()

        wgu_bf[...] = wgu_f32[s].astype(_BF16)
        wdn_bf[...] = wdn_f32[s].astype(_BF16)

    def ffn(rows):
        x = _from_row_tiles(xs_ref, rows, GATHER_PITCH).astype(_BF16)
        gu = jnp.dot(x, wgu_bf[...], preferred_element_type=_F32) + bgu_ref[0]
        gate = jnp.minimum(gu[:, :D_FF], SWIGLU_LIMIT)
        up = jnp.clip(gu[:, D_FF:], -SWIGLU_LIMIT, SWIGLU_LIMIT)
        glu = gate * jax.nn.sigmoid(SWIGLU_ALPHA * gate)
        act = ((up + 1.0) * glu).astype(_BF16)
        y = jnp.dot(act, wdn_bf[...], preferred_element_type=_F32) + bdn_ref[0]
        _to_row_tiles(out_ref, y, GATHER_PITCH)
        if rows < tm:
            out_ref[pl.ds(rows * GATHER_PITCH, (tm - rows) * GATHER_PITCH), :] = jnp.zeros(
                ((tm - rows) * GATHER_PITCH, LANES), _F32)

    @pl.when(jnp.logical_and(valid, rows_used > tm // 2))
    def _():
        ffn(tm)

    @pl.when(jnp.logical_and(valid, rows_used <= tm // 2))
    def _():
        ffn(tm // 2)

    @pl.when(j >= nv_ref[0])
    def _():
        out_ref[...] = jnp.zeros_like(out_ref)


def _experts(blk_e, n_valid, slot_par, next_e, table, first, cnt, xs, wgu, bgu, wdn, bdn):
    n_slots = xs.shape[0] // GATHER_PITCH
    nb = n_slots // TM_EXPERT
    rows = TM_EXPERT * GATHER_PITCH

    def tile_of(j, be, tab, first):
        e = be[j]
        return tab[e, jnp.clip(j - first[e], 0, LANES - 1)]

    def xs_map(j, be, nv, par, nxt, tab, first, cnt):
        return (tile_of(jnp.minimum(j, nv[0] - 1), be, tab, first), 0)

    def out_map(j, be, nv, par, nxt, tab, first, cnt):
        return (jnp.where(j < nv[0], tile_of(j, be, tab, first), j), 0)

    by_expert = lambda j, be, nv, par, nxt, tab, first, cnt: (be[j], 0, 0)
    grid_spec = pltpu.PrefetchScalarGridSpec(
        num_scalar_prefetch=7,
        grid=(nb,),
        in_specs=[
            pl.BlockSpec((rows, LANES), xs_map),
            pl.BlockSpec(memory_space=pl.ANY),
            pl.BlockSpec((1, 1, 2 * D_FF), by_expert),
            pl.BlockSpec(memory_space=pl.ANY),
            pl.BlockSpec((1, 1, D_MODEL), by_expert),
        ],
        out_specs=pl.BlockSpec((rows, LANES), out_map),
        scratch_shapes=[
            pltpu.VMEM((2, D_MODEL, 2 * D_FF), _F32),
            pltpu.VMEM((2, D_FF, D_MODEL), _F32),
            pltpu.VMEM((D_MODEL, 2 * D_FF), _BF16),
            pltpu.VMEM((D_FF, D_MODEL), _BF16),
            pltpu.SemaphoreType.DMA((2, 2)),
        ],
    )
    return pl.pallas_call(
        _experts_kernel,
        grid_spec=grid_spec,
        out_shape=jax.ShapeDtypeStruct((n_slots * GATHER_PITCH, LANES), _F32),
        compiler_params=pltpu.CompilerParams(
            dimension_semantics=("arbitrary",),
            vmem_limit_bytes=60 * 1024 * 1024),
        name="experts",
    )(blk_e, n_valid, slot_par, next_e, table, first, cnt, xs, wgu, bgu, wdn, bdn)


def _combine_kernel(dcur_ref, dnext_ref, yb_ref, x1_ref, gt_ref, p_ref, g2_ref, b2_ref, pw_ref, gw_ref,
                    gb_ref, g3_ref, b3_ref, out_ref, gbuf, hbuf, sems):
    ts = TS_COMBINE
    grp = COMBINE_GROUP
    i = pl.program_id(0)
    slot = i % 2
    nxt = 1 - slot

    def start_rows(d_ref, s, t0):
        for j in range(grp):
            t = t0 + j
            for k in range(TOP_K):
                pltpu.make_async_copy(_row_tile(yb_ref, d_ref[t * TOP_K + k], GATHER_PITCH),
                                      gbuf.at[s, k, pl.ds(t * GATHER_PITCH, ROW_CHUNKS)],
                                      sems.at[s]).start(priority=k % 2)

    def wait_tile(s):
        for k in range(TOP_K):
            pltpu.make_async_copy(yb_ref.at[pl.ds(0, ts * ROW_CHUNKS)],
                                  gbuf.at[s, k, pl.ds(0, ts * ROW_CHUNKS)], sems.at[s]).wait()

    @pl.when(i == 0)
    def _():
        def first(r, carry):
            start_rows(dcur_ref, 0, r * grp)
            return carry
        lax.fori_loop(0, ts // grp, first, 0)

    wait_tile(slot)

    def group(r, carry):
        t0 = pl.multiple_of(r * grp, grp)
        x1g = x1_ref[pl.ds(t0, grp), :]
        gts = gt_ref[pl.ds(t0, grp), :]
        ys = []
        for k in range(TOP_K):
            rows_k = gbuf.at[slot, k]
            ys.append(jnp.concatenate(
                [rows_k[pl.ds(t0 * GATHER_PITCH + c, grp, stride=GATHER_PITCH), :] for c in range(ROW_CHUNKS)],
                axis=1))
        start_rows(dnext_ref, nxt, t0)
        ffn = gts[:, 0:1] * ys[0]
        for k in range(1, TOP_K):
            ffn = ffn + gts[:, k:k + 1] * ys[k]
        hbuf[pl.ds(t0, grp), :] = DEEPNORM_ALPHA * x1g + ffn
        return carry

    lax.fori_loop(0, ts // grp, group, 0)

    @pl.when(i == pl.num_programs(0) - 1)
    def _():
        wait_tile(nxt)

    x2 = _layer_norm(hbuf[...], g2_ref[...], b2_ref[...])
    z = jnp.dot(x2.astype(_BF16), gw_ref[...], preferred_element_type=_F32) + gb_ref[...]
    ple = jnp.dot(p_ref[...].astype(_BF16), pw_ref[...], preferred_element_type=_F32)
    x3 = _layer_norm(DEEPNORM_ALPHA * x2 + jax.nn.sigmoid(z) * ple, g3_ref[...], b3_ref[...])
    out_ref[...] = x3


def _combine(dest_flat, yb, x1, gates_t, p2d, g2, b2, pw, gw, gb, g3, b3):
    n = dest_flat.shape[0] // TOP_K
    ts = TS_COMBINE
    last = n // ts - 1
    full = lambda shape: pl.BlockSpec(shape, lambda i: (0,) * len(shape))
    return pl.pallas_call(
        _combine_kernel,
        grid=(n // ts,),
        in_specs=[
            pl.BlockSpec((TOP_K * ts,), lambda i: (i,), memory_space=pltpu.SMEM),
            pl.BlockSpec((TOP_K * ts,), lambda i: (jnp.minimum(i + 1, last),), memory_space=pltpu.SMEM),
            pl.BlockSpec(memory_space=pl.ANY),
            pl.BlockSpec((ts, D_MODEL), lambda i: (i, 0)),
            pl.BlockSpec((ts, TOP_K), lambda i: (i, 0)),
            pl.BlockSpec((ts, PLE_DIM), lambda i: (i, 0)),
            full(g2.shape), full(b2.shape), full(pw.shape), full(gw.shape), full(gb.shape),
            full(g3.shape), full(b3.shape),
        ],
        out_specs=pl.BlockSpec((ts, D_MODEL), lambda i: (i, 0)),
        out_shape=jax.ShapeDtypeStruct((n, D_MODEL), _F32),
        scratch_shapes=[
            pltpu.VMEM((2, TOP_K, ts * GATHER_PITCH, LANES), _F32),
            pltpu.VMEM((ts, D_MODEL), _F32),
            pltpu.SemaphoreType.DMA((2,)),
        ],
        compiler_params=pltpu.CompilerParams(
            dimension_semantics=("arbitrary",),
            vmem_limit_bytes=48 * 1024 * 1024),
        name="combine",
    )(dest_flat, dest_flat, yb, x1, gates_t, p2d, g2, b2, pw, gw, gb, g3, b3)


def kernel(x, p, w_in, pool_mix, pool_scale, conv_w, w_out, ln1_g, ln1_b, router_w, router_b,
           w_gate_up, b_gate_up, w_down, b_down, ln2_g, ln2_b, ple_proj, ple_gate_w, ple_gate_b,
           ln3_g, ln3_b):
    assert DEPTH == 1 and x.shape[-1] == D_MODEL
    bsz, seq, d = x.shape
    n = bsz * seq
    assert seq % TS_MIX == 0 and n % TS_COMBINE == 0
    row = lambda v: v.reshape(1, -1)

    tri = jnp.triu(jnp.ones((TS_MIX, TS_MIX), _BF16))
    low = jnp.tril(jnp.ones((N_EXPERTS, N_EXPERTS), _BF16), k=-1)
    tm = TM_EXPERT
    n_slots = n * TOP_K + N_EXPERTS * tm
    nb = n_slots // tm
    x1, dest, gates, counts, table, xs = _mix_route(
        x, w_in[0].astype(_BF16), pool_mix[0].astype(_BF16), row(pool_scale[0]), conv_w[0],
        w_out[0].astype(_BF16), row(ln1_g[0]), row(ln1_b[0]),
        router_w[0].T.astype(_BF16), jnp.broadcast_to(router_b[0][:, None], (N_EXPERTS, LANES)), tri, low,
        n_slots)

    cnt = counts[:, 0]
    tiles_e = (cnt + tm - 1) // tm
    tile_end = jnp.cumsum(tiles_e)
    n_valid = tile_end[-1].astype(jnp.int32)
    steps = jnp.arange(nb, dtype=jnp.int32)
    e_ids = jnp.arange(N_EXPERTS, dtype=jnp.int32)
    last_owner = jnp.max(jnp.where(tiles_e > 0, e_ids, 0))
    blk_e = jnp.minimum(jnp.sum(steps[:, None] >= tile_end[None, :], axis=1), N_EXPERTS - 1)
    blk_e = jnp.where(steps < n_valid, blk_e, last_owner).astype(jnp.int32)
    first = (tile_end - tiles_e).astype(jnp.int32)

    padded = tiles_e
    later = jnp.logical_and(e_ids[None, :] > e_ids[:, None], (padded > 0)[None, :])
    next_nonempty = jnp.min(jnp.where(later, e_ids[None, :], N_EXPERTS), axis=1)
    next_e = jnp.where(next_nonempty == N_EXPERTS, -1, next_nonempty).astype(jnp.int32)
    switched = jnp.concatenate([jnp.zeros((1,), jnp.int32), (blk_e[1:] != blk_e[:-1]).astype(jnp.int32)])
    slot_par = (jnp.cumsum(switched) % 2).astype(jnp.int32)
    n_valid = n_valid.reshape(1)

    dest = dest[:TOP_K].T.reshape(-1)
    yb = _experts(blk_e, n_valid, slot_par, next_e, table, first, cnt, xs,
                  w_gate_up[0], b_gate_up[0][:, None, :], w_down[0], b_down[0][:, None, :])
    out = _combine(dest, yb, x1, gates[:TOP_K].T, p[0].reshape(n, PLE_DIM),
                   row(ln2_g[0]), row(ln2_b[0]), ple_proj[0].astype(_BF16),
                   ple_gate_w[0].astype(_BF16), row(ple_gate_b[0]), row(ln3_g[0]), row(ln3_b[0]))
    return out.reshape(bsz, seq, d)
```

```python
import functools

import jax
import jax.numpy as jnp
from jax import lax
from jax.experimental import pallas as pl
from jax.experimental.pallas import tpu as pltpu

D_MODEL = 1024
D_POOL = 512
D_CONV = 512
POOL_WINDOWS = (2, 4, 8, 16)
POOL_GROUP_DIM = 128
CONV_WIDTH = 3
D_IN_PROJ = D_POOL + 3 * D_CONV
N_EXPERTS = 32
TOP_K = 4
D_FF = 1024
SWIGLU_LIMIT = 7.0
SWIGLU_ALPHA = 1.702
PLE_DIM = 256
DEPTH = 1
DEEPNORM_ALPHA = (2.0 * DEPTH) ** 0.25
LN_EPS = 1e-5

LANES = 128
SUBLANES = 8
ROW_CHUNKS = D_MODEL // LANES
POOL_HALO = 16
CONV_HALO = 8
GATHER_PITCH = ROW_CHUNKS + 1

TS_MIX = 512
TM_EXPERT = 512
TS_COMBINE = 256
COMBINE_GROUP = 8

_BF16 = jnp.bfloat16
_F32 = jnp.float32


def _layer_norm(h, g, b):
    mu = jnp.mean(h, axis=-1, keepdims=True)
    c = h - mu
    var = jnp.mean(c * c, axis=-1, keepdims=True)
    return c * lax.rsqrt(var + LN_EPS) * g + b


def _to_row_tiles(ref, val, pitch=ROW_CHUNKS):
    rows = val.shape[0]
    for c in range(ROW_CHUNKS):
        ref[pl.ds(c, rows, stride=pitch), :] = val[:, c * LANES:(c + 1) * LANES]
    for c in range(ROW_CHUNKS, pitch):
        ref[pl.ds(c, rows, stride=pitch), :] = jnp.zeros((rows, LANES), val.dtype)


def _from_row_tiles(ref, rows, pitch=ROW_CHUNKS):
    return jnp.concatenate(
        [ref[pl.ds(c, rows, stride=pitch), :] for c in range(ROW_CHUNKS)], axis=1)


def _row_tile(ref, r, pitch=ROW_CHUNKS):
    if pitch == ROW_CHUNKS:
        return ref.at[pl.ds(pl.multiple_of(r * ROW_CHUNKS, ROW_CHUNKS), ROW_CHUNKS)]
    return ref.at[pl.ds(r * pitch, ROW_CHUNKS)]


def _mix_route_kernel(ns, x_ref, w_in_ref, pmix_ref, pscale_ref, convw_ref, w_out_ref, g1_ref, b1_ref,
                      rwt_ref, rb_ref, tri_ref, low_ref,
                      x1_ref, dest_ref, gate_ref, cnt_ref, tab_ref, xs_ref,
                      xbuf, cbuf, rbuf, dvm, dsm, carry_v, carry_u, cnt_sc, used_sc, cur_sc, tab_sc,
                      meta_v, meta_s, zbuf, row_sems, dest_sem):
    g = pl.program_id(0)
    n_tiles = pl.num_programs(0) - 1
    ts = TS_MIX
    grp = SUBLANES
    par = lax.rem(g, 2)
    buf = lax.rem(g, 3)
    prev = lax.rem(g + 2, 3)
    tile_rows = ts * GATHER_PITCH

    def wait_rows(q):
        for _ in range(TOP_K):
            pltpu.make_async_copy(xbuf.at[q], xs_ref.at[pl.ds(0, tile_rows)], row_sems.at[q]).wait()

    @pl.when(g >= 1)
    def _():
        for c in _slots_to_scalar(dvm, dsm, 1 - par, dest_sem):
            c.wait()

    @pl.when(g < n_tiles)
    def _():
        _mix_tile(lax.rem(g, ns), g == 0, x_ref, w_in_ref, pmix_ref, pscale_ref, convw_ref, w_out_ref,
                  cbuf, rbuf, carry_v, carry_u, cnt_sc, used_sc, cur_sc, tab_sc)

    base = (1 - par) * (TOP_K * ts)

    def start_rows(t0):
        for j in range(grp):
            t = t0 + j
            for k in range(TOP_K):
                slot = dsm[base + k * ts + t]
                pltpu.make_async_copy(xbuf.at[prev, pl.ds(t * GATHER_PITCH, GATHER_PITCH)],
                                      xs_ref.at[pl.ds(slot * GATHER_PITCH, GATHER_PITCH)],
                                      row_sems.at[prev]).start(priority=k % 2)

    def row_loop(finish, send):
        gain = g1_ref[...]
        bias = b1_ref[...]

        def group(r, carry):
            t0 = pl.multiple_of(r * grp, grp)
            if finish:
                c = cbuf[pl.ds(t0, grp), :]
                rstd = rbuf[pl.ds(t0, grp), :]
            if send:
                start_rows(t0)
            if finish:
                rows = xbuf.at[buf]
                for ch in range(ROW_CHUNKS):
                    lanes = slice(ch * LANES, (ch + 1) * LANES)
                    x1 = c[:, lanes] * rstd * gain[:, lanes] + bias[:, lanes]
                    x1_ref[pl.ds(t0, grp), lanes] = x1
                    rows[pl.ds(t0 * GATHER_PITCH + ch, grp, stride=GATHER_PITCH), :] = x1
                rows[pl.ds(t0 * GATHER_PITCH + ROW_CHUNKS, grp, stride=GATHER_PITCH), :] = jnp.zeros((grp, LANES), _F32)
            return carry
        lax.fori_loop(0, ts // grp, group, 0)

    @pl.when(g >= 3)
    def _():
        wait_rows(buf)

    @pl.when(g == 0)
    def _():
        row_loop(finish=True, send=False)

    @pl.when(jnp.logical_and(g >= 1, g < n_tiles))
    def _():
        row_loop(finish=True, send=True)

    @pl.when(g == n_tiles)
    def _():
        row_loop(finish=False, send=True)

    @pl.when(g < n_tiles)
    def _():
        _route_tile(par, x1_ref, rwt_ref, rb_ref, tri_ref, low_ref, dest_ref, gate_ref, cnt_ref, tab_ref,
                    dvm, dsm, cnt_sc, used_sc, cur_sc, tab_sc, dest_sem)

    @pl.when(g == n_tiles)
    def _():
        wait_rows(lax.rem(g + 1, 3))
        wait_rows(prev)
        _seal_slots(xs_ref, cnt_sc, tab_sc, meta_v, meta_s, zbuf, dest_sem)


def _slots_to_scalar(dvm, dsm, q, sem):
    ts = TS_MIX
    return [pltpu.make_async_copy(dvm.at[q, k], dsm.at[pl.ds((q * TOP_K + k) * ts, ts)], sem)
            for k in range(TOP_K)]


def _seal_slots(xs_ref, cnt_sc, tab_sc, meta_v, meta_s, zbuf, sem):
    tm = TM_EXPERT
    tile_rows = tm * GATHER_PITCH
    n_tiles = xs_ref.shape[0] // tile_rows
    meta_v[0] = cnt_sc[...].astype(jnp.int32)
    meta_v[1] = tab_sc[...].astype(jnp.int32)
    to_scalar = pltpu.make_async_copy(meta_v, meta_s, sem)
    to_scalar.start()
    to_scalar.wait()
    zbuf[...] = jnp.zeros_like(zbuf)
    pieces = [1 << b for b in reversed(range(tm.bit_length() - 1))]

    def each_copy(fn):
        owned = 0
        for e in range(N_EXPERTS):
            cnt = meta_s[0, e, 0]
            tiles = lax.shift_right_logical(cnt + (tm - 1), tm.bit_length() - 1)
            owned = owned + tiles
            last = meta_s[1, e, jnp.maximum(tiles - 1, 0)]
            fill = cnt - jnp.maximum(tiles - 1, 0) * tm
            spare = tm - fill
            pos = last * tm + fill
            for piece in pieces:
                take = jnp.logical_and(tiles > 0, (spare & piece) != 0)

                @pl.when(take)
                def _():
                    fn(pltpu.make_async_copy(
                        zbuf.at[pl.ds(0, piece * GATHER_PITCH)],
                        xs_ref.at[pl.ds(pos * GATHER_PITCH, piece * GATHER_PITCH)], sem))
                pos = pos + jnp.where((spare & piece) != 0, piece, 0)

        def unowned(j, carry):
            fn(pltpu.make_async_copy(
                zbuf, xs_ref.at[pl.ds(pl.multiple_of(j * tile_rows, tile_rows), tile_rows)], sem))
            return carry
        lax.fori_loop(owned, n_tiles, unowned, 0)

    each_copy(lambda c: c.start())
    each_copy(lambda c: c.wait())


def _mix_tile(s, first, x_ref, w_in_ref, pmix_ref, pscale_ref, convw_ref, w_out_ref, cbuf, rbuf,
              carry_v, carry_u, cnt_sc, used_sc, cur_sc, tab_sc):
    ts = TS_MIX

    @pl.when(first)
    def _():
        cnt_sc[...] = jnp.zeros_like(cnt_sc)
        used_sc[...] = jnp.zeros_like(used_sc)
        cur_sc[...] = jnp.zeros_like(cur_sc)
        tab_sc[...] = jnp.zeros_like(tab_sc)

    @pl.when(s == 0)
    def _():
        carry_v[...] = jnp.zeros_like(carry_v)
        carry_u[...] = jnp.zeros_like(carry_u)

    xb = x_ref[0]
    proj = jnp.dot(xb.astype(_BF16), w_in_ref[...], preferred_element_type=_F32)
    vp = proj[:, :D_POOL]
    bg = proj[:, D_POOL:D_POOL + D_CONV]
    cg = proj[:, D_POOL + D_CONV:D_POOL + 2 * D_CONV]
    vc = proj[:, D_POOL + 2 * D_CONV:]

    ext = jnp.concatenate([carry_v[...], vp], axis=0)
    pos = lax.broadcasted_iota(jnp.int32, (ts, 1), 0) + s * ts
    mixed = []
    for g, w in enumerate(POOL_WINDOWS):
        lo, hi = g * POOL_GROUP_DIM, (g + 1) * POOL_GROUP_DIM
        acc = ext[:, lo:hi]
        sh = 1
        while sh < w:
            acc = acc + pltpu.roll(acc, sh, 0)
            sh *= 2
        cnt = jnp.minimum(pos + 1, w).astype(_F32)
        d = acc[POOL_HALO:] / cnt - vp[:, lo:hi]
        yg = jnp.dot(d.astype(_BF16), pmix_ref[g], preferred_element_type=_F32)
        mixed.append(yg * pscale_ref[:, lo:hi])
    carry_v[...] = vp[ts - POOL_HALO:]

    u = cg * vc
    extu = jnp.concatenate([carry_u[...], u], axis=0)
    u1 = pltpu.roll(extu, 1, 0)[CONV_HALO:]
    u2 = pltpu.roll(extu, 2, 0)[CONV_HALO:]
    yc = bg * (convw_ref[0:1, :] * u2 + convw_ref[1:2, :] * u1 + convw_ref[2:3, :] * u)
    carry_u[...] = u[ts - CONV_HALO:]
    mixed.append(yc)

    mix_in = jnp.concatenate(mixed, axis=1).astype(_BF16)
    mix = jnp.dot(mix_in, w_out_ref[...], preferred_element_type=_F32)
    h = DEEPNORM_ALPHA * xb + mix
    c = h - jnp.mean(h, axis=-1, keepdims=True)
    cbuf[...] = c
    rbuf[...] = jnp.broadcast_to(lax.rsqrt(jnp.mean(c * c, axis=-1, keepdims=True) + LN_EPS), rbuf.shape)


def _route_tile(par, x1_ref, rwt_ref, rb_ref, tri_ref, low_ref, dest_ref, gate_ref, cnt_ref, tab_ref,
                dvm, dsm, cnt_sc, used_sc, cur_sc, tab_sc, dest_sem):
    ts = TS_MIX
    x1 = x1_ref[...]

    logits = lax.dot_general(rwt_ref[...], x1.astype(_BF16), (((1,), (1,)), ((), ())),
                             preferred_element_type=_F32) + rb_ref[:, 0:1]
    eidx = lax.broadcasted_iota(jnp.int32, (N_EXPERTS, ts), 0).astype(_F32)
    vals, sels = [], []
    work = logits
    for k in range(TOP_K):
        m = jnp.max(work, axis=0, keepdims=True)
        first = jnp.min(jnp.where(work == m, eidx, float(N_EXPERTS)), axis=0, keepdims=True)
        sel = eidx == first
        work = jnp.where(sel, -jnp.inf, work)
        vals.append(m)
        sels.append(sel)
    exps = [jnp.exp(v - vals[0]) for v in vals]
    denom = exps[0] + exps[1] + exps[2] + exps[3]
    for k in range(TOP_K):
        gate_ref[k:k + 1, :] = exps[k] / denom
    gate_ref[TOP_K:, :] = jnp.zeros((SUBLANES - TOP_K, ts), _F32)

    tm = float(TM_EXPERT)
    chosen = jnp.zeros((N_EXPERTS, ts), _F32)
    for sel in sels:
        chosen = chosen + sel.astype(_F32)
    incl = jnp.dot(chosen.astype(_BF16), tri_ref[...], preferred_element_type=_F32)
    cnt_b = cnt_sc[:, 0:1]
    cnt_a = cnt_b + jnp.sum(chosen, axis=1, keepdims=True)
    q_b = jnp.floor((cnt_b + (tm - 1.0)) * (1.0 / tm))
    q_a = jnp.floor((cnt_a + (tm - 1.0)) * (1.0 / tm))
    n_new = q_a - q_b
    before = jnp.dot(low_ref[...], jnp.broadcast_to(n_new, (N_EXPERTS, LANES)).astype(_BF16),
                     preferred_element_type=_F32)[:, 0:1]
    used = used_sc[:, 0:1]
    new_tile = used + before
    cur_tile = cur_sc[:, 0:1]
    rank = cnt_b + (incl - chosen)
    page = jnp.floor(rank * (1.0 / tm))
    slot = jnp.where(page < q_b, cur_tile, new_tile) * tm + (rank - page * tm)
    for k in range(TOP_K):
        dest_k = jnp.sum(jnp.where(sels[k], slot, 0.0), axis=0, keepdims=True).astype(jnp.int32)
        dest_ref[k:k + 1, :] = dest_k
        dvm[par, k:k + 1, :] = dest_k
    dest_ref[TOP_K:, :] = jnp.zeros((SUBLANES - TOP_K, ts), jnp.int32)

    took = n_new > 0.0
    lane = lax.broadcasted_iota(jnp.int32, (N_EXPERTS, LANES), 1).astype(_F32)
    table = jnp.where(jnp.logical_and(lane == q_b, took), new_tile, tab_sc[...])
    tab_sc[...] = table
    cur_sc[...] = jnp.broadcast_to(jnp.where(took, new_tile, cur_tile), (N_EXPERTS, LANES))
    used_sc[...] = jnp.broadcast_to(used + jnp.sum(n_new, axis=0, keepdims=True), (N_EXPERTS, LANES))
    cnt_sc[...] = jnp.broadcast_to(cnt_a, (N_EXPERTS, LANES))
    cnt_ref[...] = jnp.broadcast_to(cnt_a, (N_EXPERTS, LANES)).astype(jnp.int32)
    tab_ref[...] = table.astype(jnp.int32)

    for c in _slots_to_scalar(dvm, dsm, par, dest_sem):
        c.start()


def _mix_route(x, w_in, pmix, pscale, convw, w_out, g1, b1, rwt, rb, tri, low, n_slots):
    bsz, seq, d = x.shape
    n = bsz * seq
    ns = seq // TS_MIX
    n_tiles = bsz * ns
    assert n // TM_EXPERT <= LANES
    full = lambda shape: pl.BlockSpec(shape, lambda g: (0,) * len(shape))
    tile = lambda g: jnp.minimum(g, n_tiles - 1)
    state = pltpu.VMEM((N_EXPERTS, LANES), _F32)
    return pl.pallas_call(
        functools.partial(_mix_route_kernel, ns),
        grid=(n_tiles + 1,),
        in_specs=[
            pl.BlockSpec((1, TS_MIX, d), lambda g: (tile(g) // ns, tile(g) % ns, 0)),
            full(w_in.shape), full(pmix.shape), full(pscale.shape), full(convw.shape),
            full(w_out.shape), full(g1.shape), full(b1.shape), full(rwt.shape), full(rb.shape),
            full(tri.shape), full(low.shape),
        ],
        out_specs=[
            pl.BlockSpec((TS_MIX, D_MODEL), lambda g: (tile(g), 0)),
            pl.BlockSpec((SUBLANES, TS_MIX), lambda g: (0, tile(g))),
            pl.BlockSpec((SUBLANES, TS_MIX), lambda g: (0, tile(g))),
            pl.BlockSpec((N_EXPERTS, LANES), lambda g: (0, 0)),
            pl.BlockSpec((N_EXPERTS, LANES), lambda g: (0, 0)),
            pl.BlockSpec(memory_space=pl.ANY),
        ],
        out_shape=[
            jax.ShapeDtypeStruct((n, D_MODEL), _F32),
            jax.ShapeDtypeStruct((SUBLANES, n), jnp.int32),
            jax.ShapeDtypeStruct((SUBLANES, n), _F32),
            jax.ShapeDtypeStruct((N_EXPERTS, LANES), jnp.int32),
            jax.ShapeDtypeStruct((N_EXPERTS, LANES), jnp.int32),
            jax.ShapeDtypeStruct((n_slots * GATHER_PITCH, LANES), _F32),
        ],
        scratch_shapes=[
            pltpu.VMEM((3, TS_MIX * GATHER_PITCH, LANES), _F32),
            pltpu.VMEM((TS_MIX, D_MODEL), _F32),
            pltpu.VMEM((TS_MIX, LANES), _F32),
            pltpu.VMEM((2, SUBLANES, TS_MIX), jnp.int32),
            pltpu.SMEM((2 * TOP_K * TS_MIX,), jnp.int32),
            pltpu.VMEM((POOL_HALO, D_POOL), _F32),
            pltpu.VMEM((CONV_HALO, D_CONV), _F32),
            state, state, state, state,
            pltpu.VMEM((2, N_EXPERTS, LANES), jnp.int32),
            pltpu.SMEM((2, N_EXPERTS, LANES), jnp.int32),
            pltpu.VMEM((TM_EXPERT * GATHER_PITCH, LANES), _F32),
            pltpu.SemaphoreType.DMA((3,)),
            pltpu.SemaphoreType.DMA(()),
        ],
        compiler_params=pltpu.CompilerParams(
            dimension_semantics=("arbitrary",),
            vmem_limit_bytes=48 * 1024 * 1024),
        name="mix_route",
    )(x, w_in, pmix, pscale, convw, w_out, g1, b1, rwt, rb, tri, low)


def _experts_kernel(be_ref, nv_ref, par_ref, nxt_ref, tab_ref, first_ref, cnt_ref, xs_ref, wgu_hbm, bgu_ref,
                    wdn_hbm, bdn_ref, out_ref, wgu_f32, wdn_f32, wgu_bf, wdn_bf, sems):
    del tab_ref
    j = pl.program_id(0)
    tm = TM_EXPERT
    valid = j < nv_ref[0]
    e_j = be_ref[j]
    rows_used = cnt_ref[e_j] - (j - first_ref[e_j]) * tm
    new_expert = jnp.logical_or(j == 0, be_ref[j] != be_ref[jnp.maximum(j - 1, 0)])

    def weight_copies(e, s):
        return (pltpu.make_async_copy(wgu_hbm.at[e], wgu_f32.at[s], sems.at[0, s]),
                pltpu.make_async_copy(wdn_hbm.at[e], wdn_f32.at[s], sems.at[1, s]))

    @pl.when(j == 0)
    def _():
        for c in weight_copies(be_ref[0], par_ref[0]):
            c.start()

    @pl.when(jnp.logical_and(valid, new_expert))
    def _():
        s = par_ref[j]
        for c in weight_copies(be_ref[j], s):
            c.wait()

        nxt = nxt_ref[be_ref[j]]

        @pl.when(nxt >= 0)
        def _():
            for c in weight_copies(nxt, 1 - s):
                c.start()

        wgu_bf[...] = wgu_f32[s].astype(_BF16)
        wdn_bf[...] = wdn_f32[s].astype(_BF16)

    def ffn(rows):
        x = _from_row_tiles(xs_ref, rows, GATHER_PITCH).astype(_BF16)
        gu = jnp.dot(x, wgu_bf[...], preferred_element_type=_F32) + bgu_ref[e_j]
        gate = jnp.minimum(gu[:, :D_FF], SWIGLU_LIMIT)
        up = jnp.clip(gu[:, D_FF:], -SWIGLU_LIMIT, SWIGLU_LIMIT)
        glu = gate * jax.nn.sigmoid(SWIGLU_ALPHA * gate)
        act = ((up + 1.0) * glu).astype(_BF16)
        y = jnp.dot(act, wdn_bf[...], preferred_element_type=_F32) + bdn_ref[e_j]
        _to_row_tiles(out_ref, y, GATHER_PITCH)
        if rows < tm:
            out_ref[pl.ds(rows * GATHER_PITCH, (tm - rows) * GATHER_PITCH), :] = jnp.zeros(
                ((tm - rows) * GATHER_PITCH, LANES), _F32)

    @pl.when(jnp.logical_and(valid, rows_used > tm // 2))
    def _():
        ffn(tm)

    @pl.when(jnp.logical_and(valid, rows_used <= tm // 2))
    def _():
        ffn(tm // 2)

    @pl.when(j >= nv_ref[0])
    def _():
        out_ref[...] = jnp.zeros_like(out_ref)


def _experts(blk_e, n_valid, slot_par, next_e, table, first, cnt, xs, wgu, bgu, wdn, bdn):
    n_slots = xs.shape[0] // GATHER_PITCH
    nb = n_slots // TM_EXPERT
    rows = TM_EXPERT * GATHER_PITCH

    def tile_of(j, be, tab, first):
        e = be[j]
        return tab[e, jnp.clip(j - first[e], 0, LANES - 1)]

    def xs_map(j, be, nv, par, nxt, tab, first, cnt):
        return (tile_of(jnp.minimum(j, nv[0] - 1), be, tab, first), 0)

    def out_map(j, be, nv, par, nxt, tab, first, cnt):
        return (jnp.where(j < nv[0], tile_of(j, be, tab, first), j), 0)

    whole = lambda j, be, nv, par, nxt, tab, first, cnt: (0, 0, 0)
    grid_spec = pltpu.PrefetchScalarGridSpec(
        num_scalar_prefetch=7,
        grid=(nb,),
        in_specs=[
            pl.BlockSpec((rows, LANES), xs_map),
            pl.BlockSpec(memory_space=pl.ANY),
            pl.BlockSpec((N_EXPERTS, 1, 2 * D_FF), whole),
            pl.BlockSpec(memory_space=pl.ANY),
            pl.BlockSpec((N_EXPERTS, 1, D_MODEL), whole),
        ],
        out_specs=pl.BlockSpec((rows, LANES), out_map),
        scratch_shapes=[
            pltpu.VMEM((2, D_MODEL, 2 * D_FF), _F32),
            pltpu.VMEM((2, D_FF, D_MODEL), _F32),
            pltpu.VMEM((D_MODEL, 2 * D_FF), _BF16),
            pltpu.VMEM((D_FF, D_MODEL), _BF16),
            pltpu.SemaphoreType.DMA((2, 2)),
        ],
    )
    return pl.pallas_call(
        _experts_kernel,
        grid_spec=grid_spec,
        out_shape=jax.ShapeDtypeStruct((n_slots * GATHER_PITCH, LANES), _F32),
        compiler_params=pltpu.CompilerParams(
            dimension_semantics=("arbitrary",),
            vmem_limit_bytes=60 * 1024 * 1024),
        name="experts",
    )(blk_e, n_valid, slot_par, next_e, table, first, cnt, xs, wgu, bgu, wdn, bdn)


def _combine_kernel(dcur_ref, dnext_ref, yb_ref, x1_ref, gt_ref, p_ref, g2_ref, b2_ref, pw_ref, gw_ref,
                    gb_ref, g3_ref, b3_ref, out_ref, gbuf, hbuf, sems):
    ts = TS_COMBINE
    grp = COMBINE_GROUP
    i = pl.program_id(0)
    slot = i % 2
    nxt = 1 - slot

    def start_rows(d_ref, s, t0):
        for j in range(grp):
            t = t0 + j
            for k in range(TOP_K):
                pltpu.make_async_copy(_row_tile(yb_ref, d_ref[t * TOP_K + k], GATHER_PITCH),
                                      gbuf.at[s, k, pl.ds(t * GATHER_PITCH, ROW_CHUNKS)],
                                      sems.at[s]).start(priority=k % 2)

    def wait_tile(s):
        for k in range(TOP_K):
            pltpu.make_async_copy(yb_ref.at[pl.ds(0, ts * ROW_CHUNKS)],
                                  gbuf.at[s, k, pl.ds(0, ts * ROW_CHUNKS)], sems.at[s]).wait()

    @pl.when(i == 0)
    def _():
        def first(r, carry):
            start_rows(dcur_ref, 0, r * grp)
            return carry
        lax.fori_loop(0, ts // grp, first, 0)

    wait_tile(slot)

    def group(r, carry):
        t0 = pl.multiple_of(r * grp, grp)
        x1g = x1_ref[pl.ds(t0, grp), :]
        gts = gt_ref[pl.ds(t0, grp), :]
        ys = []
        for k in range(TOP_K):
            rows_k = gbuf.at[slot, k]
            ys.append(jnp.concatenate(
                [rows_k[pl.ds(t0 * GATHER_PITCH + c, grp, stride=GATHER_PITCH), :] for c in range(ROW_CHUNKS)],
                axis=1))
        start_rows(dnext_ref, nxt, t0)
        ffn = gts[:, 0:1] * ys[0]
        for k in range(1, TOP_K):
            ffn = ffn + gts[:, k:k + 1] * ys[k]
        hbuf[pl.ds(t0, grp), :] = DEEPNORM_ALPHA * x1g + ffn
        return carry

    lax.fori_loop(0, ts // grp, group, 0)

    @pl.when(i == pl.num_programs(0) - 1)
    def _():
        wait_tile(nxt)

    x2 = _layer_norm(hbuf[...], g2_ref[...], b2_ref[...])
    z = jnp.dot(x2.astype(_BF16), gw_ref[...], preferred_element_type=_F32) + gb_ref[...]
    ple = jnp.dot(p_ref[...].astype(_BF16), pw_ref[...], preferred_element_type=_F32)
    x3 = _layer_norm(DEEPNORM_ALPHA * x2 + jax.nn.sigmoid(z) * ple, g3_ref[...], b3_ref[...])
    out_ref[...] = x3


def _combine(dest_flat, yb, x1, gates_t, p2d, g2, b2, pw, gw, gb, g3, b3):
    n = dest_flat.shape[0] // TOP_K
    ts = TS_COMBINE
    last = n // ts - 1
    full = lambda shape: pl.BlockSpec(shape, lambda i: (0,) * len(shape))
    return pl.pallas_call(
        _combine_kernel,
        grid=(n // ts,),
        in_specs=[
            pl.BlockSpec((TOP_K * ts,), lambda i: (i,), memory_space=pltpu.SMEM),
            pl.BlockSpec((TOP_K * ts,), lambda i: (jnp.minimum(i + 1, last),), memory_space=pltpu.SMEM),
            pl.BlockSpec(memory_space=pl.ANY),
            pl.BlockSpec((ts, D_MODEL), lambda i: (i, 0)),
            pl.BlockSpec((ts, TOP_K), lambda i: (i, 0)),
            pl.BlockSpec((ts, PLE_DIM), lambda i: (i, 0)),
            full(g2.shape), full(b2.shape), full(pw.shape), full(gw.shape), full(gb.shape),
            full(g3.shape), full(b3.shape),
        ],
        out_specs=pl.BlockSpec((ts, D_MODEL), lambda i: (i, 0)),
        out_shape=jax.ShapeDtypeStruct((n, D_MODEL), _F32),
        scratch_shapes=[
            pltpu.VMEM((2, TOP_K, ts * GATHER_PITCH, LANES), _F32),
            pltpu.VMEM((ts, D_MODEL), _F32),
            pltpu.SemaphoreType.DMA((2,)),
        ],
        compiler_params=pltpu.CompilerParams(
            dimension_semantics=("arbitrary",),
            vmem_limit_bytes=48 * 1024 * 1024),
        name="combine",
    )(dest_flat, dest_flat, yb, x1, gates_t, p2d, g2, b2, pw, gw, gb, g3, b3)


def kernel(x, p, w_in, pool_mix, pool_scale, conv_w, w_out, ln1_g, ln1_b, router_w, router_b,
           w_gate_up, b_gate_up, w_down, b_down, ln2_g, ln2_b, ple_proj, ple_gate_w, ple_gate_b,
           ln3_g, ln3_b):
    assert DEPTH == 1 and x.shape[-1] == D_MODEL
    bsz, seq, d = x.shape
    n = bsz * seq
    assert seq % TS_MIX == 0 and n % TS_COMBINE == 0
    row = lambda v: v.reshape(1, -1)

    tri = jnp.triu(jnp.ones((TS_MIX, TS_MIX), _BF16))
    low = jnp.tril(jnp.ones((N_EXPERTS, N_EXPERTS), _BF16), k=-1)
    tm = TM_EXPERT
    n_slots = n * TOP_K + N_EXPERTS * tm
    nb = n_slots // tm
    x1, dest, gates, counts, table, xs = _mix_route(
        x, w_in[0].astype(_BF16), pool_mix[0].astype(_BF16), row(pool_scale[0]), conv_w[0],
        w_out[0].astype(_BF16), row(ln1_g[0]), row(ln1_b[0]),
        router_w[0].T.astype(_BF16), jnp.broadcast_to(router_b[0][:, None], (N_EXPERTS, LANES)), tri, low,
        n_slots)

    cnt = counts[:, 0]
    tiles_e = (cnt + tm - 1) // tm
    tile_end = jnp.cumsum(tiles_e)
    n_valid = tile_end[-1].astype(jnp.int32)
    steps = jnp.arange(nb, dtype=jnp.int32)
    e_ids = jnp.arange(N_EXPERTS, dtype=jnp.int32)
    last_owner = jnp.max(jnp.where(tiles_e > 0, e_ids, 0))
    blk_e = jnp.minimum(jnp.sum(steps[:, None] >= tile_end[None, :], axis=1), N_EXPERTS - 1)
    blk_e = jnp.where(steps < n_valid, blk_e, last_owner).astype(jnp.int32)
    first = (tile_end - tiles_e).astype(jnp.int32)

    padded = tiles_e
    later = jnp.logical_and(e_ids[None, :] > e_ids[:, None], (padded > 0)[None, :])
    next_nonempty = jnp.min(jnp.where(later, e_ids[None, :], N_EXPERTS), axis=1)
    next_e = jnp.where(next_nonempty == N_EXPERTS, -1, next_nonempty).astype(jnp.int32)
    switched = jnp.concatenate([jnp.zeros((1,), jnp.int32), (blk_e[1:] != blk_e[:-1]).astype(jnp.int32)])
    slot_par = (jnp.cumsum(switched) % 2).astype(jnp.int32)
    n_valid = n_valid.reshape(1)

    dest = dest[:TOP_K].T.reshape(-1)
    yb = _experts(blk_e, n_valid, slot_par, next_e, table, first, cnt, xs,
                  w_gate_up[0], b_gate_up[0][:, None, :], w_down[0], b_down[0][:, None, :])
    out = _combine(dest, yb, x1, gates[:TOP_K].T, p[0].reshape(n, PLE_DIM),
                   row(ln2_g[0]), row(ln2_b[0]), ple_proj[0].astype(_BF16),
                   ple_gate_w[0].astype(_BF16), row(ple_gate_b[0]), row(ln3_g[0]), row(ln3_b[0]))
    return out.reshape(bsz, seq, d)
```

```python
import functools

import jax
import jax.numpy as jnp
from jax import lax
from jax.experimental import pallas as pl
from jax.experimental.pallas import tpu as pltpu

D_MODEL = 1024
D_POOL = 512
D_CONV = 512
POOL_WINDOWS = (2, 4, 8, 16)
POOL_GROUP_DIM = 128
CONV_WIDTH = 3
D_IN_PROJ = D_POOL + 3 * D_CONV
N_EXPERTS = 32
TOP_K = 4
D_FF = 1024
SWIGLU_LIMIT = 7.0
SWIGLU_ALPHA = 1.702
PLE_DIM = 256
DEPTH = 1
DEEPNORM_ALPHA = (2.0 * DEPTH) ** 0.25
LN_EPS = 1e-5

LANES = 128
SUBLANES = 8
ROW_CHUNKS = D_MODEL // LANES
POOL_HALO = 16
CONV_HALO = 8
GATHER_PITCH = ROW_CHUNKS + 1

TS_MIX = 512
TM_EXPERT = 512
TS_COMBINE = 256
COMBINE_GROUP = 16

_BF16 = jnp.bfloat16
_F32 = jnp.float32


def _layer_norm(h, g, b):
    mu = jnp.mean(h, axis=-1, keepdims=True)
    c = h - mu
    var = jnp.mean(c * c, axis=-1, keepdims=True)
    return c * lax.rsqrt(var + LN_EPS) * g + b


def _to_row_tiles(ref, val, pitch=ROW_CHUNKS):
    rows = val.shape[0]
    for c in range(ROW_CHUNKS):
        ref[pl.ds(c, rows, stride=pitch), :] = val[:, c * LANES:(c + 1) * LANES]
    for c in range(ROW_CHUNKS, pitch):
        ref[pl.ds(c, rows, stride=pitch), :] = jnp.zeros((rows, LANES), val.dtype)


def _from_row_tiles(ref, rows, pitch=ROW_CHUNKS):
    return jnp.concatenate(
        [ref[pl.ds(c, rows, stride=pitch), :] for c in range(ROW_CHUNKS)], axis=1)


def _row_tile(ref, r, pitch=ROW_CHUNKS):
    if pitch == ROW_CHUNKS:
        return ref.at[pl.ds(pl.multiple_of(r * ROW_CHUNKS, ROW_CHUNKS), ROW_CHUNKS)]
    return ref.at[pl.ds(r * pitch, ROW_CHUNKS)]


def _mix_route_kernel(ns, x_ref, w_in_ref, pmix_ref, pscale_ref, convw_ref, w_out_ref, g1_ref, b1_ref,
                      rwt_ref, rb_ref, tri_ref, low_ref,
                      x1_ref, dest_ref, gate_ref, cnt_ref, tab_ref, xs_ref,
                      xbuf, cbuf, rbuf, dvm, dsm, carry_v, carry_u, cnt_sc, used_sc, cur_sc, tab_sc,
                      meta_v, meta_s, zbuf, row_sems, dest_sem):
    g = pl.program_id(0)
    n_tiles = pl.num_programs(0) - 1
    ts = TS_MIX
    grp = SUBLANES
    par = lax.rem(g, 2)
    buf = lax.rem(g, 3)
    prev = lax.rem(g + 2, 3)
    tile_rows = ts * GATHER_PITCH

    def wait_rows(q):
        for _ in range(TOP_K):
            pltpu.make_async_copy(xbuf.at[q], xs_ref.at[pl.ds(0, tile_rows)], row_sems.at[q]).wait()

    @pl.when(g >= 1)
    def _():
        for c in _slots_to_scalar(dvm, dsm, 1 - par, dest_sem):
            c.wait()

    @pl.when(g < n_tiles)
    def _():
        _mix_tile(lax.rem(g, ns), g == 0, x_ref, w_in_ref, pmix_ref, pscale_ref, convw_ref, w_out_ref,
                  cbuf, rbuf, carry_v, carry_u, cnt_sc, used_sc, cur_sc, tab_sc)

    base = (1 - par) * (TOP_K * ts)

    def start_rows(t0):
        for j in range(grp):
            t = t0 + j
            for k in range(TOP_K):
                slot = dsm[base + k * ts + t]
                pltpu.make_async_copy(xbuf.at[prev, pl.ds(t * GATHER_PITCH, GATHER_PITCH)],
                                      xs_ref.at[pl.ds(slot * GATHER_PITCH, GATHER_PITCH)],
                                      row_sems.at[prev]).start(priority=k % 2)

    def row_loop(finish, send):
        gain = g1_ref[...]
        bias = b1_ref[...]

        def group(r, carry):
            t0 = pl.multiple_of(r * grp, grp)
            if finish:
                c = cbuf[pl.ds(t0, grp), :]
                rstd = rbuf[pl.ds(t0, grp), :]
            if send:
                start_rows(t0)
            if finish:
                rows = xbuf.at[buf]
                for ch in range(ROW_CHUNKS):
                    lanes = slice(ch * LANES, (ch + 1) * LANES)
                    x1 = c[:, lanes] * rstd * gain[:, lanes] + bias[:, lanes]
                    x1_ref[pl.ds(t0, grp), lanes] = x1
                    rows[pl.ds(t0 * GATHER_PITCH + ch, grp, stride=GATHER_PITCH), :] = x1
                rows[pl.ds(t0 * GATHER_PITCH + ROW_CHUNKS, grp, stride=GATHER_PITCH), :] = jnp.zeros((grp, LANES), _F32)
            return carry
        lax.fori_loop(0, ts // grp, group, 0)

    @pl.when(g >= 3)
    def _():
        wait_rows(buf)

    @pl.when(g == 0)
    def _():
        row_loop(finish=True, send=False)

    @pl.when(jnp.logical_and(g >= 1, g < n_tiles))
    def _():
        row_loop(finish=True, send=True)

    @pl.when(g == n_tiles)
    def _():
        row_loop(finish=False, send=True)

    @pl.when(g < n_tiles)
    def _():
        _route_tile(par, x1_ref, rwt_ref, rb_ref, tri_ref, low_ref, dest_ref, gate_ref, cnt_ref, tab_ref,
                    dvm, dsm, cnt_sc, used_sc, cur_sc, tab_sc, dest_sem)

    @pl.when(g == n_tiles)
    def _():
        wait_rows(lax.rem(g + 1, 3))
        wait_rows(prev)
        _seal_slots(xs_ref, cnt_sc, tab_sc, meta_v, meta_s, zbuf, dest_sem)


def _slots_to_scalar(dvm, dsm, q, sem):
    ts = TS_MIX
    return [pltpu.make_async_copy(dvm.at[q, k], dsm.at[pl.ds((q * TOP_K + k) * ts, ts)], sem)
            for k in range(TOP_K)]


def _seal_slots(xs_ref, cnt_sc, tab_sc, meta_v, meta_s, zbuf, sem):
    tm = TM_EXPERT
    tile_rows = tm * GATHER_PITCH
    n_tiles = xs_ref.shape[0] // tile_rows
    meta_v[0] = cnt_sc[...].astype(jnp.int32)
    meta_v[1] = tab_sc[...].astype(jnp.int32)
    to_scalar = pltpu.make_async_copy(meta_v, meta_s, sem)
    to_scalar.start()
    to_scalar.wait()
    zbuf[...] = jnp.zeros_like(zbuf)
    pieces = [1 << b for b in reversed(range(tm.bit_length() - 1))]

    def each_copy(fn):
        owned = 0
        for e in range(N_EXPERTS):
            cnt = meta_s[0, e, 0]
            tiles = lax.shift_right_logical(cnt + (tm - 1), tm.bit_length() - 1)
            owned = owned + tiles
            last = meta_s[1, e, jnp.maximum(tiles - 1, 0)]
            fill = cnt - jnp.maximum(tiles - 1, 0) * tm
            spare = tm - fill
            pos = last * tm + fill
            for piece in pieces:
                take = jnp.logical_and(tiles > 0, (spare & piece) != 0)

                @pl.when(take)
                def _():
                    fn(pltpu.make_async_copy(
                        zbuf.at[pl.ds(0, piece * GATHER_PITCH)],
                        xs_ref.at[pl.ds(pos * GATHER_PITCH, piece * GATHER_PITCH)], sem))
                pos = pos + jnp.where((spare & piece) != 0, piece, 0)

        def unowned(j, carry):
            fn(pltpu.make_async_copy(
                zbuf, xs_ref.at[pl.ds(pl.multiple_of(j * tile_rows, tile_rows), tile_rows)], sem))
            return carry
        lax.fori_loop(owned, n_tiles, unowned, 0)

    each_copy(lambda c: c.start())
    each_copy(lambda c: c.wait())


def _mix_tile(s, first, x_ref, w_in_ref, pmix_ref, pscale_ref, convw_ref, w_out_ref, cbuf, rbuf,
              carry_v, carry_u, cnt_sc, used_sc, cur_sc, tab_sc):
    ts = TS_MIX

    @pl.when(first)
    def _():
        cnt_sc[...] = jnp.zeros_like(cnt_sc)
        used_sc[...] = jnp.zeros_like(used_sc)
        cur_sc[...] = jnp.zeros_like(cur_sc)
        tab_sc[...] = jnp.zeros_like(tab_sc)

    @pl.when(s == 0)
    def _():
        carry_v[...] = jnp.zeros_like(carry_v)
        carry_u[...] = jnp.zeros_like(carry_u)

    xb = x_ref[0]
    proj = jnp.dot(xb.astype(_BF16), w_in_ref[...], preferred_element_type=_F32)
    vp = proj[:, :D_POOL]
    bg = proj[:, D_POOL:D_POOL + D_CONV]
    cg = proj[:, D_POOL + D_CONV:D_POOL + 2 * D_CONV]
    vc = proj[:, D_POOL + 2 * D_CONV:]

    ext = jnp.concatenate([carry_v[...], vp], axis=0)
    pos = lax.broadcasted_iota(jnp.int32, (ts, 1), 0) + s * ts
    mixed = []
    for g, w in enumerate(POOL_WINDOWS):
        lo, hi = g * POOL_GROUP_DIM, (g + 1) * POOL_GROUP_DIM
        acc = ext[:, lo:hi]
        sh = 1
        while sh < w:
            acc = acc + pltpu.roll(acc, sh, 0)
            sh *= 2
        cnt = jnp.minimum(pos + 1, w).astype(_F32)
        d = acc[POOL_HALO:] / cnt - vp[:, lo:hi]
        yg = jnp.dot(d.astype(_BF16), pmix_ref[g], preferred_element_type=_F32)
        mixed.append(yg * pscale_ref[:, lo:hi])
    carry_v[...] = vp[ts - POOL_HALO:]

    u = cg * vc
    extu = jnp.concatenate([carry_u[...], u], axis=0)
    u1 = pltpu.roll(extu, 1, 0)[CONV_HALO:]
    u2 = pltpu.roll(extu, 2, 0)[CONV_HALO:]
    yc = bg * (convw_ref[0:1, :] * u2 + convw_ref[1:2, :] * u1 + convw_ref[2:3, :] * u)
    carry_u[...] = u[ts - CONV_HALO:]
    mixed.append(yc)

    mix_in = jnp.concatenate(mixed, axis=1).astype(_BF16)
    mix = jnp.dot(mix_in, w_out_ref[...], preferred_element_type=_F32)
    h = DEEPNORM_ALPHA * xb + mix
    c = h - jnp.mean(h, axis=-1, keepdims=True)
    cbuf[...] = c
    rbuf[...] = jnp.broadcast_to(lax.rsqrt(jnp.mean(c * c, axis=-1, keepdims=True) + LN_EPS), rbuf.shape)


def _route_tile(par, x1_ref, rwt_ref, rb_ref, tri_ref, low_ref, dest_ref, gate_ref, cnt_ref, tab_ref,
                dvm, dsm, cnt_sc, used_sc, cur_sc, tab_sc, dest_sem):
    ts = TS_MIX
    x1 = x1_ref[...]

    logits = lax.dot_general(rwt_ref[...], x1.astype(_BF16), (((1,), (1,)), ((), ())),
                             preferred_element_type=_F32) + rb_ref[:, 0:1]
    eidx = lax.broadcasted_iota(jnp.int32, (N_EXPERTS, ts), 0).astype(_F32)
    vals, sels = [], []
    work = logits
    for k in range(TOP_K):
        m = jnp.max(work, axis=0, keepdims=True)
        first = jnp.min(jnp.where(work == m, eidx, float(N_EXPERTS)), axis=0, keepdims=True)
        sel = eidx == first
        work = jnp.where(sel, -jnp.inf, work)
        vals.append(m)
        sels.append(sel)
    exps = [jnp.exp(v - vals[0]) for v in vals]
    denom = exps[0] + exps[1] + exps[2] + exps[3]
    for k in range(TOP_K):
        gate_ref[k:k + 1, :] = exps[k] / denom
    gate_ref[TOP_K:, :] = jnp.zeros((SUBLANES - TOP_K, ts), _F32)

    tm = float(TM_EXPERT)
    chosen = jnp.zeros((N_EXPERTS, ts), _F32)
    for sel in sels:
        chosen = chosen + sel.astype(_F32)
    incl = jnp.dot(chosen.astype(_BF16), tri_ref[...], preferred_element_type=_F32)
    cnt_b = cnt_sc[:, 0:1]
    cnt_a = cnt_b + jnp.sum(chosen, axis=1, keepdims=True)
    q_b = jnp.floor((cnt_b + (tm - 1.0)) * (1.0 / tm))
    q_a = jnp.floor((cnt_a + (tm - 1.0)) * (1.0 / tm))
    n_new = q_a - q_b
    before = jnp.dot(low_ref[...], jnp.broadcast_to(n_new, (N_EXPERTS, LANES)).astype(_BF16),
                     preferred_element_type=_F32)[:, 0:1]
    used = used_sc[:, 0:1]
    new_tile = used + before
    cur_tile = cur_sc[:, 0:1]
    rank = cnt_b + (incl - chosen)
    page = jnp.floor(rank * (1.0 / tm))
    slot = jnp.where(page < q_b, cur_tile, new_tile) * tm + (rank - page * tm)
    for k in range(TOP_K):
        dest_k = jnp.sum(jnp.where(sels[k], slot, 0.0), axis=0, keepdims=True).astype(jnp.int32)
        dest_ref[k:k + 1, :] = dest_k
        dvm[par, k:k + 1, :] = dest_k
    dest_ref[TOP_K:, :] = jnp.zeros((SUBLANES - TOP_K, ts), jnp.int32)

    took = n_new > 0.0
    lane = lax.broadcasted_iota(jnp.int32, (N_EXPERTS, LANES), 1).astype(_F32)
    table = jnp.where(jnp.logical_and(lane == q_b, took), new_tile, tab_sc[...])
    tab_sc[...] = table
    cur_sc[...] = jnp.broadcast_to(jnp.where(took, new_tile, cur_tile), (N_EXPERTS, LANES))
    used_sc[...] = jnp.broadcast_to(used + jnp.sum(n_new, axis=0, keepdims=True), (N_EXPERTS, LANES))
    cnt_sc[...] = jnp.broadcast_to(cnt_a, (N_EXPERTS, LANES))
    cnt_ref[...] = jnp.broadcast_to(cnt_a, (N_EXPERTS, LANES)).astype(jnp.int32)
    tab_ref[...] = table.astype(jnp.int32)

    for c in _slots_to_scalar(dvm, dsm, par, dest_sem):
        c.start()


def _mix_route(x, w_in, pmix, pscale, convw, w_out, g1, b1, rwt, rb, tri, low, n_slots):
    bsz, seq, d = x.shape
    n = bsz * seq
    ns = seq // TS_MIX
    n_tiles = bsz * ns
    assert n // TM_EXPERT <= LANES
    full = lambda shape: pl.BlockSpec(shape, lambda g: (0,) * len(shape))
    tile = lambda g: jnp.minimum(g, n_tiles - 1)
    state = pltpu.VMEM((N_EXPERTS, LANES), _F32)
    return pl.pallas_call(
        functools.partial(_mix_route_kernel, ns),
        grid=(n_tiles + 1,),
        in_specs=[
            pl.BlockSpec((1, TS_MIX, d), lambda g: (tile(g) // ns, tile(g) % ns, 0)),
            full(w_in.shape), full(pmix.shape), full(pscale.shape), full(convw.shape),
            full(w_out.shape), full(g1.shape), full(b1.shape), full(rwt.shape), full(rb.shape),
            full(tri.shape), full(low.shape),
        ],
        out_specs=[
            pl.BlockSpec((TS_MIX, D_MODEL), lambda g: (tile(g), 0)),
            pl.BlockSpec((SUBLANES, TS_MIX), lambda g: (0, tile(g))),
            pl.BlockSpec((SUBLANES, TS_MIX), lambda g: (0, tile(g))),
            pl.BlockSpec((N_EXPERTS, LANES), lambda g: (0, 0)),
            pl.BlockSpec((N_EXPERTS, LANES), lambda g: (0, 0)),
            pl.BlockSpec(memory_space=pl.ANY),
        ],
        out_shape=[
            jax.ShapeDtypeStruct((n, D_MODEL), _F32),
            jax.ShapeDtypeStruct((SUBLANES, n), jnp.int32),
            jax.ShapeDtypeStruct((SUBLANES, n), _F32),
            jax.ShapeDtypeStruct((N_EXPERTS, LANES), jnp.int32),
            jax.ShapeDtypeStruct((N_EXPERTS, LANES), jnp.int32),
            jax.ShapeDtypeStruct((n_slots * GATHER_PITCH, LANES), _F32),
        ],
        scratch_shapes=[
            pltpu.VMEM((3, TS_MIX * GATHER_PITCH, LANES), _F32),
            pltpu.VMEM((TS_MIX, D_MODEL), _F32),
            pltpu.VMEM((TS_MIX, LANES), _F32),
            pltpu.VMEM((2, SUBLANES, TS_MIX), jnp.int32),
            pltpu.SMEM((2 * TOP_K * TS_MIX,), jnp.int32),
            pltpu.VMEM((POOL_HALO, D_POOL), _F32),
            pltpu.VMEM((CONV_HALO, D_CONV), _F32),
            state, state, state, state,
            pltpu.VMEM((2, N_EXPERTS, LANES), jnp.int32),
            pltpu.SMEM((2, N_EXPERTS, LANES), jnp.int32),
            pltpu.VMEM((TM_EXPERT * GATHER_PITCH, LANES), _F32),
            pltpu.SemaphoreType.DMA((3,)),
            pltpu.SemaphoreType.DMA(()),
        ],
        compiler_params=pltpu.CompilerParams(
            dimension_semantics=("arbitrary",),
            vmem_limit_bytes=48 * 1024 * 1024),
        name="mix_route",
    )(x, w_in, pmix, pscale, convw, w_out, g1, b1, rwt, rb, tri, low)


def _experts_kernel(be_ref, nv_ref, par_ref, nxt_ref, tab_ref, first_ref, cnt_ref, xs_ref, wgu_hbm, bgu_ref,
                    wdn_hbm, bdn_ref, out_ref, wgu_f32, wdn_f32, wgu_bf, wdn_bf, sems):
    del tab_ref
    j = pl.program_id(0)
    tm = TM_EXPERT
    valid = j < nv_ref[0]
    e_j = be_ref[j]
    rows_used = cnt_ref[e_j] - (j - first_ref[e_j]) * tm
    new_expert = jnp.logical_or(j == 0, be_ref[j] != be_ref[jnp.maximum(j - 1, 0)])

    def weight_copies(e, s):
        return (pltpu.make_async_copy(wgu_hbm.at[e], wgu_f32.at[s], sems.at[0, s]),
                pltpu.make_async_copy(wdn_hbm.at[e], wdn_f32.at[s], sems.at[1, s]))

    @pl.when(j == 0)
    def _():
        for c in weight_copies(be_ref[0], par_ref[0]):
            c.start()

    @pl.when(jnp.logical_and(valid, new_expert))
    def _():
        s = par_ref[j]
        for c in weight_copies(be_ref[j], s):
            c.wait()

        nxt = nxt_ref[be_ref[j]]

        @pl.when(nxt >= 0)
        def _():
            for c in weight_copies(nxt, 1 - s):
                c.start()

        wgu_bf[...] = wgu_f32[s].astype(_BF16)
        wdn_bf[...] = wdn_f32[s].astype(_BF16)

    def ffn(rows):
        x = _from_row_tiles(xs_ref, rows, GATHER_PITCH).astype(_BF16)
        gu = jnp.dot(x, wgu_bf[...], preferred_element_type=_F32) + bgu_ref[e_j]
        gate = jnp.minimum(gu[:, :D_FF], SWIGLU_LIMIT)
        up = jnp.clip(gu[:, D_FF:], -SWIGLU_LIMIT, SWIGLU_LIMIT)
        glu = gate * jax.nn.sigmoid(SWIGLU_ALPHA * gate)
        act = ((up + 1.0) * glu).astype(_BF16)
        y = jnp.dot(act, wdn_bf[...], preferred_element_type=_F32) + bdn_ref[e_j]
        _to_row_tiles(out_ref, y, GATHER_PITCH)
        if rows < tm:
            out_ref[pl.ds(rows * GATHER_PITCH, (tm - rows) * GATHER_PITCH), :] = jnp.zeros(
                ((tm - rows) * GATHER_PITCH, LANES), _F32)

    @pl.when(jnp.logical_and(valid, rows_used > tm // 2))
    def _():
        ffn(tm)

    @pl.when(jnp.logical_and(valid, rows_used <= tm // 2))
    def _():
        ffn(tm // 2)

    @pl.when(j >= nv_ref[0])
    def _():
        out_ref[...] = jnp.zeros_like(out_ref)


def _experts(blk_e, n_valid, slot_par, next_e, table, first, cnt, xs, wgu, bgu, wdn, bdn):
    n_slots = xs.shape[0] // GATHER_PITCH
    nb = n_slots // TM_EXPERT
    rows = TM_EXPERT * GATHER_PITCH

    def tile_of(j, be, tab, first):
        e = be[j]
        return tab[e, jnp.clip(j - first[e], 0, LANES - 1)]

    def xs_map(j, be, nv, par, nxt, tab, first, cnt):
        return (tile_of(jnp.minimum(j, nv[0] - 1), be, tab, first), 0)

    def out_map(j, be, nv, par, nxt, tab, first, cnt):
        return (jnp.where(j < nv[0], tile_of(j, be, tab, first), j), 0)

    whole = lambda j, be, nv, par, nxt, tab, first, cnt: (0, 0, 0)
    grid_spec = pltpu.PrefetchScalarGridSpec(
        num_scalar_prefetch=7,
        grid=(nb,),
        in_specs=[
            pl.BlockSpec((rows, LANES), xs_map),
            pl.BlockSpec(memory_space=pl.ANY),
            pl.BlockSpec((N_EXPERTS, 1, 2 * D_FF), whole),
            pl.BlockSpec(memory_space=pl.ANY),
            pl.BlockSpec((N_EXPERTS, 1, D_MODEL), whole),
        ],
        out_specs=pl.BlockSpec((rows, LANES), out_map),
        scratch_shapes=[
            pltpu.VMEM((2, D_MODEL, 2 * D_FF), _F32),
            pltpu.VMEM((2, D_FF, D_MODEL), _F32),
            pltpu.VMEM((D_MODEL, 2 * D_FF), _BF16),
            pltpu.VMEM((D_FF, D_MODEL), _BF16),
            pltpu.SemaphoreType.DMA((2, 2)),
        ],
    )
    return pl.pallas_call(
        _experts_kernel,
        grid_spec=grid_spec,
        out_shape=jax.ShapeDtypeStruct((n_slots * GATHER_PITCH, LANES), _F32),
        compiler_params=pltpu.CompilerParams(
            dimension_semantics=("arbitrary",),
            vmem_limit_bytes=60 * 1024 * 1024),
        name="experts",
    )(blk_e, n_valid, slot_par, next_e, table, first, cnt, xs, wgu, bgu, wdn, bdn)


def _combine_kernel(dcur_ref, dnext_ref, yb_ref, x1_ref, gt_ref, p_ref, g2_ref, b2_ref, pw_ref, gw_ref,
                    gb_ref, g3_ref, b3_ref, out_ref, gbuf, hbuf, sems):
    ts = TS_COMBINE
    grp = COMBINE_GROUP
    i = pl.program_id(0)
    slot = i % 2
    nxt = 1 - slot

    def start_rows(d_ref, s, t0):
        for j in range(grp):
            t = t0 + j
            for k in range(TOP_K):
                pltpu.make_async_copy(_row_tile(yb_ref, d_ref[t * TOP_K + k], GATHER_PITCH),
                                      gbuf.at[s, k, pl.ds(t * GATHER_PITCH, ROW_CHUNKS)],
                                      sems.at[s]).start(priority=k % 2)

    def wait_tile(s):
        for k in range(TOP_K):
            pltpu.make_async_copy(yb_ref.at[pl.ds(0, ts * ROW_CHUNKS)],
                                  gbuf.at[s, k, pl.ds(0, ts * ROW_CHUNKS)], sems.at[s]).wait()

    @pl.when(i == 0)
    def _():
        def first(r, carry):
            start_rows(dcur_ref, 0, r * grp)
            return carry
        lax.fori_loop(0, ts // grp, first, 0)

    wait_tile(slot)

    def group(r, carry):
        t0 = pl.multiple_of(r * grp, grp)
        x1g = x1_ref[pl.ds(t0, grp), :]
        gts = gt_ref[pl.ds(t0, grp), :]
        ys = []
        for k in range(TOP_K):
            rows_k = gbuf.at[slot, k]
            ys.append(jnp.concatenate(
                [rows_k[pl.ds(t0 * GATHER_PITCH + c, grp, stride=GATHER_PITCH), :] for c in range(ROW_CHUNKS)],
                axis=1))
        start_rows(dnext_ref, nxt, t0)
        ffn = gts[:, 0:1] * ys[0]
        for k in range(1, TOP_K):
            ffn = ffn + gts[:, k:k + 1] * ys[k]
        hbuf[pl.ds(t0, grp), :] = DEEPNORM_ALPHA * x1g + ffn
        return carry

    lax.fori_loop(0, ts // grp, group, 0)

    @pl.when(i == pl.num_programs(0) - 1)
    def _():
        wait_tile(nxt)

    x2 = _layer_norm(hbuf[...], g2_ref[...], b2_ref[...])
    z = jnp.dot(x2.astype(_BF16), gw_ref[...], preferred_element_type=_F32) + gb_ref[...]
    ple = jnp.dot(p_ref[...].astype(_BF16), pw_ref[...], preferred_element_type=_F32)
    x3 = _layer_norm(DEEPNORM_ALPHA * x2 + jax.nn.sigmoid(z) * ple, g3_ref[...], b3_ref[...])
    out_ref[...] = x3


def _combine(dest_flat, yb, x1, gates_t, p2d, g2, b2, pw, gw, gb, g3, b3):
    n = dest_flat.shape[0] // TOP_K
    ts = TS_COMBINE
    last = n // ts - 1
    full = lambda shape: pl.BlockSpec(shape, lambda i: (0,) * len(shape))
    return pl.pallas_call(
        _combine_kernel,
        grid=(n // ts,),
        in_specs=[
            pl.BlockSpec((TOP_K * ts,), lambda i: (i,), memory_space=pltpu.SMEM),
            pl.BlockSpec((TOP_K * ts,), lambda i: (jnp.minimum(i + 1, last),), memory_space=pltpu.SMEM),
            pl.BlockSpec(memory_space=pl.ANY),
            pl.BlockSpec((ts, D_MODEL), lambda i: (i, 0)),
            pl.BlockSpec((ts, TOP_K), lambda i: (i, 0)),
            pl.BlockSpec((ts, PLE_DIM), lambda i: (i, 0)),
            full(g2.shape), full(b2.shape), full(pw.shape), full(gw.shape), full(gb.shape),
            full(g3.shape), full(b3.shape),
        ],
        out_specs=pl.BlockSpec((ts, D_MODEL), lambda i: (i, 0)),
        out_shape=jax.ShapeDtypeStruct((n, D_MODEL), _F32),
        scratch_shapes=[
            pltpu.VMEM((2, TOP_K, ts * GATHER_PITCH, LANES), _F32),
            pltpu.VMEM((ts, D_MODEL), _F32),
            pltpu.SemaphoreType.DMA((2,)),
        ],
        compiler_params=pltpu.CompilerParams(
            dimension_semantics=("arbitrary",),
            vmem_limit_bytes=48 * 1024 * 1024),
        name="combine",
    )(dest_flat, dest_flat, yb, x1, gates_t, p2d, g2, b2, pw, gw, gb, g3, b3)


def kernel(x, p, w_in, pool_mix, pool_scale, conv_w, w_out, ln1_g, ln1_b, router_w, router_b,
           w_gate_up, b_gate_up, w_down, b_down, ln2_g, ln2_b, ple_proj, ple_gate_w, ple_gate_b,
           ln3_g, ln3_b):
    assert DEPTH == 1 and x.shape[-1] == D_MODEL
    bsz, seq, d = x.shape
    n = bsz * seq
    assert seq % TS_MIX == 0 and n % TS_COMBINE == 0
    row = lambda v: v.reshape(1, -1)

    tri = jnp.triu(jnp.ones((TS_MIX, TS_MIX), _BF16))
    low = jnp.tril(jnp.ones((N_EXPERTS, N_EXPERTS), _BF16), k=-1)
    tm = TM_EXPERT
    n_slots = n * TOP_K + N_EXPERTS * tm
    nb = n_slots // tm
    x1, dest, gates, counts, table, xs = _mix_route(
        x, w_in[0].astype(_BF16), pool_mix[0].astype(_BF16), row(pool_scale[0]), conv_w[0],
        w_out[0].astype(_BF16), row(ln1_g[0]), row(ln1_b[0]),
        router_w[0].T.astype(_BF16), jnp.broadcast_to(router_b[0][:, None], (N_EXPERTS, LANES)), tri, low,
        n_slots)

    cnt = counts[:, 0]
    tiles_e = (cnt + tm - 1) // tm
    tile_end = jnp.cumsum(tiles_e)
    n_valid = tile_end[-1].astype(jnp.int32)
    steps = jnp.arange(nb, dtype=jnp.int32)
    e_ids = jnp.arange(N_EXPERTS, dtype=jnp.int32)
    last_owner = jnp.max(jnp.where(tiles_e > 0, e_ids, 0))
    blk_e = jnp.minimum(jnp.sum(steps[:, None] >= tile_end[None, :], axis=1), N_EXPERTS - 1)
    blk_e = jnp.where(steps < n_valid, blk_e, last_owner).astype(jnp.int32)
    first = (tile_end - tiles_e).astype(jnp.int32)

    padded = tiles_e
    later = jnp.logical_and(e_ids[None, :] > e_ids[:, None], (padded > 0)[None, :])
    next_nonempty = jnp.min(jnp.where(later, e_ids[None, :], N_EXPERTS), axis=1)
    next_e = jnp.where(next_nonempty == N_EXPERTS, -1, next_nonempty).astype(jnp.int32)
    switched = jnp.concatenate([jnp.zeros((1,), jnp.int32), (blk_e[1:] != blk_e[:-1]).astype(jnp.int32)])
    slot_par = (jnp.cumsum(switched) % 2).astype(jnp.int32)
    n_valid = n_valid.reshape(1)

    dest = dest[:TOP_K].T.reshape(-1)
    yb = _experts(blk_e, n_valid, slot_par, next_e, table, first, cnt, xs,
                  w_gate_up[0], b_gate_up[0][:, None, :], w_down[0], b_down[0][:, None, :])
    out = _combine(dest, yb, x1, gates[:TOP_K].T, p[0].reshape(n, PLE_DIM),
                   row(ln2_g[0]), row(ln2_b[0]), ple_proj[0].astype(_BF16),
                   ple_gate_w[0].astype(_BF16), row(ple_gate_b[0]), row(ln3_g[0]), row(ln3_b[0]))
    return out.reshape(bsz, seq, d)
```
